```python
import math
import jax, jax.numpy as jnp
from jax import lax
import numpy as np

D_MODEL = 1024
BATCH = 8
SEQ = 2048
DEPTH = 2

CHUNK = 64
Q_BLOCK = 128
N_MIXERS = 2
N_CONV_LAYERS = (DEPTH + 1) // 2
N_ATTN_LAYERS = DEPTH // 2
CONV_WIDTH = 31
N_HEADS = 8
HEAD_DIM = D_MODEL // N_HEADS // 2
V_DIM = 2 * HEAD_DIM
N_GROUPS = 4
EXPERTS_PER_GROUP = 8
N_EXPERTS = N_GROUPS * EXPERTS_PER_GROUP
TOP_K = 2
D_EXPERT = D_MODEL // 2
ROW_BLOCK = 256
EPS = 1e-6

kernel_name = "hybrid_conv_diffattn_hmoe"


def rms_norm(x, g):
    xf = x.astype(jnp.float32)
    y = xf * lax.rsqrt(jnp.mean(xf * xf, axis=-1, keepdims=True) + EPS)
    return (y * g.astype(jnp.float32)).astype(x.dtype)


def layer_norm(x, g, b):
    xf = x.astype(jnp.float32)
    mu = jnp.mean(xf, axis=-1, keepdims=True)
    var = jnp.mean(jnp.square(xf - mu), axis=-1, keepdims=True)
    y = (xf - mu) * lax.rsqrt(var + EPS)
    return (y * g.astype(jnp.float32) + b.astype(jnp.float32)).astype(x.dtype)


def lambda_init_fn(layer_idx):
    return 0.8 - 0.6 * math.exp(-0.3 * layer_idx)


def conformer_conv(h, w_in, b_in, dw, dw_b, ln_g, ln_b, w_out, b_out):
    u = h @ w_in + b_in
    a, g = jnp.split(u, 2, axis=-1)
    u = a * jax.nn.sigmoid(g)
    u = lax.conv_general_dilated(
        u, dw[:, None, :].astype(u.dtype), window_strides=(1,),
        padding=[(CONV_WIDTH - 1, 0)],
        dimension_numbers=("NWC", "WIO", "NWC"),
        feature_group_count=D_MODEL) + dw_b
    u = jax.nn.silu(layer_norm(u, ln_g, ln_b))
    return u @ w_out + b_out


def diff_attention(h, w_qkv, q_g, k_g, lq1, lk1, lq2, lk2, subln_g, w_o, lambda_init):
    B, S, _ = h.shape
    qkv = h @ w_qkv
    q, k, v = jnp.split(qkv, 3, axis=-1)
    q = rms_norm(q.reshape(B, S, N_HEADS, 2, HEAD_DIM), q_g) * (HEAD_DIM ** -0.5)
    k = rms_norm(k.reshape(B, S, N_HEADS, 2, HEAD_DIM), k_g)
    q = q.transpose(0, 2, 3, 1, 4)
    k = k.transpose(0, 2, 3, 1, 4)
    v = v.reshape(B, S, N_HEADS, V_DIM).transpose(0, 2, 1, 3)
    lam = (jnp.exp(jnp.sum(lq1.astype(jnp.float32) * lk1.astype(jnp.float32)))
           - jnp.exp(jnp.sum(lq2.astype(jnp.float32) * lk2.astype(jnp.float32)))
           + lambda_init)
    outs = []
    for start in range(0, S, Q_BLOCK):
        end = start + Q_BLOCK
        qb = q[:, :, :, start:end]
        kb = k[:, :, :, :end]
        vb = v[:, :, :end]
        s = jnp.einsum("bhcqd,bhckd->bhcqk", qb, kb).astype(jnp.float32)
        q_chunk = (start + jnp.arange(Q_BLOCK)) // CHUNK
        k_chunk = jnp.arange(end) // CHUNK
        mask = k_chunk[None, :] <= q_chunk[:, None]
        p = jax.nn.softmax(jnp.where(mask, s, -jnp.inf), axis=-1)
        a = p[:, :, 0] - lam * p[:, :, 1]
        outs.append(jnp.einsum("bhqk,bhkd->bhqd", a.astype(vb.dtype), vb))
    o = jnp.concatenate(outs, axis=2)
    o = rms_norm(o, subln_g) * (1.0 - lambda_init)
    o = o.transpose(0, 2, 1, 3).reshape(B, S, D_MODEL)
    return o @ w_o


def hierarchical_moe(h, w_grp, b_grp, w_exp, b_exp, w1, w3, w2):
    B, S, D = h.shape
    T = B * S
    hf = h.reshape(T, D)
    h32 = hf.astype(jnp.float32)
    grp_prob = jax.nn.softmax(h32 @ w_grp.astype(jnp.float32) + b_grp.astype(jnp.float32), axis=-1)
    grp_p, grp_idx = lax.top_k(grp_prob, 1)
    exp_logits = (h32 @ w_exp.astype(jnp.float32) + b_exp.astype(jnp.float32)
                  ).reshape(T, N_GROUPS, EXPERTS_PER_GROUP)
    sel = jnp.take_along_axis(exp_logits, grp_idx[:, :, None], axis=1)[:, 0]
    top_p, top_j = lax.top_k(jax.nn.softmax(sel, axis=-1), TOP_K)
    gates = grp_p * top_p / jnp.sum(top_p, axis=-1, keepdims=True)
    experts = grp_idx * EXPERTS_PER_GROUP + top_j

    A = T * TOP_K
    flat_e = experts.reshape(-1)
    order = jnp.argsort(flat_e)
    e_sorted = flat_e[order]
    tok_sorted = order // TOP_K
    gate_sorted = gates.reshape(-1)[order]
    counts = jnp.bincount(flat_e, length=N_EXPERTS)
    starts = jnp.cumsum(counts) - counts
    padded = (counts + ROW_BLOCK - 1) // ROW_BLOCK * ROW_BLOCK
    pad_ends = jnp.cumsum(padded)
    pad_starts = pad_ends - padded
    dest = pad_starts[e_sorted] + jnp.arange(A) - starts[e_sorted]
    n_blocks = -(-A // ROW_BLOCK) + N_EXPERTS
    row_tok = jnp.zeros((n_blocks * ROW_BLOCK,), jnp.int32).at[dest].set(tok_sorted)
    block_e = jnp.minimum(
        jnp.searchsorted(pad_ends, jnp.arange(n_blocks) * ROW_BLOCK, side="right"),
        N_EXPERTS - 1)
    xb = hf[row_tok].reshape(n_blocks, ROW_BLOCK, D)

    def expert_block(args):
        xr, e = args
        return (jax.nn.silu(xr @ w1[e]) * (xr @ w3[e])) @ w2[e]

    yb = lax.map(expert_block, (xb, block_e)).reshape(n_blocks * ROW_BLOCK, D)
    y_assign = yb[dest] * gate_sorted[:, None].astype(yb.dtype)
    out = jnp.zeros((T, D), dtype=yb.dtype).at[tok_sorted].add(y_assign)
    return out.reshape(B, S, D)


def setup_inputs(seed: int = 0) -> dict:
    key = jax.random.key(seed)
    ks = iter(jax.random.split(key, 32))
    f32 = jnp.float32

    def nrm(shape, scale):
        return jax.random.normal(next(ks), shape, f32) * scale

    def gain(shape):
        return 1.0 + nrm(shape, 0.02)

    D, NC, NA = D_MODEL, N_CONV_LAYERS, N_ATTN_LAYERS
    return {
        "x": nrm((BATCH, SEQ, D), 1.0),
        "norm1_g": gain((DEPTH, D)),
        "norm2_g": gain((DEPTH, D)),
        "conv_w_in": nrm((NC, D, 2 * D), D ** -0.5),
        "conv_b_in": nrm((NC, 2 * D), 0.02),
        "conv_dw": nrm((NC, CONV_WIDTH, D), CONV_WIDTH ** -0.5),
        "conv_dw_b": nrm((NC, D), 0.02),
        "conv_ln_g": gain((NC, D)),
        "conv_ln_b": nrm((NC, D), 0.02),
        "conv_w_out": nrm((NC, D, D), D ** -0.5),
        "conv_b_out": nrm((NC, D), 0.02),
        "attn_w_qkv": nrm((NA, D, 3 * D), D ** -0.5),
        "attn_q_g": gain((NA, HEAD_DIM)),
        "attn_k_g": gain((NA, HEAD_DIM)),
        "attn_lq1": nrm((NA, HEAD_DIM), 0.1),
        "attn_lk1": nrm((NA, HEAD_DIM), 0.1),
        "attn_lq2": nrm((NA, HEAD_DIM), 0.1),
        "attn_lk2": nrm((NA, HEAD_DIM), 0.1),
        "attn_subln_g": gain((NA, V_DIM)),
        "attn_w_o": nrm((NA, D, D), D ** -0.5),
        "moe_w_grp": nrm((DEPTH, D, N_GROUPS), D ** -0.5),
        "moe_b_grp": nrm((DEPTH, N_GROUPS), 0.01),
        "moe_w_exp": nrm((DEPTH, D, N_EXPERTS), D ** -0.5),
        "moe_b_exp": nrm((DEPTH, N_EXPERTS), 0.01),
        "moe_w1": nrm((DEPTH, N_EXPERTS, D, D_EXPERT), D ** -0.5),
        "moe_w3": nrm((DEPTH, N_EXPERTS, D, D_EXPERT), D ** -0.5),
        "moe_w2": nrm((DEPTH, N_EXPERTS, D_EXPERT, D), D_EXPERT ** -0.5),
    }


def reference(x, norm1_g, norm2_g,
              conv_w_in, conv_b_in, conv_dw, conv_dw_b, conv_ln_g, conv_ln_b, conv_w_out, conv_b_out,
              attn_w_qkv, attn_q_g, attn_k_g, attn_lq1, attn_lk1, attn_lq2, attn_lk2, attn_subln_g, attn_w_o,
              moe_w_grp, moe_b_grp, moe_w_exp, moe_b_exp, moe_w1, moe_w3, moe_w2):
    for i in range(DEPTH):
        h = rms_norm(x, norm1_g[i])
        j = i // N_MIXERS
        if i % N_MIXERS == 0:
            mix = conformer_conv(h, conv_w_in[j], conv_b_in[j], conv_dw[j], conv_dw_b[j],
                                 conv_ln_g[j], conv_ln_b[j], conv_w_out[j], conv_b_out[j])
        else:
            mix = diff_attention(h, attn_w_qkv[j], attn_q_g[j], attn_k_g[j],
                                 attn_lq1[j], attn_lk1[j], attn_lq2[j], attn_lk2[j],
                                 attn_subln_g[j], attn_w_o[j], lambda_init_fn(i))
        x = x + mix
        h = rms_norm(x, norm2_g[i])
        x = x + hierarchical_moe(h, moe_w_grp[i], moe_b_grp[i], moe_w_exp[i], moe_b_exp[i],
                                 moe_w1[i], moe_w3[i], moe_w2[i])
    return x
```

```python
import functools
import math

import jax
import jax.numpy as jnp
from jax import lax
from jax.experimental import pallas as pl
from jax.experimental.pallas import tpu as pltpu

D_MODEL = 1024
BATCH = 8
SEQ = 2048
N_TOK = BATCH * SEQ
CHUNK = 64
CONV_WIDTH = 31
N_HEADS = 8
HEAD_DIM = 64
V_DIM = 128
N_GROUPS = 4
EXPERTS_PER_GROUP = 8
N_EXPERTS = 32
TOP_K = 2
D_EXPERT = 512
EPS = 1e-6

LANES = 128
SUBLANES = 8
SLAB = D_MODEL // LANES
VMEM_LIMIT = 56 * 1024 * 1024

CONV_TS = 256
HALO = 32
CONV_RC = 64
POST_TS = 256
QKV_TS = 512
ATT_TQ = 256
ROW_BLOCK = 256
N_ASSIGN = N_TOK * TOP_K
N_BLOCKS = N_ASSIGN // ROW_BLOCK + N_EXPERTS
N_ROWS = N_BLOCKS * ROW_BLOCK
DISP_CH = 1024
COMB_TM = 256
GRP_LANE0 = N_EXPERTS


def _cparams(n_axes):
    return pltpu.CompilerParams(
        dimension_semantics=("arbitrary",) * n_axes, vmem_limit_bytes=VMEM_LIMIT)


def _rms(x, g):
    return x * lax.rsqrt(jnp.mean(x * x, axis=-1, keepdims=True) + EPS) * g


def _to_slab(ref, val, rows):
    for j in range(SLAB):
        ref[pl.ds(j, rows, stride=SLAB), :] = val[:, j * LANES:(j + 1) * LANES]


def _from_slab(ref, rows):
    return jnp.concatenate(
        [ref[pl.ds(j, rows, stride=SLAB), :] for j in range(SLAB)], axis=-1)


def _residual_norm_route(x1, g2_ref, wr_ref, br_ref, run_ref,
                         x1_ref, h2s_ref, rf_ref, cnt_ref, rows):
    x1_ref[...] = x1
    h2 = _rms(x1, g2_ref[...])
    _to_slab(h2s_ref, h2, rows)

    logits = jnp.dot(h2, wr_ref[...], preferred_element_type=jnp.float32,
                     precision=lax.Precision.HIGHEST) + br_ref[...]
    lane = lax.broadcasted_iota(jnp.int32, (rows, LANES), 1)
    lane_f = lane.astype(jnp.float32)
    neg = jnp.float32(-jnp.inf)
    big = jnp.float32(1e9)

    gmask = (lane >= GRP_LANE0) & (lane < GRP_LANE0 + N_GROUPS)
    gl = jnp.where(gmask, logits, neg)
    gmax = jnp.max(gl, axis=1, keepdims=True)
    gidx = jnp.min(jnp.where(gl == gmax, lane_f, big), axis=1, keepdims=True) - GRP_LANE0
    gsum = jnp.sum(jnp.where(gmask, jnp.exp(gl - gmax), 0.0), axis=1, keepdims=True)
    grp_p = 1.0 / gsum

    lo = gidx * EXPERTS_PER_GROUP
    emask = (lane_f >= lo) & (lane_f < lo + EXPERTS_PER_GROUP)
    el = jnp.where(emask, logits, neg)
    m1 = jnp.max(el, axis=1, keepdims=True)
    i1 = jnp.min(jnp.where(el == m1, lane_f, big), axis=1, keepdims=True)
    el2 = jnp.where(lane_f == i1, neg, el)
    m2 = jnp.max(el2, axis=1, keepdims=True)
    i2 = jnp.min(jnp.where(el2 == m2, lane_f, big), axis=1, keepdims=True)
    t = jnp.exp(m2 - m1)
    inv = 1.0 / (1.0 + t)
    g_first = grp_p * inv
    g_second = grp_p * t * inv

    sel1 = lane_f == i1
    sel2 = lane_f == i2
    member = jnp.where(sel1 | sel2, 1.0, 0.0).astype(jnp.bfloat16)
    r_i = lax.broadcasted_iota(jnp.int32, (rows, rows), 0)
    c_i = lax.broadcasted_iota(jnp.int32, (rows, rows), 1)
    tri = jnp.where(c_i < r_i, 1.0, 0.0).astype(jnp.bfloat16)
    cum = jnp.dot(tri, member, preferred_element_type=jnp.float32) + run_ref[...]
    rank1 = jnp.sum(jnp.where(sel1, cum, 0.0), axis=1, keepdims=True)
    rank2 = jnp.sum(jnp.where(sel2, cum, 0.0), axis=1, keepdims=True)
    run_new = run_ref[...] + jnp.sum(member.astype(jnp.float32), axis=0, keepdims=True)
    run_ref[...] = run_new
    cnt_ref[...] = jnp.broadcast_to(run_new, (SUBLANES, LANES))

    out = jnp.where(lane == 0, g_first, 0.0)
    out = jnp.where(lane == 1, g_second, out)
    out = jnp.where(lane == 2, i1, out)
    out = jnp.where(lane == 3, i2, out)
    out = jnp.where(lane == 4, rank1, out)
    out = jnp.where(lane == 5, rank2, out)
    rf_ref[...] = out


def _route_out_shapes():
    return (
        jax.ShapeDtypeStruct((N_TOK, D_MODEL), jnp.float32),
        jax.ShapeDtypeStruct((N_TOK * SLAB, LANES), jnp.float32),
        jax.ShapeDtypeStruct((N_TOK, LANES), jnp.float32),
        jax.ShapeDtypeStruct((SUBLANES, LANES), jnp.float32),
    )


def _route_out_specs(ts, idx):
    return (
        pl.BlockSpec((ts, D_MODEL), lambda *a: (idx(*a), 0)),
        pl.BlockSpec((ts * SLAB, LANES), lambda *a: (idx(*a), 0)),
        pl.BlockSpec((ts, LANES), lambda *a: (idx(*a), 0)),
        pl.BlockSpec((SUBLANES, LANES), lambda *a: (0, 0)),
    )


def _full(shape):
    return pl.BlockSpec(shape, lambda *a: (0,) * len(shape))


def _conv_kernel(x_ref, g1_ref, win_ref, bin_ref, dw_ref, dwb_ref, lng_ref, lnb_ref,
                 wout_ref, bout_ref, g2_ref, wr_ref, br_ref,
                 x1_ref, h2s_ref, rf_ref, cnt_ref,
                 ext_ref, conv_ref, run_ref):
    b = pl.program_id(0)
    s = pl.program_id(1)
    ts = CONV_TS

    @pl.when((b == 0) & (s == 0))
    def _():
        run_ref[...] = jnp.zeros_like(run_ref)

    @pl.when(s == 0)
    def _():
        ext_ref[0:HALO, :] = jnp.zeros((HALO, D_MODEL), jnp.float32)

    x = x_ref[...]
    h = _rms(x, g1_ref[...]).astype(jnp.bfloat16)
    u = jnp.dot(h, win_ref[...], preferred_element_type=jnp.float32) + bin_ref[...]
    glu = u[:, :D_MODEL] * jax.nn.sigmoid(u[:, D_MODEL:])
    ext_ref[HALO:HALO + ts, :] = glu

    base = HALO - (CONV_WIDTH - 1)

    def lane_chunk(c, carry):
        col = pl.multiple_of(c * LANES, LANES)
        for r0 in range(0, ts, CONV_RC):
            acc = jnp.zeros((CONV_RC, LANES), jnp.float32)
            for k in range(CONV_WIDTH):
                acc = acc + (ext_ref[pl.ds(r0 + base + k, CONV_RC), pl.ds(col, LANES)]
                             * dw_ref[pl.ds(k, 1), pl.ds(col, LANES)])
            conv_ref[pl.ds(r0, CONV_RC), pl.ds(col, LANES)] = acc
        return carry

    lax.fori_loop(0, D_MODEL // LANES, lane_chunk, 0)
    ext_ref[0:HALO, :] = ext_ref[ts:ts + HALO, :]

    v = conv_ref[...] + dwb_ref[...]
    mu = jnp.mean(v, axis=-1, keepdims=True)
    vc = v - mu
    var = jnp.mean(vc * vc, axis=-1, keepdims=True)
    y = vc * lax.rsqrt(var + EPS) * lng_ref[...] + lnb_ref[...]
    y = (y * jax.nn.sigmoid(y)).astype(jnp.bfloat16)
    mix = jnp.dot(y, wout_ref[...], preferred_element_type=jnp.float32) + bout_ref[...]
    _residual_norm_route(x + mix, g2_ref, wr_ref, br_ref, run_ref,
                         x1_ref, h2s_ref, rf_ref, cnt_ref, ts)


def _conv_layer(x, g1, w_in, b_in, dw, dw_b, ln_g, ln_b, w_out, b_out, g2, wr, br):
    ns = SEQ // CONV_TS
    tile = lambda b, s: b * ns + s
    return pl.pallas_call(
        _conv_kernel,
        grid=(BATCH, ns),
        in_specs=[
            pl.BlockSpec((CONV_TS, D_MODEL), lambda b, s: (tile(b, s), 0)),
            _full((1, D_MODEL)),
            _full((D_MODEL, 2 * D_MODEL)),
            _full((1, 2 * D_MODEL)),
            _full((HALO, D_MODEL)),
            _full((1, D_MODEL)), _full((1, D_MODEL)), _full((1, D_MODEL)),
            _full((D_MODEL, D_MODEL)),
            _full((1, D_MODEL)), _full((1, D_MODEL)),
            _full((D_MODEL, LANES)), _full((1, LANES)),
        ],
        out_specs=_route_out_specs(CONV_TS, tile),
        out_shape=_route_out_shapes(),
        scratch_shapes=[
            pltpu.VMEM((HALO + CONV_TS, D_MODEL), jnp.float32),
            pltpu.VMEM((CONV_TS, D_MODEL), jnp.float32),
            pltpu.VMEM((1, LANES), jnp.float32),
        ],
        compiler_params=_cparams(2),
        name="conv_mixer",
    )(x, g1, w_in, b_in, dw, dw_b, ln_g, ln_b, w_out, b_out, g2, wr, br)


def _qkv_kernel(x_ref, g1_ref, w_ref, qk_ref, v_ref):
    h = _rms(x_ref[...], g1_ref[...]).astype(jnp.bfloat16)
    qkv = jnp.dot(h, w_ref[...], preferred_element_type=jnp.float32)
    qk_ref[...] = qkv[:, :2 * D_MODEL]
    v_ref[...] = qkv[:, 2 * D_MODEL:].astype(jnp.bfloat16)


def _qkv(x, g1, w_qkv):
    return pl.pallas_call(
        _qkv_kernel,
        grid=(N_TOK // QKV_TS,),
        in_specs=[
            pl.BlockSpec((QKV_TS, D_MODEL), lambda i: (i, 0)),
            _full((1, D_MODEL)),
            _full((D_MODEL, 3 * D_MODEL)),
        ],
        out_specs=(
            pl.BlockSpec((QKV_TS, 2 * D_MODEL), lambda i: (i, 0)),
            pl.BlockSpec((QKV_TS, D_MODEL), lambda i: (i, 0)),
        ),
        out_shape=(
            jax.ShapeDtypeStruct((N_TOK, 2 * D_MODEL), jnp.float32),
            jax.ShapeDtypeStruct((N_TOK, D_MODEL), jnp.bfloat16),
        ),
        compiler_params=_cparams(1),
        name="qkv_proj",
    )(x, g1, w_qkv)


def _half_norm(z, gain):
    lane = lax.broadcasted_iota(jnp.int32, z.shape, 1)
    first = lane < HEAD_DIM
    zz = z * z
    ss_a = jnp.sum(jnp.where(first, zz, 0.0), axis=1, keepdims=True)
    ss_b = jnp.sum(jnp.where(first, 0.0, zz), axis=1, keepdims=True)
    inv = jnp.where(first, lax.rsqrt(ss_a * (1.0 / HEAD_DIM) + EPS),
                    lax.rsqrt(ss_b * (1.0 / HEAD_DIM) + EPS))
    return z * inv * gain


def _attn_kernel(q_ref, k_ref, v_ref, qg_ref, kg_ref, lam_ref, sg_ref, o_ref,
                 qn_ref, kn_ref, *, lambda_init):
    lp = lam_ref[...]
    lam = (jnp.exp(jnp.sum(lp[0:1, :] * lp[1:2, :], axis=1, keepdims=True))
           - jnp.exp(jnp.sum(lp[2:3, :] * lp[3:4, :], axis=1, keepdims=True))
           + lambda_init)

    kn_ref[...] = _half_norm(k_ref[...], kg_ref[...]).astype(jnp.bfloat16)
    qn_ref[...] = (_half_norm(q_ref[...], qg_ref[...]) * (HEAD_DIM ** -0.5)).astype(jnp.bfloat16)

    tq = ATT_TQ
    lane = lax.broadcasted_iota(jnp.int32, (tq, V_DIM), 1)
    first = lane < HEAD_DIM
    nt = (((1,), (1,)), ((), ()))
    for qi in range(SEQ // tq):
        kend = (qi + 1) * tq
        qn = qn_ref[qi * tq:(qi + 1) * tq, :]
        zero = jnp.zeros_like(qn)
        q_a = jnp.where(first, qn, zero)
        q_b = jnp.where(first, zero, qn)
        kn = kn_ref[0:kend, :]
        s_a = lax.dot_general(q_a, kn, nt, preferred_element_type=jnp.float32)
        s_b = lax.dot_general(q_b, kn, nt, preferred_element_type=jnp.float32)
        q_chunk = (lax.broadcasted_iota(jnp.int32, (tq, kend), 0) + qi * tq) // CHUNK
        k_chunk = lax.broadcasted_iota(jnp.int32, (tq, kend), 1) // CHUNK
        mask = k_chunk <= q_chunk

        def probs(sc):
            sc = jnp.where(mask, sc, -jnp.inf)
            e = jnp.exp(sc - jnp.max(sc, axis=1, keepdims=True))
            return e * (1.0 / jnp.sum(e, axis=1, keepdims=True))

        a = (probs(s_a) - lam * probs(s_b)).astype(jnp.bfloat16)
        o = jnp.dot(a, v_ref[0:kend, :], preferred_element_type=jnp.float32)
        o = _rms(o, sg_ref[...]) * (1.0 - lambda_init)
        o_ref[qi * tq:(qi + 1) * tq, :] = o.astype(jnp.bfloat16)


def _attention(qk, v, qg2, kg2, lam_rows, subln_g, lambda_init):
    return pl.pallas_call(
        functools.partial(_attn_kernel, lambda_init=lambda_init),
        grid=(BATCH, N_HEADS),
        in_specs=[
            pl.BlockSpec((SEQ, V_DIM), lambda b, h: (b, h)),
            pl.BlockSpec((SEQ, V_DIM), lambda b, h: (b, N_HEADS + h)),
            pl.BlockSpec((SEQ, V_DIM), lambda b, h: (b, h)),
            _full((1, V_DIM)), _full((1, V_DIM)),
            _full((SUBLANES, HEAD_DIM)),
            _full((1, V_DIM)),
        ],
        out_specs=pl.BlockSpec((SEQ, V_DIM), lambda b, h: (b, h)),
        out_shape=jax.ShapeDtypeStruct((N_TOK, D_MODEL), jnp.bfloat16),
        scratch_shapes=[
            pltpu.VMEM((SEQ, V_DIM), jnp.bfloat16),
            pltpu.VMEM((SEQ, V_DIM), jnp.bfloat16),
        ],
        compiler_params=_cparams(2),
        name="diff_attention",
    )(qk, qk, v, qg2, kg2, lam_rows, subln_g)


def _post_kernel(x_ref, m_ref, w_ref, g2_ref, wr_ref, br_ref,
                 x1_ref, h2s_ref, rf_ref, cnt_ref, run_ref):
    @pl.when(pl.program_id(0) == 0)
    def _():
        run_ref[...] = jnp.zeros_like(run_ref)

    mix = jnp.dot(m_ref[...], w_ref[...], preferred_element_type=jnp.float32)
    _residual_norm_route(x_ref[...] + mix, g2_ref, wr_ref, br_ref, run_ref,
                         x1_ref, h2s_ref, rf_ref, cnt_ref, POST_TS)


def _attn_post(x, o, w_o, g2, wr, br):
    return pl.pallas_call(
        _post_kernel,
        grid=(N_TOK // POST_TS,),
        in_specs=[
            pl.BlockSpec((POST_TS, D_MODEL), lambda i: (i, 0)),
            pl.BlockSpec((POST_TS, D_MODEL), lambda i: (i, 0)),
            _full((D_MODEL, D_MODEL)),
            _full((1, D_MODEL)),
            _full((D_MODEL, LANES)), _full((1, LANES)),
        ],
        out_specs=_route_out_specs(POST_TS, lambda i: i),
        out_shape=_route_out_shapes(),
        scratch_shapes=[pltpu.VMEM((1, LANES), jnp.float32)],
        compiler_params=_cparams(1),
        name="attn_post",
    )(x, o, w_o, g2, wr, br)


def _dispatch_kernel(dest_ref, h2s_ref, xb_in_ref, xb_ref, sem):
    del xb_in_ref
    i = pl.program_id(0)

    def copy(t, k):
        d = dest_ref[TOP_K * t + k]
        return pltpu.make_async_copy(
            h2s_ref.at[pl.ds(pl.multiple_of(t * SLAB, SLAB), SLAB)],
            xb_ref.at[pl.ds(pl.multiple_of(d * SLAB, SLAB), SLAB)],
            sem)

    def start(j, carry):
        t = i * DISP_CH + j
        copy(t, 0).start()
        copy(t, 1).start()
        return carry

    def wait(j, carry):
        t = i * DISP_CH + j
        copy(t, 0).wait()
        copy(t, 1).wait()
        return carry

    lax.fori_loop(0, DISP_CH, start, 0)
    lax.fori_loop(0, DISP_CH, wait, 0)


def _dispatch(dest_flat, h2s):
    xb0 = jnp.zeros((N_ROWS * SLAB, LANES), jnp.float32)
    return pl.pallas_call(
        _dispatch_kernel,
        grid_spec=pltpu.PrefetchScalarGridSpec(
            num_scalar_prefetch=1,
            grid=(N_TOK // DISP_CH,),
            in_specs=[pl.BlockSpec(memory_space=pl.ANY), pl.BlockSpec(memory_space=pl.ANY)],
            out_specs=pl.BlockSpec(memory_space=pl.ANY),
            scratch_shapes=[pltpu.SemaphoreType.DMA(())],
        ),
        out_shape=jax.ShapeDtypeStruct((N_ROWS * SLAB, LANES), jnp.float32),
        input_output_aliases={2: 0},
        compiler_params=_cparams(1),
        name="moe_dispatch",
    )(dest_flat, h2s, xb0)


def _expert_kernel(be_ref, nv_ref, xb_ref, w1_ref, w3_ref, w2_ref, yb_ref,
                   w1b_ref, w3b_ref, w2b_ref):
    b = pl.program_id(0)
    prev = be_ref[jnp.maximum(b - 1, 0)]
    fresh = (b == 0) | (be_ref[b] != prev)

    @pl.when(fresh)
    def _():
        w1b_ref[...] = w1_ref[...].astype(jnp.bfloat16)
        w3b_ref[...] = w3_ref[...].astype(jnp.bfloat16)
        w2b_ref[...] = w2_ref[...].astype(jnp.bfloat16)

    @pl.when(b < nv_ref[0])
    def _():
        x = _from_slab(xb_ref, ROW_BLOCK).astype(jnp.bfloat16)
        h1 = jnp.dot(x, w1b_ref[...], preferred_element_type=jnp.float32)
        h3 = jnp.dot(x, w3b_ref[...], preferred_element_type=jnp.float32)
        act = (h1 * jax.nn.sigmoid(h1) * h3).astype(jnp.bfloat16)
        y = jnp.dot(act, w2b_ref[...], preferred_element_type=jnp.float32)
        _to_slab(yb_ref, y, ROW_BLOCK)

    @pl.when(b >= nv_ref[0])
    def _():
        yb_ref[...] = jnp.zeros_like(yb_ref)


def _experts(block_e, n_valid, xb, w1, w3, w2):
    blk = lambda b, be, nv: jnp.minimum(b, nv[0] - 1)
    return pl.pallas_call(
        _expert_kernel,
        grid_spec=pltpu.PrefetchScalarGridSpec(
            num_scalar_prefetch=2,
            grid=(N_BLOCKS,),
            in_specs=[
                pl.BlockSpec((ROW_BLOCK * SLAB, LANES), lambda b, be, nv: (blk(b, be, nv), 0)),
                pl.BlockSpec((None, D_MODEL, D_EXPERT), lambda b, be, nv: (be[b], 0, 0)),
                pl.BlockSpec((None, D_MODEL, D_EXPERT), lambda b, be, nv: (be[b], 0, 0)),
                pl.BlockSpec((None, D_EXPERT, D_MODEL), lambda b, be, nv: (be[b], 0, 0)),
            ],
            out_specs=pl.BlockSpec((ROW_BLOCK * SLAB, LANES), lambda b, be, nv: (b, 0)),
            scratch_shapes=[
                pltpu.VMEM((D_MODEL, D_EXPERT), jnp.bfloat16),
                pltpu.VMEM((D_MODEL, D_EXPERT), jnp.bfloat16),
                pltpu.VMEM((D_EXPERT, D_MODEL), jnp.bfloat16),
            ],
        ),
        out_shape=jax.ShapeDtypeStruct((N_ROWS * SLAB, LANES), jnp.float32),
        compiler_params=_cparams(1),
        name="moe_experts",
    )(block_e, n_valid, xb, w1, w3, w2)


def _combine_kernel(pos_ref, x1_ref, rf_ref, yb_ref, out_ref, buf_ref, sem):
    i = pl.program_id(0)
    n = pl.num_programs(0)
    tm = COMB_TM

    def copy(tile, slot, j, k):
        p = pos_ref[TOP_K * (tile * tm + j) + k]
        return pltpu.make_async_copy(
            yb_ref.at[pl.ds(pl.multiple_of(p * SLAB, SLAB), SLAB)],
            buf_ref.at[slot, k, pl.ds(pl.multiple_of(j * SLAB, SLAB), SLAB)],
            sem.at[slot])

    def issue(tile, slot):
        def body(j, carry):
            copy(tile, slot, j, 0).start()
            copy(tile, slot, j, 1).start()
            return carry
        lax.fori_loop(0, tm, body, 0)

    def drain(tile, slot):
        def body(j, carry):
            copy(tile, slot, j, 0).wait()
            copy(tile, slot, j, 1).wait()
            return carry
        lax.fori_loop(0, tm, body, 0)

    slot = i % 2

    @pl.when(i == 0)
    def _():
        issue(0, 0)

    @pl.when(i + 1 < n)
    def _():
        issue(i + 1, 1 - slot)

    drain(i, slot)
    rf = rf_ref[...]
    y0 = _from_slab(buf_ref.at[slot, 0], tm)
    y1 = _from_slab(buf_ref.at[slot, 1], tm)
    out_ref[...] = x1_ref[...] + rf[:, 0:1] * y0 + rf[:, 1:2] * y1


def _combine(pos_flat, x1, rf, yb):
    return pl.pallas_call(
        _combine_kernel,
        grid_spec=pltpu.PrefetchScalarGridSpec(
            num_scalar_prefetch=1,
            grid=(N_TOK // COMB_TM,),
            in_specs=[
                pl.BlockSpec((COMB_TM, D_MODEL), lambda i, p: (i, 0)),
                pl.BlockSpec((COMB_TM, LANES), lambda i, p: (i, 0)),
                pl.BlockSpec(memory_space=pl.ANY),
            ],
            out_specs=pl.BlockSpec((COMB_TM, D_MODEL), lambda i, p: (i, 0)),
            scratch_shapes=[
                pltpu.VMEM((2, TOP_K, COMB_TM * SLAB, LANES), jnp.float32),
                pltpu.SemaphoreType.DMA((2,)),
            ],
        ),
        out_shape=jax.ShapeDtypeStruct((N_TOK, D_MODEL), jnp.float32),
        compiler_params=_cparams(1),
        name="moe_combine",
    )(pos_flat, x1, rf, yb)


def _moe(x1, h2s, rf, cnt, w1, w3, w2):
    experts = rf[:, 2:4].astype(jnp.int32)
    rank = rf[:, 4:6].astype(jnp.int32)
    counts = cnt[0, :N_EXPERTS].astype(jnp.int32)
    padded = (counts + ROW_BLOCK - 1) // ROW_BLOCK * ROW_BLOCK
    pad_ends = jnp.cumsum(padded)
    pad_starts = pad_ends - padded
    dest = (pad_starts[experts] + rank).reshape(-1).astype(jnp.int32)
    n_valid = (pad_ends[-1:] // ROW_BLOCK).astype(jnp.int32)
    block_row0 = jnp.arange(N_BLOCKS, dtype=jnp.int32) * ROW_BLOCK
    block_e = jnp.minimum(
        jnp.sum((pad_ends[None, :] <= block_row0[:, None]).astype(jnp.int32), axis=1),
        N_EXPERTS - 1).astype(jnp.int32)
    xb = _dispatch(dest, h2s)
    yb = _experts(block_e, n_valid, xb, w1, w3, w2)
    return _combine(dest, x1, rf, yb)


def _router_params(w_grp, b_grp, w_exp, b_exp):
    wr = jnp.zeros((D_MODEL, LANES), jnp.float32)
    wr = wr.at[:, :N_EXPERTS].set(w_exp).at[:, GRP_LANE0:GRP_LANE0 + N_GROUPS].set(w_grp)
    br = jnp.zeros((1, LANES), jnp.float32)
    br = br.at[0, :N_EXPERTS].set(b_exp).at[0, GRP_LANE0:GRP_LANE0 + N_GROUPS].set(b_grp)
    return wr, br


def kernel(x, norm1_g, norm2_g, conv_w_in, conv_b_in, conv_dw, conv_dw_b, conv_ln_g, conv_ln_b, conv_w_out, conv_b_out, attn_w_qkv, attn_q_g, attn_k_g, attn_lq1, attn_lk1, attn_lq2, attn_lk2, attn_subln_g, attn_w_o, moe_w_grp, moe_b_grp, moe_w_exp, moe_b_exp, moe_w1, moe_w3, moe_w2):
    bf16 = jnp.bfloat16
    row = lambda a: a.reshape(1, -1)
    xf = x.reshape(N_TOK, D_MODEL)

    wr, br = _router_params(moe_w_grp[0], moe_b_grp[0], moe_w_exp[0], moe_b_exp[0])
    dw = jnp.zeros((HALO, D_MODEL), jnp.float32).at[:CONV_WIDTH].set(conv_dw[0])
    x1, h2s, rf, cnt = _conv_layer(
        xf, row(norm1_g[0]), conv_w_in[0].astype(bf16), row(conv_b_in[0]), dw,
        row(conv_dw_b[0]), row(conv_ln_g[0]), row(conv_ln_b[0]),
        conv_w_out[0].astype(bf16), row(conv_b_out[0]), row(norm2_g[0]), wr, br)
    xf = _moe(x1, h2s, rf, cnt, moe_w1[0], moe_w3[0], moe_w2[0])

    lambda_init = 0.8 - 0.6 * math.exp(-0.3 * 1)
    wr, br = _router_params(moe_w_grp[1], moe_b_grp[1], moe_w_exp[1], moe_b_exp[1])
    qk, v = _qkv(xf, row(norm1_g[1]), attn_w_qkv[0].astype(bf16))
    lam_rows = jnp.zeros((SUBLANES, HEAD_DIM), jnp.float32)
    lam_rows = lam_rows.at[0].set(attn_lq1[0]).at[1].set(attn_lk1[0])
    lam_rows = lam_rows.at[2].set(attn_lq2[0]).at[3].set(attn_lk2[0])
    two = lambda g: jnp.concatenate([g, g]).reshape(1, V_DIM)
    o = _attention(qk, v, two(attn_q_g[0]), two(attn_k_g[0]), lam_rows,
                   row(attn_subln_g[0]), lambda_init)
    x1, h2s, rf, cnt = _attn_post(xf, o, attn_w_o[0].astype(bf16), row(norm2_g[1]), wr, br)
    xf = _moe(x1, h2s, rf, cnt, moe_w1[1], moe_w3[1], moe_w2[1])
    return xf.reshape(BATCH, SEQ, D_MODEL)
```

```python
import functools
import math

import jax
import jax.numpy as jnp
from jax import lax
from jax.experimental import pallas as pl
from jax.experimental.pallas import tpu as pltpu

D_MODEL = 1024
BATCH = 8
SEQ = 2048
N_TOK = BATCH * SEQ
CHUNK = 64
CONV_WIDTH = 31
N_HEADS = 8
HEAD_DIM = 64
V_DIM = 128
N_GROUPS = 4
EXPERTS_PER_GROUP = 8
N_EXPERTS = 32
TOP_K = 2
D_EXPERT = 512
EPS = 1e-6
LOG2E = math.log2(math.e)

LANES = 128
SUBLANES = 8
SLAB = D_MODEL // LANES
VMEM_LIMIT = 56 * 1024 * 1024

CONV_TS = 256
HALO = 32
CONV_RC = 64
POST_TS = 256
QKV_TS = 512
ATT_TQ = 256
ROW_BLOCK = 256
N_ASSIGN = N_TOK * TOP_K
N_BLOCKS = N_ASSIGN // ROW_BLOCK + N_EXPERTS
N_ROWS = N_BLOCKS * ROW_BLOCK
DISP_CH = 1024
COMB_TM = 256
DMA_UNROLL = 8
GRP_LANE0 = N_EXPERTS


def _cparams(n_axes):
    return pltpu.CompilerParams(
        dimension_semantics=("arbitrary",) * n_axes, vmem_limit_bytes=VMEM_LIMIT)


def _rms(x, g):
    return x * lax.rsqrt(jnp.mean(x * x, axis=-1, keepdims=True) + EPS) * g


def _to_slab(ref, val, rows):
    for j in range(SLAB):
        ref[pl.ds(j, rows, stride=SLAB), :] = val[:, j * LANES:(j + 1) * LANES]


def _from_slab(ref, rows):
    return jnp.concatenate(
        [ref[pl.ds(j, rows, stride=SLAB), :] for j in range(SLAB)], axis=-1)


def _residual_norm_route(x1, g2_ref, wr_ref, br_ref, run_ref,
                         x1_ref, h2s_ref, rf_ref, cnt_ref, rows):
    x1_ref[...] = x1
    h2 = _rms(x1, g2_ref[...])
    _to_slab(h2s_ref, h2, rows)

    hi = h2.astype(jnp.bfloat16)
    lo = (h2 - hi.astype(jnp.float32)).astype(jnp.bfloat16)
    ab = jnp.dot(hi, wr_ref[...], preferred_element_type=jnp.float32)
    c = jnp.dot(lo, wr_ref[:, 0:LANES], preferred_element_type=jnp.float32)
    logits = ab[:, 0:LANES] + ab[:, LANES:2 * LANES] + c + br_ref[...]
    lane = lax.broadcasted_iota(jnp.int32, (rows, LANES), 1)
    lane_f = lane.astype(jnp.float32)
    neg = jnp.float32(-jnp.inf)
    big = jnp.float32(1e9)

    gmask = (lane >= GRP_LANE0) & (lane < GRP_LANE0 + N_GROUPS)
    gl = jnp.where(gmask, logits, neg)
    gmax = jnp.max(gl, axis=1, keepdims=True)
    gidx = jnp.min(jnp.where(gl == gmax, lane_f, big), axis=1, keepdims=True) - GRP_LANE0
    gsum = jnp.sum(jnp.where(gmask, jnp.exp(gl - gmax), 0.0), axis=1, keepdims=True)
    grp_p = 1.0 / gsum

    lane0 = gidx * EXPERTS_PER_GROUP
    emask = (lane_f >= lane0) & (lane_f < lane0 + EXPERTS_PER_GROUP)
    el = jnp.where(emask, logits, neg)
    m1 = jnp.max(el, axis=1, keepdims=True)
    i1 = jnp.min(jnp.where(el == m1, lane_f, big), axis=1, keepdims=True)
    el2 = jnp.where(lane_f == i1, neg, el)
    m2 = jnp.max(el2, axis=1, keepdims=True)
    i2 = jnp.min(jnp.where(el2 == m2, lane_f, big), axis=1, keepdims=True)
    t = jnp.exp(m2 - m1)
    inv = 1.0 / (1.0 + t)
    g_first = grp_p * inv
    g_second = grp_p * t * inv

    sel1 = lane_f == i1
    sel2 = lane_f == i2
    member = jnp.where(sel1 | sel2, 1.0, 0.0).astype(jnp.bfloat16)
    r_i = lax.broadcasted_iota(jnp.int32, (rows, rows), 0)
    c_i = lax.broadcasted_iota(jnp.int32, (rows, rows), 1)
    tri = jnp.where(c_i < r_i, 1.0, 0.0).astype(jnp.bfloat16)
    cum = jnp.dot(tri, member, preferred_element_type=jnp.float32) + run_ref[...]
    rank1 = jnp.sum(jnp.where(sel1, cum, 0.0), axis=1, keepdims=True)
    rank2 = jnp.sum(jnp.where(sel2, cum, 0.0), axis=1, keepdims=True)
    run_new = run_ref[...] + jnp.sum(member.astype(jnp.float32), axis=0, keepdims=True)
    run_ref[...] = run_new
    cnt_ref[...] = jnp.broadcast_to(run_new, (SUBLANES, LANES))

    out = jnp.where(lane == 0, g_first, 0.0)
    out = jnp.where(lane == 1, g_second, out)
    out = jnp.where(lane == 2, i1, out)
    out = jnp.where(lane == 3, i2, out)
    out = jnp.where(lane == 4, rank1, out)
    out = jnp.where(lane == 5, rank2, out)
    rf_ref[...] = out


def _route_out_shapes():
    return (
        jax.ShapeDtypeStruct((N_TOK, D_MODEL), jnp.float32),
        jax.ShapeDtypeStruct((N_TOK * SLAB, LANES), jnp.float32),
        jax.ShapeDtypeStruct((N_TOK, LANES), jnp.float32),
        jax.ShapeDtypeStruct((SUBLANES, LANES), jnp.float32),
    )


def _route_out_specs(ts, idx):
    return (
        pl.BlockSpec((ts, D_MODEL), lambda *a: (idx(*a), 0)),
        pl.BlockSpec((ts * SLAB, LANES), lambda *a: (idx(*a), 0)),
        pl.BlockSpec((ts, LANES), lambda *a: (idx(*a), 0)),
        pl.BlockSpec((SUBLANES, LANES), lambda *a: (0, 0)),
    )


def _full(shape):
    return pl.BlockSpec(shape, lambda *a: (0,) * len(shape))


def _conv_kernel(x_ref, g1_ref, win_ref, bin_ref, dw_ref, dwb_ref, lng_ref, lnb_ref,
                 wout_ref, bout_ref, g2_ref, wr_ref, br_ref,
                 x1_ref, h2s_ref, rf_ref, cnt_ref,
                 ext_ref, conv_ref, run_ref):
    b = pl.program_id(0)
    s = pl.program_id(1)
    ts = CONV_TS

    @pl.when((b == 0) & (s == 0))
    def _():
        run_ref[...] = jnp.zeros_like(run_ref)

    @pl.when(s == 0)
    def _():
        ext_ref[0:HALO, :] = jnp.zeros((HALO, D_MODEL), jnp.float32)

    x = x_ref[...]
    h = _rms(x, g1_ref[...]).astype(jnp.bfloat16)
    u = jnp.dot(h, win_ref[...], preferred_element_type=jnp.float32) + bin_ref[...]
    glu = u[:, :D_MODEL] * jax.nn.sigmoid(u[:, D_MODEL:])
    ext_ref[HALO:HALO + ts, :] = glu

    base = HALO - (CONV_WIDTH - 1)

    def lane_chunk(c, carry):
        cols = pl.ds(pl.multiple_of(c * LANES, LANES), LANES)
        taps = [dw_ref[pl.ds(k, 1), cols] for k in range(CONV_WIDTH)]
        for r0 in range(0, ts, CONV_RC):
            acc = None
            for r in range(SUBLANES):
                rows = CONV_RC + (SUBLANES if r else 0)
                part = None
                for q in range((base + CONV_WIDTH - 1) // SUBLANES + 1):
                    k = SUBLANES * q + r - base
                    if 0 <= k < CONV_WIDTH:
                        term = ext_ref[pl.ds(r0 + SUBLANES * q, rows), cols] * taps[k]
                        part = term if part is None else part + term
                part = part[r:r + CONV_RC, :] if r else part
                acc = part if acc is None else acc + part
            conv_ref[pl.ds(r0, CONV_RC), cols] = acc
        return carry

    lax.fori_loop(0, D_MODEL // LANES, lane_chunk, 0)
    ext_ref[0:HALO, :] = ext_ref[ts:ts + HALO, :]

    v = conv_ref[...] + dwb_ref[...]
    mu = jnp.mean(v, axis=-1, keepdims=True)
    vc = v - mu
    var = jnp.mean(vc * vc, axis=-1, keepdims=True)
    y = vc * lax.rsqrt(var + EPS) * lng_ref[...] + lnb_ref[...]
    y = (y * jax.nn.sigmoid(y)).astype(jnp.bfloat16)
    mix = jnp.dot(y, wout_ref[...], preferred_element_type=jnp.float32) + bout_ref[...]
    _residual_norm_route(x + mix, g2_ref, wr_ref, br_ref, run_ref,
                         x1_ref, h2s_ref, rf_ref, cnt_ref, ts)


def _conv_layer(x, g1, w_in, b_in, dw, dw_b, ln_g, ln_b, w_out, b_out, g2, wr, br):
    ns = SEQ // CONV_TS
    tile = lambda b, s: b * ns + s
    return pl.pallas_call(
        _conv_kernel,
        grid=(BATCH, ns),
        in_specs=[
            pl.BlockSpec((CONV_TS, D_MODEL), lambda b, s: (tile(b, s), 0)),
            _full((1, D_MODEL)),
            _full((D_MODEL, 2 * D_MODEL)),
            _full((1, 2 * D_MODEL)),
            _full((HALO, D_MODEL)),
            _full((1, D_MODEL)), _full((1, D_MODEL)), _full((1, D_MODEL)),
            _full((D_MODEL, D_MODEL)),
            _full((1, D_MODEL)), _full((1, D_MODEL)),
            _full((D_MODEL, 2 * LANES)), _full((1, LANES)),
        ],
        out_specs=_route_out_specs(CONV_TS, tile),
        out_shape=_route_out_shapes(),
        scratch_shapes=[
            pltpu.VMEM((HALO + CONV_TS, D_MODEL), jnp.float32),
            pltpu.VMEM((CONV_TS, D_MODEL), jnp.float32),
            pltpu.VMEM((1, LANES), jnp.float32),
        ],
        compiler_params=_cparams(2),
        name="conv_mixer",
    )(x, g1, w_in, b_in, dw, dw_b, ln_g, ln_b, w_out, b_out, g2, wr, br)


def _qkv_kernel(x_ref, g1_ref, w_ref, qk_ref, v_ref):
    h = _rms(x_ref[...], g1_ref[...]).astype(jnp.bfloat16)
    qkv = jnp.dot(h, w_ref[...], preferred_element_type=jnp.float32)
    qk_ref[...] = qkv[:, :2 * D_MODEL]
    v_ref[...] = qkv[:, 2 * D_MODEL:].astype(jnp.bfloat16)


def _qkv(x, g1, w_qkv):
    return pl.pallas_call(
        _qkv_kernel,
        grid=(N_TOK // QKV_TS,),
        in_specs=[
            pl.BlockSpec((QKV_TS, D_MODEL), lambda i: (i, 0)),
            _full((1, D_MODEL)),
            _full((D_MODEL, 3 * D_MODEL)),
        ],
        out_specs=(
            pl.BlockSpec((QKV_TS, 2 * D_MODEL), lambda i: (i, 0)),
            pl.BlockSpec((QKV_TS, D_MODEL), lambda i: (i, 0)),
        ),
        out_shape=(
            jax.ShapeDtypeStruct((N_TOK, 2 * D_MODEL), jnp.float32),
            jax.ShapeDtypeStruct((N_TOK, D_MODEL), jnp.bfloat16),
        ),
        compiler_params=_cparams(1),
        name="qkv_proj",
    )(x, g1, w_qkv)


def _half_norm(z, gain):
    lane = lax.broadcasted_iota(jnp.int32, z.shape, 1)
    first = lane < HEAD_DIM
    zz = z * z
    ss_a = jnp.sum(jnp.where(first, zz, 0.0), axis=1, keepdims=True)
    ss_b = jnp.sum(jnp.where(first, 0.0, zz), axis=1, keepdims=True)
    inv = jnp.where(first, lax.rsqrt(ss_a * (1.0 / HEAD_DIM) + EPS),
                    lax.rsqrt(ss_b * (1.0 / HEAD_DIM) + EPS))
    return z * inv * gain


def _attn_kernel(q_ref, k_ref, v_ref, qg_ref, kg_ref, lam_ref, sg_ref, o_ref,
                 qa_ref, qb_ref, kn_ref, *, lambda_init):
    lp = lam_ref[...]
    lam = (jnp.exp(jnp.sum(lp[0:1, :] * lp[1:2, :], axis=1, keepdims=True))
           - jnp.exp(jnp.sum(lp[2:3, :] * lp[3:4, :], axis=1, keepdims=True))
           + lambda_init)

    kn_ref[...] = _half_norm(k_ref[...], kg_ref[...]).astype(jnp.bfloat16)
    qn = _half_norm(q_ref[...], qg_ref[...]) * (HEAD_DIM ** -0.5 * LOG2E)
    first = lax.broadcasted_iota(jnp.int32, (SEQ, V_DIM), 1) < HEAD_DIM
    qa_ref[...] = jnp.where(first, qn, 0.0).astype(jnp.bfloat16)
    qb_ref[...] = jnp.where(first, 0.0, qn).astype(jnp.bfloat16)

    tq = ATT_TQ
    nt = (((1,), (1,)), ((), ()))
    visible = (lax.broadcasted_iota(jnp.int32, (tq, tq), 1) // CHUNK
               <= lax.broadcasted_iota(jnp.int32, (tq, tq), 0) // CHUNK)
    for qi in range(SEQ // tq):
        k0 = qi * tq
        rows = slice(k0, k0 + tq)

        def exp_scores(qh_ref):
            q = qh_ref[rows, :]
            dg = lax.dot_general(q, kn_ref[rows, :], nt, preferred_element_type=jnp.float32)
            dg = jnp.where(visible, dg, -jnp.inf)
            m = jnp.max(dg, axis=1, keepdims=True)
            if qi == 0:
                e_dg = jnp.exp2(dg - m)
                return None, e_dg, jnp.sum(e_dg, axis=1, keepdims=True)
            off = lax.dot_general(q, kn_ref[0:k0, :], nt, preferred_element_type=jnp.float32)
            m = jnp.maximum(m, jnp.max(off, axis=1, keepdims=True))
            e_dg = jnp.exp2(dg - m)
            e_off = jnp.exp2(off - m)
            total = jnp.sum(e_dg, axis=1, keepdims=True) + jnp.sum(e_off, axis=1, keepdims=True)
            return e_off, e_dg, total

        ea_off, ea_dg, la = exp_scores(qa_ref)
        eb_off, eb_dg, lb = exp_scores(qb_ref)
        w = lam * la / lb
        a_dg = (ea_dg - w * eb_dg).astype(jnp.bfloat16)
        o = jnp.dot(a_dg, v_ref[rows, :], preferred_element_type=jnp.float32)
        if qi:
            a_off = (ea_off - w * eb_off).astype(jnp.bfloat16)
            o = o + jnp.dot(a_off, v_ref[0:k0, :], preferred_element_type=jnp.float32)
        o = o * (1.0 / la)
        o = _rms(o, sg_ref[...]) * (1.0 - lambda_init)
        o_ref[rows, :] = o.astype(jnp.bfloat16)


def _attention(qk, v, qg2, kg2, lam_rows, subln_g, lambda_init):
    return pl.pallas_call(
        functools.partial(_attn_kernel, lambda_init=lambda_init),
        grid=(BATCH, N_HEADS),
        in_specs=[
            pl.BlockSpec((SEQ, V_DIM), lambda b, h: (b, h)),
            pl.BlockSpec((SEQ, V_DIM), lambda b, h: (b, N_HEADS + h)),
            pl.BlockSpec((SEQ, V_DIM), lambda b, h: (b, h)),
            _full((1, V_DIM)), _full((1, V_DIM)),
            _full((SUBLANES, HEAD_DIM)),
            _full((1, V_DIM)),
        ],
        out_specs=pl.BlockSpec((SEQ, V_DIM), lambda b, h: (b, h)),
        out_shape=jax.ShapeDtypeStruct((N_TOK, D_MODEL), jnp.bfloat16),
        scratch_shapes=[pltpu.VMEM((SEQ, V_DIM), jnp.bfloat16)] * 3,
        compiler_params=_cparams(2),
        name="diff_attention",
    )(qk, qk, v, qg2, kg2, lam_rows, subln_g)


def _post_kernel(x_ref, m_ref, w_ref, g2_ref, wr_ref, br_ref,
                 x1_ref, h2s_ref, rf_ref, cnt_ref, run_ref):
    @pl.when(pl.program_id(0) == 0)
    def _():
        run_ref[...] = jnp.zeros_like(run_ref)

    mix = jnp.dot(m_ref[...], w_ref[...], preferred_element_type=jnp.float32)
    _residual_norm_route(x_ref[...] + mix, g2_ref, wr_ref, br_ref, run_ref,
                         x1_ref, h2s_ref, rf_ref, cnt_ref, POST_TS)


def _attn_post(x, o, w_o, g2, wr, br):
    return pl.pallas_call(
        _post_kernel,
        grid=(N_TOK // POST_TS,),
        in_specs=[
            pl.BlockSpec((POST_TS, D_MODEL), lambda i: (i, 0)),
            pl.BlockSpec((POST_TS, D_MODEL), lambda i: (i, 0)),
            _full((D_MODEL, D_MODEL)),
            _full((1, D_MODEL)),
            _full((D_MODEL, 2 * LANES)), _full((1, LANES)),
        ],
        out_specs=_route_out_specs(POST_TS, lambda i: i),
        out_shape=_route_out_shapes(),
        scratch_shapes=[pltpu.VMEM((1, LANES), jnp.float32)],
        compiler_params=_cparams(1),
        name="attn_post",
    )(x, o, w_o, g2, wr, br)


def _unrolled(n, body):
    def group(g, carry):
        for u in range(DMA_UNROLL):
            body(g * DMA_UNROLL + u)
        return carry
    lax.fori_loop(0, n // DMA_UNROLL, group, 0)


def _dispatch_kernel(dest_ref, h2s_ref, xb_in_ref, xb_ref, sem):
    del xb_in_ref
    tok0 = pl.program_id(0) * DISP_CH

    def copy(j, k):
        d = dest_ref[TOP_K * (tok0 + j) + k]
        return pltpu.make_async_copy(
            h2s_ref.at[pl.ds(pl.multiple_of(j * SLAB, SLAB), SLAB)],
            xb_ref.at[pl.ds(pl.multiple_of(d * SLAB, SLAB), SLAB)],
            sem)

    def start(j):
        copy(j, 0).start()
        copy(j, 1).start()

    def wait(j):
        copy(j, 0).wait()
        copy(j, 1).wait()

    _unrolled(DISP_CH, start)
    _unrolled(DISP_CH, wait)


def _dispatch(dest_flat, h2s):
    xb0 = jnp.zeros((N_ROWS * SLAB, LANES), jnp.float32)
    return pl.pallas_call(
        _dispatch_kernel,
        grid_spec=pltpu.PrefetchScalarGridSpec(
            num_scalar_prefetch=1,
            grid=(N_TOK // DISP_CH,),
            in_specs=[pl.BlockSpec((DISP_CH * SLAB, LANES), lambda i, d: (i, 0)),
                      pl.BlockSpec(memory_space=pl.ANY)],
            out_specs=pl.BlockSpec(memory_space=pl.ANY),
            scratch_shapes=[pltpu.SemaphoreType.DMA(())],
        ),
        out_shape=jax.ShapeDtypeStruct((N_ROWS * SLAB, LANES), jnp.float32),
        input_output_aliases={2: 0},
        compiler_params=_cparams(1),
        name="moe_dispatch",
    )(dest_flat, h2s, xb0)


def _expert_kernel(be_ref, nv_ref, xb_ref, w1_ref, w3_ref, w2_ref, yb_ref,
                   w1b_ref, w3b_ref, w2b_ref):
    b = pl.program_id(0)
    prev = be_ref[jnp.maximum(b - 1, 0)]
    fresh = (b == 0) | (be_ref[b] != prev)

    @pl.when(fresh)
    def _():
        w1b_ref[...] = w1_ref[...].astype(jnp.bfloat16)
        w3b_ref[...] = w3_ref[...].astype(jnp.bfloat16)
        w2b_ref[...] = w2_ref[...].astype(jnp.bfloat16)

    @pl.when(b < nv_ref[0])
    def _():
        x = _from_slab(xb_ref, ROW_BLOCK).astype(jnp.bfloat16)
        h1 = jnp.dot(x, w1b_ref[...], preferred_element_type=jnp.float32)
        h3 = jnp.dot(x, w3b_ref[...], preferred_element_type=jnp.float32)
        act = (h1 * jax.nn.sigmoid(h1) * h3).astype(jnp.bfloat16)
        y = jnp.dot(act, w2b_ref[...], preferred_element_type=jnp.float32)
        _to_slab(yb_ref, y, ROW_BLOCK)

    @pl.when(b >= nv_ref[0])
    def _():
        yb_ref[...] = jnp.zeros_like(yb_ref)


def _experts(layer, block_e, n_valid, xb, w1, w3, w2):
    blk = lambda b, be, nv: jnp.minimum(b, nv[0] - 1)
    w_in = pl.BlockSpec((None, None, D_MODEL, D_EXPERT), lambda b, be, nv: (layer, be[b], 0, 0))
    w_out = pl.BlockSpec((None, None, D_EXPERT, D_MODEL), lambda b, be, nv: (layer, be[b], 0, 0))
    return pl.pallas_call(
        _expert_kernel,
        grid_spec=pltpu.PrefetchScalarGridSpec(
            num_scalar_prefetch=2,
            grid=(N_BLOCKS,),
            in_specs=[
                pl.BlockSpec((ROW_BLOCK * SLAB, LANES), lambda b, be, nv: (blk(b, be, nv), 0)),
                w_in, w_in, w_out,
            ],
            out_specs=pl.BlockSpec((ROW_BLOCK * SLAB, LANES), lambda b, be, nv: (b, 0)),
            scratch_shapes=[
                pltpu.VMEM((D_MODEL, D_EXPERT), jnp.bfloat16),
                pltpu.VMEM((D_MODEL, D_EXPERT), jnp.bfloat16),
                pltpu.VMEM((D_EXPERT, D_MODEL), jnp.bfloat16),
            ],
        ),
        out_shape=jax.ShapeDtypeStruct((N_ROWS * SLAB, LANES), jnp.float32),
        compiler_params=_cparams(1),
        name="moe_experts",
    )(block_e, n_valid, xb, w1, w3, w2)


def _combine_kernel(pos_ref, x1_ref, rf_ref, yb_ref, out_ref, buf_ref, sem):
    i = pl.program_id(0)
    n = pl.num_programs(0)
    tm = COMB_TM

    def copy(tile, slot, j, k):
        p = pos_ref[TOP_K * (tile * tm + j) + k]
        return pltpu.make_async_copy(
            yb_ref.at[pl.ds(pl.multiple_of(p * SLAB, SLAB), SLAB)],
            buf_ref.at[slot, k, pl.ds(pl.multiple_of(j * SLAB, SLAB), SLAB)],
            sem.at[slot])

    def issue(tile, slot):
        def body(j):
            copy(tile, slot, j, 0).start()
            copy(tile, slot, j, 1).start()
        _unrolled(tm, body)

    def drain(tile, slot):
        def body(j):
            copy(tile, slot, j, 0).wait()
            copy(tile, slot, j, 1).wait()
        _unrolled(tm, body)

    slot = i % 2

    @pl.when(i == 0)
    def _():
        issue(0, 0)

    @pl.when(i + 1 < n)
    def _():
        issue(i + 1, 1 - slot)

    drain(i, slot)
    rf = rf_ref[...]
    y0 = _from_slab(buf_ref.at[slot, 0], tm)
    y1 = _from_slab(buf_ref.at[slot, 1], tm)
    out_ref[...] = x1_ref[...] + rf[:, 0:1] * y0 + rf[:, 1:2] * y1


def _combine(pos_flat, x1, rf, yb):
    return pl.pallas_call(
        _combine_kernel,
        grid_spec=pltpu.PrefetchScalarGridSpec(
            num_scalar_prefetch=1,
            grid=(N_TOK // COMB_TM,),
            in_specs=[
                pl.BlockSpec((COMB_TM, D_MODEL), lambda i, p: (i, 0)),
                pl.BlockSpec((COMB_TM, LANES), lambda i, p: (i, 0)),
                pl.BlockSpec(memory_space=pl.ANY),
            ],
            out_specs=pl.BlockSpec((COMB_TM, D_MODEL), lambda i, p: (i, 0)),
            scratch_shapes=[
                pltpu.VMEM((2, TOP_K, COMB_TM * SLAB, LANES), jnp.float32),
                pltpu.SemaphoreType.DMA((2,)),
            ],
        ),
        out_shape=jax.ShapeDtypeStruct((N_TOK, D_MODEL), jnp.float32),
        compiler_params=_cparams(1),
        name="moe_combine",
    )(pos_flat, x1, rf, yb)


def _moe(layer, x1, h2s, rf, cnt, w1, w3, w2):
    experts = rf[:, 2:4].astype(jnp.int32)
    rank = rf[:, 4:6].astype(jnp.int32)
    counts = cnt[0, :N_EXPERTS].astype(jnp.int32)
    padded = (counts + ROW_BLOCK - 1) // ROW_BLOCK * ROW_BLOCK
    pad_ends = jnp.cumsum(padded)
    pad_starts = pad_ends - padded
    dest = (pad_starts[experts] + rank).reshape(-1).astype(jnp.int32)
    n_valid = (pad_ends[-1:] // ROW_BLOCK).astype(jnp.int32)
    block_row0 = jnp.arange(N_BLOCKS, dtype=jnp.int32) * ROW_BLOCK
    block_e = jnp.minimum(
        jnp.sum((pad_ends[None, :] <= block_row0[:, None]).astype(jnp.int32), axis=1),
        N_EXPERTS - 1).astype(jnp.int32)
    xb = _dispatch(dest, h2s)
    yb = _experts(layer, block_e, n_valid, xb, w1, w3, w2)
    return _combine(dest, x1, rf, yb)


def _router_params(w_grp, b_grp, w_exp, b_exp):
    wr = jnp.zeros((D_MODEL, LANES), jnp.float32)
    wr = wr.at[:, :N_EXPERTS].set(w_exp).at[:, GRP_LANE0:GRP_LANE0 + N_GROUPS].set(w_grp)
    br = jnp.zeros((1, LANES), jnp.float32)
    br = br.at[0, :N_EXPERTS].set(b_exp).at[0, GRP_LANE0:GRP_LANE0 + N_GROUPS].set(b_grp)
    w_hi = wr.astype(jnp.bfloat16)
    w_lo = (wr - w_hi.astype(jnp.float32)).astype(jnp.bfloat16)
    return jnp.concatenate([w_hi, w_lo], axis=1), br


def kernel(x, norm1_g, norm2_g, conv_w_in, conv_b_in, conv_dw, conv_dw_b, conv_ln_g, conv_ln_b, conv_w_out, conv_b_out, attn_w_qkv, attn_q_g, attn_k_g, attn_lq1, attn_lk1, attn_lq2, attn_lk2, attn_subln_g, attn_w_o, moe_w_grp, moe_b_grp, moe_w_exp, moe_b_exp, moe_w1, moe_w3, moe_w2):
    bf16 = jnp.bfloat16
    row = lambda a: a.reshape(1, -1)
    xf = x.reshape(N_TOK, D_MODEL)

    wr, br = _router_params(moe_w_grp[0], moe_b_grp[0], moe_w_exp[0], moe_b_exp[0])
    dw = jnp.zeros((HALO, D_MODEL), jnp.float32).at[:CONV_WIDTH].set(conv_dw[0])
    x1, h2s, rf, cnt = _conv_layer(
        xf, row(norm1_g[0]), conv_w_in[0].astype(bf16), row(conv_b_in[0]), dw,
        row(conv_dw_b[0]), row(conv_ln_g[0]), row(conv_ln_b[0]),
        conv_w_out[0].astype(bf16), row(conv_b_out[0]), row(norm2_g[0]), wr, br)
    xf = _moe(0, x1, h2s, rf, cnt, moe_w1, moe_w3, moe_w2)

    lambda_init = 0.8 - 0.6 * math.exp(-0.3 * 1)
    wr, br = _router_params(moe_w_grp[1], moe_b_grp[1], moe_w_exp[1], moe_b_exp[1])
    qk, v = _qkv(xf, row(norm1_g[1]), attn_w_qkv[0].astype(bf16))
    lam_rows = jnp.zeros((SUBLANES, HEAD_DIM), jnp.float32)
    lam_rows = lam_rows.at[0].set(attn_lq1[0]).at[1].set(attn_lk1[0])
    lam_rows = lam_rows.at[2].set(attn_lq2[0]).at[3].set(attn_lk2[0])
    two = lambda g: jnp.concatenate([g, g]).reshape(1, V_DIM)
    o = _attention(qk, v, two(attn_q_g[0]), two(attn_k_g[0]), lam_rows,
                   row(attn_subln_g[0]), lambda_init)
    x1, h2s, rf, cnt = _attn_post(xf, o, attn_w_o[0].astype(bf16), row(norm2_g[1]), wr, br)
    xf = _moe(1, x1, h2s, rf, cnt, moe_w1, moe_w3, moe_w2)
    return xf.reshape(BATCH, SEQ, D_MODEL)
```

```python
import functools
import math

import jax
import jax.numpy as jnp
from jax import lax
from jax.experimental import pallas as pl
from jax.experimental.pallas import tpu as pltpu

D_MODEL = 1024
BATCH = 8
SEQ = 2048
N_TOK = BATCH * SEQ
CHUNK = 64
CONV_WIDTH = 31
N_HEADS = 8
HEAD_DIM = 64
V_DIM = 128
N_GROUPS = 4
EXPERTS_PER_GROUP = 8
N_EXPERTS = 32
TOP_K = 2
D_EXPERT = 512
EPS = 1e-6
LOG2E = math.log2(math.e)

LANES = 128
SUBLANES = 8
PACK = D_MODEL // LANES
VMEM_LIMIT = 56 * 1024 * 1024

CONV_TS = 512
HALO = 32
CONV_RC = 64
POST_TS = 512
QKV_TS = 512
ATT_TQ = 256
ROW_BLOCK = 256
N_ASSIGN = N_TOK * TOP_K
N_BLOCKS = N_ASSIGN // ROW_BLOCK + N_EXPERTS
N_ROWS = N_BLOCKS * ROW_BLOCK
COMB_TM = 256
DMA_UNROLL = 8
GRP_LANE0 = N_EXPERTS


def _cparams(n_axes):
    return pltpu.CompilerParams(
        dimension_semantics=("arbitrary",) * n_axes, vmem_limit_bytes=VMEM_LIMIT)


def _rms(x, g):
    return x * lax.rsqrt(jnp.mean(x * x, axis=-1, keepdims=True) + EPS) * g


def _pack_rows(ref, val, rows):
    for j in range(PACK):
        ref[pl.ds(j, rows, stride=PACK), :] = val[:, j * LANES:(j + 1) * LANES]


def _unpack_rows(ref, rows):
    return jnp.concatenate(
        [ref[pl.ds(j, rows, stride=PACK), :] for j in range(PACK)], axis=-1)


def _residual_norm_route(x1, g2_ref, wr_ref, br_ref, run_ref,
                         x1_ref, h2p_ref, rf_ref, cnt_ref, rows):
    x1_ref[...] = x1
    h2 = _rms(x1, g2_ref[...])
    _pack_rows(h2p_ref, h2, rows)

    hi = h2.astype(jnp.bfloat16)
    lo = (h2 - hi.astype(jnp.float32)).astype(jnp.bfloat16)
    ab = jnp.dot(hi, wr_ref[...], preferred_element_type=jnp.float32)
    c = jnp.dot(lo, wr_ref[:, 0:LANES], preferred_element_type=jnp.float32)
    logits = ab[:, 0:LANES] + ab[:, LANES:2 * LANES] + c + br_ref[...]
    lane = lax.broadcasted_iota(jnp.int32, (rows, LANES), 1)
    lane_f = lane.astype(jnp.float32)
    neg = jnp.float32(-jnp.inf)
    big = jnp.float32(1e9)

    gmask = (lane >= GRP_LANE0) & (lane < GRP_LANE0 + N_GROUPS)
    gl = jnp.where(gmask, logits, neg)
    gmax = jnp.max(gl, axis=1, keepdims=True)
    gidx = jnp.min(jnp.where(gl == gmax, lane_f, big), axis=1, keepdims=True) - GRP_LANE0
    gsum = jnp.sum(jnp.where(gmask, jnp.exp(gl - gmax), 0.0), axis=1, keepdims=True)
    grp_p = 1.0 / gsum

    lane0 = gidx * EXPERTS_PER_GROUP
    emask = (lane_f >= lane0) & (lane_f < lane0 + EXPERTS_PER_GROUP)
    el = jnp.where(emask, logits, neg)
    m1 = jnp.max(el, axis=1, keepdims=True)
    i1 = jnp.min(jnp.where(el == m1, lane_f, big), axis=1, keepdims=True)
    el2 = jnp.where(lane_f == i1, neg, el)
    m2 = jnp.max(el2, axis=1, keepdims=True)
    i2 = jnp.min(jnp.where(el2 == m2, lane_f, big), axis=1, keepdims=True)
    t = jnp.exp(m2 - m1)
    inv = 1.0 / (1.0 + t)
    g_first = grp_p * inv
    g_second = grp_p * t * inv

    sel1 = lane_f == i1
    sel2 = lane_f == i2
    member = jnp.where(sel1 | sel2, 1.0, 0.0).astype(jnp.bfloat16)
    r_i = lax.broadcasted_iota(jnp.int32, (rows, rows), 0)
    c_i = lax.broadcasted_iota(jnp.int32, (rows, rows), 1)
    tri = jnp.where(c_i < r_i, 1.0, 0.0).astype(jnp.bfloat16)
    cum = jnp.dot(tri, member, preferred_element_type=jnp.float32) + run_ref[...]
    rank1 = jnp.sum(jnp.where(sel1, cum, 0.0), axis=1, keepdims=True)
    rank2 = jnp.sum(jnp.where(sel2, cum, 0.0), axis=1, keepdims=True)
    run_new = run_ref[...] + jnp.sum(member.astype(jnp.float32), axis=0, keepdims=True)
    run_ref[...] = run_new
    cnt_ref[...] = jnp.broadcast_to(run_new, (SUBLANES, LANES))

    out = jnp.where(lane == 0, g_first, 0.0)
    out = jnp.where(lane == 1, g_second, out)
    out = jnp.where(lane == 2, i1, out)
    out = jnp.where(lane == 3, i2, out)
    out = jnp.where(lane == 4, rank1, out)
    out = jnp.where(lane == 5, rank2, out)
    rf_ref[...] = out


def _route_out_shapes():
    return (
        jax.ShapeDtypeStruct((N_TOK, D_MODEL), jnp.float32),
        jax.ShapeDtypeStruct((N_TOK * PACK, LANES), jnp.float32),
        jax.ShapeDtypeStruct((N_TOK, LANES), jnp.float32),
        jax.ShapeDtypeStruct((SUBLANES, LANES), jnp.float32),
    )


def _route_out_specs(ts, idx):
    return (
        pl.BlockSpec((ts, D_MODEL), lambda *a: (idx(*a), 0)),
        pl.BlockSpec((ts * PACK, LANES), lambda *a: (idx(*a), 0)),
        pl.BlockSpec((ts, LANES), lambda *a: (idx(*a), 0)),
        pl.BlockSpec((SUBLANES, LANES), lambda *a: (0, 0)),
    )


def _full(shape):
    return pl.BlockSpec(shape, lambda *a: (0,) * len(shape))


def _conv_kernel(x_ref, g1_ref, win_ref, bin_ref, dw_ref, dwb_ref, lng_ref, lnb_ref,
                 wout_ref, bout_ref, g2_ref, wr_ref, br_ref,
                 x1_ref, h2p_ref, rf_ref, cnt_ref,
                 ext_ref, conv_ref, run_ref):
    b = pl.program_id(0)
    s = pl.program_id(1)
    ts = CONV_TS

    @pl.when((b == 0) & (s == 0))
    def _():
        run_ref[...] = jnp.zeros_like(run_ref)

    @pl.when(s == 0)
    def _():
        ext_ref[0:HALO, :] = jnp.zeros((HALO, D_MODEL), jnp.float32)

    x = x_ref[...]
    h = _rms(x, g1_ref[...]).astype(jnp.bfloat16)
    u = jnp.dot(h, win_ref[...], preferred_element_type=jnp.float32) + bin_ref[...]
    glu = u[:, :D_MODEL] * jax.nn.sigmoid(u[:, D_MODEL:])
    ext_ref[HALO:HALO + ts, :] = glu

    base = HALO - (CONV_WIDTH - 1)

    def lane_chunk(c, carry):
        cols = pl.ds(pl.multiple_of(c * LANES, LANES), LANES)
        taps = [dw_ref[pl.ds(k, 1), cols] for k in range(CONV_WIDTH)]
        for r0 in range(0, ts, CONV_RC):
            acc = None
            for r in range(SUBLANES):
                rows = CONV_RC + (SUBLANES if r else 0)
                part = None
                for q in range((base + CONV_WIDTH - 1) // SUBLANES + 1):
                    k = SUBLANES * q + r - base
                    if 0 <= k < CONV_WIDTH:
                        term = ext_ref[pl.ds(r0 + SUBLANES * q, rows), cols] * taps[k]
                        part = term if part is None else part + term
                part = part[r:r + CONV_RC, :] if r else part
                acc = part if acc is None else acc + part
            conv_ref[pl.ds(r0, CONV_RC), cols] = acc
        return carry

    lax.fori_loop(0, D_MODEL // LANES, lane_chunk, 0)
    ext_ref[0:HALO, :] = ext_ref[ts:ts + HALO, :]

    v = conv_ref[...] + dwb_ref[...]
    mu = jnp.mean(v, axis=-1, keepdims=True)
    vc = v - mu
    var = jnp.mean(vc * vc, axis=-1, keepdims=True)
    y = vc * lax.rsqrt(var + EPS) * lng_ref[...] + lnb_ref[...]
    y = (y * jax.nn.sigmoid(y)).astype(jnp.bfloat16)
    mix = jnp.dot(y, wout_ref[...], preferred_element_type=jnp.float32) + bout_ref[...]
    _residual_norm_route(x + mix, g2_ref, wr_ref, br_ref, run_ref,
                         x1_ref, h2p_ref, rf_ref, cnt_ref, ts)


def _conv_layer(x, g1, w_in, b_in, dw, dw_b, ln_g, ln_b, w_out, b_out, g2, wr, br):
    ns = SEQ // CONV_TS
    tile = lambda b, s: b * ns + s
    return pl.pallas_call(
        _conv_kernel,
        grid=(BATCH, ns),
        in_specs=[
            pl.BlockSpec((CONV_TS, D_MODEL), lambda b, s: (tile(b, s), 0)),
            _full((1, D_MODEL)),
            _full((D_MODEL, 2 * D_MODEL)),
            _full((1, 2 * D_MODEL)),
            _full((HALO, D_MODEL)),
            _full((1, D_MODEL)), _full((1, D_MODEL)), _full((1, D_MODEL)),
            _full((D_MODEL, D_MODEL)),
            _full((1, D_MODEL)), _full((1, D_MODEL)),
            _full((D_MODEL, 2 * LANES)), _full((1, LANES)),
        ],
        out_specs=_route_out_specs(CONV_TS, tile),
        out_shape=_route_out_shapes(),
        scratch_shapes=[
            pltpu.VMEM((HALO + CONV_TS, D_MODEL), jnp.float32),
            pltpu.VMEM((CONV_TS, D_MODEL), jnp.float32),
            pltpu.VMEM((1, LANES), jnp.float32),
        ],
        compiler_params=_cparams(2),
        name="conv_mixer",
    )(x, g1, w_in, b_in, dw, dw_b, ln_g, ln_b, w_out, b_out, g2, wr, br)


def _qkv_kernel(x_ref, g1_ref, w_ref, qk_ref, v_ref):
    h = _rms(x_ref[...], g1_ref[...]).astype(jnp.bfloat16)
    qkv = jnp.dot(h, w_ref[...], preferred_element_type=jnp.float32)
    qk_ref[...] = qkv[:, :2 * D_MODEL]
    v_ref[...] = qkv[:, 2 * D_MODEL:].astype(jnp.bfloat16)


def _qkv(x, g1, w_qkv):
    return pl.pallas_call(
        _qkv_kernel,
        grid=(N_TOK // QKV_TS,),
        in_specs=[
            pl.BlockSpec((QKV_TS, D_MODEL), lambda i: (i, 0)),
            _full((1, D_MODEL)),
            _full((D_MODEL, 3 * D_MODEL)),
        ],
        out_specs=(
            pl.BlockSpec((QKV_TS, 2 * D_MODEL), lambda i: (i, 0)),
            pl.BlockSpec((QKV_TS, D_MODEL), lambda i: (i, 0)),
        ),
        out_shape=(
            jax.ShapeDtypeStruct((N_TOK, 2 * D_MODEL), jnp.float32),
            jax.ShapeDtypeStruct((N_TOK, D_MODEL), jnp.bfloat16),
        ),
        compiler_params=_cparams(1),
        name="qkv_proj",
    )(x, g1, w_qkv)


def _half_norm(z, gain):
    lane = lax.broadcasted_iota(jnp.int32, z.shape, 1)
    first = lane < HEAD_DIM
    zz = z * z
    ss_a = jnp.sum(jnp.where(first, zz, 0.0), axis=1, keepdims=True)
    ss_b = jnp.sum(jnp.where(first, 0.0, zz), axis=1, keepdims=True)
    inv = jnp.where(first, lax.rsqrt(ss_a * (1.0 / HEAD_DIM) + EPS),
                    lax.rsqrt(ss_b * (1.0 / HEAD_DIM) + EPS))
    return z * inv * gain


def _attn_kernel(q_ref, k_ref, v_ref, qg_ref, kg_ref, lam_ref, sg_ref, o_ref,
                 qa_ref, qb_ref, kn_ref, v1_ref, *, lambda_init):
    lp = lam_ref[...]
    lam = (jnp.exp(jnp.sum(lp[0:1, :] * lp[1:2, :], axis=1, keepdims=True))
           - jnp.exp(jnp.sum(lp[2:3, :] * lp[3:4, :], axis=1, keepdims=True))
           + lambda_init)

    kn_ref[...] = _half_norm(k_ref[...], kg_ref[...]).astype(jnp.bfloat16)
    qn = _half_norm(q_ref[...], qg_ref[...]) * (HEAD_DIM ** -0.5 * LOG2E)
    first = lax.broadcasted_iota(jnp.int32, (SEQ, V_DIM), 1) < HEAD_DIM
    qa_ref[...] = jnp.where(first, qn, 0.0).astype(jnp.bfloat16)
    qb_ref[...] = jnp.where(first, 0.0, qn).astype(jnp.bfloat16)

    ones_col = lax.broadcasted_iota(jnp.int32, (SEQ, V_DIM), 1) == 0
    v1_ref[:, 0:V_DIM] = v_ref[...]
    v1_ref[:, V_DIM:2 * V_DIM] = jnp.where(ones_col, 1.0, 0.0).astype(jnp.bfloat16)

    tq = ATT_TQ
    nt = (((1,), (1,)), ((), ()))
    visible = (lax.broadcasted_iota(jnp.int32, (tq, tq), 1) // CHUNK
               <= lax.broadcasted_iota(jnp.int32, (tq, tq), 0) // CHUNK)
    for qi in range(SEQ // tq):
        k0 = qi * tq
        rows = slice(k0, k0 + tq)

        def half_attention(qh_ref):
            q = qh_ref[rows, :]
            dg = lax.dot_general(q, kn_ref[rows, :], nt, preferred_element_type=jnp.float32)
            dg = jnp.where(visible, dg, -jnp.inf)
            m = jnp.max(dg, axis=1, keepdims=True)
            if qi:
                off = lax.dot_general(q, kn_ref[0:k0, :], nt,
                                      preferred_element_type=jnp.float32)
                m = jnp.maximum(m, jnp.max(off, axis=1, keepdims=True))
            acc = jnp.dot(jnp.exp2(dg - m).astype(jnp.bfloat16), v1_ref[rows, :],
                          preferred_element_type=jnp.float32)
            if qi:
                acc = acc + jnp.dot(jnp.exp2(off - m).astype(jnp.bfloat16), v1_ref[0:k0, :],
                                    preferred_element_type=jnp.float32)
            return acc[:, 0:V_DIM] * (1.0 / acc[:, V_DIM:V_DIM + 1])

        o = half_attention(qa_ref) - lam * half_attention(qb_ref)
        o = _rms(o, sg_ref[...]) * (1.0 - lambda_init)
        o_ref[rows, :] = o.astype(jnp.bfloat16)


def _attention(qk, v, qg2, kg2, lam_rows, subln_g, lambda_init):
    return pl.pallas_call(
        functools.partial(_attn_kernel, lambda_init=lambda_init),
        grid=(BATCH, N_HEADS),
        in_specs=[
            pl.BlockSpec((SEQ, V_DIM), lambda b, h: (b, h)),
            pl.BlockSpec((SEQ, V_DIM), lambda b, h: (b, N_HEADS + h)),
            pl.BlockSpec((SEQ, V_DIM), lambda b, h: (b, h)),
            _full((1, V_DIM)), _full((1, V_DIM)),
            _full((SUBLANES, HEAD_DIM)),
            _full((1, V_DIM)),
        ],
        out_specs=pl.BlockSpec((SEQ, V_DIM), lambda b, h: (b, h)),
        out_shape=jax.ShapeDtypeStruct((N_TOK, D_MODEL), jnp.bfloat16),
        scratch_shapes=[pltpu.VMEM((SEQ, V_DIM), jnp.bfloat16)] * 3
        + [pltpu.VMEM((SEQ, 2 * V_DIM), jnp.bfloat16)],
        compiler_params=_cparams(2),
        name="diff_attention",
    )(qk, qk, v, qg2, kg2, lam_rows, subln_g)


def _post_kernel(x_ref, m_ref, w_ref, g2_ref, wr_ref, br_ref,
                 x1_ref, h2p_ref, rf_ref, cnt_ref, run_ref):
    @pl.when(pl.program_id(0) == 0)
    def _():
        run_ref[...] = jnp.zeros_like(run_ref)

    mix = jnp.dot(m_ref[...], w_ref[...], preferred_element_type=jnp.float32)
    _residual_norm_route(x_ref[...] + mix, g2_ref, wr_ref, br_ref, run_ref,
                         x1_ref, h2p_ref, rf_ref, cnt_ref, POST_TS)


def _attn_post(x, o, w_o, g2, wr, br):
    return pl.pallas_call(
        _post_kernel,
        grid=(N_TOK // POST_TS,),
        in_specs=[
            pl.BlockSpec((POST_TS, D_MODEL), lambda i: (i, 0)),
            pl.BlockSpec((POST_TS, D_MODEL), lambda i: (i, 0)),
            _full((D_MODEL, D_MODEL)),
            _full((1, D_MODEL)),
            _full((D_MODEL, 2 * LANES)), _full((1, LANES)),
        ],
        out_specs=_route_out_specs(POST_TS, lambda i: i),
        out_shape=_route_out_shapes(),
        scratch_shapes=[pltpu.VMEM((1, LANES), jnp.float32)],
        compiler_params=_cparams(1),
        name="attn_post",
    )(x, o, w_o, g2, wr, br)


def _unrolled(n, body):
    def group(g, carry):
        for u in range(DMA_UNROLL):
            body(g * DMA_UNROLL + u)
        return carry
    lax.fori_loop(0, n // DMA_UNROLL, group, 0)


def _expert_kernel(be_ref, nv_ref, dest_ref, pad_ref, h2p_ref, w1_ref, w3_ref, w2_ref, yb_ref,
                   rowtok_ref, xbuf_ref, sem, w1b_ref, w3b_ref, w2b_ref):
    b = pl.program_id(0)
    n_valid = nv_ref[0]

    def row_copy(block, slot, r):
        tok = rowtok_ref[block * ROW_BLOCK + r]
        return pltpu.make_async_copy(
            h2p_ref.at[pl.ds(pl.multiple_of(tok * PACK, PACK), PACK)],
            xbuf_ref.at[slot, pl.ds(r * PACK, PACK)],
            sem.at[slot])

    def block_arrival(slot):
        return pltpu.make_async_copy(
            h2p_ref.at[pl.ds(0, ROW_BLOCK * PACK)], xbuf_ref.at[slot], sem.at[slot])

    @pl.when(b == 0)
    def _():
        def clear_padding(e, carry):
            def clear(j, c):
                rowtok_ref[j] = 0
                return c
            return lax.fori_loop(pad_ref[e], pad_ref[N_EXPERTS + e], clear, carry)
        lax.fori_loop(0, N_EXPERTS, clear_padding, 0)

        def place(t):
            rowtok_ref[dest_ref[TOP_K * t]] = t
            rowtok_ref[dest_ref[TOP_K * t + 1]] = t
        _unrolled(N_TOK, place)
        _unrolled(ROW_BLOCK, lambda r: row_copy(0, 0, r).start())

    prev = be_ref[jnp.maximum(b - 1, 0)]
    fresh = (b == 0) | (be_ref[b] != prev)

    @pl.when(fresh)
    def _():
        w1b_ref[...] = w1_ref[...].astype(jnp.bfloat16)
        w3b_ref[...] = w3_ref[...].astype(jnp.bfloat16)
        w2b_ref[...] = w2_ref[...].astype(jnp.bfloat16)

    @pl.when(b < n_valid)
    def _():
        slot = b % 2
        block_arrival(slot).wait()
        x = _unpack_rows(xbuf_ref.at[slot], ROW_BLOCK).astype(jnp.bfloat16)
        nxt = jnp.minimum(b + 1, n_valid - 1)
        for r in range(ROW_BLOCK):
            row_copy(nxt, 1 - slot, r).start()
        h1 = jnp.dot(x, w1b_ref[...], preferred_element_type=jnp.float32)
        h3 = jnp.dot(x, w3b_ref[...], preferred_element_type=jnp.float32)
        act = (h1 * jax.nn.sigmoid(h1) * h3).astype(jnp.bfloat16)
        y = jnp.dot(act, w2b_ref[...], preferred_element_type=jnp.float32)
        _pack_rows(yb_ref, y, ROW_BLOCK)

    @pl.when(b == n_valid - 1)
    def _():
        block_arrival(1 - b % 2).wait()

    @pl.when(b >= n_valid)
    def _():
        yb_ref[...] = jnp.zeros_like(yb_ref)


def _experts(layer, block_e, n_valid, dest_flat, pad_rows, h2p, w1, w3, w2):
    w_in = pl.BlockSpec((None, None, D_MODEL, D_EXPERT),
                        lambda b, be, *_: (layer, be[b], 0, 0))
    w_out = pl.BlockSpec((None, None, D_EXPERT, D_MODEL),
                         lambda b, be, *_: (layer, be[b], 0, 0))
    return pl.pallas_call(
        _expert_kernel,
        grid_spec=pltpu.PrefetchScalarGridSpec(
            num_scalar_prefetch=4,
            grid=(N_BLOCKS,),
            in_specs=[pl.BlockSpec(memory_space=pl.ANY), w_in, w_in, w_out],
            out_specs=pl.BlockSpec((ROW_BLOCK * PACK, LANES), lambda b, *_: (b, 0)),
            scratch_shapes=[
                pltpu.SMEM((N_ROWS,), jnp.int32),
                pltpu.VMEM((2, ROW_BLOCK * PACK, LANES), jnp.float32),
                pltpu.SemaphoreType.DMA((2,)),
                pltpu.VMEM((D_MODEL, D_EXPERT), jnp.bfloat16),
                pltpu.VMEM((D_MODEL, D_EXPERT), jnp.bfloat16),
                pltpu.VMEM((D_EXPERT, D_MODEL), jnp.bfloat16),
            ],
        ),
        out_shape=jax.ShapeDtypeStruct((N_ROWS * PACK, LANES), jnp.float32),
        compiler_params=_cparams(1),
        name="moe_experts",
    )(block_e, n_valid, dest_flat, pad_rows, h2p, w1, w3, w2)


def _combine_kernel(pos_ref, x1_ref, rf_ref, yb_ref, out_ref, buf_ref, sem):
    i = pl.program_id(0)
    n = pl.num_programs(0)
    tm = COMB_TM

    def copy(tile, slot, j, k):
        p = pos_ref[TOP_K * (tile * tm + j) + k]
        return pltpu.make_async_copy(
            yb_ref.at[pl.ds(pl.multiple_of(p * PACK, PACK), PACK)],
            buf_ref.at[slot, k, pl.ds(pl.multiple_of(j * PACK, PACK), PACK)],
            sem.at[slot])

    def issue(tile, slot):
        def body(j):
            copy(tile, slot, j, 0).start()
            copy(tile, slot, j, 1).start()
        _unrolled(tm, body)

    def drain(tile, slot):
        def body(j):
            copy(tile, slot, j, 0).wait()
            copy(tile, slot, j, 1).wait()
        _unrolled(tm, body)

    slot = i % 2

    @pl.when(i == 0)
    def _():
        issue(0, 0)

    @pl.when(i + 1 < n)
    def _():
        issue(i + 1, 1 - slot)

    drain(i, slot)
    rf = rf_ref[...]
    y0 = _unpack_rows(buf_ref.at[slot, 0], tm)
    y1 = _unpack_rows(buf_ref.at[slot, 1], tm)
    out_ref[...] = x1_ref[...] + rf[:, 0:1] * y0 + rf[:, 1:2] * y1


def _combine(pos_flat, x1, rf, yb):
    return pl.pallas_call(
        _combine_kernel,
        grid_spec=pltpu.PrefetchScalarGridSpec(
            num_scalar_prefetch=1,
            grid=(N_TOK // COMB_TM,),
            in_specs=[
                pl.BlockSpec((COMB_TM, D_MODEL), lambda i, p: (i, 0)),
                pl.BlockSpec((COMB_TM, LANES), lambda i, p: (i, 0)),
                pl.BlockSpec(memory_space=pl.ANY),
            ],
            out_specs=pl.BlockSpec((COMB_TM, D_MODEL), lambda i, p: (i, 0)),
            scratch_shapes=[
                pltpu.VMEM((2, TOP_K, COMB_TM * PACK, LANES), jnp.float32),
                pltpu.SemaphoreType.DMA((2,)),
            ],
        ),
        out_shape=jax.ShapeDtypeStruct((N_TOK, D_MODEL), jnp.float32),
        compiler_params=_cparams(1),
        name="moe_combine",
    )(pos_flat, x1, rf, yb)


def _moe(layer, x1, h2p, rf, cnt, w1, w3, w2):
    experts = rf[:, 2:4].astype(jnp.int32)
    rank = rf[:, 4:6].astype(jnp.int32)
    counts = cnt[0, :N_EXPERTS].astype(jnp.int32)
    padded = (counts + ROW_BLOCK - 1) // ROW_BLOCK * ROW_BLOCK
    pad_ends = jnp.cumsum(padded)
    pad_starts = pad_ends - padded
    is_e = experts[:, :, None] == jnp.arange(N_EXPERTS, dtype=jnp.int32)
    dest = (jnp.sum(jnp.where(is_e, pad_starts, 0), axis=-1) + rank).reshape(-1)
    n_valid = (pad_ends[-1:] // ROW_BLOCK).astype(jnp.int32)
    block_row0 = jnp.arange(N_BLOCKS, dtype=jnp.int32) * ROW_BLOCK
    block_e = jnp.minimum(
        jnp.sum((pad_ends[None, :] <= block_row0[:, None]).astype(jnp.int32), axis=1),
        N_EXPERTS - 1).astype(jnp.int32)
    pad_rows = jnp.concatenate([pad_starts + counts, pad_ends]).astype(jnp.int32)
    yb = _experts(layer, block_e, n_valid, dest, pad_rows, h2p, w1, w3, w2)
    return _combine(dest, x1, rf, yb)


def _router_params(w_grp, b_grp, w_exp, b_exp):
    wr = jnp.zeros((D_MODEL, LANES), jnp.float32)
    wr = wr.at[:, :N_EXPERTS].set(w_exp).at[:, GRP_LANE0:GRP_LANE0 + N_GROUPS].set(w_grp)
    br = jnp.zeros((1, LANES), jnp.float32)
    br = br.at[0, :N_EXPERTS].set(b_exp).at[0, GRP_LANE0:GRP_LANE0 + N_GROUPS].set(b_grp)
    w_hi = wr.astype(jnp.bfloat16)
    w_lo = (wr - w_hi.astype(jnp.float32)).astype(jnp.bfloat16)
    return jnp.concatenate([w_hi, w_lo], axis=1), br


def kernel(x, norm1_g, norm2_g, conv_w_in, conv_b_in, conv_dw, conv_dw_b, conv_ln_g, conv_ln_b, conv_w_out, conv_b_out, attn_w_qkv, attn_q_g, attn_k_g, attn_lq1, attn_lk1, attn_lq2, attn_lk2, attn_subln_g, attn_w_o, moe_w_grp, moe_b_grp, moe_w_exp, moe_b_exp, moe_w1, moe_w3, moe_w2):
    bf16 = jnp.bfloat16
    row = lambda a: a.reshape(1, -1)
    xf = x.reshape(N_TOK, D_MODEL)

    wr, br = _router_params(moe_w_grp[0], moe_b_grp[0], moe_w_exp[0], moe_b_exp[0])
    dw = jnp.zeros((HALO, D_MODEL), jnp.float32).at[:CONV_WIDTH].set(conv_dw[0])
    x1, h2p, rf, cnt = _conv_layer(
        xf, row(norm1_g[0]), conv_w_in[0].astype(bf16), row(conv_b_in[0]), dw,
        row(conv_dw_b[0]), row(conv_ln_g[0]), row(conv_ln_b[0]),
        conv_w_out[0].astype(bf16), row(conv_b_out[0]), row(norm2_g[0]), wr, br)
    xf = _moe(0, x1, h2p, rf, cnt, moe_w1, moe_w3, moe_w2)

    lambda_init = 0.8 - 0.6 * math.exp(-0.3 * 1)
    wr, br = _router_params(moe_w_grp[1], moe_b_grp[1], moe_w_exp[1], moe_b_exp[1])
    qk, v = _qkv(xf, row(norm1_g[1]), attn_w_qkv[0].astype(bf16))
    lam_rows = jnp.zeros((SUBLANES, HEAD_DIM), jnp.float32)
    lam_rows = lam_rows.at[0].set(attn_lq1[0]).at[1].set(attn_lk1[0])
    lam_rows = lam_rows.at[2].set(attn_lq2[0]).at[3].set(attn_lk2[0])
    two = lambda g: jnp.concatenate([g, g]).reshape(1, V_DIM)
    o = _attention(qk, v, two(attn_q_g[0]), two(attn_k_g[0]), lam_rows,
                   row(attn_subln_g[0]), lambda_init)
    x1, h2p, rf, cnt = _attn_post(xf, o, attn_w_o[0].astype(bf16), row(norm2_g[1]), wr, br)
    xf = _moe(1, x1, h2p, rf, cnt, moe_w1, moe_w3, moe_w2)
    return xf.reshape(BATCH, SEQ, D_MODEL)
```

```python
import functools
import math

import jax
import jax.numpy as jnp
from jax import lax
from jax.experimental import pallas as pl
from jax.experimental.pallas import tpu as pltpu

D_MODEL = 1024
BATCH = 8
SEQ = 2048
N_TOK = BATCH * SEQ
CHUNK = 64
CONV_WIDTH = 31
N_HEADS = 8
HEAD_DIM = 64
V_DIM = 128
N_GROUPS = 4
EXPERTS_PER_GROUP = 8
N_EXPERTS = 32
TOP_K = 2
D_EXPERT = 512
EPS = 1e-6
LOG2E = math.log2(math.e)

LANES = 128
SUBLANES = 8
PACK = D_MODEL // LANES
VMEM_LIMIT = 56 * 1024 * 1024

CONV_TS = 512
HALO = 32
CONV_RC = 64
POST_TS = 512
QKV_TS = 512
ATT_TQ = 256
ROW_BLOCK = 256
GATHER_SLOTS = 3
N_ASSIGN = N_TOK * TOP_K
N_BLOCKS = N_ASSIGN // ROW_BLOCK + N_EXPERTS
N_ROWS = N_BLOCKS * ROW_BLOCK
COMB_TM = 256
DMA_UNROLL = 8
GRP_LANE0 = N_EXPERTS


def _cparams(n_axes):
    return pltpu.CompilerParams(
        dimension_semantics=("arbitrary",) * n_axes, vmem_limit_bytes=VMEM_LIMIT)


def _rms(x, g):
    return x * lax.rsqrt(jnp.mean(x * x, axis=-1, keepdims=True) + EPS) * g


def _pack_rows(ref, val, rows):
    for j in range(PACK):
        ref[pl.ds(j, rows, stride=PACK), :] = val[:, j * LANES:(j + 1) * LANES]


def _unpack_rows(ref, rows):
    return jnp.concatenate(
        [ref[pl.ds(j, rows, stride=PACK), :] for j in range(PACK)], axis=-1)


def _residual_norm_route(x1, g2_ref, wr_ref, br_ref, run_ref,
                         x1_ref, h2p_ref, rf_ref, cnt_ref, rows):
    x1_ref[...] = x1
    h2 = _rms(x1, g2_ref[...])
    _pack_rows(h2p_ref, h2, rows)

    hi = h2.astype(jnp.bfloat16)
    lo = (h2 - hi.astype(jnp.float32)).astype(jnp.bfloat16)
    ab = jnp.dot(hi, wr_ref[...], preferred_element_type=jnp.float32)
    c = jnp.dot(lo, wr_ref[:, 0:LANES], preferred_element_type=jnp.float32)
    logits = ab[:, 0:LANES] + ab[:, LANES:2 * LANES] + c + br_ref[...]
    lane = lax.broadcasted_iota(jnp.int32, (rows, LANES), 1)
    lane_f = lane.astype(jnp.float32)
    neg = jnp.float32(-jnp.inf)
    big = jnp.float32(1e9)

    gmask = (lane >= GRP_LANE0) & (lane < GRP_LANE0 + N_GROUPS)
    gl = jnp.where(gmask, logits, neg)
    gmax = jnp.max(gl, axis=1, keepdims=True)
    gidx = jnp.min(jnp.where(gl == gmax, lane_f, big), axis=1, keepdims=True) - GRP_LANE0
    gsum = jnp.sum(jnp.where(gmask, jnp.exp(gl - gmax), 0.0), axis=1, keepdims=True)
    grp_p = 1.0 / gsum

    lane0 = gidx * EXPERTS_PER_GROUP
    emask = (lane_f >= lane0) & (lane_f < lane0 + EXPERTS_PER_GROUP)
    el = jnp.where(emask, logits, neg)
    m1 = jnp.max(el, axis=1, keepdims=True)
    i1 = jnp.min(jnp.where(el == m1, lane_f, big), axis=1, keepdims=True)
    el2 = jnp.where(lane_f == i1, neg, el)
    m2 = jnp.max(el2, axis=1, keepdims=True)
    i2 = jnp.min(jnp.where(el2 == m2, lane_f, big), axis=1, keepdims=True)
    t = jnp.exp(m2 - m1)
    inv = 1.0 / (1.0 + t)
    g_first = grp_p * inv
    g_second = grp_p * t * inv

    sel1 = lane_f == i1
    sel2 = lane_f == i2
    member = jnp.where(sel1 | sel2, 1.0, 0.0).astype(jnp.bfloat16)
    r_i = lax.broadcasted_iota(jnp.int32, (rows, rows), 0)
    c_i = lax.broadcasted_iota(jnp.int32, (rows, rows), 1)
    tri = jnp.where(c_i < r_i, 1.0, 0.0).astype(jnp.bfloat16)
    cum = jnp.dot(tri, member, preferred_element_type=jnp.float32) + run_ref[...]
    rank1 = jnp.sum(jnp.where(sel1, cum, 0.0), axis=1, keepdims=True)
    rank2 = jnp.sum(jnp.where(sel2, cum, 0.0), axis=1, keepdims=True)
    run_new = run_ref[...] + jnp.sum(member.astype(jnp.float32), axis=0, keepdims=True)
    run_ref[...] = run_new
    cnt_ref[...] = jnp.broadcast_to(run_new, (SUBLANES, LANES))

    out = jnp.where(lane == 0, g_first, 0.0)
    out = jnp.where(lane == 1, g_second, out)
    out = jnp.where(lane == 2, i1, out)
    out = jnp.where(lane == 3, i2, out)
    out = jnp.where(lane == 4, rank1, out)
    out = jnp.where(lane == 5, rank2, out)
    rf_ref[...] = out


def _route_out_shapes():
    return (
        jax.ShapeDtypeStruct((N_TOK, D_MODEL), jnp.float32),
        jax.ShapeDtypeStruct((N_TOK * PACK, LANES), jnp.float32),
        jax.ShapeDtypeStruct((N_TOK, LANES), jnp.float32),
        jax.ShapeDtypeStruct((SUBLANES, LANES), jnp.float32),
    )


def _route_out_specs(ts, idx):
    return (
        pl.BlockSpec((ts, D_MODEL), lambda *a: (idx(*a), 0)),
        pl.BlockSpec((ts * PACK, LANES), lambda *a: (idx(*a), 0)),
        pl.BlockSpec((ts, LANES), lambda *a: (idx(*a), 0)),
        pl.BlockSpec((SUBLANES, LANES), lambda *a: (0, 0)),
    )


def _full(shape):
    return pl.BlockSpec(shape, lambda *a: (0,) * len(shape))


def _conv_kernel(x_ref, g1_ref, win_ref, bin_ref, dw_ref, dwb_ref, lng_ref, lnb_ref,
                 wout_ref, bout_ref, g2_ref, wr_ref, br_ref,
                 x1_ref, h2p_ref, rf_ref, cnt_ref,
                 ext_ref, conv_ref, run_ref):
    b = pl.program_id(0)
    s = pl.program_id(1)
    ts = CONV_TS

    @pl.when((b == 0) & (s == 0))
    def _():
        run_ref[...] = jnp.zeros_like(run_ref)

    @pl.when(s == 0)
    def _():
        ext_ref[0:HALO, :] = jnp.zeros((HALO, D_MODEL), jnp.float32)

    x = x_ref[...]
    h = _rms(x, g1_ref[...]).astype(jnp.bfloat16)
    u = jnp.dot(h, win_ref[...], preferred_element_type=jnp.float32) + bin_ref[...]
    glu = u[:, :D_MODEL] * jax.nn.sigmoid(u[:, D_MODEL:])
    ext_ref[HALO:HALO + ts, :] = glu

    base = HALO - (CONV_WIDTH - 1)

    def lane_chunk(c, carry):
        cols = pl.ds(pl.multiple_of(c * LANES, LANES), LANES)
        taps = [dw_ref[pl.ds(k, 1), cols] for k in range(CONV_WIDTH)]
        for r0 in range(0, ts, CONV_RC):
            acc = None
            for r in range(SUBLANES):
                rows = CONV_RC + (SUBLANES if r else 0)
                part = None
                for q in range((base + CONV_WIDTH - 1) // SUBLANES + 1):
                    k = SUBLANES * q + r - base
                    if 0 <= k < CONV_WIDTH:
                        term = ext_ref[pl.ds(r0 + SUBLANES * q, rows), cols] * taps[k]
                        part = term if part is None else part + term
                part = part[r:r + CONV_RC, :] if r else part
                acc = part if acc is None else acc + part
            conv_ref[pl.ds(r0, CONV_RC), cols] = acc
        return carry

    lax.fori_loop(0, D_MODEL // LANES, lane_chunk, 0)
    ext_ref[0:HALO, :] = ext_ref[ts:ts + HALO, :]

    v = conv_ref[...] + dwb_ref[...]
    mu = jnp.mean(v, axis=-1, keepdims=True)
    vc = v - mu
    var = jnp.mean(vc * vc, axis=-1, keepdims=True)
    y = vc * lax.rsqrt(var + EPS) * lng_ref[...] + lnb_ref[...]
    y = (y * jax.nn.sigmoid(y)).astype(jnp.bfloat16)
    mix = jnp.dot(y, wout_ref[...], preferred_element_type=jnp.float32) + bout_ref[...]
    _residual_norm_route(x + mix, g2_ref, wr_ref, br_ref, run_ref,
                         x1_ref, h2p_ref, rf_ref, cnt_ref, ts)


def _conv_layer(x, g1, w_in, b_in, dw, dw_b, ln_g, ln_b, w_out, b_out, g2, wr, br):
    ns = SEQ // CONV_TS
    tile = lambda b, s: b * ns + s
    return pl.pallas_call(
        _conv_kernel,
        grid=(BATCH, ns),
        in_specs=[
            pl.BlockSpec((CONV_TS, D_MODEL), lambda b, s: (tile(b, s), 0)),
            _full((1, D_MODEL)),
            _full((D_MODEL, 2 * D_MODEL)),
            _full((1, 2 * D_MODEL)),
            _full((HALO, D_MODEL)),
            _full((1, D_MODEL)), _full((1, D_MODEL)), _full((1, D_MODEL)),
            _full((D_MODEL, D_MODEL)),
            _full((1, D_MODEL)), _full((1, D_MODEL)),
            _full((D_MODEL, 2 * LANES)), _full((1, LANES)),
        ],
        out_specs=_route_out_specs(CONV_TS, tile),
        out_shape=_route_out_shapes(),
        scratch_shapes=[
            pltpu.VMEM((HALO + CONV_TS, D_MODEL), jnp.float32),
            pltpu.VMEM((CONV_TS, D_MODEL), jnp.float32),
            pltpu.VMEM((1, LANES), jnp.float32),
        ],
        compiler_params=_cparams(2),
        name="conv_mixer",
    )(x, g1, w_in, b_in, dw, dw_b, ln_g, ln_b, w_out, b_out, g2, wr, br)


def _qkv_kernel(x_ref, g1_ref, w_ref, qk_ref, v_ref):
    h = _rms(x_ref[...], g1_ref[...]).astype(jnp.bfloat16)
    qkv = jnp.dot(h, w_ref[...], preferred_element_type=jnp.float32)
    qk_ref[...] = qkv[:, :2 * D_MODEL]
    v_ref[...] = qkv[:, 2 * D_MODEL:].astype(jnp.bfloat16)


def _qkv(x, g1, w_qkv):
    return pl.pallas_call(
        _qkv_kernel,
        grid=(N_TOK // QKV_TS,),
        in_specs=[
            pl.BlockSpec((QKV_TS, D_MODEL), lambda i: (i, 0)),
            _full((1, D_MODEL)),
            _full((D_MODEL, 3 * D_MODEL)),
        ],
        out_specs=(
            pl.BlockSpec((QKV_TS, 2 * D_MODEL), lambda i: (i, 0)),
            pl.BlockSpec((QKV_TS, D_MODEL), lambda i: (i, 0)),
        ),
        out_shape=(
            jax.ShapeDtypeStruct((N_TOK, 2 * D_MODEL), jnp.float32),
            jax.ShapeDtypeStruct((N_TOK, D_MODEL), jnp.bfloat16),
        ),
        compiler_params=_cparams(1),
        name="qkv_proj",
    )(x, g1, w_qkv)


def _half_norm(z, gain):
    lane = lax.broadcasted_iota(jnp.int32, z.shape, 1)
    first = lane < HEAD_DIM
    zz = z * z
    ss_a = jnp.sum(jnp.where(first, zz, 0.0), axis=1, keepdims=True)
    ss_b = jnp.sum(jnp.where(first, 0.0, zz), axis=1, keepdims=True)
    inv = jnp.where(first, lax.rsqrt(ss_a * (1.0 / HEAD_DIM) + EPS),
                    lax.rsqrt(ss_b * (1.0 / HEAD_DIM) + EPS))
    return z * inv * gain


def _attn_kernel(q_ref, k_ref, v_ref, qg_ref, kg_ref, lam_ref, sg_ref, o_ref,
                 qa_ref, qb_ref, kn_ref, v1_ref, *, lambda_init):
    lp = lam_ref[...]
    lam = (jnp.exp(jnp.sum(lp[0:1, :] * lp[1:2, :], axis=1, keepdims=True))
           - jnp.exp(jnp.sum(lp[2:3, :] * lp[3:4, :], axis=1, keepdims=True))
           + lambda_init)

    kn_ref[...] = _half_norm(k_ref[...], kg_ref[...]).astype(jnp.bfloat16)
    qn = _half_norm(q_ref[...], qg_ref[...]) * (HEAD_DIM ** -0.5 * LOG2E)
    first = lax.broadcasted_iota(jnp.int32, (SEQ, V_DIM), 1) < HEAD_DIM
    qa_ref[...] = jnp.where(first, qn, 0.0).astype(jnp.bfloat16)
    qb_ref[...] = jnp.where(first, 0.0, qn).astype(jnp.bfloat16)

    ones_col = lax.broadcasted_iota(jnp.int32, (SEQ, V_DIM), 1) == 0
    v1_ref[:, 0:V_DIM] = v_ref[...]
    v1_ref[:, V_DIM:2 * V_DIM] = jnp.where(ones_col, 1.0, 0.0).astype(jnp.bfloat16)

    tq = ATT_TQ
    nt = (((1,), (1,)), ((), ()))
    visible = (lax.broadcasted_iota(jnp.int32, (tq, tq), 1) // CHUNK
               <= lax.broadcasted_iota(jnp.int32, (tq, tq), 0) // CHUNK)
    for qi in range(SEQ // tq):
        k0 = qi * tq
        rows = slice(k0, k0 + tq)

        def half_attention(qh_ref):
            q = qh_ref[rows, :]
            dg = lax.dot_general(q, kn_ref[rows, :], nt, preferred_element_type=jnp.float32)
            dg = jnp.where(visible, dg, -jnp.inf)
            m = jnp.max(dg, axis=1, keepdims=True)
            if qi:
                off = lax.dot_general(q, kn_ref[0:k0, :], nt,
                                      preferred_element_type=jnp.float32)
                m = jnp.maximum(m, jnp.max(off, axis=1, keepdims=True))
            acc = jnp.dot(jnp.exp2(dg - m).astype(jnp.bfloat16), v1_ref[rows, :],
                          preferred_element_type=jnp.float32)
            if qi:
                acc = acc + jnp.dot(jnp.exp2(off - m).astype(jnp.bfloat16), v1_ref[0:k0, :],
                                    preferred_element_type=jnp.float32)
            return acc[:, 0:V_DIM] * (1.0 / acc[:, V_DIM:V_DIM + 1])

        o = half_attention(qa_ref) - lam * half_attention(qb_ref)
        o = _rms(o, sg_ref[...]) * (1.0 - lambda_init)
        o_ref[rows, :] = o.astype(jnp.bfloat16)


def _attention(qk, v, qg2, kg2, lam_rows, subln_g, lambda_init):
    return pl.pallas_call(
        functools.partial(_attn_kernel, lambda_init=lambda_init),
        grid=(BATCH, N_HEADS),
        in_specs=[
            pl.BlockSpec((SEQ, V_DIM), lambda b, h: (b, h)),
            pl.BlockSpec((SEQ, V_DIM), lambda b, h: (b, N_HEADS + h)),
            pl.BlockSpec((SEQ, V_DIM), lambda b, h: (b, h)),
            _full((1, V_DIM)), _full((1, V_DIM)),
            _full((SUBLANES, HEAD_DIM)),
            _full((1, V_DIM)),
        ],
        out_specs=pl.BlockSpec((SEQ, V_DIM), lambda b, h: (b, h)),
        out_shape=jax.ShapeDtypeStruct((N_TOK, D_MODEL), jnp.bfloat16),
        scratch_shapes=[pltpu.VMEM((SEQ, V_DIM), jnp.bfloat16)] * 3
        + [pltpu.VMEM((SEQ, 2 * V_DIM), jnp.bfloat16)],
        compiler_params=_cparams(2),
        name="diff_attention",
    )(qk, qk, v, qg2, kg2, lam_rows, subln_g)


def _post_kernel(x_ref, m_ref, w_ref, g2_ref, wr_ref, br_ref,
                 x1_ref, h2p_ref, rf_ref, cnt_ref, run_ref):
    @pl.when(pl.program_id(0) == 0)
    def _():
        run_ref[...] = jnp.zeros_like(run_ref)

    mix = jnp.dot(m_ref[...], w_ref[...], preferred_element_type=jnp.float32)
    _residual_norm_route(x_ref[...] + mix, g2_ref, wr_ref, br_ref, run_ref,
                         x1_ref, h2p_ref, rf_ref, cnt_ref, POST_TS)


def _attn_post(x, o, w_o, g2, wr, br):
    return pl.pallas_call(
        _post_kernel,
        grid=(N_TOK // POST_TS,),
        in_specs=[
            pl.BlockSpec((POST_TS, D_MODEL), lambda i: (i, 0)),
            pl.BlockSpec((POST_TS, D_MODEL), lambda i: (i, 0)),
            _full((D_MODEL, D_MODEL)),
            _full((1, D_MODEL)),
            _full((D_MODEL, 2 * LANES)), _full((1, LANES)),
        ],
        out_specs=_route_out_specs(POST_TS, lambda i: i),
        out_shape=_route_out_shapes(),
        scratch_shapes=[pltpu.VMEM((1, LANES), jnp.float32)],
        compiler_params=_cparams(1),
        name="attn_post",
    )(x, o, w_o, g2, wr, br)


def _unrolled(n, body):
    def group(g, carry):
        for u in range(DMA_UNROLL):
            body(g * DMA_UNROLL + u)
        return carry
    lax.fori_loop(0, n // DMA_UNROLL, group, 0)


def _expert_kernel(be_ref, nv_ref, dest_ref, pad_ref, h2p_ref, w1_ref, w3_ref, w2_ref, yb_ref,
                   rowtok_ref, xbuf_ref, sem, w1b_ref, w3b_ref, w2b_ref):
    b = pl.program_id(0)
    n_valid = nv_ref[0]

    def row_copy(block, slot, r):
        tok = rowtok_ref[block * ROW_BLOCK + r]
        return pltpu.make_async_copy(
            h2p_ref.at[pl.ds(pl.multiple_of(tok * PACK, PACK), PACK)],
            xbuf_ref.at[slot, pl.ds(r * PACK, PACK)],
            sem.at[slot])

    def block_arrival(slot):
        return pltpu.make_async_copy(
            h2p_ref.at[pl.ds(0, ROW_BLOCK * PACK)], xbuf_ref.at[slot], sem.at[slot])

    @pl.when(b == 0)
    def _():
        def clear_padding(e, carry):
            def clear(j, c):
                rowtok_ref[j] = 0
                return c
            return lax.fori_loop(pad_ref[e], pad_ref[N_EXPERTS + e], clear, carry)
        lax.fori_loop(0, N_EXPERTS, clear_padding, 0)

        def place(t):
            rowtok_ref[dest_ref[TOP_K * t]] = t
            rowtok_ref[dest_ref[TOP_K * t + 1]] = t
        _unrolled(N_TOK, place)
        second = jnp.minimum(1, n_valid - 1)
        _unrolled(ROW_BLOCK, lambda r: row_copy(0, 0, r).start())
        _unrolled(ROW_BLOCK, lambda r: row_copy(second, 1, r).start())

    prev = be_ref[jnp.maximum(b - 1, 0)]
    fresh = (b == 0) | (be_ref[b] != prev)

    @pl.when(fresh)
    def _():
        w1b_ref[...] = w1_ref[...].astype(jnp.bfloat16)
        w3b_ref[...] = w3_ref[...].astype(jnp.bfloat16)
        w2b_ref[...] = w2_ref[...].astype(jnp.bfloat16)

    @pl.when(b < n_valid)
    def _():
        slot = lax.rem(b, GATHER_SLOTS)
        block_arrival(slot).wait()
        x = _unpack_rows(xbuf_ref.at[slot], ROW_BLOCK).astype(jnp.bfloat16)
        ahead = jnp.minimum(b + 2, n_valid - 1)
        ahead_slot = lax.rem(b + 2, GATHER_SLOTS)
        for r in range(ROW_BLOCK):
            row_copy(ahead, ahead_slot, r).start()
        h1 = jnp.dot(x, w1b_ref[...], preferred_element_type=jnp.float32)
        h3 = jnp.dot(x, w3b_ref[...], preferred_element_type=jnp.float32)
        act = (h1 * jax.nn.sigmoid(h1) * h3).astype(jnp.bfloat16)
        y = jnp.dot(act, w2b_ref[...], preferred_element_type=jnp.float32)
        _pack_rows(yb_ref, y, ROW_BLOCK)

    @pl.when(b == n_valid - 1)
    def _():
        block_arrival(lax.rem(b + 1, GATHER_SLOTS)).wait()
        block_arrival(lax.rem(b + 2, GATHER_SLOTS)).wait()

    @pl.when(b >= n_valid)
    def _():
        yb_ref[...] = jnp.zeros_like(yb_ref)


def _experts(layer, block_e, n_valid, dest_flat, pad_rows, h2p, w1, w3, w2):
    w_in = pl.BlockSpec((None, None, D_MODEL, D_EXPERT),
                        lambda b, be, *_: (layer, be[b], 0, 0))
    w_out = pl.BlockSpec((None, None, D_EXPERT, D_MODEL),
                         lambda b, be, *_: (layer, be[b], 0, 0))
    return pl.pallas_call(
        _expert_kernel,
        grid_spec=pltpu.PrefetchScalarGridSpec(
            num_scalar_prefetch=4,
            grid=(N_BLOCKS,),
            in_specs=[pl.BlockSpec(memory_space=pl.ANY), w_in, w_in, w_out],
            out_specs=pl.BlockSpec((ROW_BLOCK * PACK, LANES), lambda b, *_: (b, 0)),
            scratch_shapes=[
                pltpu.SMEM((N_ROWS,), jnp.int32),
                pltpu.VMEM((GATHER_SLOTS, ROW_BLOCK * PACK, LANES), jnp.float32),
                pltpu.SemaphoreType.DMA((GATHER_SLOTS,)),
                pltpu.VMEM((D_MODEL, D_EXPERT), jnp.bfloat16),
                pltpu.VMEM((D_MODEL, D_EXPERT), jnp.bfloat16),
                pltpu.VMEM((D_EXPERT, D_MODEL), jnp.bfloat16),
            ],
        ),
        out_shape=jax.ShapeDtypeStruct((N_ROWS * PACK, LANES), jnp.float32),
        compiler_params=_cparams(1),
        name="moe_experts",
    )(block_e, n_valid, dest_flat, pad_rows, h2p, w1, w3, w2)


def _combine_kernel(pos_ref, x1_ref, rf_ref, yb_ref, out_ref, buf_ref, sem):
    i = pl.program_id(0)
    n = pl.num_programs(0)
    tm = COMB_TM

    def copy(tile, slot, j, k):
        p = pos_ref[TOP_K * (tile * tm + j) + k]
        return pltpu.make_async_copy(
            yb_ref.at[pl.ds(pl.multiple_of(p * PACK, PACK), PACK)],
            buf_ref.at[slot, k, pl.ds(pl.multiple_of(j * PACK, PACK), PACK)],
            sem.at[slot])

    def issue(tile, slot):
        def body(j):
            copy(tile, slot, j, 0).start()
            copy(tile, slot, j, 1).start()
        _unrolled(tm, body)

    def drain(tile, slot):
        def body(j):
            copy(tile, slot, j, 0).wait()
            copy(tile, slot, j, 1).wait()
        _unrolled(tm, body)

    slot = i % 2

    @pl.when(i == 0)
    def _():
        issue(0, 0)

    @pl.when(i + 1 < n)
    def _():
        issue(i + 1, 1 - slot)

    drain(i, slot)
    rf = rf_ref[...]
    y0 = _unpack_rows(buf_ref.at[slot, 0], tm)
    y1 = _unpack_rows(buf_ref.at[slot, 1], tm)
    out_ref[...] = x1_ref[...] + rf[:, 0:1] * y0 + rf[:, 1:2] * y1


def _combine(pos_flat, x1, rf, yb):
    return pl.pallas_call(
        _combine_kernel,
        grid_spec=pltpu.PrefetchScalarGridSpec(
            num_scalar_prefetch=1,
            grid=(N_TOK // COMB_TM,),
            in_specs=[
                pl.BlockSpec((COMB_TM, D_MODEL), lambda i, p: (i, 0)),
                pl.BlockSpec((COMB_TM, LANES), lambda i, p: (i, 0)),
                pl.BlockSpec(memory_space=pl.ANY),
            ],
            out_specs=pl.BlockSpec((COMB_TM, D_MODEL), lambda i, p: (i, 0)),
            scratch_shapes=[
                pltpu.VMEM((2, TOP_K, COMB_TM * PACK, LANES), jnp.float32),
                pltpu.SemaphoreType.DMA((2,)),
            ],
        ),
        out_shape=jax.ShapeDtypeStruct((N_TOK, D_MODEL), jnp.float32),
        compiler_params=_cparams(1),
        name="moe_combine",
    )(pos_flat, x1, rf, yb)


def _moe(layer, x1, h2p, rf, cnt, w1, w3, w2):
    experts = rf[:, 2:4].astype(jnp.int32)
    rank = rf[:, 4:6].astype(jnp.int32)
    counts = cnt[0, :N_EXPERTS].astype(jnp.int32)
    padded = (counts + ROW_BLOCK - 1) // ROW_BLOCK * ROW_BLOCK
    pad_ends = jnp.cumsum(padded)
    pad_starts = pad_ends - padded
    is_e = experts[:, :, None] == jnp.arange(N_EXPERTS, dtype=jnp.int32)
    dest = (jnp.sum(jnp.where(is_e, pad_starts, 0), axis=-1) + rank).reshape(-1)
    n_valid = (pad_ends[-1:] // ROW_BLOCK).astype(jnp.int32)
    block_row0 = jnp.arange(N_BLOCKS, dtype=jnp.int32) * ROW_BLOCK
    block_e = jnp.minimum(
        jnp.sum((pad_ends[None, :] <= block_row0[:, None]).astype(jnp.int32), axis=1),
        N_EXPERTS - 1).astype(jnp.int32)
    pad_rows = jnp.concatenate([pad_starts + counts, pad_ends]).astype(jnp.int32)
    yb = _experts(layer, block_e, n_valid, dest, pad_rows, h2p, w1, w3, w2)
    return _combine(dest, x1, rf, yb)


def _router_params(w_grp, b_grp, w_exp, b_exp):
    wr = jnp.zeros((D_MODEL, LANES), jnp.float32)
    wr = wr.at[:, :N_EXPERTS].set(w_exp).at[:, GRP_LANE0:GRP_LANE0 + N_GROUPS].set(w_grp)
    br = jnp.zeros((1, LANES), jnp.float32)
    br = br.at[0, :N_EXPERTS].set(b_exp).at[0, GRP_LANE0:GRP_LANE0 + N_GROUPS].set(b_grp)
    w_hi = wr.astype(jnp.bfloat16)
    w_lo = (wr - w_hi.astype(jnp.float32)).astype(jnp.bfloat16)
    return jnp.concatenate([w_hi, w_lo], axis=1), br


def kernel(x, norm1_g, norm2_g, conv_w_in, conv_b_in, conv_dw, conv_dw_b, conv_ln_g, conv_ln_b, conv_w_out, conv_b_out, attn_w_qkv, attn_q_g, attn_k_g, attn_lq1, attn_lk1, attn_lq2, attn_lk2, attn_subln_g, attn_w_o, moe_w_grp, moe_b_grp, moe_w_exp, moe_b_exp, moe_w1, moe_w3, moe_w2):
    bf16 = jnp.bfloat16
    row = lambda a: a.reshape(1, -1)
    xf = x.reshape(N_TOK, D_MODEL)

    wr, br = _router_params(moe_w_grp[0], moe_b_grp[0], moe_w_exp[0], moe_b_exp[0])
    dw = jnp.zeros((HALO, D_MODEL), jnp.float32).at[:CONV_WIDTH].set(conv_dw[0])
    x1, h2p, rf, cnt = _conv_layer(
        xf, row(norm1_g[0]), conv_w_in[0].astype(bf16), row(conv_b_in[0]), dw,
        row(conv_dw_b[0]), row(conv_ln_g[0]), row(conv_ln_b[0]),
        conv_w_out[0].astype(bf16), row(conv_b_out[0]), row(norm2_g[0]), wr, br)
    xf = _moe(0, x1, h2p, rf, cnt, moe_w1, moe_w3, moe_w2)

    lambda_init = 0.8 - 0.6 * math.exp(-0.3 * 1)
    wr, br = _router_params(moe_w_grp[1], moe_b_grp[1], moe_w_exp[1], moe_b_exp[1])
    qk, v = _qkv(xf, row(norm1_g[1]), attn_w_qkv[0].astype(bf16))
    lam_rows = jnp.zeros((SUBLANES, HEAD_DIM), jnp.float32)
    lam_rows = lam_rows.at[0].set(attn_lq1[0]).at[1].set(attn_lk1[0])
    lam_rows = lam_rows.at[2].set(attn_lq2[0]).at[3].set(attn_lk2[0])
    two = lambda g: jnp.concatenate([g, g]).reshape(1, V_DIM)
    o = _attention(qk, v, two(attn_q_g[0]), two(attn_k_g[0]), lam_rows,
                   row(attn_subln_g[0]), lambda_init)
    x1, h2p, rf, cnt = _attn_post(xf, o, attn_w_o[0].astype(bf16), row(norm2_g[1]), wr, br)
    xf = _moe(1, x1, h2p, rf, cnt, moe_w1, moe_w3, moe_w2)
    return xf.reshape(BATCH, SEQ, D_MODEL)
```

```python
import functools
import math

import jax
import jax.numpy as jnp
from jax import lax
from jax.experimental import pallas as pl
from jax.experimental.pallas import tpu as pltpu

D_MODEL = 1024
BATCH = 8
SEQ = 2048
N_TOK = BATCH * SEQ
CHUNK = 64
CONV_WIDTH = 31
N_HEADS = 8
HEAD_DIM = 64
V_DIM = 128
N_GROUPS = 4
EXPERTS_PER_GROUP = 8
N_EXPERTS = 32
TOP_K = 2
D_EXPERT = 512
EPS = 1e-6
LOG2E = math.log2(math.e)

LANES = 128
SUBLANES = 8
PACK = D_MODEL // LANES
VMEM_LIMIT = 56 * 1024 * 1024

CONV_TS = 512
HALO = 32
CONV_RC = 64
POST_TS = 512
QKV_TS = 512
ATT_TQ = 256
ROW_BLOCK = 256
GATHER_SLOTS = 3
N_ASSIGN = N_TOK * TOP_K
N_BLOCKS = N_ASSIGN // ROW_BLOCK + N_EXPERTS
N_ROWS = N_BLOCKS * ROW_BLOCK
COMB_TM = 256
DMA_UNROLL = 8
ROW_DMA_PRIORITY = 1
GRP_LANE0 = N_EXPERTS


def _cparams(n_axes):
    return pltpu.CompilerParams(
        dimension_semantics=("arbitrary",) * n_axes, vmem_limit_bytes=VMEM_LIMIT)


def _rms(x, g):
    return x * lax.rsqrt(jnp.mean(x * x, axis=-1, keepdims=True) + EPS) * g


def _pack_rows(ref, val, rows):
    for j in range(PACK):
        ref[pl.ds(j, rows, stride=PACK), :] = val[:, j * LANES:(j + 1) * LANES]


def _unpack_rows(ref, rows):
    return jnp.concatenate(
        [ref[pl.ds(j, rows, stride=PACK), :] for j in range(PACK)], axis=-1)


def _residual_norm_route(x1, g2_ref, wr_ref, br_ref, run_ref,
                         x1_ref, h2p_ref, rf_ref, cnt_ref, rows):
    x1_ref[...] = x1
    h2 = _rms(x1, g2_ref[...])
    _pack_rows(h2p_ref, h2, rows)

    hi = h2.astype(jnp.bfloat16)
    lo = (h2 - hi.astype(jnp.float32)).astype(jnp.bfloat16)
    ab = jnp.dot(hi, wr_ref[...], preferred_element_type=jnp.float32)
    c = jnp.dot(lo, wr_ref[:, 0:LANES], preferred_element_type=jnp.float32)
    logits = ab[:, 0:LANES] + ab[:, LANES:2 * LANES] + c + br_ref[...]
    lane = lax.broadcasted_iota(jnp.int32, (rows, LANES), 1)
    lane_f = lane.astype(jnp.float32)
    neg = jnp.float32(-jnp.inf)
    big = jnp.float32(1e9)

    gmask = (lane >= GRP_LANE0) & (lane < GRP_LANE0 + N_GROUPS)
    gl = jnp.where(gmask, logits, neg)
    gmax = jnp.max(gl, axis=1, keepdims=True)
    gidx = jnp.min(jnp.where(gl == gmax, lane_f, big), axis=1, keepdims=True) - GRP_LANE0
    gsum = jnp.sum(jnp.where(gmask, jnp.exp(gl - gmax), 0.0), axis=1, keepdims=True)
    grp_p = 1.0 / gsum

    lane0 = gidx * EXPERTS_PER_GROUP
    emask = (lane_f >= lane0) & (lane_f < lane0 + EXPERTS_PER_GROUP)
    el = jnp.where(emask, logits, neg)
    m1 = jnp.max(el, axis=1, keepdims=True)
    i1 = jnp.min(jnp.where(el == m1, lane_f, big), axis=1, keepdims=True)
    el2 = jnp.where(lane_f == i1, neg, el)
    m2 = jnp.max(el2, axis=1, keepdims=True)
    i2 = jnp.min(jnp.where(el2 == m2, lane_f, big), axis=1, keepdims=True)
    t = jnp.exp(m2 - m1)
    inv = 1.0 / (1.0 + t)
    g_first = grp_p * inv
    g_second = grp_p * t * inv

    sel1 = lane_f == i1
    sel2 = lane_f == i2
    member = jnp.where(sel1 | sel2, 1.0, 0.0).astype(jnp.bfloat16)
    r_i = lax.broadcasted_iota(jnp.int32, (rows, rows), 0)
    c_i = lax.broadcasted_iota(jnp.int32, (rows, rows), 1)
    tri = jnp.where(c_i < r_i, 1.0, 0.0).astype(jnp.bfloat16)
    cum = jnp.dot(tri, member, preferred_element_type=jnp.float32) + run_ref[...]
    rank1 = jnp.sum(jnp.where(sel1, cum, 0.0), axis=1, keepdims=True)
    rank2 = jnp.sum(jnp.where(sel2, cum, 0.0), axis=1, keepdims=True)
    run_new = run_ref[...] + jnp.sum(member.astype(jnp.float32), axis=0, keepdims=True)
    run_ref[...] = run_new
    cnt_ref[...] = jnp.broadcast_to(run_new, (SUBLANES, LANES))

    out = jnp.where(lane == 0, g_first, 0.0)
    out = jnp.where(lane == 1, g_second, out)
    out = jnp.where(lane == 2, i1, out)
    out = jnp.where(lane == 3, i2, out)
    out = jnp.where(lane == 4, rank1, out)
    out = jnp.where(lane == 5, rank2, out)
    rf_ref[...] = out


def _route_out_shapes():
    return (
        jax.ShapeDtypeStruct((N_TOK, D_MODEL), jnp.float32),
        jax.ShapeDtypeStruct((N_TOK * PACK, LANES), jnp.float32),
        jax.ShapeDtypeStruct((N_TOK, LANES), jnp.float32),
        jax.ShapeDtypeStruct((SUBLANES, LANES), jnp.float32),
    )


def _route_out_specs(ts, idx):
    return (
        pl.BlockSpec((ts, D_MODEL), lambda *a: (idx(*a), 0)),
        pl.BlockSpec((ts * PACK, LANES), lambda *a: (idx(*a), 0)),
        pl.BlockSpec((ts, LANES), lambda *a: (idx(*a), 0)),
        pl.BlockSpec((SUBLANES, LANES), lambda *a: (0, 0)),
    )


def _full(shape):
    return pl.BlockSpec(shape, lambda *a: (0,) * len(shape))


def _conv_kernel(x_ref, g1_ref, win_ref, bin_ref, dw_ref, dwb_ref, lng_ref, lnb_ref,
                 wout_ref, bout_ref, g2_ref, wr_ref, br_ref,
                 x1_ref, h2p_ref, rf_ref, cnt_ref,
                 ext_ref, conv_ref, run_ref):
    b = pl.program_id(0)
    s = pl.program_id(1)
    ts = CONV_TS

    @pl.when((b == 0) & (s == 0))
    def _():
        run_ref[...] = jnp.zeros_like(run_ref)

    @pl.when(s == 0)
    def _():
        ext_ref[0:HALO, :] = jnp.zeros((HALO, D_MODEL), jnp.float32)

    x = x_ref[...]
    h = _rms(x, g1_ref[...]).astype(jnp.bfloat16)
    u = jnp.dot(h, win_ref[...], preferred_element_type=jnp.float32) + bin_ref[...]
    glu = u[:, :D_MODEL] * jax.nn.sigmoid(u[:, D_MODEL:])
    ext_ref[HALO:HALO + ts, :] = glu

    base = HALO - (CONV_WIDTH - 1)

    def lane_chunk(c, carry):
        cols = pl.ds(pl.multiple_of(c * LANES, LANES), LANES)
        taps = [dw_ref[pl.ds(k, 1), cols] for k in range(CONV_WIDTH)]
        for r0 in range(0, ts, CONV_RC):
            acc = None
            for r in range(SUBLANES):
                rows = CONV_RC + (SUBLANES if r else 0)
                part = None
                for q in range((base + CONV_WIDTH - 1) // SUBLANES + 1):
                    k = SUBLANES * q + r - base
                    if 0 <= k < CONV_WIDTH:
                        term = ext_ref[pl.ds(r0 + SUBLANES * q, rows), cols] * taps[k]
                        part = term if part is None else part + term
                part = part[r:r + CONV_RC, :] if r else part
                acc = part if acc is None else acc + part
            conv_ref[pl.ds(r0, CONV_RC), cols] = acc
        return carry

    lax.fori_loop(0, D_MODEL // LANES, lane_chunk, 0)
    ext_ref[0:HALO, :] = ext_ref[ts:ts + HALO, :]

    v = conv_ref[...] + dwb_ref[...]
    mu = jnp.mean(v, axis=-1, keepdims=True)
    vc = v - mu
    var = jnp.mean(vc * vc, axis=-1, keepdims=True)
    y = vc * lax.rsqrt(var + EPS) * lng_ref[...] + lnb_ref[...]
    y = (y * jax.nn.sigmoid(y)).astype(jnp.bfloat16)
    mix = jnp.dot(y, wout_ref[...], preferred_element_type=jnp.float32) + bout_ref[...]
    _residual_norm_route(x + mix, g2_ref, wr_ref, br_ref, run_ref,
                         x1_ref, h2p_ref, rf_ref, cnt_ref, ts)


def _conv_layer(x, g1, w_in, b_in, dw, dw_b, ln_g, ln_b, w_out, b_out, g2, wr, br):
    ns = SEQ // CONV_TS
    tile = lambda b, s: b * ns + s
    return pl.pallas_call(
        _conv_kernel,
        grid=(BATCH, ns),
        in_specs=[
            pl.BlockSpec((CONV_TS, D_MODEL), lambda b, s: (tile(b, s), 0)),
            _full((1, D_MODEL)),
            _full((D_MODEL, 2 * D_MODEL)),
            _full((1, 2 * D_MODEL)),
            _full((HALO, D_MODEL)),
            _full((1, D_MODEL)), _full((1, D_MODEL)), _full((1, D_MODEL)),
            _full((D_MODEL, D_MODEL)),
            _full((1, D_MODEL)), _full((1, D_MODEL)),
            _full((D_MODEL, 2 * LANES)), _full((1, LANES)),
        ],
        out_specs=_route_out_specs(CONV_TS, tile),
        out_shape=_route_out_shapes(),
        scratch_shapes=[
            pltpu.VMEM((HALO + CONV_TS, D_MODEL), jnp.float32),
            pltpu.VMEM((CONV_TS, D_MODEL), jnp.float32),
            pltpu.VMEM((1, LANES), jnp.float32),
        ],
        compiler_params=_cparams(2),
        name="conv_mixer",
    )(x, g1, w_in, b_in, dw, dw_b, ln_g, ln_b, w_out, b_out, g2, wr, br)


def _qkv_kernel(x_ref, g1_ref, w_ref, qk_ref, v_ref):
    h = _rms(x_ref[...], g1_ref[...]).astype(jnp.bfloat16)
    qkv = jnp.dot(h, w_ref[...], preferred_element_type=jnp.float32)
    qk_ref[...] = qkv[:, :2 * D_MODEL]
    v_ref[...] = qkv[:, 2 * D_MODEL:].astype(jnp.bfloat16)


def _qkv(x, g1, w_qkv):
    return pl.pallas_call(
        _qkv_kernel,
        grid=(N_TOK // QKV_TS,),
        in_specs=[
            pl.BlockSpec((QKV_TS, D_MODEL), lambda i: (i, 0)),
            _full((1, D_MODEL)),
            _full((D_MODEL, 3 * D_MODEL)),
        ],
        out_specs=(
            pl.BlockSpec((QKV_TS, 2 * D_MODEL), lambda i: (i, 0)),
            pl.BlockSpec((QKV_TS, D_MODEL), lambda i: (i, 0)),
        ),
        out_shape=(
            jax.ShapeDtypeStruct((N_TOK, 2 * D_MODEL), jnp.float32),
            jax.ShapeDtypeStruct((N_TOK, D_MODEL), jnp.bfloat16),
        ),
        compiler_params=_cparams(1),
        name="qkv_proj",
    )(x, g1, w_qkv)


def _half_norm(z, gain):
    lane = lax.broadcasted_iota(jnp.int32, z.shape, 1)
    first = lane < HEAD_DIM
    zz = z * z
    ss_a = jnp.sum(jnp.where(first, zz, 0.0), axis=1, keepdims=True)
    ss_b = jnp.sum(jnp.where(first, 0.0, zz), axis=1, keepdims=True)
    inv = jnp.where(first, lax.rsqrt(ss_a * (1.0 / HEAD_DIM) + EPS),
                    lax.rsqrt(ss_b * (1.0 / HEAD_DIM) + EPS))
    return z * inv * gain


def _attn_kernel(q_ref, k_ref, v_ref, qg_ref, kg_ref, lam_ref, sg_ref, o_ref,
                 qa_ref, qb_ref, kn_ref, v1_ref, *, lambda_init):
    lp = lam_ref[...]
    lam = (jnp.exp(jnp.sum(lp[0:1, :] * lp[1:2, :], axis=1, keepdims=True))
           - jnp.exp(jnp.sum(lp[2:3, :] * lp[3:4, :], axis=1, keepdims=True))
           + lambda_init)

    kn_ref[...] = _half_norm(k_ref[...], kg_ref[...]).astype(jnp.bfloat16)
    qn = _half_norm(q_ref[...], qg_ref[...]) * (HEAD_DIM ** -0.5 * LOG2E)
    first = lax.broadcasted_iota(jnp.int32, (SEQ, V_DIM), 1) < HEAD_DIM
    qa_ref[...] = jnp.where(first, qn, 0.0).astype(jnp.bfloat16)
    qb_ref[...] = jnp.where(first, 0.0, qn).astype(jnp.bfloat16)

    ones_col = lax.broadcasted_iota(jnp.int32, (SEQ, V_DIM), 1) == 0
    v1_ref[:, 0:V_DIM] = v_ref[...]
    v1_ref[:, V_DIM:2 * V_DIM] = jnp.where(ones_col, 1.0, 0.0).astype(jnp.bfloat16)

    tq = ATT_TQ
    nt = (((1,), (1,)), ((), ()))
    visible = (lax.broadcasted_iota(jnp.int32, (tq, tq), 1) // CHUNK
               <= lax.broadcasted_iota(jnp.int32, (tq, tq), 0) // CHUNK)
    for qi in range(SEQ // tq):
        k0 = qi * tq
        rows = slice(k0, k0 + tq)

        def half_attention(qh_ref):
            q = qh_ref[rows, :]
            dg = lax.dot_general(q, kn_ref[rows, :], nt, preferred_element_type=jnp.float32)
            dg = jnp.where(visible, dg, -jnp.inf)
            m = jnp.max(dg, axis=1, keepdims=True)
            if qi:
                off = lax.dot_general(q, kn_ref[0:k0, :], nt,
                                      preferred_element_type=jnp.float32)
                m = jnp.maximum(m, jnp.max(off, axis=1, keepdims=True))
            acc = jnp.dot(jnp.exp2(dg - m).astype(jnp.bfloat16), v1_ref[rows, :],
                          preferred_element_type=jnp.float32)
            if qi:
                acc = acc + jnp.dot(jnp.exp2(off - m).astype(jnp.bfloat16), v1_ref[0:k0, :],
                                    preferred_element_type=jnp.float32)
            return acc[:, 0:V_DIM] * (1.0 / acc[:, V_DIM:V_DIM + 1])

        o = half_attention(qa_ref) - lam * half_attention(qb_ref)
        o = _rms(o, sg_ref[...]) * (1.0 - lambda_init)
        o_ref[rows, :] = o.astype(jnp.bfloat16)


def _attention(qk, v, qg2, kg2, lam_rows, subln_g, lambda_init):
    return pl.pallas_call(
        functools.partial(_attn_kernel, lambda_init=lambda_init),
        grid=(BATCH, N_HEADS),
        in_specs=[
            pl.BlockSpec((SEQ, V_DIM), lambda b, h: (b, h)),
            pl.BlockSpec((SEQ, V_DIM), lambda b, h: (b, N_HEADS + h)),
            pl.BlockSpec((SEQ, V_DIM), lambda b, h: (b, h)),
            _full((1, V_DIM)), _full((1, V_DIM)),
            _full((SUBLANES, HEAD_DIM)),
            _full((1, V_DIM)),
        ],
        out_specs=pl.BlockSpec((SEQ, V_DIM), lambda b, h: (b, h)),
        out_shape=jax.ShapeDtypeStruct((N_TOK, D_MODEL), jnp.bfloat16),
        scratch_shapes=[pltpu.VMEM((SEQ, V_DIM), jnp.bfloat16)] * 3
        + [pltpu.VMEM((SEQ, 2 * V_DIM), jnp.bfloat16)],
        compiler_params=_cparams(2),
        name="diff_attention",
    )(qk, qk, v, qg2, kg2, lam_rows, subln_g)


def _post_kernel(x_ref, m_ref, w_ref, g2_ref, wr_ref, br_ref,
                 x1_ref, h2p_ref, rf_ref, cnt_ref, run_ref):
    @pl.when(pl.program_id(0) == 0)
    def _():
        run_ref[...] = jnp.zeros_like(run_ref)

    mix = jnp.dot(m_ref[...], w_ref[...], preferred_element_type=jnp.float32)
    _residual_norm_route(x_ref[...] + mix, g2_ref, wr_ref, br_ref, run_ref,
                         x1_ref, h2p_ref, rf_ref, cnt_ref, POST_TS)


def _attn_post(x, o, w_o, g2, wr, br):
    return pl.pallas_call(
        _post_kernel,
        grid=(N_TOK // POST_TS,),
        in_specs=[
            pl.BlockSpec((POST_TS, D_MODEL), lambda i: (i, 0)),
            pl.BlockSpec((POST_TS, D_MODEL), lambda i: (i, 0)),
            _full((D_MODEL, D_MODEL)),
            _full((1, D_MODEL)),
            _full((D_MODEL, 2 * LANES)), _full((1, LANES)),
        ],
        out_specs=_route_out_specs(POST_TS, lambda i: i),
        out_shape=_route_out_shapes(),
        scratch_shapes=[pltpu.VMEM((1, LANES), jnp.float32)],
        compiler_params=_cparams(1),
        name="attn_post",
    )(x, o, w_o, g2, wr, br)


def _unrolled(n, body):
    def group(g, carry):
        for u in range(DMA_UNROLL):
            body(g * DMA_UNROLL + u)
        return carry
    lax.fori_loop(0, n // DMA_UNROLL, group, 0)


def _expert_kernel(be_ref, nv_ref, dest_ref, pad_ref, h2p_ref, w1_ref, w3_ref, w2_ref, yb_ref,
                   rowtok_ref, xbuf_ref, sem, w1b_ref, w3b_ref, w2b_ref):
    b = pl.program_id(0)
    n_valid = nv_ref[0]

    def row_copy(block, slot, r):
        tok = rowtok_ref[block * ROW_BLOCK + r]
        return pltpu.make_async_copy(
            h2p_ref.at[pl.ds(pl.multiple_of(tok * PACK, PACK), PACK)],
            xbuf_ref.at[slot, pl.ds(r * PACK, PACK)],
            sem.at[slot])

    def block_arrival(slot):
        return pltpu.make_async_copy(
            h2p_ref.at[pl.ds(0, ROW_BLOCK * PACK)], xbuf_ref.at[slot], sem.at[slot])

    @pl.when(b == 0)
    def _():
        def clear_padding(e, carry):
            def clear(j, c):
                rowtok_ref[j] = 0
                return c
            return lax.fori_loop(pad_ref[e], pad_ref[N_EXPERTS + e], clear, carry)
        lax.fori_loop(0, N_EXPERTS, clear_padding, 0)

        def place(t):
            rowtok_ref[dest_ref[TOP_K * t]] = t
            rowtok_ref[dest_ref[TOP_K * t + 1]] = t
        _unrolled(N_TOK, place)
        second = jnp.minimum(1, n_valid - 1)
        _unrolled(ROW_BLOCK, lambda r: row_copy(0, 0, r).start(priority=ROW_DMA_PRIORITY))
        _unrolled(ROW_BLOCK, lambda r: row_copy(second, 1, r).start(priority=ROW_DMA_PRIORITY))

    prev = be_ref[jnp.maximum(b - 1, 0)]
    fresh = (b == 0) | (be_ref[b] != prev)

    @pl.when(fresh)
    def _():
        w1b_ref[...] = w1_ref[...].astype(jnp.bfloat16)
        w3b_ref[...] = w3_ref[...].astype(jnp.bfloat16)
        w2b_ref[...] = w2_ref[...].astype(jnp.bfloat16)

    @pl.when(b < n_valid)
    def _():
        slot = lax.rem(b, GATHER_SLOTS)
        block_arrival(slot).wait()
        x = _unpack_rows(xbuf_ref.at[slot], ROW_BLOCK).astype(jnp.bfloat16)
        ahead = jnp.minimum(b + 2, n_valid - 1)
        ahead_slot = lax.rem(b + 2, GATHER_SLOTS)
        for r in range(ROW_BLOCK):
            row_copy(ahead, ahead_slot, r).start(priority=ROW_DMA_PRIORITY)
        h1 = jnp.dot(x, w1b_ref[...], preferred_element_type=jnp.float32)
        h3 = jnp.dot(x, w3b_ref[...], preferred_element_type=jnp.float32)
        act = (h1 * jax.nn.sigmoid(h1) * h3).astype(jnp.bfloat16)
        y = jnp.dot(act, w2b_ref[...], preferred_element_type=jnp.float32)
        _pack_rows(yb_ref, y, ROW_BLOCK)

    @pl.when(b == n_valid - 1)
    def _():
        block_arrival(lax.rem(b + 1, GATHER_SLOTS)).wait()
        block_arrival(lax.rem(b + 2, GATHER_SLOTS)).wait()

    @pl.when(b >= n_valid)
    def _():
        yb_ref[...] = jnp.zeros_like(yb_ref)


def _experts(layer, block_e, n_valid, dest_flat, pad_rows, h2p, w1, w3, w2):
    w_in = pl.BlockSpec((None, None, D_MODEL, D_EXPERT),
                        lambda b, be, *_: (layer, be[b], 0, 0))
    w_out = pl.BlockSpec((None, None, D_EXPERT, D_MODEL),
                         lambda b, be, *_: (layer, be[b], 0, 0))
    return pl.pallas_call(
        _expert_kernel,
        grid_spec=pltpu.PrefetchScalarGridSpec(
            num_scalar_prefetch=4,
            grid=(N_BLOCKS,),
            in_specs=[pl.BlockSpec(memory_space=pl.ANY), w_in, w_in, w_out],
            out_specs=pl.BlockSpec((ROW_BLOCK * PACK, LANES), lambda b, *_: (b, 0)),
            scratch_shapes=[
                pltpu.SMEM((N_ROWS,), jnp.int32),
                pltpu.VMEM((GATHER_SLOTS, ROW_BLOCK * PACK, LANES), jnp.float32),
                pltpu.SemaphoreType.DMA((GATHER_SLOTS,)),
                pltpu.VMEM((D_MODEL, D_EXPERT), jnp.bfloat16),
                pltpu.VMEM((D_MODEL, D_EXPERT), jnp.bfloat16),
                pltpu.VMEM((D_EXPERT, D_MODEL), jnp.bfloat16),
            ],
        ),
        out_shape=jax.ShapeDtypeStruct((N_ROWS * PACK, LANES), jnp.float32),
        compiler_params=_cparams(1),
        name="moe_experts",
    )(block_e, n_valid, dest_flat, pad_rows, h2p, w1, w3, w2)


def _combine_kernel(pos_ref, x1_ref, rf_ref, yb_ref, out_ref, buf_ref, sem):
    i = pl.program_id(0)
    n = pl.num_programs(0)
    tm = COMB_TM

    def copy(tile, slot, j, k):
        p = pos_ref[TOP_K * (tile * tm + j) + k]
        return pltpu.make_async_copy(
            yb_ref.at[pl.ds(pl.multiple_of(p * PACK, PACK), PACK)],
            buf_ref.at[slot, k, pl.ds(pl.multiple_of(j * PACK, PACK), PACK)],
            sem.at[slot])

    def issue(tile, slot):
        def body(j):
            copy(tile, slot, j, 0).start(priority=ROW_DMA_PRIORITY)
            copy(tile, slot, j, 1).start(priority=ROW_DMA_PRIORITY)
        _unrolled(tm, body)

    def drain(tile, slot):
        def body(j):
            copy(tile, slot, j, 0).wait()
            copy(tile, slot, j, 1).wait()
        _unrolled(tm, body)

    slot = i % 2

    @pl.when(i == 0)
    def _():
        issue(0, 0)

    @pl.when(i + 1 < n)
    def _():
        issue(i + 1, 1 - slot)

    drain(i, slot)
    rf = rf_ref[...]
    y0 = _unpack_rows(buf_ref.at[slot, 0], tm)
    y1 = _unpack_rows(buf_ref.at[slot, 1], tm)
    out_ref[...] = x1_ref[...] + rf[:, 0:1] * y0 + rf[:, 1:2] * y1


def _combine(pos_flat, x1, rf, yb):
    return pl.pallas_call(
        _combine_kernel,
        grid_spec=pltpu.PrefetchScalarGridSpec(
            num_scalar_prefetch=1,
            grid=(N_TOK // COMB_TM,),
            in_specs=[
                pl.BlockSpec((COMB_TM, D_MODEL), lambda i, p: (i, 0)),
                pl.BlockSpec((COMB_TM, LANES), lambda i, p: (i, 0)),
                pl.BlockSpec(memory_space=pl.ANY),
            ],
            out_specs=pl.BlockSpec((COMB_TM, D_MODEL), lambda i, p: (i, 0)),
            scratch_shapes=[
                pltpu.VMEM((2, TOP_K, COMB_TM * PACK, LANES), jnp.float32),
                pltpu.SemaphoreType.DMA((2,)),
            ],
        ),
        out_shape=jax.ShapeDtypeStruct((N_TOK, D_MODEL), jnp.float32),
        compiler_params=_cparams(1),
        name="moe_combine",
    )(pos_flat, x1, rf, yb)


def _moe(layer, x1, h2p, rf, cnt, w1, w3, w2):
    experts = rf[:, 2:4].astype(jnp.int32)
    rank = rf[:, 4:6].astype(jnp.int32)
    counts = cnt[0, :N_EXPERTS].astype(jnp.int32)
    padded = (counts + ROW_BLOCK - 1) // ROW_BLOCK * ROW_BLOCK
    pad_ends = jnp.cumsum(padded)
    pad_starts = pad_ends - padded
    is_e = experts[:, :, None] == jnp.arange(N_EXPERTS, dtype=jnp.int32)
    dest = (jnp.sum(jnp.where(is_e, pad_starts, 0), axis=-1) + rank).reshape(-1)
    n_valid = (pad_ends[-1:] // ROW_BLOCK).astype(jnp.int32)
    block_row0 = jnp.arange(N_BLOCKS, dtype=jnp.int32) * ROW_BLOCK
    block_e = jnp.minimum(
        jnp.sum((pad_ends[None, :] <= block_row0[:, None]).astype(jnp.int32), axis=1),
        N_EXPERTS - 1).astype(jnp.int32)
    pad_rows = jnp.concatenate([pad_starts + counts, pad_ends]).astype(jnp.int32)
    yb = _experts(layer, block_e, n_valid, dest, pad_rows, h2p, w1, w3, w2)
    return _combine(dest, x1, rf, yb)


def _router_params(w_grp, b_grp, w_exp, b_exp):
    wr = jnp.zeros((D_MODEL, LANES), jnp.float32)
    wr = wr.at[:, :N_EXPERTS].set(w_exp).at[:, GRP_LANE0:GRP_LANE0 + N_GROUPS].set(w_grp)
    br = jnp.zeros((1, LANES), jnp.float32)
    br = br.at[0, :N_EXPERTS].set(b_exp).at[0, GRP_LANE0:GRP_LANE0 + N_GROUPS].set(b_grp)
    w_hi = wr.astype(jnp.bfloat16)
    w_lo = (wr - w_hi.astype(jnp.float32)).astype(jnp.bfloat16)
    return jnp.concatenate([w_hi, w_lo], axis=1), br


def kernel(x, norm1_g, norm2_g, conv_w_in, conv_b_in, conv_dw, conv_dw_b, conv_ln_g, conv_ln_b, conv_w_out, conv_b_out, attn_w_qkv, attn_q_g, attn_k_g, attn_lq1, attn_lk1, attn_lq2, attn_lk2, attn_subln_g, attn_w_o, moe_w_grp, moe_b_grp, moe_w_exp, moe_b_exp, moe_w1, moe_w3, moe_w2):
    bf16 = jnp.bfloat16
    row = lambda a: a.reshape(1, -1)
    xf = x.reshape(N_TOK, D_MODEL)

    wr, br = _router_params(moe_w_grp[0], moe_b_grp[0], moe_w_exp[0], moe_b_exp[0])
    dw = jnp.zeros((HALO, D_MODEL), jnp.float32).at[:CONV_WIDTH].set(conv_dw[0])
    x1, h2p, rf, cnt = _conv_layer(
        xf, row(norm1_g[0]), conv_w_in[0].astype(bf16), row(conv_b_in[0]), dw,
        row(conv_dw_b[0]), row(conv_ln_g[0]), row(conv_ln_b[0]),
        conv_w_out[0].astype(bf16), row(conv_b_out[0]), row(norm2_g[0]), wr, br)
    xf = _moe(0, x1, h2p, rf, cnt, moe_w1, moe_w3, moe_w2)

    lambda_init = 0.8 - 0.6 * math.exp(-0.3 * 1)
    wr, br = _router_params(moe_w_grp[1], moe_b_grp[1], moe_w_exp[1], moe_b_exp[1])
    qk, v = _qkv(xf, row(norm1_g[1]), attn_w_qkv[0].astype(bf16))
    lam_rows = jnp.zeros((SUBLANES, HEAD_DIM), jnp.float32)
    lam_rows = lam_rows.at[0].set(attn_lq1[0]).at[1].set(attn_lk1[0])
    lam_rows = lam_rows.at[2].set(attn_lq2[0]).at[3].set(attn_lk2[0])
    two = lambda g: jnp.concatenate([g, g]).reshape(1, V_DIM)
    o = _attention(qk, v, two(attn_q_g[0]), two(attn_k_g[0]), lam_rows,
                   row(attn_subln_g[0]), lambda_init)
    x1, h2p, rf, cnt = _attn_post(xf, o, attn_w_o[0].astype(bf16), row(norm2_g[1]), wr, br)
    xf = _moe(1, x1, h2p, rf, cnt, moe_w1, moe_w3, moe_w2)
    return xf.reshape(BATCH, SEQ, D_MODEL)
```

```python
import functools
import math

import jax
import jax.numpy as jnp
from jax import lax
from jax.experimental import pallas as pl
from jax.experimental.pallas import tpu as pltpu

D_MODEL = 1024
BATCH = 8
SEQ = 2048
N_TOK = BATCH * SEQ
CHUNK = 64
CONV_WIDTH = 31
N_HEADS = 8
HEAD_DIM = 64
V_DIM = 128
N_GROUPS = 4
EXPERTS_PER_GROUP = 8
N_EXPERTS = 32
TOP_K = 2
D_EXPERT = 512
EPS = 1e-6
LOG2E = math.log2(math.e)

LANES = 128
SUBLANES = 8
PACK = D_MODEL // LANES
VMEM_LIMIT = 56 * 1024 * 1024

CONV_TS = 512
HALO = 32
CONV_RC = 64
POST_TS = 512
QKV_TS = 512
ATT_TQ = 256
ROW_BLOCK = 256
GATHER_SLOTS = 3
N_ASSIGN = N_TOK * TOP_K
N_BLOCKS = N_ASSIGN // ROW_BLOCK + N_EXPERTS
N_ROWS = N_BLOCKS * ROW_BLOCK
COMB_TM = 256
DMA_UNROLL = 8
GRP_LANE0 = N_EXPERTS


def _cparams(n_axes):
    return pltpu.CompilerParams(
        dimension_semantics=("arbitrary",) * n_axes, vmem_limit_bytes=VMEM_LIMIT)


def _rms(x, g):
    return x * lax.rsqrt(jnp.mean(x * x, axis=-1, keepdims=True) + EPS) * g


def _pack_rows(ref, val, rows):
    for j in range(PACK):
        ref[pl.ds(j, rows, stride=PACK), :] = val[:, j * LANES:(j + 1) * LANES]


def _unpack_rows(ref, rows):
    return jnp.concatenate(
        [ref[pl.ds(j, rows, stride=PACK), :] for j in range(PACK)], axis=-1)


def _residual_norm_route(x1, g2_ref, wr_ref, br_ref, run_ref,
                         x1_ref, h2p_ref, rf_ref, cnt_ref, rows):
    x1_ref[...] = x1
    h2 = _rms(x1, g2_ref[...])
    _pack_rows(h2p_ref, h2, rows)

    hi = h2.astype(jnp.bfloat16)
    lo = (h2 - hi.astype(jnp.float32)).astype(jnp.bfloat16)
    ab = jnp.dot(hi, wr_ref[...], preferred_element_type=jnp.float32)
    c = jnp.dot(lo, wr_ref[:, 0:LANES], preferred_element_type=jnp.float32)
    logits = ab[:, 0:LANES] + ab[:, LANES:2 * LANES] + c + br_ref[...]
    lane = lax.broadcasted_iota(jnp.int32, (rows, LANES), 1)
    lane_f = lane.astype(jnp.float32)
    neg = jnp.float32(-jnp.inf)
    big = jnp.float32(1e9)

    gmask = (lane >= GRP_LANE0) & (lane < GRP_LANE0 + N_GROUPS)
    gl = jnp.where(gmask, logits, neg)
    gmax = jnp.max(gl, axis=1, keepdims=True)
    gidx = jnp.min(jnp.where(gl == gmax, lane_f, big), axis=1, keepdims=True) - GRP_LANE0
    gsum = jnp.sum(jnp.where(gmask, jnp.exp(gl - gmax), 0.0), axis=1, keepdims=True)
    grp_p = 1.0 / gsum

    lane0 = gidx * EXPERTS_PER_GROUP
    emask = (lane_f >= lane0) & (lane_f < lane0 + EXPERTS_PER_GROUP)
    el = jnp.where(emask, logits, neg)
    m1 = jnp.max(el, axis=1, keepdims=True)
    i1 = jnp.min(jnp.where(el == m1, lane_f, big), axis=1, keepdims=True)
    el2 = jnp.where(lane_f == i1, neg, el)
    m2 = jnp.max(el2, axis=1, keepdims=True)
    i2 = jnp.min(jnp.where(el2 == m2, lane_f, big), axis=1, keepdims=True)
    t = jnp.exp(m2 - m1)
    inv = 1.0 / (1.0 + t)
    g_first = grp_p * inv
    g_second = grp_p * t * inv

    sel1 = lane_f == i1
    sel2 = lane_f == i2
    member = jnp.where(sel1 | sel2, 1.0, 0.0).astype(jnp.bfloat16)
    r_i = lax.broadcasted_iota(jnp.int32, (rows, rows), 0)
    c_i = lax.broadcasted_iota(jnp.int32, (rows, rows), 1)
    tri = jnp.where(c_i < r_i, 1.0, 0.0).astype(jnp.bfloat16)
    cum = jnp.dot(tri, member, preferred_element_type=jnp.float32) + run_ref[...]
    rank1 = jnp.sum(jnp.where(sel1, cum, 0.0), axis=1, keepdims=True)
    rank2 = jnp.sum(jnp.where(sel2, cum, 0.0), axis=1, keepdims=True)
    run_new = run_ref[...] + jnp.sum(member.astype(jnp.float32), axis=0, keepdims=True)
    run_ref[...] = run_new
    cnt_ref[...] = jnp.broadcast_to(run_new, (SUBLANES, LANES))

    out = jnp.where(lane == 0, g_first, 0.0)
    out = jnp.where(lane == 1, g_second, out)
    out = jnp.where(lane == 2, i1, out)
    out = jnp.where(lane == 3, i2, out)
    out = jnp.where(lane == 4, rank1, out)
    out = jnp.where(lane == 5, rank2, out)
    rf_ref[...] = out


def _route_out_shapes():
    return (
        jax.ShapeDtypeStruct((N_TOK, D_MODEL), jnp.float32),
        jax.ShapeDtypeStruct((N_TOK * PACK, LANES), jnp.float32),
        jax.ShapeDtypeStruct((N_TOK, LANES), jnp.float32),
        jax.ShapeDtypeStruct((SUBLANES, LANES), jnp.float32),
    )


def _route_out_specs(ts, idx):
    return (
        pl.BlockSpec((ts, D_MODEL), lambda *a: (idx(*a), 0)),
        pl.BlockSpec((ts * PACK, LANES), lambda *a: (idx(*a), 0)),
        pl.BlockSpec((ts, LANES), lambda *a: (idx(*a), 0)),
        pl.BlockSpec((SUBLANES, LANES), lambda *a: (0, 0)),
    )


def _full(shape):
    return pl.BlockSpec(shape, lambda *a: (0,) * len(shape))


def _conv_kernel(x_ref, g1_ref, win_ref, bin_ref, dw_ref, dwb_ref, lng_ref, lnb_ref,
                 wout_ref, bout_ref, g2_ref, wr_ref, br_ref,
                 x1_ref, h2p_ref, rf_ref, cnt_ref,
                 ext_ref, conv_ref, run_ref):
    b = pl.program_id(0)
    s = pl.program_id(1)
    ts = CONV_TS

    @pl.when((b == 0) & (s == 0))
    def _():
        run_ref[...] = jnp.zeros_like(run_ref)

    @pl.when(s == 0)
    def _():
        ext_ref[0:HALO, :] = jnp.zeros((HALO, D_MODEL), jnp.float32)

    x = x_ref[...]
    h = _rms(x, g1_ref[...]).astype(jnp.bfloat16)
    u = jnp.dot(h, win_ref[...], preferred_element_type=jnp.float32) + bin_ref[...]
    glu = u[:, :D_MODEL] * jax.nn.sigmoid(u[:, D_MODEL:])
    ext_ref[HALO:HALO + ts, :] = glu

    base = HALO - (CONV_WIDTH - 1)

    def lane_chunk(c, carry):
        cols = pl.ds(pl.multiple_of(c * LANES, LANES), LANES)
        taps = [dw_ref[pl.ds(k, 1), cols] for k in range(CONV_WIDTH)]
        for r0 in range(0, ts, CONV_RC):
            acc = None
            for r in range(SUBLANES):
                rows = CONV_RC + (SUBLANES if r else 0)
                part = None
                for q in range((base + CONV_WIDTH - 1) // SUBLANES + 1):
                    k = SUBLANES * q + r - base
                    if 0 <= k < CONV_WIDTH:
                        term = ext_ref[pl.ds(r0 + SUBLANES * q, rows), cols] * taps[k]
                        part = term if part is None else part + term
                part = part[r:r + CONV_RC, :] if r else part
                acc = part if acc is None else acc + part
            conv_ref[pl.ds(r0, CONV_RC), cols] = acc
        return carry

    lax.fori_loop(0, D_MODEL // LANES, lane_chunk, 0)
    ext_ref[0:HALO, :] = ext_ref[ts:ts + HALO, :]

    v = conv_ref[...] + dwb_ref[...]
    mu = jnp.mean(v, axis=-1, keepdims=True)
    vc = v - mu
    var = jnp.mean(vc * vc, axis=-1, keepdims=True)
    y = vc * lax.rsqrt(var + EPS) * lng_ref[...] + lnb_ref[...]
    y = (y * jax.nn.sigmoid(y)).astype(jnp.bfloat16)
    mix = jnp.dot(y, wout_ref[...], preferred_element_type=jnp.float32) + bout_ref[...]
    _residual_norm_route(x + mix, g2_ref, wr_ref, br_ref, run_ref,
                         x1_ref, h2p_ref, rf_ref, cnt_ref, ts)


def _conv_layer(x, g1, w_in, b_in, dw, dw_b, ln_g, ln_b, w_out, b_out, g2, wr, br):
    ns = SEQ // CONV_TS
    tile = lambda b, s: b * ns + s
    return pl.pallas_call(
        _conv_kernel,
        grid=(BATCH, ns),
        in_specs=[
            pl.BlockSpec((CONV_TS, D_MODEL), lambda b, s: (tile(b, s), 0)),
            _full((1, D_MODEL)),
            _full((D_MODEL, 2 * D_MODEL)),
            _full((1, 2 * D_MODEL)),
            _full((HALO, D_MODEL)),
            _full((1, D_MODEL)), _full((1, D_MODEL)), _full((1, D_MODEL)),
            _full((D_MODEL, D_MODEL)),
            _full((1, D_MODEL)), _full((1, D_MODEL)),
            _full((D_MODEL, 2 * LANES)), _full((1, LANES)),
        ],
        out_specs=_route_out_specs(CONV_TS, tile),
        out_shape=_route_out_shapes(),
        scratch_shapes=[
            pltpu.VMEM((HALO + CONV_TS, D_MODEL), jnp.float32),
            pltpu.VMEM((CONV_TS, D_MODEL), jnp.float32),
            pltpu.VMEM((1, LANES), jnp.float32),
        ],
        compiler_params=_cparams(2),
        name="conv_mixer",
    )(x, g1, w_in, b_in, dw, dw_b, ln_g, ln_b, w_out, b_out, g2, wr, br)


def _qkv_kernel(x_ref, g1_ref, w_ref, qk_ref, v_ref):
    h = _rms(x_ref[...], g1_ref[...]).astype(jnp.bfloat16)
    qkv = jnp.dot(h, w_ref[...], preferred_element_type=jnp.float32)
    qk_ref[...] = qkv[:, :2 * D_MODEL]
    v_ref[...] = qkv[:, 2 * D_MODEL:].astype(jnp.bfloat16)


def _qkv(x, g1, w_qkv):
    return pl.pallas_call(
        _qkv_kernel,
        grid=(N_TOK // QKV_TS,),
        in_specs=[
            pl.BlockSpec((QKV_TS, D_MODEL), lambda i: (i, 0)),
            _full((1, D_MODEL)),
            _full((D_MODEL, 3 * D_MODEL)),
        ],
        out_specs=(
            pl.BlockSpec((QKV_TS, 2 * D_MODEL), lambda i: (i, 0)),
            pl.BlockSpec((QKV_TS, D_MODEL), lambda i: (i, 0)),
        ),
        out_shape=(
            jax.ShapeDtypeStruct((N_TOK, 2 * D_MODEL), jnp.float32),
            jax.ShapeDtypeStruct((N_TOK, D_MODEL), jnp.bfloat16),
        ),
        compiler_params=_cparams(1),
        name="qkv_proj",
    )(x, g1, w_qkv)


def _half_norm(z, gain):
    lane = lax.broadcasted_iota(jnp.int32, z.shape, 1)
    first = lane < HEAD_DIM
    zz = z * z
    ss_a = jnp.sum(jnp.where(first, zz, 0.0), axis=1, keepdims=True)
    ss_b = jnp.sum(jnp.where(first, 0.0, zz), axis=1, keepdims=True)
    inv = jnp.where(first, lax.rsqrt(ss_a * (1.0 / HEAD_DIM) + EPS),
                    lax.rsqrt(ss_b * (1.0 / HEAD_DIM) + EPS))
    return z * inv * gain


def _attn_kernel(q_ref, k_ref, v_ref, qg_ref, kg_ref, lam_ref, sg_ref, o_ref,
                 qa_ref, qb_ref, kn_ref, v1_ref, s_ref, e_ref, part_ref, *, lambda_init):
    lp = lam_ref[...]
    lam = (jnp.exp(jnp.sum(lp[0:1, :] * lp[1:2, :], axis=1, keepdims=True))
           - jnp.exp(jnp.sum(lp[2:3, :] * lp[3:4, :], axis=1, keepdims=True))
           + lambda_init)

    kn_ref[...] = _half_norm(k_ref[...], kg_ref[...]).astype(jnp.bfloat16)
    qn = _half_norm(q_ref[...], qg_ref[...]) * (HEAD_DIM ** -0.5 * LOG2E)
    first = lax.broadcasted_iota(jnp.int32, (SEQ, V_DIM), 1) < HEAD_DIM
    qa_ref[...] = jnp.where(first, qn, 0.0).astype(jnp.bfloat16)
    qb_ref[...] = jnp.where(first, 0.0, qn).astype(jnp.bfloat16)

    ones_col = lax.broadcasted_iota(jnp.int32, (SEQ, V_DIM), 1) == 0
    v1_ref[:, 0:V_DIM] = v_ref[...]
    v1_ref[:, V_DIM:2 * V_DIM] = jnp.where(ones_col, 1.0, 0.0).astype(jnp.bfloat16)

    tq = ATT_TQ
    nt = (((1,), (1,)), ((), ()))
    visible = (lax.broadcasted_iota(jnp.int32, (tq, tq), 1) // CHUNK
               <= lax.broadcasted_iota(jnp.int32, (tq, tq), 0) // CHUNK)
    halves = (qa_ref, qb_ref)
    units = [(qi, h) for qi in range(SEQ // tq) for h in range(2)]

    def stage_scores(u, slot):
        qi, h = units[u]
        k0 = qi * tq
        q = halves[h][k0:k0 + tq, :]
        dg = lax.dot_general(q, kn_ref[k0:k0 + tq, :], nt, preferred_element_type=jnp.float32)
        s_ref[slot, :, k0:k0 + tq] = jnp.where(visible, dg, -jnp.inf)
        if qi:
            s_ref[slot, :, 0:k0] = lax.dot_general(q, kn_ref[0:k0, :], nt,
                                                   preferred_element_type=jnp.float32)

    def stage_numerators(u, slot):
        kend = (units[u][0] + 1) * tq
        sc = s_ref[slot, :, 0:kend]
        m = jnp.max(sc, axis=1, keepdims=True)
        e_ref[slot, :, 0:kend] = jnp.exp2(sc - m).astype(jnp.bfloat16)

    def stage_values(u, slot):
        qi, h = units[u]
        k0 = qi * tq
        kend = k0 + tq
        acc = jnp.dot(e_ref[slot, :, 0:kend], v1_ref[0:kend, :],
                      preferred_element_type=jnp.float32)
        attn = acc[:, 0:V_DIM] * (1.0 / acc[:, V_DIM:V_DIM + 1])
        if h == 0:
            part_ref[...] = attn
        else:
            o = part_ref[...] - lam * attn
            o = _rms(o, sg_ref[...]) * (1.0 - lambda_init)
            o_ref[k0:kend, :] = o.astype(jnp.bfloat16)

    for step in range(len(units) + 2):
        if step < len(units):
            stage_scores(step, step % 2)
        if 1 <= step <= len(units):
            stage_numerators(step - 1, (step - 1) % 2)
        if step >= 2:
            stage_values(step - 2, step % 2)


def _attention(qk, v, qg2, kg2, lam_rows, subln_g, lambda_init):
    return pl.pallas_call(
        functools.partial(_attn_kernel, lambda_init=lambda_init),
        grid=(BATCH, N_HEADS),
        in_specs=[
            pl.BlockSpec((SEQ, V_DIM), lambda b, h: (b, h)),
            pl.BlockSpec((SEQ, V_DIM), lambda b, h: (b, N_HEADS + h)),
            pl.BlockSpec((SEQ, V_DIM), lambda b, h: (b, h)),
            _full((1, V_DIM)), _full((1, V_DIM)),
            _full((SUBLANES, HEAD_DIM)),
            _full((1, V_DIM)),
        ],
        out_specs=pl.BlockSpec((SEQ, V_DIM), lambda b, h: (b, h)),
        out_shape=jax.ShapeDtypeStruct((N_TOK, D_MODEL), jnp.bfloat16),
        scratch_shapes=[pltpu.VMEM((SEQ, V_DIM), jnp.bfloat16)] * 3 + [
            pltpu.VMEM((SEQ, 2 * V_DIM), jnp.bfloat16),
            pltpu.VMEM((2, ATT_TQ, SEQ), jnp.float32),
            pltpu.VMEM((2, ATT_TQ, SEQ), jnp.bfloat16),
            pltpu.VMEM((ATT_TQ, V_DIM), jnp.float32),
        ],
        compiler_params=_cparams(2),
        name="diff_attention",
    )(qk, qk, v, qg2, kg2, lam_rows, subln_g)


def _post_kernel(x_ref, m_ref, w_ref, g2_ref, wr_ref, br_ref,
                 x1_ref, h2p_ref, rf_ref, cnt_ref, run_ref):
    @pl.when(pl.program_id(0) == 0)
    def _():
        run_ref[...] = jnp.zeros_like(run_ref)

    mix = jnp.dot(m_ref[...], w_ref[...], preferred_element_type=jnp.float32)
    _residual_norm_route(x_ref[...] + mix, g2_ref, wr_ref, br_ref, run_ref,
                         x1_ref, h2p_ref, rf_ref, cnt_ref, POST_TS)


def _attn_post(x, o, w_o, g2, wr, br):
    return pl.pallas_call(
        _post_kernel,
        grid=(N_TOK // POST_TS,),
        in_specs=[
            pl.BlockSpec((POST_TS, D_MODEL), lambda i: (i, 0)),
            pl.BlockSpec((POST_TS, D_MODEL), lambda i: (i, 0)),
            _full((D_MODEL, D_MODEL)),
            _full((1, D_MODEL)),
            _full((D_MODEL, 2 * LANES)), _full((1, LANES)),
        ],
        out_specs=_route_out_specs(POST_TS, lambda i: i),
        out_shape=_route_out_shapes(),
        scratch_shapes=[pltpu.VMEM((1, LANES), jnp.float32)],
        compiler_params=_cparams(1),
        name="attn_post",
    )(x, o, w_o, g2, wr, br)


def _unrolled(n, body):
    def group(g, carry):
        for u in range(DMA_UNROLL):
            body(g * DMA_UNROLL + u)
        return carry
    lax.fori_loop(0, n // DMA_UNROLL, group, 0)


def _expert_kernel(be_ref, nv_ref, dest_ref, pad_ref, h2p_ref, w1_ref, w3_ref, w2_ref, yb_ref,
                   rowtok_ref, xbuf_ref, sem, w1b_ref, w3b_ref, w2b_ref):
    b = pl.program_id(0)
    n_valid = nv_ref[0]

    def row_copy(block, slot, r):
        tok = rowtok_ref[block * ROW_BLOCK + r]
        return pltpu.make_async_copy(
            h2p_ref.at[pl.ds(pl.multiple_of(tok * PACK, PACK), PACK)],
            xbuf_ref.at[slot, pl.ds(r * PACK, PACK)],
            sem.at[slot])

    def block_arrival(slot):
        return pltpu.make_async_copy(
            h2p_ref.at[pl.ds(0, ROW_BLOCK * PACK)], xbuf_ref.at[slot], sem.at[slot])

    @pl.when(b == 0)
    def _():
        def clear_padding(e, carry):
            def clear(j, c):
                rowtok_ref[j] = 0
                return c
            return lax.fori_loop(pad_ref[e], pad_ref[N_EXPERTS + e], clear, carry)
        lax.fori_loop(0, N_EXPERTS, clear_padding, 0)

        def place(t):
            rowtok_ref[dest_ref[TOP_K * t]] = t
            rowtok_ref[dest_ref[TOP_K * t + 1]] = t
        _unrolled(N_TOK, place)
        second = jnp.minimum(1, n_valid - 1)
        _unrolled(ROW_BLOCK, lambda r: row_copy(0, 0, r).start())
        _unrolled(ROW_BLOCK, lambda r: row_copy(second, 1, r).start())

    prev = be_ref[jnp.maximum(b - 1, 0)]
    fresh = (b == 0) | (be_ref[b] != prev)

    @pl.when(fresh)
    def _():
        w1b_ref[...] = w1_ref[...].astype(jnp.bfloat16)
        w3b_ref[...] = w3_ref[...].astype(jnp.bfloat16)
        w2b_ref[...] = w2_ref[...].astype(jnp.bfloat16)

    @pl.when(b < n_valid)
    def _():
        slot = lax.rem(b, GATHER_SLOTS)
        block_arrival(slot).wait()
        x = _unpack_rows(xbuf_ref.at[slot], ROW_BLOCK).astype(jnp.bfloat16)
        ahead = jnp.minimum(b + 2, n_valid - 1)
        ahead_slot = lax.rem(b + 2, GATHER_SLOTS)
        for r in range(ROW_BLOCK):
            row_copy(ahead, ahead_slot, r).start()
        h1 = jnp.dot(x, w1b_ref[...], preferred_element_type=jnp.float32)
        h3 = jnp.dot(x, w3b_ref[...], preferred_element_type=jnp.float32)
        act = (h1 * jax.nn.sigmoid(h1) * h3).astype(jnp.bfloat16)
        y = jnp.dot(act, w2b_ref[...], preferred_element_type=jnp.float32)
        _pack_rows(yb_ref, y, ROW_BLOCK)

    @pl.when(b == n_valid - 1)
    def _():
        block_arrival(lax.rem(b + 1, GATHER_SLOTS)).wait()
        block_arrival(lax.rem(b + 2, GATHER_SLOTS)).wait()

    @pl.when(b >= n_valid)
    def _():
        yb_ref[...] = jnp.zeros_like(yb_ref)


def _experts(layer, block_e, n_valid, dest_flat, pad_rows, h2p, w1, w3, w2):
    w_in = pl.BlockSpec((None, None, D_MODEL, D_EXPERT),
                        lambda b, be, *_: (layer, be[b], 0, 0))
    w_out = pl.BlockSpec((None, None, D_EXPERT, D_MODEL),
                         lambda b, be, *_: (layer, be[b], 0, 0))
    return pl.pallas_call(
        _expert_kernel,
        grid_spec=pltpu.PrefetchScalarGridSpec(
            num_scalar_prefetch=4,
            grid=(N_BLOCKS,),
            in_specs=[pl.BlockSpec(memory_space=pl.ANY), w_in, w_in, w_out],
            out_specs=pl.BlockSpec((ROW_BLOCK * PACK, LANES), lambda b, *_: (b, 0)),
            scratch_shapes=[
                pltpu.SMEM((N_ROWS,), jnp.int32),
                pltpu.VMEM((GATHER_SLOTS, ROW_BLOCK * PACK, LANES), jnp.float32),
                pltpu.SemaphoreType.DMA((GATHER_SLOTS,)),
                pltpu.VMEM((D_MODEL, D_EXPERT), jnp.bfloat16),
                pltpu.VMEM((D_MODEL, D_EXPERT), jnp.bfloat16),
                pltpu.VMEM((D_EXPERT, D_MODEL), jnp.bfloat16),
            ],
        ),
        out_shape=jax.ShapeDtypeStruct((N_ROWS * PACK, LANES), jnp.float32),
        compiler_params=_cparams(1),
        name="moe_experts",
    )(block_e, n_valid, dest_flat, pad_rows, h2p, w1, w3, w2)


def _combine_kernel(pos_ref, x1_ref, rf_ref, yb_ref, out_ref, buf_ref, sem):
    i = pl.program_id(0)
    n = pl.num_programs(0)
    tm = COMB_TM

    def copy(tile, slot, j, k):
        p = pos_ref[TOP_K * (tile * tm + j) + k]
        return pltpu.make_async_copy(
            yb_ref.at[pl.ds(pl.multiple_of(p * PACK, PACK), PACK)],
            buf_ref.at[slot, k, pl.ds(pl.multiple_of(j * PACK, PACK), PACK)],
            sem.at[slot])

    def issue(tile, slot):
        def body(j):
            copy(tile, slot, j, 0).start()
            copy(tile, slot, j, 1).start()
        _unrolled(tm, body)

    def drain(tile, slot):
        def body(j):
            copy(tile, slot, j, 0).wait()
            copy(tile, slot, j, 1).wait()
        _unrolled(tm, body)

    slot = i % 2

    @pl.when(i == 0)
    def _():
        issue(0, 0)

    @pl.when(i + 1 < n)
    def _():
        issue(i + 1, 1 - slot)

    drain(i, slot)
    rf = rf_ref[...]
    y0 = _unpack_rows(buf_ref.at[slot, 0], tm)
    y1 = _unpack_rows(buf_ref.at[slot, 1], tm)
    out_ref[...] = x1_ref[...] + rf[:, 0:1] * y0 + rf[:, 1:2] * y1


def _combine(pos_flat, x1, rf, yb):
    return pl.pallas_call(
        _combine_kernel,
        grid_spec=pltpu.PrefetchScalarGridSpec(
            num_scalar_prefetch=1,
            grid=(N_TOK // COMB_TM,),
            in_specs=[
                pl.BlockSpec((COMB_TM, D_MODEL), lambda i, p: (i, 0)),
                pl.BlockSpec((COMB_TM, LANES), lambda i, p: (i, 0)),
                pl.BlockSpec(memory_space=pl.ANY),
            ],
            out_specs=pl.BlockSpec((COMB_TM, D_MODEL), lambda i, p: (i, 0)),
            scratch_shapes=[
                pltpu.VMEM((2, TOP_K, COMB_TM * PACK, LANES), jnp.float32),
                pltpu.SemaphoreType.DMA((2,)),
            ],
        ),
        out_shape=jax.ShapeDtypeStruct((N_TOK, D_MODEL), jnp.float32),
        compiler_params=_cparams(1),
        name="moe_combine",
    )(pos_flat, x1, rf, yb)


def _moe(layer, x1, h2p, rf, cnt, w1, w3, w2):
    experts = rf[:, 2:4].astype(jnp.int32)
    rank = rf[:, 4:6].astype(jnp.int32)
    counts = cnt[0, :N_EXPERTS].astype(jnp.int32)
    padded = (counts + ROW_BLOCK - 1) // ROW_BLOCK * ROW_BLOCK
    pad_ends = jnp.cumsum(padded)
    pad_starts = pad_ends - padded
    is_e = experts[:, :, None] == jnp.arange(N_EXPERTS, dtype=jnp.int32)
    dest = (jnp.sum(jnp.where(is_e, pad_starts, 0), axis=-1) + rank).reshape(-1)
    n_valid = (pad_ends[-1:] // ROW_BLOCK).astype(jnp.int32)
    block_row0 = jnp.arange(N_BLOCKS, dtype=jnp.int32) * ROW_BLOCK
    block_e = jnp.minimum(
        jnp.sum((pad_ends[None, :] <= block_row0[:, None]).astype(jnp.int32), axis=1),
        N_EXPERTS - 1).astype(jnp.int32)
    pad_rows = jnp.concatenate([pad_starts + counts, pad_ends]).astype(jnp.int32)
    yb = _experts(layer, block_e, n_valid, dest, pad_rows, h2p, w1, w3, w2)
    return _combine(dest, x1, rf, yb)


def _router_params(w_grp, b_grp, w_exp, b_exp):
    wr = jnp.zeros((D_MODEL, LANES), jnp.float32)
    wr = wr.at[:, :N_EXPERTS].set(w_exp).at[:, GRP_LANE0:GRP_LANE0 + N_GROUPS].set(w_grp)
    br = jnp.zeros((1, LANES), jnp.float32)
    br = br.at[0, :N_EXPERTS].set(b_exp).at[0, GRP_LANE0:GRP_LANE0 + N_GROUPS].set(b_grp)
    w_hi = wr.astype(jnp.bfloat16)
    w_lo = (wr - w_hi.astype(jnp.float32)).astype(jnp.bfloat16)
    return jnp.concatenate([w_hi, w_lo], axis=1), br


def kernel(x, norm1_g, norm2_g, conv_w_in, conv_b_in, conv_dw, conv_dw_b, conv_ln_g, conv_ln_b, conv_w_out, conv_b_out, attn_w_qkv, attn_q_g, attn_k_g, attn_lq1, attn_lk1, attn_lq2, attn_lk2, attn_subln_g, attn_w_o, moe_w_grp, moe_b_grp, moe_w_exp, moe_b_exp, moe_w1, moe_w3, moe_w2):
    bf16 = jnp.bfloat16
    row = lambda a: a.reshape(1, -1)
    xf = x.reshape(N_TOK, D_MODEL)

    wr, br = _router_params(moe_w_grp[0], moe_b_grp[0], moe_w_exp[0], moe_b_exp[0])
    dw = jnp.zeros((HALO, D_MODEL), jnp.float32).at[:CONV_WIDTH].set(conv_dw[0])
    x1, h2p, rf, cnt = _conv_layer(
        xf, row(norm1_g[0]), conv_w_in[0].astype(bf16), row(conv_b_in[0]), dw,
        row(conv_dw_b[0]), row(conv_ln_g[0]), row(conv_ln_b[0]),
        conv_w_out[0].astype(bf16), row(conv_b_out[0]), row(norm2_g[0]), wr, br)
    xf = _moe(0, x1, h2p, rf, cnt, moe_w1, moe_w3, moe_w2)

    lambda_init = 0.8 - 0.6 * math.exp(-0.3 * 1)
    wr, br = _router_params(moe_w_grp[1], moe_b_grp[1], moe_w_exp[1], moe_b_exp[1])
    qk, v = _qkv(xf, row(norm1_g[1]), attn_w_qkv[0].astype(bf16))
    lam_rows = jnp.zeros((SUBLANES, HEAD_DIM), jnp.float32)
    lam_rows = lam_rows.at[0].set(attn_lq1[0]).at[1].set(attn_lk1[0])
    lam_rows = lam_rows.at[2].set(attn_lq2[0]).at[3].set(attn_lk2[0])
    two = lambda g: jnp.concatenate([g, g]).reshape(1, V_DIM)
    o = _attention(qk, v, two(attn_q_g[0]), two(attn_k_g[0]), lam_rows,
                   row(attn_subln_g[0]), lambda_init)
    x1, h2p, rf, cnt = _attn_post(xf, o, attn_w_o[0].astype(bf16), row(norm2_g[1]), wr, br)
    xf = _moe(1, x1, h2p, rf, cnt, moe_w1, moe_w3, moe_w2)
    return xf.reshape(BATCH, SEQ, D_MODEL)
```

```python
import functools
import math

import jax
import jax.numpy as jnp
from jax import lax
from jax.experimental import pallas as pl
from jax.experimental.pallas import tpu as pltpu

D_MODEL = 1024
BATCH = 8
SEQ = 2048
N_TOK = BATCH * SEQ
CHUNK = 64
CONV_WIDTH = 31
N_HEADS = 8
HEAD_DIM = 64
V_DIM = 128
N_GROUPS = 4
EXPERTS_PER_GROUP = 8
N_EXPERTS = 32
TOP_K = 2
D_EXPERT = 512
EPS = 1e-6
LOG2E = math.log2(math.e)

LANES = 128
SUBLANES = 8
PACK = D_MODEL // LANES
VMEM_LIMIT = 56 * 1024 * 1024

CONV_TS = 512
HALO = 32
CONV_RC = 64
POST_TS = 512
QKV_TS = 512
ATT_TQ = 256
ROW_BLOCK = 512
GATHER_SLOTS = 3
N_ASSIGN = N_TOK * TOP_K
N_BLOCKS = N_ASSIGN // ROW_BLOCK + N_EXPERTS
N_ROWS = N_BLOCKS * ROW_BLOCK
COMB_TM = 256
DMA_UNROLL = 8
GRP_LANE0 = N_EXPERTS


def _cparams(n_axes):
    return pltpu.CompilerParams(
        dimension_semantics=("arbitrary",) * n_axes, vmem_limit_bytes=VMEM_LIMIT)


def _rms(x, g):
    return x * lax.rsqrt(jnp.mean(x * x, axis=-1, keepdims=True) + EPS) * g


def _pack_rows(ref, val, rows):
    for j in range(PACK):
        ref[pl.ds(j, rows, stride=PACK), :] = val[:, j * LANES:(j + 1) * LANES]


def _unpack_rows(ref, rows):
    return jnp.concatenate(
        [ref[pl.ds(j, rows, stride=PACK), :] for j in range(PACK)], axis=-1)


def _residual_norm_route(x1, g2_ref, wr_ref, br_ref, run_ref,
                         x1_ref, h2p_ref, rf_ref, cnt_ref, rows):
    x1_ref[...] = x1
    h2 = _rms(x1, g2_ref[...])
    _pack_rows(h2p_ref, h2, rows)

    hi = h2.astype(jnp.bfloat16)
    lo = (h2 - hi.astype(jnp.float32)).astype(jnp.bfloat16)
    ab = jnp.dot(hi, wr_ref[...], preferred_element_type=jnp.float32)
    c = jnp.dot(lo, wr_ref[:, 0:LANES], preferred_element_type=jnp.float32)
    logits = ab[:, 0:LANES] + ab[:, LANES:2 * LANES] + c + br_ref[...]
    lane = lax.broadcasted_iota(jnp.int32, (rows, LANES), 1)
    lane_f = lane.astype(jnp.float32)
    neg = jnp.float32(-jnp.inf)
    big = jnp.float32(1e9)

    gmask = (lane >= GRP_LANE0) & (lane < GRP_LANE0 + N_GROUPS)
    gl = jnp.where(gmask, logits, neg)
    gmax = jnp.max(gl, axis=1, keepdims=True)
    gidx = jnp.min(jnp.where(gl == gmax, lane_f, big), axis=1, keepdims=True) - GRP_LANE0
    gsum = jnp.sum(jnp.where(gmask, jnp.exp(gl - gmax), 0.0), axis=1, keepdims=True)
    grp_p = 1.0 / gsum

    lane0 = gidx * EXPERTS_PER_GROUP
    emask = (lane_f >= lane0) & (lane_f < lane0 + EXPERTS_PER_GROUP)
    el = jnp.where(emask, logits, neg)
    m1 = jnp.max(el, axis=1, keepdims=True)
    i1 = jnp.min(jnp.where(el == m1, lane_f, big), axis=1, keepdims=True)
    el2 = jnp.where(lane_f == i1, neg, el)
    m2 = jnp.max(el2, axis=1, keepdims=True)
    i2 = jnp.min(jnp.where(el2 == m2, lane_f, big), axis=1, keepdims=True)
    t = jnp.exp(m2 - m1)
    inv = 1.0 / (1.0 + t)
    g_first = grp_p * inv
    g_second = grp_p * t * inv

    sel1 = lane_f == i1
    sel2 = lane_f == i2
    member = jnp.where(sel1 | sel2, 1.0, 0.0).astype(jnp.bfloat16)
    r_i = lax.broadcasted_iota(jnp.int32, (rows, rows), 0)
    c_i = lax.broadcasted_iota(jnp.int32, (rows, rows), 1)
    tri = jnp.where(c_i < r_i, 1.0, 0.0).astype(jnp.bfloat16)
    cum = jnp.dot(tri, member, preferred_element_type=jnp.float32) + run_ref[...]
    rank1 = jnp.sum(jnp.where(sel1, cum, 0.0), axis=1, keepdims=True)
    rank2 = jnp.sum(jnp.where(sel2, cum, 0.0), axis=1, keepdims=True)
    run_new = run_ref[...] + jnp.sum(member.astype(jnp.float32), axis=0, keepdims=True)
    run_ref[...] = run_new
    cnt_ref[...] = jnp.broadcast_to(run_new, (SUBLANES, LANES))

    out = jnp.where(lane == 0, g_first, 0.0)
    out = jnp.where(lane == 1, g_second, out)
    out = jnp.where(lane == 2, i1, out)
    out = jnp.where(lane == 3, i2, out)
    out = jnp.where(lane == 4, rank1, out)
    out = jnp.where(lane == 5, rank2, out)
    rf_ref[...] = out


def _route_out_shapes():
    return (
        jax.ShapeDtypeStruct((N_TOK, D_MODEL), jnp.float32),
        jax.ShapeDtypeStruct((N_TOK * PACK, LANES), jnp.float32),
        jax.ShapeDtypeStruct((N_TOK, LANES), jnp.float32),
        jax.ShapeDtypeStruct((SUBLANES, LANES), jnp.float32),
    )


def _route_out_specs(ts, idx):
    return (
        pl.BlockSpec((ts, D_MODEL), lambda *a: (idx(*a), 0)),
        pl.BlockSpec((ts * PACK, LANES), lambda *a: (idx(*a), 0)),
        pl.BlockSpec((ts, LANES), lambda *a: (idx(*a), 0)),
        pl.BlockSpec((SUBLANES, LANES), lambda *a: (0, 0)),
    )


def _full(shape):
    return pl.BlockSpec(shape, lambda *a: (0,) * len(shape))


def _conv_kernel(x_ref, g1_ref, win_ref, bin_ref, dw_ref, dwb_ref, lng_ref, lnb_ref,
                 wout_ref, bout_ref, g2_ref, wr_ref, br_ref,
                 x1_ref, h2p_ref, rf_ref, cnt_ref,
                 ext_ref, conv_ref, run_ref):
    b = pl.program_id(0)
    s = pl.program_id(1)
    ts = CONV_TS

    @pl.when((b == 0) & (s == 0))
    def _():
        run_ref[...] = jnp.zeros_like(run_ref)

    @pl.when(s == 0)
    def _():
        ext_ref[0:HALO, :] = jnp.zeros((HALO, D_MODEL), jnp.float32)

    x = x_ref[...]
    h = _rms(x, g1_ref[...]).astype(jnp.bfloat16)
    u = jnp.dot(h, win_ref[...], preferred_element_type=jnp.float32) + bin_ref[...]
    glu = u[:, :D_MODEL] * jax.nn.sigmoid(u[:, D_MODEL:])
    ext_ref[HALO:HALO + ts, :] = glu

    base = HALO - (CONV_WIDTH - 1)

    def lane_chunk(c, carry):
        cols = pl.ds(pl.multiple_of(c * LANES, LANES), LANES)
        taps = [dw_ref[pl.ds(k, 1), cols] for k in range(CONV_WIDTH)]
        for r0 in range(0, ts, CONV_RC):
            acc = None
            for r in range(SUBLANES):
                rows = CONV_RC + (SUBLANES if r else 0)
                part = None
                for q in range((base + CONV_WIDTH - 1) // SUBLANES + 1):
                    k = SUBLANES * q + r - base
                    if 0 <= k < CONV_WIDTH:
                        term = ext_ref[pl.ds(r0 + SUBLANES * q, rows), cols] * taps[k]
                        part = term if part is None else part + term
                part = part[r:r + CONV_RC, :] if r else part
                acc = part if acc is None else acc + part
            conv_ref[pl.ds(r0, CONV_RC), cols] = acc
        return carry

    lax.fori_loop(0, D_MODEL // LANES, lane_chunk, 0)
    ext_ref[0:HALO, :] = ext_ref[ts:ts + HALO, :]

    v = conv_ref[...] + dwb_ref[...]
    mu = jnp.mean(v, axis=-1, keepdims=True)
    vc = v - mu
    var = jnp.mean(vc * vc, axis=-1, keepdims=True)
    y = vc * lax.rsqrt(var + EPS) * lng_ref[...] + lnb_ref[...]
    y = (y * jax.nn.sigmoid(y)).astype(jnp.bfloat16)
    mix = jnp.dot(y, wout_ref[...], preferred_element_type=jnp.float32) + bout_ref[...]
    _residual_norm_route(x + mix, g2_ref, wr_ref, br_ref, run_ref,
                         x1_ref, h2p_ref, rf_ref, cnt_ref, ts)


def _conv_layer(x, g1, w_in, b_in, dw, dw_b, ln_g, ln_b, w_out, b_out, g2, wr, br):
    ns = SEQ // CONV_TS
    tile = lambda b, s: b * ns + s
    return pl.pallas_call(
        _conv_kernel,
        grid=(BATCH, ns),
        in_specs=[
            pl.BlockSpec((CONV_TS, D_MODEL), lambda b, s: (tile(b, s), 0)),
            _full((1, D_MODEL)),
            _full((D_MODEL, 2 * D_MODEL)),
            _full((1, 2 * D_MODEL)),
            _full((HALO, D_MODEL)),
            _full((1, D_MODEL)), _full((1, D_MODEL)), _full((1, D_MODEL)),
            _full((D_MODEL, D_MODEL)),
            _full((1, D_MODEL)), _full((1, D_MODEL)),
            _full((D_MODEL, 2 * LANES)), _full((1, LANES)),
        ],
        out_specs=_route_out_specs(CONV_TS, tile),
        out_shape=_route_out_shapes(),
        scratch_shapes=[
            pltpu.VMEM((HALO + CONV_TS, D_MODEL), jnp.float32),
            pltpu.VMEM((CONV_TS, D_MODEL), jnp.float32),
            pltpu.VMEM((1, LANES), jnp.float32),
        ],
        compiler_params=_cparams(2),
        name="conv_mixer",
    )(x, g1, w_in, b_in, dw, dw_b, ln_g, ln_b, w_out, b_out, g2, wr, br)


def _qkv_kernel(x_ref, g1_ref, w_ref, qk_ref, v_ref):
    h = _rms(x_ref[...], g1_ref[...]).astype(jnp.bfloat16)
    qkv = jnp.dot(h, w_ref[...], preferred_element_type=jnp.float32)
    qk_ref[...] = qkv[:, :2 * D_MODEL]
    v_ref[...] = qkv[:, 2 * D_MODEL:].astype(jnp.bfloat16)


def _qkv(x, g1, w_qkv):
    return pl.pallas_call(
        _qkv_kernel,
        grid=(N_TOK // QKV_TS,),
        in_specs=[
            pl.BlockSpec((QKV_TS, D_MODEL), lambda i: (i, 0)),
            _full((1, D_MODEL)),
            _full((D_MODEL, 3 * D_MODEL)),
        ],
        out_specs=(
            pl.BlockSpec((QKV_TS, 2 * D_MODEL), lambda i: (i, 0)),
            pl.BlockSpec((QKV_TS, D_MODEL), lambda i: (i, 0)),
        ),
        out_shape=(
            jax.ShapeDtypeStruct((N_TOK, 2 * D_MODEL), jnp.float32),
            jax.ShapeDtypeStruct((N_TOK, D_MODEL), jnp.bfloat16),
        ),
        compiler_params=_cparams(1),
        name="qkv_proj",
    )(x, g1, w_qkv)


def _half_norm(z, gain):
    lane = lax.broadcasted_iota(jnp.int32, z.shape, 1)
    first = lane < HEAD_DIM
    zz = z * z
    ss_a = jnp.sum(jnp.where(first, zz, 0.0), axis=1, keepdims=True)
    ss_b = jnp.sum(jnp.where(first, 0.0, zz), axis=1, keepdims=True)
    inv = jnp.where(first, lax.rsqrt(ss_a * (1.0 / HEAD_DIM) + EPS),
                    lax.rsqrt(ss_b * (1.0 / HEAD_DIM) + EPS))
    return z * inv * gain


def _attn_kernel(q_ref, k_ref, v_ref, qg_ref, kg_ref, lam_ref, sg_ref, o_ref,
                 qa_ref, qb_ref, kn_ref, v1_ref, s_ref, e_ref, part_ref, *, lambda_init):
    lp = lam_ref[...]
    lam = (jnp.exp(jnp.sum(lp[0:1, :] * lp[1:2, :], axis=1, keepdims=True))
           - jnp.exp(jnp.sum(lp[2:3, :] * lp[3:4, :], axis=1, keepdims=True))
           + lambda_init)

    kn_ref[...] = _half_norm(k_ref[...], kg_ref[...]).astype(jnp.bfloat16)
    qn = _half_norm(q_ref[...], qg_ref[...]) * (HEAD_DIM ** -0.5 * LOG2E)
    first = lax.broadcasted_iota(jnp.int32, (SEQ, V_DIM), 1) < HEAD_DIM
    qa_ref[...] = jnp.where(first, qn, 0.0).astype(jnp.bfloat16)
    qb_ref[...] = jnp.where(first, 0.0, qn).astype(jnp.bfloat16)

    ones_col = lax.broadcasted_iota(jnp.int32, (SEQ, V_DIM), 1) == 0
    v1_ref[:, 0:V_DIM] = v_ref[...]
    v1_ref[:, V_DIM:2 * V_DIM] = jnp.where(ones_col, 1.0, 0.0).astype(jnp.bfloat16)

    tq = ATT_TQ
    nt = (((1,), (1,)), ((), ()))
    visible = (lax.broadcasted_iota(jnp.int32, (tq, tq), 1) // CHUNK
               <= lax.broadcasted_iota(jnp.int32, (tq, tq), 0) // CHUNK)
    halves = (qa_ref, qb_ref)
    units = [(qi, h) for qi in range(SEQ // tq) for h in range(2)]

    def stage_scores(u, slot):
        qi, h = units[u]
        k0 = qi * tq
        q = halves[h][k0:k0 + tq, :]
        dg = lax.dot_general(q, kn_ref[k0:k0 + tq, :], nt, preferred_element_type=jnp.float32)
        s_ref[slot, :, k0:k0 + tq] = jnp.where(visible, dg, -jnp.inf)
        if qi:
            s_ref[slot, :, 0:k0] = lax.dot_general(q, kn_ref[0:k0, :], nt,
                                                   preferred_element_type=jnp.float32)

    def stage_numerators(u, slot):
        kend = (units[u][0] + 1) * tq
        sc = s_ref[slot, :, 0:kend]
        m = jnp.max(sc, axis=1, keepdims=True)
        e_ref[slot, :, 0:kend] = jnp.exp2(sc - m).astype(jnp.bfloat16)

    def stage_values(u, slot):
        qi, h = units[u]
        k0 = qi * tq
        kend = k0 + tq
        acc = jnp.dot(e_ref[slot, :, 0:kend], v1_ref[0:kend, :],
                      preferred_element_type=jnp.float32)
        attn = acc[:, 0:V_DIM] * (1.0 / acc[:, V_DIM:V_DIM + 1])
        if h == 0:
            part_ref[...] = attn
        else:
            o = part_ref[...] - lam * attn
            o = _rms(o, sg_ref[...]) * (1.0 - lambda_init)
            o_ref[k0:kend, :] = o.astype(jnp.bfloat16)

    for step in range(len(units) + 2):
        if step < len(units):
            stage_scores(step, step % 2)
        if 1 <= step <= len(units):
            stage_numerators(step - 1, (step - 1) % 2)
        if step >= 2:
            stage_values(step - 2, step % 2)


def _attention(qk, v, qg2, kg2, lam_rows, subln_g, lambda_init):
    return pl.pallas_call(
        functools.partial(_attn_kernel, lambda_init=lambda_init),
        grid=(BATCH, N_HEADS),
        in_specs=[
            pl.BlockSpec((SEQ, V_DIM), lambda b, h: (b, h)),
            pl.BlockSpec((SEQ, V_DIM), lambda b, h: (b, N_HEADS + h)),
            pl.BlockSpec((SEQ, V_DIM), lambda b, h: (b, h)),
            _full((1, V_DIM)), _full((1, V_DIM)),
            _full((SUBLANES, HEAD_DIM)),
            _full((1, V_DIM)),
        ],
        out_specs=pl.BlockSpec((SEQ, V_DIM), lambda b, h: (b, h)),
        out_shape=jax.ShapeDtypeStruct((N_TOK, D_MODEL), jnp.bfloat16),
        scratch_shapes=[pltpu.VMEM((SEQ, V_DIM), jnp.bfloat16)] * 3 + [
            pltpu.VMEM((SEQ, 2 * V_DIM), jnp.bfloat16),
            pltpu.VMEM((2, ATT_TQ, SEQ), jnp.float32),
            pltpu.VMEM((2, ATT_TQ, SEQ), jnp.bfloat16),
            pltpu.VMEM((ATT_TQ, V_DIM), jnp.float32),
        ],
        compiler_params=_cparams(2),
        name="diff_attention",
    )(qk, qk, v, qg2, kg2, lam_rows, subln_g)


def _post_kernel(x_ref, m_ref, w_ref, g2_ref, wr_ref, br_ref,
                 x1_ref, h2p_ref, rf_ref, cnt_ref, run_ref):
    @pl.when(pl.program_id(0) == 0)
    def _():
        run_ref[...] = jnp.zeros_like(run_ref)

    mix = jnp.dot(m_ref[...], w_ref[...], preferred_element_type=jnp.float32)
    _residual_norm_route(x_ref[...] + mix, g2_ref, wr_ref, br_ref, run_ref,
                         x1_ref, h2p_ref, rf_ref, cnt_ref, POST_TS)


def _attn_post(x, o, w_o, g2, wr, br):
    return pl.pallas_call(
        _post_kernel,
        grid=(N_TOK // POST_TS,),
        in_specs=[
            pl.BlockSpec((POST_TS, D_MODEL), lambda i: (i, 0)),
            pl.BlockSpec((POST_TS, D_MODEL), lambda i: (i, 0)),
            _full((D_MODEL, D_MODEL)),
            _full((1, D_MODEL)),
            _full((D_MODEL, 2 * LANES)), _full((1, LANES)),
        ],
        out_specs=_route_out_specs(POST_TS, lambda i: i),
        out_shape=_route_out_shapes(),
        scratch_shapes=[pltpu.VMEM((1, LANES), jnp.float32)],
        compiler_params=_cparams(1),
        name="attn_post",
    )(x, o, w_o, g2, wr, br)


def _unrolled(n, body):
    def group(g, carry):
        for u in range(DMA_UNROLL):
            body(g * DMA_UNROLL + u)
        return carry
    lax.fori_loop(0, n // DMA_UNROLL, group, 0)


def _expert_kernel(be_ref, nv_ref, dest_ref, pad_ref, h2p_ref, w1_ref, w3_ref, w2_ref, yb_ref,
                   rowtok_ref, xbuf_ref, sem, w1b_ref, w3b_ref, w2b_ref):
    b = pl.program_id(0)
    n_valid = nv_ref[0]

    def row_copy(block, slot, r):
        tok = rowtok_ref[block * ROW_BLOCK + r]
        return pltpu.make_async_copy(
            h2p_ref.at[pl.ds(pl.multiple_of(tok * PACK, PACK), PACK)],
            xbuf_ref.at[slot, pl.ds(r * PACK, PACK)],
            sem.at[slot])

    def block_arrival(slot):
        return pltpu.make_async_copy(
            h2p_ref.at[pl.ds(0, ROW_BLOCK * PACK)], xbuf_ref.at[slot], sem.at[slot])

    @pl.when(b == 0)
    def _():
        def clear_padding(e, carry):
            def clear(j, c):
                rowtok_ref[j] = 0
                return c
            return lax.fori_loop(pad_ref[e], pad_ref[N_EXPERTS + e], clear, carry)
        lax.fori_loop(0, N_EXPERTS, clear_padding, 0)

        def place(t):
            rowtok_ref[dest_ref[TOP_K * t]] = t
            rowtok_ref[dest_ref[TOP_K * t + 1]] = t
        _unrolled(N_TOK, place)
        second = jnp.minimum(1, n_valid - 1)
        _unrolled(ROW_BLOCK, lambda r: row_copy(0, 0, r).start())
        _unrolled(ROW_BLOCK, lambda r: row_copy(second, 1, r).start())

    prev = be_ref[jnp.maximum(b - 1, 0)]
    fresh = (b == 0) | (be_ref[b] != prev)

    @pl.when(fresh)
    def _():
        w1b_ref[...] = w1_ref[...].astype(jnp.bfloat16)
        w3b_ref[...] = w3_ref[...].astype(jnp.bfloat16)
        w2b_ref[...] = w2_ref[...].astype(jnp.bfloat16)

    @pl.when(b < n_valid)
    def _():
        slot = lax.rem(b, GATHER_SLOTS)
        block_arrival(slot).wait()
        x = _unpack_rows(xbuf_ref.at[slot], ROW_BLOCK).astype(jnp.bfloat16)
        ahead = jnp.minimum(b + 2, n_valid - 1)
        ahead_slot = lax.rem(b + 2, GATHER_SLOTS)
        for r in range(ROW_BLOCK):
            row_copy(ahead, ahead_slot, r).start()
        h1 = jnp.dot(x, w1b_ref[...], preferred_element_type=jnp.float32)
        h3 = jnp.dot(x, w3b_ref[...], preferred_element_type=jnp.float32)
        act = (h1 * jax.nn.sigmoid(h1) * h3).astype(jnp.bfloat16)
        y = jnp.dot(act, w2b_ref[...], preferred_element_type=jnp.float32)
        _pack_rows(yb_ref, y, ROW_BLOCK)

    @pl.when(b == n_valid - 1)
    def _():
        block_arrival(lax.rem(b + 1, GATHER_SLOTS)).wait()
        block_arrival(lax.rem(b + 2, GATHER_SLOTS)).wait()

    @pl.when(b >= n_valid)
    def _():
        yb_ref[...] = jnp.zeros_like(yb_ref)


def _experts(layer, block_e, n_valid, dest_flat, pad_rows, h2p, w1, w3, w2):
    w_in = pl.BlockSpec((None, None, D_MODEL, D_EXPERT),
                        lambda b, be, *_: (layer, be[b], 0, 0))
    w_out = pl.BlockSpec((None, None, D_EXPERT, D_MODEL),
                         lambda b, be, *_: (layer, be[b], 0, 0))
    return pl.pallas_call(
        _expert_kernel,
        grid_spec=pltpu.PrefetchScalarGridSpec(
            num_scalar_prefetch=4,
            grid=(N_BLOCKS,),
            in_specs=[pl.BlockSpec(memory_space=pl.ANY), w_in, w_in, w_out],
            out_specs=pl.BlockSpec((ROW_BLOCK * PACK, LANES), lambda b, *_: (b, 0)),
            scratch_shapes=[
                pltpu.SMEM((N_ROWS,), jnp.int32),
                pltpu.VMEM((GATHER_SLOTS, ROW_BLOCK * PACK, LANES), jnp.float32),
                pltpu.SemaphoreType.DMA((GATHER_SLOTS,)),
                pltpu.VMEM((D_MODEL, D_EXPERT), jnp.bfloat16),
                pltpu.VMEM((D_MODEL, D_EXPERT), jnp.bfloat16),
                pltpu.VMEM((D_EXPERT, D_MODEL), jnp.bfloat16),
            ],
        ),
        out_shape=jax.ShapeDtypeStruct((N_ROWS * PACK, LANES), jnp.float32),
        compiler_params=_cparams(1),
        name="moe_experts",
    )(block_e, n_valid, dest_flat, pad_rows, h2p, w1, w3, w2)


def _combine_kernel(pos_ref, x1_ref, rf_ref, yb_ref, out_ref, buf_ref, sem):
    i = pl.program_id(0)
    n = pl.num_programs(0)
    tm = COMB_TM

    def copy(tile, slot, j, k):
        p = pos_ref[TOP_K * (tile * tm + j) + k]
        return pltpu.make_async_copy(
            yb_ref.at[pl.ds(pl.multiple_of(p * PACK, PACK), PACK)],
            buf_ref.at[slot, k, pl.ds(pl.multiple_of(j * PACK, PACK), PACK)],
            sem.at[slot])

    def issue(tile, slot):
        def body(j):
            copy(tile, slot, j, 0).start()
            copy(tile, slot, j, 1).start()
        _unrolled(tm, body)

    def drain(tile, slot):
        def body(j):
            copy(tile, slot, j, 0).wait()
            copy(tile, slot, j, 1).wait()
        _unrolled(tm, body)

    slot = i % 2

    @pl.when(i == 0)
    def _():
        issue(0, 0)

    @pl.when(i + 1 < n)
    def _():
        issue(i + 1, 1 - slot)

    drain(i, slot)
    rf = rf_ref[...]
    y0 = _unpack_rows(buf_ref.at[slot, 0], tm)
    y1 = _unpack_rows(buf_ref.at[slot, 1], tm)
    out_ref[...] = x1_ref[...] + rf[:, 0:1] * y0 + rf[:, 1:2] * y1


def _combine(pos_flat, x1, rf, yb):
    return pl.pallas_call(
        _combine_kernel,
        grid_spec=pltpu.PrefetchScalarGridSpec(
            num_scalar_prefetch=1,
            grid=(N_TOK // COMB_TM,),
            in_specs=[
                pl.BlockSpec((COMB_TM, D_MODEL), lambda i, p: (i, 0)),
                pl.BlockSpec((COMB_TM, LANES), lambda i, p: (i, 0)),
                pl.BlockSpec(memory_space=pl.ANY),
            ],
            out_specs=pl.BlockSpec((COMB_TM, D_MODEL), lambda i, p: (i, 0)),
            scratch_shapes=[
                pltpu.VMEM((2, TOP_K, COMB_TM * PACK, LANES), jnp.float32),
                pltpu.SemaphoreType.DMA((2,)),
            ],
        ),
        out_shape=jax.ShapeDtypeStruct((N_TOK, D_MODEL), jnp.float32),
        compiler_params=_cparams(1),
        name="moe_combine",
    )(pos_flat, x1, rf, yb)


def _moe(layer, x1, h2p, rf, cnt, w1, w3, w2):
    experts = rf[:, 2:4].astype(jnp.int32)
    rank = rf[:, 4:6].astype(jnp.int32)
    counts = cnt[0, :N_EXPERTS].astype(jnp.int32)
    padded = (counts + ROW_BLOCK - 1) // ROW_BLOCK * ROW_BLOCK
    pad_ends = jnp.cumsum(padded)
    pad_starts = pad_ends - padded
    is_e = experts[:, :, None] == jnp.arange(N_EXPERTS, dtype=jnp.int32)
    dest = (jnp.sum(jnp.where(is_e, pad_starts, 0), axis=-1) + rank).reshape(-1)
    n_valid = (pad_ends[-1:] // ROW_BLOCK).astype(jnp.int32)
    block_row0 = jnp.arange(N_BLOCKS, dtype=jnp.int32) * ROW_BLOCK
    block_e = jnp.minimum(
        jnp.sum((pad_ends[None, :] <= block_row0[:, None]).astype(jnp.int32), axis=1),
        N_EXPERTS - 1).astype(jnp.int32)
    pad_rows = jnp.concatenate([pad_starts + counts, pad_ends]).astype(jnp.int32)
    yb = _experts(layer, block_e, n_valid, dest, pad_rows, h2p, w1, w3, w2)
    return _combine(dest, x1, rf, yb)


def _router_params(w_grp, b_grp, w_exp, b_exp):
    wr = jnp.zeros((D_MODEL, LANES), jnp.float32)
    wr = wr.at[:, :N_EXPERTS].set(w_exp).at[:, GRP_LANE0:GRP_LANE0 + N_GROUPS].set(w_grp)
    br = jnp.zeros((1, LANES), jnp.float32)
    br = br.at[0, :N_EXPERTS].set(b_exp).at[0, GRP_LANE0:GRP_LANE0 + N_GROUPS].set(b_grp)
    w_hi = wr.astype(jnp.bfloat16)
    w_lo = (wr - w_hi.astype(jnp.float32)).astype(jnp.bfloat16)
    return jnp.concatenate([w_hi, w_lo], axis=1), br


def kernel(x, norm1_g, norm2_g, conv_w_in, conv_b_in, conv_dw, conv_dw_b, conv_ln_g, conv_ln_b, conv_w_out, conv_b_out, attn_w_qkv, attn_q_g, attn_k_g, attn_lq1, attn_lk1, attn_lq2, attn_lk2, attn_subln_g, attn_w_o, moe_w_grp, moe_b_grp, moe_w_exp, moe_b_exp, moe_w1, moe_w3, moe_w2):
    bf16 = jnp.bfloat16
    row = lambda a: a.reshape(1, -1)
    xf = x.reshape(N_TOK, D_MODEL)

    wr, br = _router_params(moe_w_grp[0], moe_b_grp[0], moe_w_exp[0], moe_b_exp[0])
    dw = jnp.zeros((HALO, D_MODEL), jnp.float32).at[:CONV_WIDTH].set(conv_dw[0])
    x1, h2p, rf, cnt = _conv_layer(
        xf, row(norm1_g[0]), conv_w_in[0].astype(bf16), row(conv_b_in[0]), dw,
        row(conv_dw_b[0]), row(conv_ln_g[0]), row(conv_ln_b[0]),
        conv_w_out[0].astype(bf16), row(conv_b_out[0]), row(norm2_g[0]), wr, br)
    xf = _moe(0, x1, h2p, rf, cnt, moe_w1, moe_w3, moe_w2)

    lambda_init = 0.8 - 0.6 * math.exp(-0.3 * 1)
    wr, br = _router_params(moe_w_grp[1], moe_b_grp[1], moe_w_exp[1], moe_b_exp[1])
    qk, v = _qkv(xf, row(norm1_g[1]), attn_w_qkv[0].astype(bf16))
    lam_rows = jnp.zeros((SUBLANES, HEAD_DIM), jnp.float32)
    lam_rows = lam_rows.at[0].set(attn_lq1[0]).at[1].set(attn_lk1[0])
    lam_rows = lam_rows.at[2].set(attn_lq2[0]).at[3].set(attn_lk2[0])
    two = lambda g: jnp.concatenate([g, g]).reshape(1, V_DIM)
    o = _attention(qk, v, two(attn_q_g[0]), two(attn_k_g[0]), lam_rows,
                   row(attn_subln_g[0]), lambda_init)
    x1, h2p, rf, cnt = _attn_post(xf, o, attn_w_o[0].astype(bf16), row(norm2_g[1]), wr, br)
    xf = _moe(1, x1, h2p, rf, cnt, moe_w1, moe_w3, moe_w2)
    return xf.reshape(BATCH, SEQ, D_MODEL)
```

```python
import functools
import math

import jax
import jax.numpy as jnp
from jax import lax
from jax.experimental import pallas as pl
from jax.experimental.pallas import tpu as pltpu

D_MODEL = 1024
BATCH = 8
SEQ = 2048
N_TOK = BATCH * SEQ
CHUNK = 64
CONV_WIDTH = 31
N_HEADS = 8
HEAD_DIM = 64
V_DIM = 128
N_GROUPS = 4
EXPERTS_PER_GROUP = 8
N_EXPERTS = 32
TOP_K = 2
D_EXPERT = 512
EPS = 1e-6
LOG2E = math.log2(math.e)

LANES = 128
SUBLANES = 8
PACK = D_MODEL // LANES
VMEM_LIMIT = 56 * 1024 * 1024

CONV_TS = 512
HALO = 32
CONV_RC = 64
POST_TS = 512
QKV_TS = 512
ATT_TQ = 256
ROW_BLOCK = 256
GATHER_SLOTS = 3
N_ASSIGN = N_TOK * TOP_K
N_BLOCKS = N_ASSIGN // ROW_BLOCK + N_EXPERTS
N_ROWS = N_BLOCKS * ROW_BLOCK
COMB_TM = 256
DMA_UNROLL = 8
GRP_LANE0 = N_EXPERTS


def _cparams(n_axes):
    return pltpu.CompilerParams(
        dimension_semantics=("arbitrary",) * n_axes, vmem_limit_bytes=VMEM_LIMIT)


def _rms(x, g):
    return x * lax.rsqrt(jnp.mean(x * x, axis=-1, keepdims=True) + EPS) * g


def _pack_rows(ref, val, rows):
    for j in range(PACK):
        ref[pl.ds(j, rows, stride=PACK), :] = val[:, j * LANES:(j + 1) * LANES]


def _unpack_rows(ref, rows):
    return jnp.concatenate(
        [ref[pl.ds(j, rows, stride=PACK), :] for j in range(PACK)], axis=-1)


def _residual_norm_route(x1, g2_ref, wr_ref, br_ref, run_ref,
                         x1_ref, h2p_ref, rf_ref, cnt_ref, rows):
    x1_ref[...] = x1
    h2 = _rms(x1, g2_ref[...])
    _pack_rows(h2p_ref, h2, rows)

    hi = h2.astype(jnp.bfloat16)
    lo = (h2 - hi.astype(jnp.float32)).astype(jnp.bfloat16)
    ab = jnp.dot(hi, wr_ref[...], preferred_element_type=jnp.float32)
    c = jnp.dot(lo, wr_ref[:, 0:LANES], preferred_element_type=jnp.float32)
    logits = ab[:, 0:LANES] + ab[:, LANES:2 * LANES] + c + br_ref[...]
    lane = lax.broadcasted_iota(jnp.int32, (rows, LANES), 1)
    lane_f = lane.astype(jnp.float32)
    neg = jnp.float32(-jnp.inf)
    big = jnp.float32(1e9)

    gmask = (lane >= GRP_LANE0) & (lane < GRP_LANE0 + N_GROUPS)
    gl = jnp.where(gmask, logits, neg)
    gmax = jnp.max(gl, axis=1, keepdims=True)
    gidx = jnp.min(jnp.where(gl == gmax, lane_f, big), axis=1, keepdims=True) - GRP_LANE0
    gsum = jnp.sum(jnp.where(gmask, jnp.exp(gl - gmax), 0.0), axis=1, keepdims=True)
    grp_p = 1.0 / gsum

    lane0 = gidx * EXPERTS_PER_GROUP
    emask = (lane_f >= lane0) & (lane_f < lane0 + EXPERTS_PER_GROUP)
    el = jnp.where(emask, logits, neg)
    m1 = jnp.max(el, axis=1, keepdims=True)
    i1 = jnp.min(jnp.where(el == m1, lane_f, big), axis=1, keepdims=True)
    el2 = jnp.where(lane_f == i1, neg, el)
    m2 = jnp.max(el2, axis=1, keepdims=True)
    i2 = jnp.min(jnp.where(el2 == m2, lane_f, big), axis=1, keepdims=True)
    t = jnp.exp(m2 - m1)
    inv = 1.0 / (1.0 + t)
    g_first = grp_p * inv
    g_second = grp_p * t * inv

    sel1 = lane_f == i1
    sel2 = lane_f == i2
    member = jnp.where(sel1 | sel2, 1.0, 0.0).astype(jnp.bfloat16)
    r_i = lax.broadcasted_iota(jnp.int32, (rows, rows), 0)
    c_i = lax.broadcasted_iota(jnp.int32, (rows, rows), 1)
    tri = jnp.where(c_i < r_i, 1.0, 0.0).astype(jnp.bfloat16)
    cum = jnp.dot(tri, member, preferred_element_type=jnp.float32) + run_ref[...]
    rank1 = jnp.sum(jnp.where(sel1, cum, 0.0), axis=1, keepdims=True)
    rank2 = jnp.sum(jnp.where(sel2, cum, 0.0), axis=1, keepdims=True)
    run_new = run_ref[...] + jnp.sum(member.astype(jnp.float32), axis=0, keepdims=True)
    run_ref[...] = run_new
    cnt_ref[...] = jnp.broadcast_to(run_new, (SUBLANES, LANES))

    out = jnp.where(lane == 0, g_first, 0.0)
    out = jnp.where(lane == 1, g_second, out)
    out = jnp.where(lane == 2, i1, out)
    out = jnp.where(lane == 3, i2, out)
    out = jnp.where(lane == 4, rank1, out)
    out = jnp.where(lane == 5, rank2, out)
    rf_ref[...] = out


def _route_out_shapes():
    return (
        jax.ShapeDtypeStruct((N_TOK, D_MODEL), jnp.float32),
        jax.ShapeDtypeStruct((N_TOK * PACK, LANES), jnp.float32),
        jax.ShapeDtypeStruct((N_TOK, LANES), jnp.float32),
        jax.ShapeDtypeStruct((SUBLANES, LANES), jnp.float32),
    )


def _route_out_specs(ts, idx):
    return (
        pl.BlockSpec((ts, D_MODEL), lambda *a: (idx(*a), 0)),
        pl.BlockSpec((ts * PACK, LANES), lambda *a: (idx(*a), 0)),
        pl.BlockSpec((ts, LANES), lambda *a: (idx(*a), 0)),
        pl.BlockSpec((SUBLANES, LANES), lambda *a: (0, 0)),
    )


def _full(shape):
    return pl.BlockSpec(shape, lambda *a: (0,) * len(shape))


def _conv_kernel(x_ref, g1_ref, win_ref, bin_ref, dw_ref, dwb_ref, lng_ref, lnb_ref,
                 wout_ref, bout_ref, g2_ref, wr_ref, br_ref,
                 x1_ref, h2p_ref, rf_ref, cnt_ref,
                 ext_ref, conv_ref, run_ref):
    b = pl.program_id(0)
    s = pl.program_id(1)
    ts = CONV_TS

    @pl.when((b == 0) & (s == 0))
    def _():
        run_ref[...] = jnp.zeros_like(run_ref)

    @pl.when(s == 0)
    def _():
        ext_ref[0:HALO, :] = jnp.zeros((HALO, D_MODEL), jnp.float32)

    x = x_ref[...]
    h = _rms(x, g1_ref[...]).astype(jnp.bfloat16)
    u = jnp.dot(h, win_ref[...], preferred_element_type=jnp.float32) + bin_ref[...]
    glu = u[:, :D_MODEL] * jax.nn.sigmoid(u[:, D_MODEL:])
    ext_ref[HALO:HALO + ts, :] = glu

    base = HALO - (CONV_WIDTH - 1)

    def lane_chunk(c, carry):
        cols = pl.ds(pl.multiple_of(c * LANES, LANES), LANES)
        taps = [dw_ref[pl.ds(k, 1), cols] for k in range(CONV_WIDTH)]
        for r0 in range(0, ts, CONV_RC):
            acc = None
            for r in range(SUBLANES):
                rows = CONV_RC + (SUBLANES if r else 0)
                part = None
                for q in range((base + CONV_WIDTH - 1) // SUBLANES + 1):
                    k = SUBLANES * q + r - base
                    if 0 <= k < CONV_WIDTH:
                        term = ext_ref[pl.ds(r0 + SUBLANES * q, rows), cols] * taps[k]
                        part = term if part is None else part + term
                part = part[r:r + CONV_RC, :] if r else part
                acc = part if acc is None else acc + part
            conv_ref[pl.ds(r0, CONV_RC), cols] = acc
        return carry

    lax.fori_loop(0, D_MODEL // LANES, lane_chunk, 0)
    ext_ref[0:HALO, :] = ext_ref[ts:ts + HALO, :]

    v = conv_ref[...] + dwb_ref[...]
    mu = jnp.mean(v, axis=-1, keepdims=True)
    vc = v - mu
    var = jnp.mean(vc * vc, axis=-1, keepdims=True)
    y = vc * lax.rsqrt(var + EPS) * lng_ref[...] + lnb_ref[...]
    y = (y * jax.nn.sigmoid(y)).astype(jnp.bfloat16)
    mix = jnp.dot(y, wout_ref[...], preferred_element_type=jnp.float32) + bout_ref[...]
    _residual_norm_route(x + mix, g2_ref, wr_ref, br_ref, run_ref,
                         x1_ref, h2p_ref, rf_ref, cnt_ref, ts)


def _conv_layer(x, g1, w_in, b_in, dw, dw_b, ln_g, ln_b, w_out, b_out, g2, wr, br):
    ns = SEQ // CONV_TS
    tile = lambda b, s: b * ns + s
    return pl.pallas_call(
        _conv_kernel,
        grid=(BATCH, ns),
        in_specs=[
            pl.BlockSpec((CONV_TS, D_MODEL), lambda b, s: (tile(b, s), 0)),
            _full((1, D_MODEL)),
            _full((D_MODEL, 2 * D_MODEL)),
            _full((1, 2 * D_MODEL)),
            _full((HALO, D_MODEL)),
            _full((1, D_MODEL)), _full((1, D_MODEL)), _full((1, D_MODEL)),
            _full((D_MODEL, D_MODEL)),
            _full((1, D_MODEL)), _full((1, D_MODEL)),
            _full((D_MODEL, 2 * LANES)), _full((1, LANES)),
        ],
        out_specs=_route_out_specs(CONV_TS, tile),
        out_shape=_route_out_shapes(),
        scratch_shapes=[
            pltpu.VMEM((HALO + CONV_TS, D_MODEL), jnp.float32),
            pltpu.VMEM((CONV_TS, D_MODEL), jnp.float32),
            pltpu.VMEM((1, LANES), jnp.float32),
        ],
        compiler_params=_cparams(2),
        name="conv_mixer",
    )(x, g1, w_in, b_in, dw, dw_b, ln_g, ln_b, w_out, b_out, g2, wr, br)


def _qkv_kernel(x_ref, g1_ref, w_ref, qk_ref, v_ref):
    h = _rms(x_ref[...], g1_ref[...]).astype(jnp.bfloat16)
    qkv = jnp.dot(h, w_ref[...], preferred_element_type=jnp.float32)
    qk_ref[...] = qkv[:, :2 * D_MODEL]
    v_ref[...] = qkv[:, 2 * D_MODEL:].astype(jnp.bfloat16)


def _qkv(x, g1, w_qkv):
    return pl.pallas_call(
        _qkv_kernel,
        grid=(N_TOK // QKV_TS,),
        in_specs=[
            pl.BlockSpec((QKV_TS, D_MODEL), lambda i: (i, 0)),
            _full((1, D_MODEL)),
            _full((D_MODEL, 3 * D_MODEL)),
        ],
        out_specs=(
            pl.BlockSpec((QKV_TS, 2 * D_MODEL), lambda i: (i, 0)),
            pl.BlockSpec((QKV_TS, D_MODEL), lambda i: (i, 0)),
        ),
        out_shape=(
            jax.ShapeDtypeStruct((N_TOK, 2 * D_MODEL), jnp.float32),
            jax.ShapeDtypeStruct((N_TOK, D_MODEL), jnp.bfloat16),
        ),
        compiler_params=_cparams(1),
        name="qkv_proj",
    )(x, g1, w_qkv)


def _half_norm(z, gain):
    lane = lax.broadcasted_iota(jnp.int32, z.shape, 1)
    first = lane < HEAD_DIM
    zz = z * z
    ss_a = jnp.sum(jnp.where(first, zz, 0.0), axis=1, keepdims=True)
    ss_b = jnp.sum(jnp.where(first, 0.0, zz), axis=1, keepdims=True)
    inv = jnp.where(first, lax.rsqrt(ss_a * (1.0 / HEAD_DIM) + EPS),
                    lax.rsqrt(ss_b * (1.0 / HEAD_DIM) + EPS))
    return z * inv * gain


def _attn_kernel(q_ref, k_ref, v_ref, qg_ref, kg_ref, lam_ref, sg_ref, o_ref,
                 qa_ref, qb_ref, kn_ref, v1_ref, s_ref, e_ref, part_ref, *, lambda_init):
    lp = lam_ref[...]
    lam = (jnp.exp(jnp.sum(lp[0:1, :] * lp[1:2, :], axis=1, keepdims=True))
           - jnp.exp(jnp.sum(lp[2:3, :] * lp[3:4, :], axis=1, keepdims=True))
           + lambda_init)

    kn_ref[...] = _half_norm(k_ref[...], kg_ref[...]).astype(jnp.bfloat16)
    qn = _half_norm(q_ref[...], qg_ref[...]) * (HEAD_DIM ** -0.5 * LOG2E)
    first = lax.broadcasted_iota(jnp.int32, (SEQ, V_DIM), 1) < HEAD_DIM
    qa_ref[...] = jnp.where(first, qn, 0.0).astype(jnp.bfloat16)
    qb_ref[...] = jnp.where(first, 0.0, qn).astype(jnp.bfloat16)

    ones_col = lax.broadcasted_iota(jnp.int32, (SEQ, V_DIM), 1) == 0
    v1_ref[:, 0:V_DIM] = v_ref[...]
    v1_ref[:, V_DIM:2 * V_DIM] = jnp.where(ones_col, 1.0, 0.0).astype(jnp.bfloat16)

    tq = ATT_TQ
    nt = (((1,), (1,)), ((), ()))
    visible = (lax.broadcasted_iota(jnp.int32, (tq, tq), 1) // CHUNK
               <= lax.broadcasted_iota(jnp.int32, (tq, tq), 0) // CHUNK)
    halves = (qa_ref, qb_ref)
    units = [(qi, h) for qi in range(SEQ // tq) for h in range(2)]

    def stage_scores(u, slot):
        qi, h = units[u]
        k0 = qi * tq
        q = halves[h][k0:k0 + tq, :]
        dg = lax.dot_general(q, kn_ref[k0:k0 + tq, :], nt, preferred_element_type=jnp.float32)
        s_ref[slot, :, k0:k0 + tq] = jnp.where(visible, dg, -jnp.inf)
        if qi:
            s_ref[slot, :, 0:k0] = lax.dot_general(q, kn_ref[0:k0, :], nt,
                                                   preferred_element_type=jnp.float32)

    def stage_numerators(u, slot):
        kend = (units[u][0] + 1) * tq
        sc = s_ref[slot, :, 0:kend]
        m = jnp.max(sc, axis=1, keepdims=True)
        e_ref[slot, :, 0:kend] = jnp.exp2(sc - m).astype(jnp.bfloat16)

    def stage_values(u, slot):
        qi, h = units[u]
        k0 = qi * tq
        kend = k0 + tq
        acc = jnp.dot(e_ref[slot, :, 0:kend], v1_ref[0:kend, :],
                      preferred_element_type=jnp.float32)
        attn = acc[:, 0:V_DIM] * (1.0 / acc[:, V_DIM:V_DIM + 1])
        if h == 0:
            part_ref[...] = attn
        else:
            o = part_ref[...] - lam * attn
            o = _rms(o, sg_ref[...]) * (1.0 - lambda_init)
            o_ref[k0:kend, :] = o.astype(jnp.bfloat16)

    for step in range(len(units) + 2):
        if step < len(units):
            stage_scores(step, step % 2)
        if 1 <= step <= len(units):
            stage_numerators(step - 1, (step - 1) % 2)
        if step >= 2:
            stage_values(step - 2, step % 2)


def _attention(qk, v, qg2, kg2, lam_rows, subln_g, lambda_init):
    return pl.pallas_call(
        functools.partial(_attn_kernel, lambda_init=lambda_init),
        grid=(BATCH, N_HEADS),
        in_specs=[
            pl.BlockSpec((SEQ, V_DIM), lambda b, h: (b, h)),
            pl.BlockSpec((SEQ, V_DIM), lambda b, h: (b, N_HEADS + h)),
            pl.BlockSpec((SEQ, V_DIM), lambda b, h: (b, h)),
            _full((1, V_DIM)), _full((1, V_DIM)),
            _full((SUBLANES, HEAD_DIM)),
            _full((1, V_DIM)),
        ],
        out_specs=pl.BlockSpec((SEQ, V_DIM), lambda b, h: (b, h)),
        out_shape=jax.ShapeDtypeStruct((N_TOK, D_MODEL), jnp.bfloat16),
        scratch_shapes=[pltpu.VMEM((SEQ, V_DIM), jnp.bfloat16)] * 3 + [
            pltpu.VMEM((SEQ, 2 * V_DIM), jnp.bfloat16),
            pltpu.VMEM((2, ATT_TQ, SEQ), jnp.float32),
            pltpu.VMEM((2, ATT_TQ, SEQ), jnp.bfloat16),
            pltpu.VMEM((ATT_TQ, V_DIM), jnp.float32),
        ],
        compiler_params=_cparams(2),
        name="diff_attention",
    )(qk, qk, v, qg2, kg2, lam_rows, subln_g)


def _post_kernel(x_ref, m_ref, w_ref, g2_ref, wr_ref, br_ref,
                 x1_ref, h2p_ref, rf_ref, cnt_ref, run_ref):
    @pl.when(pl.program_id(0) == 0)
    def _():
        run_ref[...] = jnp.zeros_like(run_ref)

    mix = jnp.dot(m_ref[...], w_ref[...], preferred_element_type=jnp.float32)
    _residual_norm_route(x_ref[...] + mix, g2_ref, wr_ref, br_ref, run_ref,
                         x1_ref, h2p_ref, rf_ref, cnt_ref, POST_TS)


def _attn_post(x, o, w_o, g2, wr, br):
    return pl.pallas_call(
        _post_kernel,
        grid=(N_TOK // POST_TS,),
        in_specs=[
            pl.BlockSpec((POST_TS, D_MODEL), lambda i: (i, 0)),
            pl.BlockSpec((POST_TS, D_MODEL), lambda i: (i, 0)),
            _full((D_MODEL, D_MODEL)),
            _full((1, D_MODEL)),
            _full((D_MODEL, 2 * LANES)), _full((1, LANES)),
        ],
        out_specs=_route_out_specs(POST_TS, lambda i: i),
        out_shape=_route_out_shapes(),
        scratch_shapes=[pltpu.VMEM((1, LANES), jnp.float32)],
        compiler_params=_cparams(1),
        name="attn_post",
    )(x, o, w_o, g2, wr, br)


def _unrolled(n, body):
    def group(g, carry):
        for u in range(DMA_UNROLL):
            body(g * DMA_UNROLL + u)
        return carry
    lax.fori_loop(0, n // DMA_UNROLL, group, 0)


def _expert_kernel(be_ref, nv_ref, dest_ref, pad_ref, next_ref,
                   h2p_ref, w1_hbm, w3_hbm, w2_hbm, yb_ref,
                   rowtok_ref, xbuf_ref, sem, w1f_ref, w3f_ref, w2f_ref, wsem,
                   w1b_ref, w3b_ref, w2b_ref, *, layer):
    b = pl.program_id(0)
    n_valid = nv_ref[0]
    w_hbm = (w1_hbm, w3_hbm, w2_hbm)
    w_f32 = (w1f_ref, w3f_ref, w2f_ref)

    def weight_copies(expert):
        return [pltpu.make_async_copy(w_hbm[i].at[layer, expert], w_f32[i], wsem.at[i])
                for i in range(3)]

    def row_copy(block, slot, r):
        tok = rowtok_ref[block * ROW_BLOCK + r]
        return pltpu.make_async_copy(
            h2p_ref.at[pl.ds(pl.multiple_of(tok * PACK, PACK), PACK)],
            xbuf_ref.at[slot, pl.ds(r * PACK, PACK)],
            sem.at[slot])

    def block_arrival(slot):
        return pltpu.make_async_copy(
            h2p_ref.at[pl.ds(0, ROW_BLOCK * PACK)], xbuf_ref.at[slot], sem.at[slot])

    @pl.when(b == 0)
    def _():
        for c in weight_copies(be_ref[0]):
            c.start()
        def clear_padding(e, carry):
            def clear(j, c):
                rowtok_ref[j] = 0
                return c
            return lax.fori_loop(pad_ref[e], pad_ref[N_EXPERTS + e], clear, carry)
        lax.fori_loop(0, N_EXPERTS, clear_padding, 0)

        def place(t):
            rowtok_ref[dest_ref[TOP_K * t]] = t
            rowtok_ref[dest_ref[TOP_K * t + 1]] = t
        _unrolled(N_TOK, place)
        second = jnp.minimum(1, n_valid - 1)
        _unrolled(ROW_BLOCK, lambda r: row_copy(0, 0, r).start())
        _unrolled(ROW_BLOCK, lambda r: row_copy(second, 1, r).start())

    expert = be_ref[b]
    fresh = (b == 0) | (expert != be_ref[jnp.maximum(b - 1, 0)])

    @pl.when(fresh & (b < n_valid))
    def _():
        for c in weight_copies(expert):
            c.wait()
        w1b_ref[...] = w1f_ref[...].astype(jnp.bfloat16)
        w3b_ref[...] = w3f_ref[...].astype(jnp.bfloat16)
        w2b_ref[...] = w2f_ref[...].astype(jnp.bfloat16)
        following = next_ref[expert]

        @pl.when(following >= 0)
        def _():
            for c in weight_copies(following):
                c.start()

    @pl.when(b < n_valid)
    def _():
        slot = lax.rem(b, GATHER_SLOTS)
        block_arrival(slot).wait()
        x = _unpack_rows(xbuf_ref.at[slot], ROW_BLOCK).astype(jnp.bfloat16)
        ahead = jnp.minimum(b + 2, n_valid - 1)
        ahead_slot = lax.rem(b + 2, GATHER_SLOTS)
        for r in range(ROW_BLOCK):
            row_copy(ahead, ahead_slot, r).start()
        h1 = jnp.dot(x, w1b_ref[...], preferred_element_type=jnp.float32)
        h3 = jnp.dot(x, w3b_ref[...], preferred_element_type=jnp.float32)
        act = (h1 * jax.nn.sigmoid(h1) * h3).astype(jnp.bfloat16)
        y = jnp.dot(act, w2b_ref[...], preferred_element_type=jnp.float32)
        _pack_rows(yb_ref, y, ROW_BLOCK)

    @pl.when(b == n_valid - 1)
    def _():
        block_arrival(lax.rem(b + 1, GATHER_SLOTS)).wait()
        block_arrival(lax.rem(b + 2, GATHER_SLOTS)).wait()

    @pl.when(b >= n_valid)
    def _():
        yb_ref[...] = jnp.zeros_like(yb_ref)


def _experts(layer, block_e, n_valid, dest_flat, pad_rows, next_expert, h2p, w1, w3, w2):
    hbm = pl.BlockSpec(memory_space=pl.ANY)
    w_in = [pltpu.VMEM((D_MODEL, D_EXPERT), dt) for dt in (jnp.float32, jnp.bfloat16)]
    w_out = [pltpu.VMEM((D_EXPERT, D_MODEL), dt) for dt in (jnp.float32, jnp.bfloat16)]
    return pl.pallas_call(
        functools.partial(_expert_kernel, layer=layer),
        grid_spec=pltpu.PrefetchScalarGridSpec(
            num_scalar_prefetch=5,
            grid=(N_BLOCKS,),
            in_specs=[hbm, hbm, hbm, hbm],
            out_specs=pl.BlockSpec((ROW_BLOCK * PACK, LANES), lambda b, *_: (b, 0)),
            scratch_shapes=[
                pltpu.SMEM((N_ROWS,), jnp.int32),
                pltpu.VMEM((GATHER_SLOTS, ROW_BLOCK * PACK, LANES), jnp.float32),
                pltpu.SemaphoreType.DMA((GATHER_SLOTS,)),
                w_in[0], w_in[0], w_out[0],
                pltpu.SemaphoreType.DMA((3,)),
                w_in[1], w_in[1], w_out[1],
            ],
        ),
        out_shape=jax.ShapeDtypeStruct((N_ROWS * PACK, LANES), jnp.float32),
        compiler_params=_cparams(1),
        name="moe_experts",
    )(block_e, n_valid, dest_flat, pad_rows, next_expert, h2p, w1, w3, w2)


def _combine_kernel(pos_ref, x1_ref, rf_ref, yb_ref, out_ref, buf_ref, sem):
    i = pl.program_id(0)
    n = pl.num_programs(0)
    tm = COMB_TM

    def copy(tile, slot, j, k):
        p = pos_ref[TOP_K * (tile * tm + j) + k]
        return pltpu.make_async_copy(
            yb_ref.at[pl.ds(pl.multiple_of(p * PACK, PACK), PACK)],
            buf_ref.at[slot, k, pl.ds(pl.multiple_of(j * PACK, PACK), PACK)],
            sem.at[slot])

    def issue(tile, slot):
        def body(j):
            copy(tile, slot, j, 0).start()
            copy(tile, slot, j, 1).start()
        _unrolled(tm, body)

    def drain(tile, slot):
        def body(j):
            copy(tile, slot, j, 0).wait()
            copy(tile, slot, j, 1).wait()
        _unrolled(tm, body)

    slot = i % 2

    @pl.when(i == 0)
    def _():
        issue(0, 0)

    @pl.when(i + 1 < n)
    def _():
        issue(i + 1, 1 - slot)

    drain(i, slot)
    rf = rf_ref[...]
    y0 = _unpack_rows(buf_ref.at[slot, 0], tm)
    y1 = _unpack_rows(buf_ref.at[slot, 1], tm)
    out_ref[...] = x1_ref[...] + rf[:, 0:1] * y0 + rf[:, 1:2] * y1


def _combine(pos_flat, x1, rf, yb):
    return pl.pallas_call(
        _combine_kernel,
        grid_spec=pltpu.PrefetchScalarGridSpec(
            num_scalar_prefetch=1,
            grid=(N_TOK // COMB_TM,),
            in_specs=[
                pl.BlockSpec((COMB_TM, D_MODEL), lambda i, p: (i, 0)),
                pl.BlockSpec((COMB_TM, LANES), lambda i, p: (i, 0)),
                pl.BlockSpec(memory_space=pl.ANY),
            ],
            out_specs=pl.BlockSpec((COMB_TM, D_MODEL), lambda i, p: (i, 0)),
            scratch_shapes=[
                pltpu.VMEM((2, TOP_K, COMB_TM * PACK, LANES), jnp.float32),
                pltpu.SemaphoreType.DMA((2,)),
            ],
        ),
        out_shape=jax.ShapeDtypeStruct((N_TOK, D_MODEL), jnp.float32),
        compiler_params=_cparams(1),
        name="moe_combine",
    )(pos_flat, x1, rf, yb)


def _moe(layer, x1, h2p, rf, cnt, w1, w3, w2):
    experts = rf[:, 2:4].astype(jnp.int32)
    rank = rf[:, 4:6].astype(jnp.int32)
    counts = cnt[0, :N_EXPERTS].astype(jnp.int32)
    padded = (counts + ROW_BLOCK - 1) // ROW_BLOCK * ROW_BLOCK
    pad_ends = jnp.cumsum(padded)
    pad_starts = pad_ends - padded
    is_e = experts[:, :, None] == jnp.arange(N_EXPERTS, dtype=jnp.int32)
    dest = (jnp.sum(jnp.where(is_e, pad_starts, 0), axis=-1) + rank).reshape(-1)
    n_valid = (pad_ends[-1:] // ROW_BLOCK).astype(jnp.int32)
    block_row0 = jnp.arange(N_BLOCKS, dtype=jnp.int32) * ROW_BLOCK
    block_e = jnp.minimum(
        jnp.sum((pad_ends[None, :] <= block_row0[:, None]).astype(jnp.int32), axis=1),
        N_EXPERTS - 1).astype(jnp.int32)
    pad_rows = jnp.concatenate([pad_starts + counts, pad_ends]).astype(jnp.int32)
    ids = jnp.arange(N_EXPERTS, dtype=jnp.int32)
    later_nonempty = (counts > 0)[None, :] & (ids[None, :] > ids[:, None])
    following = jnp.min(jnp.where(later_nonempty, ids[None, :], N_EXPERTS), axis=1)
    next_expert = jnp.where(following < N_EXPERTS, following, -1).astype(jnp.int32)
    yb = _experts(layer, block_e, n_valid, dest, pad_rows, next_expert, h2p, w1, w3, w2)
    return _combine(dest, x1, rf, yb)


def _router_params(w_grp, b_grp, w_exp, b_exp):
    wr = jnp.zeros((D_MODEL, LANES), jnp.float32)
    wr = wr.at[:, :N_EXPERTS].set(w_exp).at[:, GRP_LANE0:GRP_LANE0 + N_GROUPS].set(w_grp)
    br = jnp.zeros((1, LANES), jnp.float32)
    br = br.at[0, :N_EXPERTS].set(b_exp).at[0, GRP_LANE0:GRP_LANE0 + N_GROUPS].set(b_grp)
    w_hi = wr.astype(jnp.bfloat16)
    w_lo = (wr - w_hi.astype(jnp.float32)).astype(jnp.bfloat16)
    return jnp.concatenate([w_hi, w_lo], axis=1), br


def kernel(x, norm1_g, norm2_g, conv_w_in, conv_b_in, conv_dw, conv_dw_b, conv_ln_g, conv_ln_b, conv_w_out, conv_b_out, attn_w_qkv, attn_q_g, attn_k_g, attn_lq1, attn_lk1, attn_lq2, attn_lk2, attn_subln_g, attn_w_o, moe_w_grp, moe_b_grp, moe_w_exp, moe_b_exp, moe_w1, moe_w3, moe_w2):
    bf16 = jnp.bfloat16
    row = lambda a: a.reshape(1, -1)
    xf = x.reshape(N_TOK, D_MODEL)

    wr, br = _router_params(moe_w_grp[0], moe_b_grp[0], moe_w_exp[0], moe_b_exp[0])
    dw = jnp.zeros((HALO, D_MODEL), jnp.float32).at[:CONV_WIDTH].set(conv_dw[0])
    x1, h2p, rf, cnt = _conv_layer(
        xf, row(norm1_g[0]), conv_w_in[0].astype(bf16), row(conv_b_in[0]), dw,
        row(conv_dw_b[0]), row(conv_ln_g[0]), row(conv_ln_b[0]),
        conv_w_out[0].astype(bf16), row(conv_b_out[0]), row(norm2_g[0]), wr, br)
    xf = _moe(0, x1, h2p, rf, cnt, moe_w1, moe_w3, moe_w2)

    lambda_init = 0.8 - 0.6 * math.exp(-0.3 * 1)
    wr, br = _router_params(moe_w_grp[1], moe_b_grp[1], moe_w_exp[1], moe_b_exp[1])
    qk, v = _qkv(xf, row(norm1_g[1]), attn_w_qkv[0].astype(bf16))
    lam_rows = jnp.zeros((SUBLANES, HEAD_DIM), jnp.float32)
    lam_rows = lam_rows.at[0].set(attn_lq1[0]).at[1].set(attn_lk1[0])
    lam_rows = lam_rows.at[2].set(attn_lq2[0]).at[3].set(attn_lk2[0])
    two = lambda g: jnp.concatenate([g, g]).reshape(1, V_DIM)
    o = _attention(qk, v, two(attn_q_g[0]), two(attn_k_g[0]), lam_rows,
                   row(attn_subln_g[0]), lambda_init)
    x1, h2p, rf, cnt = _attn_post(xf, o, attn_w_o[0].astype(bf16), row(norm2_g[1]), wr, br)
    xf = _moe(1, x1, h2p, rf, cnt, moe_w1, moe_w3, moe_w2)
    return xf.reshape(BATCH, SEQ, D_MODEL)
```

```python
import functools
import math

import jax
import jax.numpy as jnp
from jax import lax
from jax.experimental import pallas as pl
from jax.experimental.pallas import tpu as pltpu

D_MODEL = 1024
BATCH = 8
SEQ = 2048
N_TOK = BATCH * SEQ
CHUNK = 64
CONV_WIDTH = 31
N_HEADS = 8
HEAD_DIM = 64
V_DIM = 128
N_GROUPS = 4
EXPERTS_PER_GROUP = 8
N_EXPERTS = 32
TOP_K = 2
D_EXPERT = 512
EPS = 1e-6
LOG2E = math.log2(math.e)

LANES = 128
SUBLANES = 8
PACK = D_MODEL // LANES
VMEM_LIMIT = 56 * 1024 * 1024

CONV_TS = 512
HALO = 32
CONV_RC = 64
POST_TS = 512
QKV_TS = 512
ATT_TQ = 256
ROW_BLOCK = 256
GATHER_SLOTS = 3
N_ASSIGN = N_TOK * TOP_K
N_BLOCKS = N_ASSIGN // ROW_BLOCK + N_EXPERTS
N_ROWS = N_BLOCKS * ROW_BLOCK
COMB_TM = 256
DMA_UNROLL = 8
GRP_LANE0 = N_EXPERTS


def _cparams(n_axes):
    return pltpu.CompilerParams(
        dimension_semantics=("arbitrary",) * n_axes, vmem_limit_bytes=VMEM_LIMIT)


def _rms(x, g):
    return x * lax.rsqrt(jnp.mean(x * x, axis=-1, keepdims=True) + EPS) * g


def _pack_rows(ref, val, rows):
    for j in range(PACK):
        ref[pl.ds(j, rows, stride=PACK), :] = val[:, j * LANES:(j + 1) * LANES]


def _unpack_rows(ref, rows):
    return jnp.concatenate(
        [ref[pl.ds(j, rows, stride=PACK), :] for j in range(PACK)], axis=-1)


def _residual_norm_route(x1, g2_ref, wr_ref, br_ref, run_ref,
                         x1_ref, h2p_ref, rf_ref, cnt_ref, rows):
    x1_ref[...] = x1
    h2 = _rms(x1, g2_ref[...])
    _pack_rows(h2p_ref, h2, rows)

    hi = h2.astype(jnp.bfloat16)
    lo = (h2 - hi.astype(jnp.float32)).astype(jnp.bfloat16)
    ab = jnp.dot(hi, wr_ref[...], preferred_element_type=jnp.float32)
    c = jnp.dot(lo, wr_ref[:, 0:LANES], preferred_element_type=jnp.float32)
    logits = ab[:, 0:LANES] + ab[:, LANES:2 * LANES] + c + br_ref[...]
    lane = lax.broadcasted_iota(jnp.int32, (rows, LANES), 1)
    lane_f = lane.astype(jnp.float32)
    neg = jnp.float32(-jnp.inf)
    big = jnp.float32(1e9)

    gmask = (lane >= GRP_LANE0) & (lane < GRP_LANE0 + N_GROUPS)
    gl = jnp.where(gmask, logits, neg)
    gmax = jnp.max(gl, axis=1, keepdims=True)
    gidx = jnp.min(jnp.where(gl == gmax, lane_f, big), axis=1, keepdims=True) - GRP_LANE0
    gsum = jnp.sum(jnp.where(gmask, jnp.exp(gl - gmax), 0.0), axis=1, keepdims=True)
    grp_p = 1.0 / gsum

    lane0 = gidx * EXPERTS_PER_GROUP
    emask = (lane_f >= lane0) & (lane_f < lane0 + EXPERTS_PER_GROUP)
    el = jnp.where(emask, logits, neg)
    m1 = jnp.max(el, axis=1, keepdims=True)
    i1 = jnp.min(jnp.where(el == m1, lane_f, big), axis=1, keepdims=True)
    el2 = jnp.where(lane_f == i1, neg, el)
    m2 = jnp.max(el2, axis=1, keepdims=True)
    i2 = jnp.min(jnp.where(el2 == m2, lane_f, big), axis=1, keepdims=True)
    t = jnp.exp(m2 - m1)
    inv = 1.0 / (1.0 + t)
    g_first = grp_p * inv
    g_second = grp_p * t * inv

    sel1 = lane_f == i1
    sel2 = lane_f == i2
    member = jnp.where(sel1 | sel2, 1.0, 0.0).astype(jnp.bfloat16)
    r_i = lax.broadcasted_iota(jnp.int32, (rows, rows), 0)
    c_i = lax.broadcasted_iota(jnp.int32, (rows, rows), 1)
    tri = jnp.where(c_i < r_i, 1.0, 0.0).astype(jnp.bfloat16)
    cum = jnp.dot(tri, member, preferred_element_type=jnp.float32) + run_ref[...]
    rank1 = jnp.sum(jnp.where(sel1, cum, 0.0), axis=1, keepdims=True)
    rank2 = jnp.sum(jnp.where(sel2, cum, 0.0), axis=1, keepdims=True)
    run_new = run_ref[...] + jnp.sum(member.astype(jnp.float32), axis=0, keepdims=True)
    run_ref[...] = run_new
    cnt_ref[...] = jnp.broadcast_to(run_new, (SUBLANES, LANES))

    out = jnp.where(lane == 0, g_first, 0.0)
    out = jnp.where(lane == 1, g_second, out)
    out = jnp.where(lane == 2, i1, out)
    out = jnp.where(lane == 3, i2, out)
    out = jnp.where(lane == 4, rank1, out)
    out = jnp.where(lane == 5, rank2, out)
    rf_ref[...] = out


def _route_out_shapes():
    return (
        jax.ShapeDtypeStruct((N_TOK, D_MODEL), jnp.float32),
        jax.ShapeDtypeStruct((N_TOK * PACK, LANES), jnp.float32),
        jax.ShapeDtypeStruct((N_TOK, LANES), jnp.float32),
        jax.ShapeDtypeStruct((SUBLANES, LANES), jnp.float32),
    )


def _route_out_specs(ts, idx):
    return (
        pl.BlockSpec((ts, D_MODEL), lambda *a: (idx(*a), 0)),
        pl.BlockSpec((ts * PACK, LANES), lambda *a: (idx(*a), 0)),
        pl.BlockSpec((ts, LANES), lambda *a: (idx(*a), 0)),
        pl.BlockSpec((SUBLANES, LANES), lambda *a: (0, 0)),
    )


def _full(shape):
    return pl.BlockSpec(shape, lambda *a: (0,) * len(shape))


def _conv_kernel(x_ref, g1_ref, win_ref, bin_ref, dw_ref, dwb_ref, lng_ref, lnb_ref,
                 wout_ref, bout_ref, g2_ref, wr_ref, br_ref,
                 x1_ref, h2p_ref, rf_ref, cnt_ref,
                 ext_ref, conv_ref, run_ref):
    b = pl.program_id(0)
    s = pl.program_id(1)
    ts = CONV_TS

    @pl.when((b == 0) & (s == 0))
    def _():
        run_ref[...] = jnp.zeros_like(run_ref)

    @pl.when(s == 0)
    def _():
        ext_ref[0:HALO, :] = jnp.zeros((HALO, D_MODEL), jnp.float32)

    x = x_ref[...]
    h = _rms(x, g1_ref[...]).astype(jnp.bfloat16)
    u = jnp.dot(h, win_ref[...], preferred_element_type=jnp.float32) + bin_ref[...]
    glu = u[:, :D_MODEL] * jax.nn.sigmoid(u[:, D_MODEL:])
    ext_ref[HALO:HALO + ts, :] = glu

    base = HALO - (CONV_WIDTH - 1)

    def lane_chunk(c, carry):
        cols = pl.ds(pl.multiple_of(c * LANES, LANES), LANES)
        taps = [dw_ref[pl.ds(k, 1), cols] for k in range(CONV_WIDTH)]
        for r0 in range(0, ts, CONV_RC):
            acc = None
            for r in range(SUBLANES):
                rows = CONV_RC + (SUBLANES if r else 0)
                part = None
                for q in range((base + CONV_WIDTH - 1) // SUBLANES + 1):
                    k = SUBLANES * q + r - base
                    if 0 <= k < CONV_WIDTH:
                        term = ext_ref[pl.ds(r0 + SUBLANES * q, rows), cols] * taps[k]
                        part = term if part is None else part + term
                part = part[r:r + CONV_RC, :] if r else part
                acc = part if acc is None else acc + part
            conv_ref[pl.ds(r0, CONV_RC), cols] = acc
        return carry

    lax.fori_loop(0, D_MODEL // LANES, lane_chunk, 0)
    ext_ref[0:HALO, :] = ext_ref[ts:ts + HALO, :]

    v = conv_ref[...] + dwb_ref[...]
    mu = jnp.mean(v, axis=-1, keepdims=True)
    vc = v - mu
    var = jnp.mean(vc * vc, axis=-1, keepdims=True)
    y = vc * lax.rsqrt(var + EPS) * lng_ref[...] + lnb_ref[...]
    y = (y * jax.nn.sigmoid(y)).astype(jnp.bfloat16)
    mix = jnp.dot(y, wout_ref[...], preferred_element_type=jnp.float32) + bout_ref[...]
    _residual_norm_route(x + mix, g2_ref, wr_ref, br_ref, run_ref,
                         x1_ref, h2p_ref, rf_ref, cnt_ref, ts)


def _conv_layer(x, g1, w_in, b_in, dw, dw_b, ln_g, ln_b, w_out, b_out, g2, wr, br):
    ns = SEQ // CONV_TS
    tile = lambda b, s: b * ns + s
    return pl.pallas_call(
        _conv_kernel,
        grid=(BATCH, ns),
        in_specs=[
            pl.BlockSpec((CONV_TS, D_MODEL), lambda b, s: (tile(b, s), 0)),
            _full((1, D_MODEL)),
            _full((D_MODEL, 2 * D_MODEL)),
            _full((1, 2 * D_MODEL)),
            _full((HALO, D_MODEL)),
            _full((1, D_MODEL)), _full((1, D_MODEL)), _full((1, D_MODEL)),
            _full((D_MODEL, D_MODEL)),
            _full((1, D_MODEL)), _full((1, D_MODEL)),
            _full((D_MODEL, 2 * LANES)), _full((1, LANES)),
        ],
        out_specs=_route_out_specs(CONV_TS, tile),
        out_shape=_route_out_shapes(),
        scratch_shapes=[
            pltpu.VMEM((HALO + CONV_TS, D_MODEL), jnp.float32),
            pltpu.VMEM((CONV_TS, D_MODEL), jnp.float32),
            pltpu.VMEM((1, LANES), jnp.float32),
        ],
        compiler_params=_cparams(2),
        name="conv_mixer",
    )(x, g1, w_in, b_in, dw, dw_b, ln_g, ln_b, w_out, b_out, g2, wr, br)


def _qkv_kernel(x_ref, g1_ref, w_ref, qk_ref, v_ref):
    h = _rms(x_ref[...], g1_ref[...]).astype(jnp.bfloat16)
    qkv = jnp.dot(h, w_ref[...], preferred_element_type=jnp.float32)
    qk_ref[...] = qkv[:, :2 * D_MODEL]
    v_ref[...] = qkv[:, 2 * D_MODEL:].astype(jnp.bfloat16)


def _qkv(x, g1, w_qkv):
    return pl.pallas_call(
        _qkv_kernel,
        grid=(N_TOK // QKV_TS,),
        in_specs=[
            pl.BlockSpec((QKV_TS, D_MODEL), lambda i: (i, 0)),
            _full((1, D_MODEL)),
            _full((D_MODEL, 3 * D_MODEL)),
        ],
        out_specs=(
            pl.BlockSpec((QKV_TS, 2 * D_MODEL), lambda i: (i, 0)),
            pl.BlockSpec((QKV_TS, D_MODEL), lambda i: (i, 0)),
        ),
        out_shape=(
            jax.ShapeDtypeStruct((N_TOK, 2 * D_MODEL), jnp.float32),
            jax.ShapeDtypeStruct((N_TOK, D_MODEL), jnp.bfloat16),
        ),
        compiler_params=_cparams(1),
        name="qkv_proj",
    )(x, g1, w_qkv)


def _half_norm(z, gain):
    lane = lax.broadcasted_iota(jnp.int32, z.shape, 1)
    first = lane < HEAD_DIM
    zz = z * z
    ss_a = jnp.sum(jnp.where(first, zz, 0.0), axis=1, keepdims=True)
    ss_b = jnp.sum(jnp.where(first, 0.0, zz), axis=1, keepdims=True)
    inv = jnp.where(first, lax.rsqrt(ss_a * (1.0 / HEAD_DIM) + EPS),
                    lax.rsqrt(ss_b * (1.0 / HEAD_DIM) + EPS))
    return z * inv * gain


def _attn_kernel(q_ref, k_ref, v_ref, qg_ref, kg_ref, lam_ref, sg_ref, o_ref,
                 qa_ref, qb_ref, kn_ref, v1_ref, s_ref, e_ref, part_ref, *, lambda_init):
    lp = lam_ref[...]
    lam = (jnp.exp(jnp.sum(lp[0:1, :] * lp[1:2, :], axis=1, keepdims=True))
           - jnp.exp(jnp.sum(lp[2:3, :] * lp[3:4, :], axis=1, keepdims=True))
           + lambda_init)

    kn_ref[...] = _half_norm(k_ref[...], kg_ref[...]).astype(jnp.bfloat16)
    qn = _half_norm(q_ref[...], qg_ref[...]) * (HEAD_DIM ** -0.5 * LOG2E)
    first = lax.broadcasted_iota(jnp.int32, (SEQ, V_DIM), 1) < HEAD_DIM
    qa_ref[...] = jnp.where(first, qn, 0.0).astype(jnp.bfloat16)
    qb_ref[...] = jnp.where(first, 0.0, qn).astype(jnp.bfloat16)

    ones_col = lax.broadcasted_iota(jnp.int32, (SEQ, V_DIM), 1) == 0
    v1_ref[:, 0:V_DIM] = v_ref[...]
    v1_ref[:, V_DIM:2 * V_DIM] = jnp.where(ones_col, 1.0, 0.0).astype(jnp.bfloat16)

    tq = ATT_TQ
    nt = (((1,), (1,)), ((), ()))
    visible = (lax.broadcasted_iota(jnp.int32, (tq, tq), 1) // CHUNK
               <= lax.broadcasted_iota(jnp.int32, (tq, tq), 0) // CHUNK)
    halves = (qa_ref, qb_ref)
    units = [(qi, h) for qi in range(SEQ // tq) for h in range(2)]

    def stage_scores(u, slot):
        qi, h = units[u]
        k0 = qi * tq
        q = halves[h][k0:k0 + tq, :]
        dg = lax.dot_general(q, kn_ref[k0:k0 + tq, :], nt, preferred_element_type=jnp.float32)
        s_ref[slot, :, k0:k0 + tq] = jnp.where(visible, dg, -jnp.inf)
        if qi:
            s_ref[slot, :, 0:k0] = lax.dot_general(q, kn_ref[0:k0, :], nt,
                                                   preferred_element_type=jnp.float32)

    def stage_numerators(u, slot):
        kend = (units[u][0] + 1) * tq
        sc = s_ref[slot, :, 0:kend]
        m = jnp.max(sc, axis=1, keepdims=True)
        e_ref[slot, :, 0:kend] = jnp.exp2(sc - m).astype(jnp.bfloat16)

    def stage_values(u, slot):
        qi, h = units[u]
        k0 = qi * tq
        kend = k0 + tq
        acc = jnp.dot(e_ref[slot, :, 0:kend], v1_ref[0:kend, :],
                      preferred_element_type=jnp.float32)
        attn = acc[:, 0:V_DIM] * (1.0 / acc[:, V_DIM:V_DIM + 1])
        if h == 0:
            part_ref[...] = attn
        else:
            o = part_ref[...] - lam * attn
            o = _rms(o, sg_ref[...]) * (1.0 - lambda_init)
            o_ref[k0:kend, :] = o.astype(jnp.bfloat16)

    for step in range(len(units) + 2):
        if step < len(units):
            stage_scores(step, step % 2)
        if 1 <= step <= len(units):
            stage_numerators(step - 1, (step - 1) % 2)
        if step >= 2:
            stage_values(step - 2, step % 2)


def _attention(qk, v, qg2, kg2, lam_rows, subln_g, lambda_init):
    return pl.pallas_call(
        functools.partial(_attn_kernel, lambda_init=lambda_init),
        grid=(BATCH, N_HEADS),
        in_specs=[
            pl.BlockSpec((SEQ, V_DIM), lambda b, h: (b, h)),
            pl.BlockSpec((SEQ, V_DIM), lambda b, h: (b, N_HEADS + h)),
            pl.BlockSpec((SEQ, V_DIM), lambda b, h: (b, h)),
            _full((1, V_DIM)), _full((1, V_DIM)),
            _full((SUBLANES, HEAD_DIM)),
            _full((1, V_DIM)),
        ],
        out_specs=pl.BlockSpec((SEQ, V_DIM), lambda b, h: (b, h)),
        out_shape=jax.ShapeDtypeStruct((N_TOK, D_MODEL), jnp.bfloat16),
        scratch_shapes=[pltpu.VMEM((SEQ, V_DIM), jnp.bfloat16)] * 3 + [
            pltpu.VMEM((SEQ, 2 * V_DIM), jnp.bfloat16),
            pltpu.VMEM((2, ATT_TQ, SEQ), jnp.float32),
            pltpu.VMEM((2, ATT_TQ, SEQ), jnp.bfloat16),
            pltpu.VMEM((ATT_TQ, V_DIM), jnp.float32),
        ],
        compiler_params=_cparams(2),
        name="diff_attention",
    )(qk, qk, v, qg2, kg2, lam_rows, subln_g)


def _post_kernel(x_ref, m_ref, w_ref, g2_ref, wr_ref, br_ref,
                 x1_ref, h2p_ref, rf_ref, cnt_ref, run_ref):
    @pl.when(pl.program_id(0) == 0)
    def _():
        run_ref[...] = jnp.zeros_like(run_ref)

    mix = jnp.dot(m_ref[...], w_ref[...], preferred_element_type=jnp.float32)
    _residual_norm_route(x_ref[...] + mix, g2_ref, wr_ref, br_ref, run_ref,
                         x1_ref, h2p_ref, rf_ref, cnt_ref, POST_TS)


def _attn_post(x, o, w_o, g2, wr, br):
    return pl.pallas_call(
        _post_kernel,
        grid=(N_TOK // POST_TS,),
        in_specs=[
            pl.BlockSpec((POST_TS, D_MODEL), lambda i: (i, 0)),
            pl.BlockSpec((POST_TS, D_MODEL), lambda i: (i, 0)),
            _full((D_MODEL, D_MODEL)),
            _full((1, D_MODEL)),
            _full((D_MODEL, 2 * LANES)), _full((1, LANES)),
        ],
        out_specs=_route_out_specs(POST_TS, lambda i: i),
        out_shape=_route_out_shapes(),
        scratch_shapes=[pltpu.VMEM((1, LANES), jnp.float32)],
        compiler_params=_cparams(1),
        name="attn_post",
    )(x, o, w_o, g2, wr, br)


def _unrolled(n, body):
    def group(g, carry):
        for u in range(DMA_UNROLL):
            body(g * DMA_UNROLL + u)
        return carry
    lax.fori_loop(0, n // DMA_UNROLL, group, 0)


def _expert_kernel(be_ref, nv_ref, dest_ref, pad_ref, next_ref,
                   h2p_ref, w1_hbm, w3_hbm, w2_hbm, yb_ref,
                   rowtok_ref, xbuf_ref, sem, w1f_ref, w3f_ref, w2f_ref, wsem,
                   w1b_ref, w3b_ref, w2b_ref, *, layer):
    b = pl.program_id(0)
    n_valid = nv_ref[0]
    w_hbm = (w1_hbm, w3_hbm, w2_hbm)
    w_f32 = (w1f_ref, w3f_ref, w2f_ref)

    def weight_copies(expert):
        return [pltpu.make_async_copy(w_hbm[i].at[layer, expert], w_f32[i], wsem.at[i])
                for i in range(3)]

    def row_copy(block, slot, r):
        tok = rowtok_ref[block * ROW_BLOCK + r]
        return pltpu.make_async_copy(
            h2p_ref.at[pl.ds(pl.multiple_of(tok * PACK, PACK), PACK)],
            xbuf_ref.at[slot, pl.ds(pl.multiple_of(r * PACK, PACK), PACK)],
            sem.at[slot])

    def block_arrival(slot):
        return pltpu.make_async_copy(
            h2p_ref.at[pl.ds(0, ROW_BLOCK * PACK)], xbuf_ref.at[slot], sem.at[slot])

    @pl.when(b == 0)
    def _():
        for c in weight_copies(be_ref[0]):
            c.start()
        def clear_padding(e, carry):
            def clear(j, c):
                rowtok_ref[j] = 0
                return c
            return lax.fori_loop(pad_ref[e], pad_ref[N_EXPERTS + e], clear, carry)
        lax.fori_loop(0, N_EXPERTS, clear_padding, 0)

        def place(t):
            rowtok_ref[dest_ref[TOP_K * t]] = t
            rowtok_ref[dest_ref[TOP_K * t + 1]] = t
        _unrolled(N_TOK, place)
        second = jnp.minimum(1, n_valid - 1)
        _unrolled(ROW_BLOCK, lambda r: row_copy(0, 0, r).start())
        _unrolled(ROW_BLOCK, lambda r: row_copy(second, 1, r).start())

    expert = be_ref[b]
    fresh = (b == 0) | (expert != be_ref[jnp.maximum(b - 1, 0)])

    @pl.when(fresh & (b < n_valid))
    def _():
        for c in weight_copies(expert):
            c.wait()
        w1b_ref[...] = w1f_ref[...].astype(jnp.bfloat16)
        w3b_ref[...] = w3f_ref[...].astype(jnp.bfloat16)
        w2b_ref[...] = w2f_ref[...].astype(jnp.bfloat16)
        following = next_ref[expert]

        @pl.when(following >= 0)
        def _():
            for c in weight_copies(following):
                c.start()

    @pl.when(b < n_valid)
    def _():
        ahead = jnp.minimum(b + 2, n_valid - 1)
        ahead_slot = lax.rem(b + 2, GATHER_SLOTS)
        _unrolled(ROW_BLOCK, lambda r: row_copy(ahead, ahead_slot, r).start())
        slot = lax.rem(b, GATHER_SLOTS)
        block_arrival(slot).wait()
        x = _unpack_rows(xbuf_ref.at[slot], ROW_BLOCK).astype(jnp.bfloat16)
        h1 = jnp.dot(x, w1b_ref[...], preferred_element_type=jnp.float32)
        h3 = jnp.dot(x, w3b_ref[...], preferred_element_type=jnp.float32)
        act = (h1 * jax.nn.sigmoid(h1) * h3).astype(jnp.bfloat16)
        y = jnp.dot(act, w2b_ref[...], preferred_element_type=jnp.float32)
        _pack_rows(yb_ref, y, ROW_BLOCK)

    @pl.when(b == n_valid - 1)
    def _():
        block_arrival(lax.rem(b + 1, GATHER_SLOTS)).wait()
        block_arrival(lax.rem(b + 2, GATHER_SLOTS)).wait()

    @pl.when(b >= n_valid)
    def _():
        yb_ref[...] = jnp.zeros_like(yb_ref)


def _experts(layer, block_e, n_valid, dest_flat, pad_rows, next_expert, h2p, w1, w3, w2):
    hbm = pl.BlockSpec(memory_space=pl.ANY)
    w_in = [pltpu.VMEM((D_MODEL, D_EXPERT), dt) for dt in (jnp.float32, jnp.bfloat16)]
    w_out = [pltpu.VMEM((D_EXPERT, D_MODEL), dt) for dt in (jnp.float32, jnp.bfloat16)]
    return pl.pallas_call(
        functools.partial(_expert_kernel, layer=layer),
        grid_spec=pltpu.PrefetchScalarGridSpec(
            num_scalar_prefetch=5,
            grid=(N_BLOCKS,),
            in_specs=[hbm, hbm, hbm, hbm],
            out_specs=pl.BlockSpec((ROW_BLOCK * PACK, LANES), lambda b, *_: (b, 0)),
            scratch_shapes=[
                pltpu.SMEM((N_ROWS,), jnp.int32),
                pltpu.VMEM((GATHER_SLOTS, ROW_BLOCK * PACK, LANES), jnp.float32),
                pltpu.SemaphoreType.DMA((GATHER_SLOTS,)),
                w_in[0], w_in[0], w_out[0],
                pltpu.SemaphoreType.DMA((3,)),
                w_in[1], w_in[1], w_out[1],
            ],
        ),
        out_shape=jax.ShapeDtypeStruct((N_ROWS * PACK, LANES), jnp.float32),
        compiler_params=_cparams(1),
        name="moe_experts",
    )(block_e, n_valid, dest_flat, pad_rows, next_expert, h2p, w1, w3, w2)


def _combine_kernel(pos_ref, x1_ref, rf_ref, yb_ref, out_ref, buf_ref, sem):
    i = pl.program_id(0)
    n = pl.num_programs(0)
    tm = COMB_TM

    def copy(tile, slot, j, k):
        p = pos_ref[TOP_K * (tile * tm + j) + k]
        return pltpu.make_async_copy(
            yb_ref.at[pl.ds(pl.multiple_of(p * PACK, PACK), PACK)],
            buf_ref.at[slot, k, pl.ds(pl.multiple_of(j * PACK, PACK), PACK)],
            sem.at[slot])

    def issue(tile, slot):
        def body(j):
            copy(tile, slot, j, 0).start()
            copy(tile, slot, j, 1).start()
        _unrolled(tm, body)

    def drain(tile, slot):
        def body(j):
            copy(tile, slot, j, 0).wait()
            copy(tile, slot, j, 1).wait()
        _unrolled(tm, body)

    slot = i % 2

    @pl.when(i == 0)
    def _():
        issue(0, 0)

    @pl.when(i + 1 < n)
    def _():
        issue(i + 1, 1 - slot)

    drain(i, slot)
    rf = rf_ref[...]
    y0 = _unpack_rows(buf_ref.at[slot, 0], tm)
    y1 = _unpack_rows(buf_ref.at[slot, 1], tm)
    out_ref[...] = x1_ref[...] + rf[:, 0:1] * y0 + rf[:, 1:2] * y1


def _combine(pos_flat, x1, rf, yb):
    return pl.pallas_call(
        _combine_kernel,
        grid_spec=pltpu.PrefetchScalarGridSpec(
            num_scalar_prefetch=1,
            grid=(N_TOK // COMB_TM,),
            in_specs=[
                pl.BlockSpec((COMB_TM, D_MODEL), lambda i, p: (i, 0)),
                pl.BlockSpec((COMB_TM, LANES), lambda i, p: (i, 0)),
                pl.BlockSpec(memory_space=pl.ANY),
            ],
            out_specs=pl.BlockSpec((COMB_TM, D_MODEL), lambda i, p: (i, 0)),
            scratch_shapes=[
                pltpu.VMEM((2, TOP_K, COMB_TM * PACK, LANES), jnp.float32),
                pltpu.SemaphoreType.DMA((2,)),
            ],
        ),
        out_shape=jax.ShapeDtypeStruct((N_TOK, D_MODEL), jnp.float32),
        compiler_params=_cparams(1),
        name="moe_combine",
    )(pos_flat, x1, rf, yb)


def _moe(layer, x1, h2p, rf, cnt, w1, w3, w2):
    experts = rf[:, 2:4].astype(jnp.int32)
    rank = rf[:, 4:6].astype(jnp.int32)
    counts = cnt[0, :N_EXPERTS].astype(jnp.int32)
    padded = (counts + ROW_BLOCK - 1) // ROW_BLOCK * ROW_BLOCK
    pad_ends = jnp.cumsum(padded)
    pad_starts = pad_ends - padded
    is_e = experts[:, :, None] == jnp.arange(N_EXPERTS, dtype=jnp.int32)
    dest = (jnp.sum(jnp.where(is_e, pad_starts, 0), axis=-1) + rank).reshape(-1)
    n_valid = (pad_ends[-1:] // ROW_BLOCK).astype(jnp.int32)
    block_row0 = jnp.arange(N_BLOCKS, dtype=jnp.int32) * ROW_BLOCK
    block_e = jnp.minimum(
        jnp.sum((pad_ends[None, :] <= block_row0[:, None]).astype(jnp.int32), axis=1),
        N_EXPERTS - 1).astype(jnp.int32)
    pad_rows = jnp.concatenate([pad_starts + counts, pad_ends]).astype(jnp.int32)
    ids = jnp.arange(N_EXPERTS, dtype=jnp.int32)
    later_nonempty = (counts > 0)[None, :] & (ids[None, :] > ids[:, None])
    following = jnp.min(jnp.where(later_nonempty, ids[None, :], N_EXPERTS), axis=1)
    next_expert = jnp.where(following < N_EXPERTS, following, -1).astype(jnp.int32)
    yb = _experts(layer, block_e, n_valid, dest, pad_rows, next_expert, h2p, w1, w3, w2)
    return _combine(dest, x1, rf, yb)


def _router_params(w_grp, b_grp, w_exp, b_exp):
    wr = jnp.zeros((D_MODEL, LANES), jnp.float32)
    wr = wr.at[:, :N_EXPERTS].set(w_exp).at[:, GRP_LANE0:GRP_LANE0 + N_GROUPS].set(w_grp)
    br = jnp.zeros((1, LANES), jnp.float32)
    br = br.at[0, :N_EXPERTS].set(b_exp).at[0, GRP_LANE0:GRP_LANE0 + N_GROUPS].set(b_grp)
    w_hi = wr.astype(jnp.bfloat16)
    w_lo = (wr - w_hi.astype(jnp.float32)).astype(jnp.bfloat16)
    return jnp.concatenate([w_hi, w_lo], axis=1), br


def kernel(x, norm1_g, norm2_g, conv_w_in, conv_b_in, conv_dw, conv_dw_b, conv_ln_g, conv_ln_b, conv_w_out, conv_b_out, attn_w_qkv, attn_q_g, attn_k_g, attn_lq1, attn_lk1, attn_lq2, attn_lk2, attn_subln_g, attn_w_o, moe_w_grp, moe_b_grp, moe_w_exp, moe_b_exp, moe_w1, moe_w3, moe_w2):
    bf16 = jnp.bfloat16
    row = lambda a: a.reshape(1, -1)
    xf = x.reshape(N_TOK, D_MODEL)

    wr, br = _router_params(moe_w_grp[0], moe_b_grp[0], moe_w_exp[0], moe_b_exp[0])
    dw = jnp.zeros((HALO, D_MODEL), jnp.float32).at[:CONV_WIDTH].set(conv_dw[0])
    x1, h2p, rf, cnt = _conv_layer(
        xf, row(norm1_g[0]), conv_w_in[0].astype(bf16), row(conv_b_in[0]), dw,
        row(conv_dw_b[0]), row(conv_ln_g[0]), row(conv_ln_b[0]),
        conv_w_out[0].astype(bf16), row(conv_b_out[0]), row(norm2_g[0]), wr, br)
    xf = _moe(0, x1, h2p, rf, cnt, moe_w1, moe_w3, moe_w2)

    lambda_init = 0.8 - 0.6 * math.exp(-0.3 * 1)
    wr, br = _router_params(moe_w_grp[1], moe_b_grp[1], moe_w_exp[1], moe_b_exp[1])
    qk, v = _qkv(xf, row(norm1_g[1]), attn_w_qkv[0].astype(bf16))
    lam_rows = jnp.zeros((SUBLANES, HEAD_DIM), jnp.float32)
    lam_rows = lam_rows.at[0].set(attn_lq1[0]).at[1].set(attn_lk1[0])
    lam_rows = lam_rows.at[2].set(attn_lq2[0]).at[3].set(attn_lk2[0])
    two = lambda g: jnp.concatenate([g, g]).reshape(1, V_DIM)
    o = _attention(qk, v, two(attn_q_g[0]), two(attn_k_g[0]), lam_rows,
                   row(attn_subln_g[0]), lambda_init)
    x1, h2p, rf, cnt = _attn_post(xf, o, attn_w_o[0].astype(bf16), row(norm2_g[1]), wr, br)
    xf = _moe(1, x1, h2p, rf, cnt, moe_w1, moe_w3, moe_w2)
    return xf.reshape(BATCH, SEQ, D_MODEL)
```

```python
import functools
import math

import jax
import jax.numpy as jnp
from jax import lax
from jax.experimental import pallas as pl
from jax.experimental.pallas import tpu as pltpu

D_MODEL = 1024
BATCH = 8
SEQ = 2048
N_TOK = BATCH * SEQ
CHUNK = 64
CONV_WIDTH = 31
N_HEADS = 8
HEAD_DIM = 64
V_DIM = 128
N_GROUPS = 4
EXPERTS_PER_GROUP = 8
N_EXPERTS = 32
TOP_K = 2
D_EXPERT = 512
EPS = 1e-6
LOG2E = math.log2(math.e)

LANES = 128
SUBLANES = 8
PACK = D_MODEL // LANES
VMEM_LIMIT = 56 * 1024 * 1024

CONV_TS = 512
HALO = 32
CONV_RC = 64
POST_TS = 512
QKV_TS = 512
ATT_TQ = 256
ROW_BLOCK = 256
GATHER_SLOTS = 3
N_ASSIGN = N_TOK * TOP_K
N_BLOCKS = N_ASSIGN // ROW_BLOCK + N_EXPERTS
N_ROWS = N_BLOCKS * ROW_BLOCK
COMB_TM = 256
DMA_UNROLL = 8
GRP_LANE0 = N_EXPERTS


def _cparams(n_axes):
    return pltpu.CompilerParams(
        dimension_semantics=("arbitrary",) * n_axes, vmem_limit_bytes=VMEM_LIMIT)


def _rms(x, g):
    return x * lax.rsqrt(jnp.mean(x * x, axis=-1, keepdims=True) + EPS) * g


def _pack_rows(ref, val, rows):
    for j in range(PACK):
        ref[pl.ds(j, rows, stride=PACK), :] = val[:, j * LANES:(j + 1) * LANES]


def _unpack_rows(ref, rows):
    return jnp.concatenate(
        [ref[pl.ds(j, rows, stride=PACK), :] for j in range(PACK)], axis=-1)


def _residual_norm_route(x1, g2_ref, wr_ref, br_ref, run_ref,
                         x1_ref, h2p_ref, rf_ref, cnt_ref, rows):
    x1_ref[...] = x1
    h2 = _rms(x1, g2_ref[...])
    _pack_rows(h2p_ref, h2, rows)

    hi = h2.astype(jnp.bfloat16)
    lo = (h2 - hi.astype(jnp.float32)).astype(jnp.bfloat16)
    ab = jnp.dot(hi, wr_ref[...], preferred_element_type=jnp.float32)
    c = jnp.dot(lo, wr_ref[:, 0:LANES], preferred_element_type=jnp.float32)
    logits = ab[:, 0:LANES] + ab[:, LANES:2 * LANES] + c + br_ref[...]
    lane = lax.broadcasted_iota(jnp.int32, (rows, LANES), 1)
    lane_f = lane.astype(jnp.float32)
    neg = jnp.float32(-jnp.inf)
    big = jnp.float32(1e9)

    gmask = (lane >= GRP_LANE0) & (lane < GRP_LANE0 + N_GROUPS)
    gl = jnp.where(gmask, logits, neg)
    gmax = jnp.max(gl, axis=1, keepdims=True)
    gidx = jnp.min(jnp.where(gl == gmax, lane_f, big), axis=1, keepdims=True) - GRP_LANE0
    gsum = jnp.sum(jnp.where(gmask, jnp.exp(gl - gmax), 0.0), axis=1, keepdims=True)
    grp_p = 1.0 / gsum

    lane0 = gidx * EXPERTS_PER_GROUP
    emask = (lane_f >= lane0) & (lane_f < lane0 + EXPERTS_PER_GROUP)
    el = jnp.where(emask, logits, neg)
    m1 = jnp.max(el, axis=1, keepdims=True)
    i1 = jnp.min(jnp.where(el == m1, lane_f, big), axis=1, keepdims=True)
    el2 = jnp.where(lane_f == i1, neg, el)
    m2 = jnp.max(el2, axis=1, keepdims=True)
    i2 = jnp.min(jnp.where(el2 == m2, lane_f, big), axis=1, keepdims=True)
    t = jnp.exp(m2 - m1)
    inv = 1.0 / (1.0 + t)
    g_first = grp_p * inv
    g_second = grp_p * t * inv

    sel1 = lane_f == i1
    sel2 = lane_f == i2
    member = jnp.where(sel1 | sel2, 1.0, 0.0).astype(jnp.bfloat16)
    r_i = lax.broadcasted_iota(jnp.int32, (rows, rows), 0)
    c_i = lax.broadcasted_iota(jnp.int32, (rows, rows), 1)
    tri = jnp.where(c_i < r_i, 1.0, 0.0).astype(jnp.bfloat16)
    cum = jnp.dot(tri, member, preferred_element_type=jnp.float32) + run_ref[...]
    rank1 = jnp.sum(jnp.where(sel1, cum, 0.0), axis=1, keepdims=True)
    rank2 = jnp.sum(jnp.where(sel2, cum, 0.0), axis=1, keepdims=True)
    run_new = run_ref[...] + jnp.sum(member.astype(jnp.float32), axis=0, keepdims=True)
    run_ref[...] = run_new
    cnt_ref[...] = jnp.broadcast_to(run_new, (SUBLANES, LANES))

    out = jnp.where(lane == 0, g_first, 0.0)
    out = jnp.where(lane == 1, g_second, out)
    out = jnp.where(lane == 2, i1, out)
    out = jnp.where(lane == 3, i2, out)
    out = jnp.where(lane == 4, rank1, out)
    out = jnp.where(lane == 5, rank2, out)
    rf_ref[...] = out


def _route_out_shapes():
    return (
        jax.ShapeDtypeStruct((N_TOK, D_MODEL), jnp.float32),
        jax.ShapeDtypeStruct((N_TOK * PACK, LANES), jnp.float32),
        jax.ShapeDtypeStruct((N_TOK, LANES), jnp.float32),
        jax.ShapeDtypeStruct((SUBLANES, LANES), jnp.float32),
    )


def _route_out_specs(ts, idx):
    return (
        pl.BlockSpec((ts, D_MODEL), lambda *a: (idx(*a), 0)),
        pl.BlockSpec((ts * PACK, LANES), lambda *a: (idx(*a), 0)),
        pl.BlockSpec((ts, LANES), lambda *a: (idx(*a), 0)),
        pl.BlockSpec((SUBLANES, LANES), lambda *a: (0, 0)),
    )


def _full(shape):
    return pl.BlockSpec(shape, lambda *a: (0,) * len(shape))


def _conv_kernel(x_ref, g1_ref, win_ref, bin_ref, dw_ref, dwb_ref, lng_ref, lnb_ref,
                 wout_ref, bout_ref, g2_ref, wr_ref, br_ref,
                 x1_ref, h2p_ref, rf_ref, cnt_ref,
                 ext_ref, conv_ref, run_ref):
    b = pl.program_id(0)
    s = pl.program_id(1)
    ts = CONV_TS

    @pl.when((b == 0) & (s == 0))
    def _():
        run_ref[...] = jnp.zeros_like(run_ref)

    @pl.when(s == 0)
    def _():
        ext_ref[0:HALO, :] = jnp.zeros((HALO, D_MODEL), jnp.float32)

    x = x_ref[...]
    h = _rms(x, g1_ref[...]).astype(jnp.bfloat16)
    u = jnp.dot(h, win_ref[...], preferred_element_type=jnp.float32) + bin_ref[...]
    glu = u[:, :D_MODEL] * jax.nn.sigmoid(u[:, D_MODEL:])
    ext_ref[HALO:HALO + ts, :] = glu

    base = HALO - (CONV_WIDTH - 1)

    def lane_chunk(c, carry):
        cols = pl.ds(pl.multiple_of(c * LANES, LANES), LANES)
        taps = [dw_ref[pl.ds(k, 1), cols] for k in range(CONV_WIDTH)]
        for r0 in range(0, ts, CONV_RC):
            acc = None
            for r in range(SUBLANES):
                rows = CONV_RC + (SUBLANES if r else 0)
                part = None
                for q in range((base + CONV_WIDTH - 1) // SUBLANES + 1):
                    k = SUBLANES * q + r - base
                    if 0 <= k < CONV_WIDTH:
                        term = ext_ref[pl.ds(r0 + SUBLANES * q, rows), cols] * taps[k]
                        part = term if part is None else part + term
                part = part[r:r + CONV_RC, :] if r else part
                acc = part if acc is None else acc + part
            conv_ref[pl.ds(r0, CONV_RC), cols] = acc
        return carry

    lax.fori_loop(0, D_MODEL // LANES, lane_chunk, 0)
    ext_ref[0:HALO, :] = ext_ref[ts:ts + HALO, :]

    v = conv_ref[...] + dwb_ref[...]
    mu = jnp.mean(v, axis=-1, keepdims=True)
    vc = v - mu
    var = jnp.mean(vc * vc, axis=-1, keepdims=True)
    y = vc * lax.rsqrt(var + EPS) * lng_ref[...] + lnb_ref[...]
    y = (y * jax.nn.sigmoid(y)).astype(jnp.bfloat16)
    mix = jnp.dot(y, wout_ref[...], preferred_element_type=jnp.float32) + bout_ref[...]
    _residual_norm_route(x + mix, g2_ref, wr_ref, br_ref, run_ref,
                         x1_ref, h2p_ref, rf_ref, cnt_ref, ts)


def _conv_layer(x, g1, w_in, b_in, dw, dw_b, ln_g, ln_b, w_out, b_out, g2, wr, br):
    ns = SEQ // CONV_TS
    tile = lambda b, s: b * ns + s
    return pl.pallas_call(
        _conv_kernel,
        grid=(BATCH, ns),
        in_specs=[
            pl.BlockSpec((CONV_TS, D_MODEL), lambda b, s: (tile(b, s), 0)),
            _full((1, D_MODEL)),
            _full((D_MODEL, 2 * D_MODEL)),
            _full((1, 2 * D_MODEL)),
            _full((HALO, D_MODEL)),
            _full((1, D_MODEL)), _full((1, D_MODEL)), _full((1, D_MODEL)),
            _full((D_MODEL, D_MODEL)),
            _full((1, D_MODEL)), _full((1, D_MODEL)),
            _full((D_MODEL, 2 * LANES)), _full((1, LANES)),
        ],
        out_specs=_route_out_specs(CONV_TS, tile),
        out_shape=_route_out_shapes(),
        scratch_shapes=[
            pltpu.VMEM((HALO + CONV_TS, D_MODEL), jnp.float32),
            pltpu.VMEM((CONV_TS, D_MODEL), jnp.float32),
            pltpu.VMEM((1, LANES), jnp.float32),
        ],
        compiler_params=_cparams(2),
        name="conv_mixer",
    )(x, g1, w_in, b_in, dw, dw_b, ln_g, ln_b, w_out, b_out, g2, wr, br)


def _half_norm(z, gain):
    lane = lax.broadcasted_iota(jnp.int32, z.shape, 1)
    first = lane < HEAD_DIM
    zz = z * z
    ss_a = jnp.sum(jnp.where(first, zz, 0.0), axis=1, keepdims=True)
    ss_b = jnp.sum(jnp.where(first, 0.0, zz), axis=1, keepdims=True)
    inv = jnp.where(first, lax.rsqrt(ss_a * (1.0 / HEAD_DIM) + EPS),
                    lax.rsqrt(ss_b * (1.0 / HEAD_DIM) + EPS))
    return z * inv * gain


def _attn_kernel(q_ref, k_ref, v_ref, qg_ref, kg_ref, lam_ref, sg_ref, o_ref,
                 qa_ref, qb_ref, kn_ref, v1_ref, s_ref, e_ref, part_ref, *, lambda_init):
    lp = lam_ref[...]
    lam = (jnp.exp(jnp.sum(lp[0:1, :] * lp[1:2, :], axis=1, keepdims=True))
           - jnp.exp(jnp.sum(lp[2:3, :] * lp[3:4, :], axis=1, keepdims=True))
           + lambda_init)

    kn_ref[...] = _half_norm(k_ref[...], kg_ref[...]).astype(jnp.bfloat16)
    qn = _half_norm(q_ref[...], qg_ref[...]) * (HEAD_DIM ** -0.5 * LOG2E)
    first = lax.broadcasted_iota(jnp.int32, (SEQ, V_DIM), 1) < HEAD_DIM
    qa_ref[...] = jnp.where(first, qn, 0.0).astype(jnp.bfloat16)
    qb_ref[...] = jnp.where(first, 0.0, qn).astype(jnp.bfloat16)

    ones_col = lax.broadcasted_iota(jnp.int32, (SEQ, V_DIM), 1) == 0
    v1_ref[:, 0:V_DIM] = v_ref[...]
    v1_ref[:, V_DIM:2 * V_DIM] = jnp.where(ones_col, 1.0, 0.0).astype(jnp.bfloat16)

    tq = ATT_TQ
    nt = (((1,), (1,)), ((), ()))
    visible = (lax.broadcasted_iota(jnp.int32, (tq, tq), 1) // CHUNK
               <= lax.broadcasted_iota(jnp.int32, (tq, tq), 0) // CHUNK)
    halves = (qa_ref, qb_ref)
    units = [(qi, h) for qi in range(SEQ // tq) for h in range(2)]

    def stage_scores(u, slot):
        qi, h = units[u]
        k0 = qi * tq
        q = halves[h][k0:k0 + tq, :]
        dg = lax.dot_general(q, kn_ref[k0:k0 + tq, :], nt, preferred_element_type=jnp.float32)
        s_ref[slot, :, k0:k0 + tq] = jnp.where(visible, dg, -jnp.inf)
        if qi:
            s_ref[slot, :, 0:k0] = lax.dot_general(q, kn_ref[0:k0, :], nt,
                                                   preferred_element_type=jnp.float32)

    def stage_numerators(u, slot):
        kend = (units[u][0] + 1) * tq
        sc = s_ref[slot, :, 0:kend]
        m = jnp.max(sc, axis=1, keepdims=True)
        e_ref[slot, :, 0:kend] = jnp.exp2(sc - m).astype(jnp.bfloat16)

    def stage_values(u, slot):
        qi, h = units[u]
        k0 = qi * tq
        kend = k0 + tq
        acc = jnp.dot(e_ref[slot, :, 0:kend], v1_ref[0:kend, :],
                      preferred_element_type=jnp.float32)
        attn = acc[:, 0:V_DIM] * (1.0 / acc[:, V_DIM:V_DIM + 1])
        if h == 0:
            part_ref[...] = attn
        else:
            o = part_ref[...] - lam * attn
            o = _rms(o, sg_ref[...]) * (1.0 - lambda_init)
            o_ref[k0:kend, :] = o.astype(jnp.bfloat16)

    for step in range(len(units) + 2):
        if step < len(units):
            stage_scores(step, step % 2)
        if 1 <= step <= len(units):
            stage_numerators(step - 1, (step - 1) % 2)
        if step >= 2:
            stage_values(step - 2, step % 2)


def _attention(qk, v, qg2, kg2, lam_rows, subln_g, lambda_init):
    return pl.pallas_call(
        functools.partial(_attn_kernel, lambda_init=lambda_init),
        grid=(BATCH, N_HEADS),
        in_specs=[
            pl.BlockSpec((SEQ, V_DIM), lambda b, h: (b, h)),
            pl.BlockSpec((SEQ, V_DIM), lambda b, h: (b, N_HEADS + h)),
            pl.BlockSpec((SEQ, V_DIM), lambda b, h: (b, h)),
            _full((1, V_DIM)), _full((1, V_DIM)),
            _full((SUBLANES, HEAD_DIM)),
            _full((1, V_DIM)),
        ],
        out_specs=pl.BlockSpec((SEQ, V_DIM), lambda b, h: (b, h)),
        out_shape=jax.ShapeDtypeStruct((N_TOK, D_MODEL), jnp.bfloat16),
        scratch_shapes=[pltpu.VMEM((SEQ, V_DIM), jnp.bfloat16)] * 3 + [
            pltpu.VMEM((SEQ, 2 * V_DIM), jnp.bfloat16),
            pltpu.VMEM((2, ATT_TQ, SEQ), jnp.float32),
            pltpu.VMEM((2, ATT_TQ, SEQ), jnp.bfloat16),
            pltpu.VMEM((ATT_TQ, V_DIM), jnp.float32),
        ],
        compiler_params=_cparams(2),
        name="diff_attention",
    )(qk, qk, v, qg2, kg2, lam_rows, subln_g)


def _post_kernel(x_ref, m_ref, w_ref, g2_ref, wr_ref, br_ref,
                 x1_ref, h2p_ref, rf_ref, cnt_ref, run_ref):
    @pl.when(pl.program_id(0) == 0)
    def _():
        run_ref[...] = jnp.zeros_like(run_ref)

    mix = jnp.dot(m_ref[...], w_ref[...], preferred_element_type=jnp.float32)
    _residual_norm_route(x_ref[...] + mix, g2_ref, wr_ref, br_ref, run_ref,
                         x1_ref, h2p_ref, rf_ref, cnt_ref, POST_TS)


def _attn_post(x, o, w_o, g2, wr, br):
    return pl.pallas_call(
        _post_kernel,
        grid=(N_TOK // POST_TS,),
        in_specs=[
            pl.BlockSpec((POST_TS, D_MODEL), lambda i: (i, 0)),
            pl.BlockSpec((POST_TS, D_MODEL), lambda i: (i, 0)),
            _full((D_MODEL, D_MODEL)),
            _full((1, D_MODEL)),
            _full((D_MODEL, 2 * LANES)), _full((1, LANES)),
        ],
        out_specs=_route_out_specs(POST_TS, lambda i: i),
        out_shape=_route_out_shapes(),
        scratch_shapes=[pltpu.VMEM((1, LANES), jnp.float32)],
        compiler_params=_cparams(1),
        name="attn_post",
    )(x, o, w_o, g2, wr, br)


def _unrolled(n, body):
    def group(g, carry):
        for u in range(DMA_UNROLL):
            body(g * DMA_UNROLL + u)
        return carry
    lax.fori_loop(0, n // DMA_UNROLL, group, 0)


def _expert_kernel(be_ref, nv_ref, dest_ref, pad_ref, next_ref,
                   h2p_ref, w1_hbm, w3_hbm, w2_hbm, yb_ref,
                   rowtok_ref, xbuf_ref, sem, w1f_ref, w3f_ref, w2f_ref, wsem,
                   w1b_ref, w3b_ref, w2b_ref, *, layer):
    b = pl.program_id(0)
    n_valid = nv_ref[0]
    w_hbm = (w1_hbm, w3_hbm, w2_hbm)
    w_f32 = (w1f_ref, w3f_ref, w2f_ref)

    def weight_copies(expert):
        return [pltpu.make_async_copy(w_hbm[i].at[layer, expert], w_f32[i], wsem.at[i])
                for i in range(3)]

    def row_copy(block, slot, r):
        tok = rowtok_ref[block * ROW_BLOCK + r]
        return pltpu.make_async_copy(
            h2p_ref.at[pl.ds(pl.multiple_of(tok * PACK, PACK), PACK)],
            xbuf_ref.at[slot, pl.ds(r * PACK, PACK)],
            sem.at[slot])

    def block_arrival(slot):
        return pltpu.make_async_copy(
            h2p_ref.at[pl.ds(0, ROW_BLOCK * PACK)], xbuf_ref.at[slot], sem.at[slot])

    @pl.when(b == 0)
    def _():
        for c in weight_copies(be_ref[0]):
            c.start()
        def clear_padding(e, carry):
            def clear(j, c):
                rowtok_ref[j] = 0
                return c
            return lax.fori_loop(pad_ref[e], pad_ref[N_EXPERTS + e], clear, carry)
        lax.fori_loop(0, N_EXPERTS, clear_padding, 0)

        def place(t):
            rowtok_ref[dest_ref[TOP_K * t]] = t
            rowtok_ref[dest_ref[TOP_K * t + 1]] = t
        _unrolled(N_TOK, place)
        second = jnp.minimum(1, n_valid - 1)
        _unrolled(ROW_BLOCK, lambda r: row_copy(0, 0, r).start())
        _unrolled(ROW_BLOCK, lambda r: row_copy(second, 1, r).start())

    expert = be_ref[b]
    fresh = (b == 0) | (expert != be_ref[jnp.maximum(b - 1, 0)])

    @pl.when(fresh & (b < n_valid))
    def _():
        for c in weight_copies(expert):
            c.wait()
        w1b_ref[...] = w1f_ref[...].astype(jnp.bfloat16)
        w3b_ref[...] = w3f_ref[...].astype(jnp.bfloat16)
        w2b_ref[...] = w2f_ref[...].astype(jnp.bfloat16)
        following = next_ref[expert]

        @pl.when(following >= 0)
        def _():
            for c in weight_copies(following):
                c.start()

    @pl.when(b < n_valid)
    def _():
        slot = lax.rem(b, GATHER_SLOTS)
        block_arrival(slot).wait()
        x = _unpack_rows(xbuf_ref.at[slot], ROW_BLOCK).astype(jnp.bfloat16)
        ahead = jnp.minimum(b + 2, n_valid - 1)
        ahead_slot = lax.rem(b + 2, GATHER_SLOTS)
        for r in range(ROW_BLOCK):
            row_copy(ahead, ahead_slot, r).start()
        h1 = jnp.dot(x, w1b_ref[...], preferred_element_type=jnp.float32)
        h3 = jnp.dot(x, w3b_ref[...], preferred_element_type=jnp.float32)
        act = (h1 * jax.nn.sigmoid(h1) * h3).astype(jnp.bfloat16)
        y = jnp.dot(act, w2b_ref[...], preferred_element_type=jnp.float32)
        _pack_rows(yb_ref, y, ROW_BLOCK)

    @pl.when(b == n_valid - 1)
    def _():
        block_arrival(lax.rem(b + 1, GATHER_SLOTS)).wait()
        block_arrival(lax.rem(b + 2, GATHER_SLOTS)).wait()

    @pl.when(b >= n_valid)
    def _():
        yb_ref[...] = jnp.zeros_like(yb_ref)


def _experts(layer, block_e, n_valid, dest_flat, pad_rows, next_expert, h2p, w1, w3, w2):
    hbm = pl.BlockSpec(memory_space=pl.ANY)
    w_in = [pltpu.VMEM((D_MODEL, D_EXPERT), dt) for dt in (jnp.float32, jnp.bfloat16)]
    w_out = [pltpu.VMEM((D_EXPERT, D_MODEL), dt) for dt in (jnp.float32, jnp.bfloat16)]
    return pl.pallas_call(
        functools.partial(_expert_kernel, layer=layer),
        grid_spec=pltpu.PrefetchScalarGridSpec(
            num_scalar_prefetch=5,
            grid=(N_BLOCKS,),
            in_specs=[hbm, hbm, hbm, hbm],
            out_specs=pl.BlockSpec((ROW_BLOCK * PACK, LANES), lambda b, *_: (b, 0)),
            scratch_shapes=[
                pltpu.SMEM((N_ROWS,), jnp.int32),
                pltpu.VMEM((GATHER_SLOTS, ROW_BLOCK * PACK, LANES), jnp.float32),
                pltpu.SemaphoreType.DMA((GATHER_SLOTS,)),
                w_in[0], w_in[0], w_out[0],
                pltpu.SemaphoreType.DMA((3,)),
                w_in[1], w_in[1], w_out[1],
            ],
        ),
        out_shape=jax.ShapeDtypeStruct((N_ROWS * PACK, LANES), jnp.float32),
        compiler_params=_cparams(1),
        name="moe_experts",
    )(block_e, n_valid, dest_flat, pad_rows, next_expert, h2p, w1, w3, w2)


def _combine_kernel(pos_ref, x1_ref, rf_ref, yb_ref, out_ref, buf_ref, sem):
    i = pl.program_id(0)
    n = pl.num_programs(0)
    tm = COMB_TM

    def copy(tile, slot, j, k):
        p = pos_ref[TOP_K * (tile * tm + j) + k]
        return pltpu.make_async_copy(
            yb_ref.at[pl.ds(pl.multiple_of(p * PACK, PACK), PACK)],
            buf_ref.at[slot, k, pl.ds(pl.multiple_of(j * PACK, PACK), PACK)],
            sem.at[slot])

    def issue(tile, slot):
        def body(j):
            copy(tile, slot, j, 0).start()
            copy(tile, slot, j, 1).start()
        _unrolled(tm, body)

    def drain(tile, slot):
        def body(j):
            copy(tile, slot, j, 0).wait()
            copy(tile, slot, j, 1).wait()
        _unrolled(tm, body)

    slot = i % 2

    @pl.when(i == 0)
    def _():
        issue(0, 0)

    @pl.when(i + 1 < n)
    def _():
        issue(i + 1, 1 - slot)

    drain(i, slot)
    rf = rf_ref[...]
    y0 = _unpack_rows(buf_ref.at[slot, 0], tm)
    y1 = _unpack_rows(buf_ref.at[slot, 1], tm)
    out_ref[...] = x1_ref[...] + rf[:, 0:1] * y0 + rf[:, 1:2] * y1


def _combine(pos_flat, x1, rf, yb):
    return pl.pallas_call(
        _combine_kernel,
        grid_spec=pltpu.PrefetchScalarGridSpec(
            num_scalar_prefetch=1,
            grid=(N_TOK // COMB_TM,),
            in_specs=[
                pl.BlockSpec((COMB_TM, D_MODEL), lambda i, p: (i, 0)),
                pl.BlockSpec((COMB_TM, LANES), lambda i, p: (i, 0)),
                pl.BlockSpec(memory_space=pl.ANY),
            ],
            out_specs=pl.BlockSpec((COMB_TM, D_MODEL), lambda i, p: (i, 0)),
            scratch_shapes=[
                pltpu.VMEM((2, TOP_K, COMB_TM * PACK, LANES), jnp.float32),
                pltpu.SemaphoreType.DMA((2,)),
            ],
        ),
        out_shape=jax.ShapeDtypeStruct((N_TOK, D_MODEL), jnp.float32),
        compiler_params=_cparams(1),
        name="moe_combine",
    )(pos_flat, x1, rf, yb)


def _combine_qkv_kernel(pos_ref, x1_ref, rf_ref, yb_ref, g1_ref, w_ref,
                        x2_ref, qk_ref, v_ref, buf_ref, sem):
    i = pl.program_id(0)
    n = pl.num_programs(0)
    tm = QKV_TS

    def copy(tile, slot, j, k):
        p = pos_ref[TOP_K * (tile * tm + j) + k]
        return pltpu.make_async_copy(
            yb_ref.at[pl.ds(pl.multiple_of(p * PACK, PACK), PACK)],
            buf_ref.at[slot, k, pl.ds(j * PACK, PACK)],
            sem.at[slot])

    def plane_arrival(slot, k):
        return pltpu.make_async_copy(
            yb_ref.at[pl.ds(0, tm * PACK)], buf_ref.at[slot, k], sem.at[slot])

    @pl.when(i == 0)
    def _():
        def first(j):
            copy(0, 0, j, 0).start()
            copy(0, 0, j, 1).start()
        _unrolled(tm, first)

    slot = i % 2
    plane_arrival(slot, 0).wait()
    plane_arrival(slot, 1).wait()
    rf = rf_ref[...]
    y0 = _unpack_rows(buf_ref.at[slot, 0], tm)
    y1 = _unpack_rows(buf_ref.at[slot, 1], tm)
    x2 = x1_ref[...] + rf[:, 0:1] * y0 + rf[:, 1:2] * y1
    x2_ref[...] = x2
    ahead = jnp.minimum(i + 1, n - 1)
    for j in range(tm):
        copy(ahead, 1 - slot, j, 0).start()
        copy(ahead, 1 - slot, j, 1).start()
    h = _rms(x2, g1_ref[...]).astype(jnp.bfloat16)
    qkv = jnp.dot(h, w_ref[...], preferred_element_type=jnp.float32)
    qk_ref[...] = qkv[:, :2 * D_MODEL]
    v_ref[...] = qkv[:, 2 * D_MODEL:].astype(jnp.bfloat16)

    @pl.when(i == n - 1)
    def _():
        plane_arrival(1 - slot, 0).wait()
        plane_arrival(1 - slot, 1).wait()


def _combine_qkv(pos_flat, x1, rf, yb, g1, w_qkv):
    tile = lambda i, p: (i, 0)
    return pl.pallas_call(
        _combine_qkv_kernel,
        grid_spec=pltpu.PrefetchScalarGridSpec(
            num_scalar_prefetch=1,
            grid=(N_TOK // QKV_TS,),
            in_specs=[
                pl.BlockSpec((QKV_TS, D_MODEL), tile),
                pl.BlockSpec((QKV_TS, LANES), tile),
                pl.BlockSpec(memory_space=pl.ANY),
                _full((1, D_MODEL)),
                _full((D_MODEL, 3 * D_MODEL)),
            ],
            out_specs=(
                pl.BlockSpec((QKV_TS, D_MODEL), tile),
                pl.BlockSpec((QKV_TS, 2 * D_MODEL), tile),
                pl.BlockSpec((QKV_TS, D_MODEL), tile),
            ),
            scratch_shapes=[
                pltpu.VMEM((2, TOP_K, QKV_TS * PACK, LANES), jnp.float32),
                pltpu.SemaphoreType.DMA((2,)),
            ],
        ),
        out_shape=(
            jax.ShapeDtypeStruct((N_TOK, D_MODEL), jnp.float32),
            jax.ShapeDtypeStruct((N_TOK, 2 * D_MODEL), jnp.float32),
            jax.ShapeDtypeStruct((N_TOK, D_MODEL), jnp.bfloat16),
        ),
        compiler_params=_cparams(1),
        name="combine_qkv",
    )(pos_flat, x1, rf, yb, g1, w_qkv)


def _moe_experts(layer, h2p, rf, cnt, w1, w3, w2):
    experts = rf[:, 2:4].astype(jnp.int32)
    rank = rf[:, 4:6].astype(jnp.int32)
    counts = cnt[0, :N_EXPERTS].astype(jnp.int32)
    padded = (counts + ROW_BLOCK - 1) // ROW_BLOCK * ROW_BLOCK
    pad_ends = jnp.cumsum(padded)
    pad_starts = pad_ends - padded
    is_e = experts[:, :, None] == jnp.arange(N_EXPERTS, dtype=jnp.int32)
    dest = (jnp.sum(jnp.where(is_e, pad_starts, 0), axis=-1) + rank).reshape(-1)
    n_valid = (pad_ends[-1:] // ROW_BLOCK).astype(jnp.int32)
    block_row0 = jnp.arange(N_BLOCKS, dtype=jnp.int32) * ROW_BLOCK
    block_e = jnp.minimum(
        jnp.sum((pad_ends[None, :] <= block_row0[:, None]).astype(jnp.int32), axis=1),
        N_EXPERTS - 1).astype(jnp.int32)
    pad_rows = jnp.concatenate([pad_starts + counts, pad_ends]).astype(jnp.int32)
    ids = jnp.arange(N_EXPERTS, dtype=jnp.int32)
    later_nonempty = (counts > 0)[None, :] & (ids[None, :] > ids[:, None])
    following = jnp.min(jnp.where(later_nonempty, ids[None, :], N_EXPERTS), axis=1)
    next_expert = jnp.where(following < N_EXPERTS, following, -1).astype(jnp.int32)
    yb = _experts(layer, block_e, n_valid, dest, pad_rows, next_expert, h2p, w1, w3, w2)
    return dest, yb


def _router_params(w_grp, b_grp, w_exp, b_exp):
    wr = jnp.zeros((D_MODEL, LANES), jnp.float32)
    wr = wr.at[:, :N_EXPERTS].set(w_exp).at[:, GRP_LANE0:GRP_LANE0 + N_GROUPS].set(w_grp)
    br = jnp.zeros((1, LANES), jnp.float32)
    br = br.at[0, :N_EXPERTS].set(b_exp).at[0, GRP_LANE0:GRP_LANE0 + N_GROUPS].set(b_grp)
    w_hi = wr.astype(jnp.bfloat16)
    w_lo = (wr - w_hi.astype(jnp.float32)).astype(jnp.bfloat16)
    return jnp.concatenate([w_hi, w_lo], axis=1), br


def kernel(x, norm1_g, norm2_g, conv_w_in, conv_b_in, conv_dw, conv_dw_b, conv_ln_g, conv_ln_b, conv_w_out, conv_b_out, attn_w_qkv, attn_q_g, attn_k_g, attn_lq1, attn_lk1, attn_lq2, attn_lk2, attn_subln_g, attn_w_o, moe_w_grp, moe_b_grp, moe_w_exp, moe_b_exp, moe_w1, moe_w3, moe_w2):
    bf16 = jnp.bfloat16
    row = lambda a: a.reshape(1, -1)
    xf = x.reshape(N_TOK, D_MODEL)

    wr, br = _router_params(moe_w_grp[0], moe_b_grp[0], moe_w_exp[0], moe_b_exp[0])
    dw = jnp.zeros((HALO, D_MODEL), jnp.float32).at[:CONV_WIDTH].set(conv_dw[0])
    x1, h2p, rf, cnt = _conv_layer(
        xf, row(norm1_g[0]), conv_w_in[0].astype(bf16), row(conv_b_in[0]), dw,
        row(conv_dw_b[0]), row(conv_ln_g[0]), row(conv_ln_b[0]),
        conv_w_out[0].astype(bf16), row(conv_b_out[0]), row(norm2_g[0]), wr, br)
    dest, yb = _moe_experts(0, h2p, rf, cnt, moe_w1, moe_w3, moe_w2)

    lambda_init = 0.8 - 0.6 * math.exp(-0.3 * 1)
    wr, br = _router_params(moe_w_grp[1], moe_b_grp[1], moe_w_exp[1], moe_b_exp[1])
    xf, qk, v = _combine_qkv(dest, x1, rf, yb, row(norm1_g[1]), attn_w_qkv[0].astype(bf16))
    lam_rows = jnp.zeros((SUBLANES, HEAD_DIM), jnp.float32)
    lam_rows = lam_rows.at[0].set(attn_lq1[0]).at[1].set(attn_lk1[0])
    lam_rows = lam_rows.at[2].set(attn_lq2[0]).at[3].set(attn_lk2[0])
    two = lambda g: jnp.concatenate([g, g]).reshape(1, V_DIM)
    o = _attention(qk, v, two(attn_q_g[0]), two(attn_k_g[0]), lam_rows,
                   row(attn_subln_g[0]), lambda_init)
    x1, h2p, rf, cnt = _attn_post(xf, o, attn_w_o[0].astype(bf16), row(norm2_g[1]), wr, br)
    dest, yb = _moe_experts(1, h2p, rf, cnt, moe_w1, moe_w3, moe_w2)
    xf = _combine(dest, x1, rf, yb)
    return xf.reshape(BATCH, SEQ, D_MODEL)
```

```python
import functools
import math

import jax
import jax.numpy as jnp
from jax import lax
from jax.experimental import pallas as pl
from jax.experimental.pallas import tpu as pltpu

D_MODEL = 1024
BATCH = 8
SEQ = 2048
N_TOK = BATCH * SEQ
CHUNK = 64
CONV_WIDTH = 31
N_HEADS = 8
HEAD_DIM = 64
V_DIM = 128
N_GROUPS = 4
EXPERTS_PER_GROUP = 8
N_EXPERTS = 32
TOP_K = 2
D_EXPERT = 512
EPS = 1e-6
LOG2E = math.log2(math.e)

LANES = 128
SUBLANES = 8
PACK = D_MODEL // LANES
VMEM_LIMIT = 56 * 1024 * 1024

CONV_TS = 512
HALO = 32
CONV_RC = 64
POST_TS = 512
QKV_TS = 512
ATT_TQ = 256
ROW_BLOCK = 256
GATHER_SLOTS = 3
N_ASSIGN = N_TOK * TOP_K
N_BLOCKS = N_ASSIGN // ROW_BLOCK + N_EXPERTS
N_ROWS = N_BLOCKS * ROW_BLOCK
COMB_TM = 256
DMA_UNROLL = 8
GRP_LANE0 = N_EXPERTS


def _cparams(n_axes):
    return pltpu.CompilerParams(
        dimension_semantics=("arbitrary",) * n_axes, vmem_limit_bytes=VMEM_LIMIT)


def _rms(x, g):
    return x * lax.rsqrt(jnp.mean(x * x, axis=-1, keepdims=True) + EPS) * g


def _pack_rows(ref, val, rows):
    for j in range(PACK):
        ref[pl.ds(j, rows, stride=PACK), :] = val[:, j * LANES:(j + 1) * LANES]


def _unpack_rows(ref, rows):
    return jnp.concatenate(
        [ref[pl.ds(j, rows, stride=PACK), :] for j in range(PACK)], axis=-1)


def _residual_norm_route(x1, g2_ref, wr_ref, br_ref, run_ref,
                         x1_ref, h2p_ref, rf_ref, rft_ref, cnt_ref, rows):
    x1_ref[...] = x1
    h2 = _rms(x1, g2_ref[...])
    _pack_rows(h2p_ref, h2, rows)

    hi = h2.astype(jnp.bfloat16)
    lo = (h2 - hi.astype(jnp.float32)).astype(jnp.bfloat16)
    ab = jnp.dot(hi, wr_ref[...], preferred_element_type=jnp.float32)
    c = jnp.dot(lo, wr_ref[:, 0:LANES], preferred_element_type=jnp.float32)
    logits = ab[:, 0:LANES] + ab[:, LANES:2 * LANES] + c + br_ref[...]
    lane = lax.broadcasted_iota(jnp.int32, (rows, LANES), 1)
    lane_f = lane.astype(jnp.float32)
    neg = jnp.float32(-jnp.inf)
    big = jnp.float32(1e9)

    gmask = (lane >= GRP_LANE0) & (lane < GRP_LANE0 + N_GROUPS)
    gl = jnp.where(gmask, logits, neg)
    gmax = jnp.max(gl, axis=1, keepdims=True)
    gidx = jnp.min(jnp.where(gl == gmax, lane_f, big), axis=1, keepdims=True) - GRP_LANE0
    gsum = jnp.sum(jnp.where(gmask, jnp.exp(gl - gmax), 0.0), axis=1, keepdims=True)
    grp_p = 1.0 / gsum

    lane0 = gidx * EXPERTS_PER_GROUP
    emask = (lane_f >= lane0) & (lane_f < lane0 + EXPERTS_PER_GROUP)
    el = jnp.where(emask, logits, neg)
    m1 = jnp.max(el, axis=1, keepdims=True)
    i1 = jnp.min(jnp.where(el == m1, lane_f, big), axis=1, keepdims=True)
    el2 = jnp.where(lane_f == i1, neg, el)
    m2 = jnp.max(el2, axis=1, keepdims=True)
    i2 = jnp.min(jnp.where(el2 == m2, lane_f, big), axis=1, keepdims=True)
    t = jnp.exp(m2 - m1)
    inv = 1.0 / (1.0 + t)
    g_first = grp_p * inv
    g_second = grp_p * t * inv

    sel1 = lane_f == i1
    sel2 = lane_f == i2
    member = jnp.where(sel1 | sel2, 1.0, 0.0).astype(jnp.bfloat16)
    r_i = lax.broadcasted_iota(jnp.int32, (rows, rows), 0)
    c_i = lax.broadcasted_iota(jnp.int32, (rows, rows), 1)
    tri = jnp.where(c_i < r_i, 1.0, 0.0).astype(jnp.bfloat16)
    cum = jnp.dot(tri, member, preferred_element_type=jnp.float32) + run_ref[...]
    rank1 = jnp.sum(jnp.where(sel1, cum, 0.0), axis=1, keepdims=True)
    rank2 = jnp.sum(jnp.where(sel2, cum, 0.0), axis=1, keepdims=True)
    run_new = run_ref[...] + jnp.sum(member.astype(jnp.float32), axis=0, keepdims=True)
    run_ref[...] = run_new
    cnt_ref[...] = jnp.broadcast_to(run_new, (SUBLANES, LANES))

    out = jnp.where(lane == 0, g_first, 0.0)
    out = jnp.where(lane == 1, g_second, out)
    out = jnp.where(lane == 2, i1, out)
    out = jnp.where(lane == 3, i2, out)
    out = jnp.where(lane == 4, rank1, out)
    out = jnp.where(lane == 5, rank2, out)
    rf_ref[...] = out
    rft_ref[...] = jnp.transpose(out)[0:SUBLANES, :]


def _route_out_shapes():
    return (
        jax.ShapeDtypeStruct((N_TOK, D_MODEL), jnp.float32),
        jax.ShapeDtypeStruct((N_TOK * PACK, LANES), jnp.float32),
        jax.ShapeDtypeStruct((N_TOK, LANES), jnp.float32),
        jax.ShapeDtypeStruct((SUBLANES, N_TOK), jnp.float32),
        jax.ShapeDtypeStruct((SUBLANES, LANES), jnp.float32),
    )


def _route_out_specs(ts, idx):
    return (
        pl.BlockSpec((ts, D_MODEL), lambda *a: (idx(*a), 0)),
        pl.BlockSpec((ts * PACK, LANES), lambda *a: (idx(*a), 0)),
        pl.BlockSpec((ts, LANES), lambda *a: (idx(*a), 0)),
        pl.BlockSpec((SUBLANES, ts), lambda *a: (0, idx(*a))),
        pl.BlockSpec((SUBLANES, LANES), lambda *a: (0, 0)),
    )


def _full(shape):
    return pl.BlockSpec(shape, lambda *a: (0,) * len(shape))


def _conv_kernel(x_ref, g1_ref, win_ref, bin_ref, dw_ref, dwb_ref, lng_ref, lnb_ref,
                 wout_ref, bout_ref, g2_ref, wr_ref, br_ref,
                 x1_ref, h2p_ref, rf_ref, rft_ref, cnt_ref,
                 ext_ref, conv_ref, run_ref):
    b = pl.program_id(0)
    s = pl.program_id(1)
    ts = CONV_TS

    @pl.when((b == 0) & (s == 0))
    def _():
        run_ref[...] = jnp.zeros_like(run_ref)

    @pl.when(s == 0)
    def _():
        ext_ref[0:HALO, :] = jnp.zeros((HALO, D_MODEL), jnp.float32)

    x = x_ref[...]
    h = _rms(x, g1_ref[...]).astype(jnp.bfloat16)
    u = jnp.dot(h, win_ref[...], preferred_element_type=jnp.float32) + bin_ref[...]
    glu = u[:, :D_MODEL] * jax.nn.sigmoid(u[:, D_MODEL:])
    ext_ref[HALO:HALO + ts, :] = glu

    base = HALO - (CONV_WIDTH - 1)

    def lane_chunk(c, carry):
        cols = pl.ds(pl.multiple_of(c * LANES, LANES), LANES)
        taps = [dw_ref[pl.ds(k, 1), cols] for k in range(CONV_WIDTH)]
        for r0 in range(0, ts, CONV_RC):
            acc = None
            for r in range(SUBLANES):
                rows = CONV_RC + (SUBLANES if r else 0)
                part = None
                for q in range((base + CONV_WIDTH - 1) // SUBLANES + 1):
                    k = SUBLANES * q + r - base
                    if 0 <= k < CONV_WIDTH:
                        term = ext_ref[pl.ds(r0 + SUBLANES * q, rows), cols] * taps[k]
                        part = term if part is None else part + term
                part = part[r:r + CONV_RC, :] if r else part
                acc = part if acc is None else acc + part
            conv_ref[pl.ds(r0, CONV_RC), cols] = acc
        return carry

    lax.fori_loop(0, D_MODEL // LANES, lane_chunk, 0)
    ext_ref[0:HALO, :] = ext_ref[ts:ts + HALO, :]

    v = conv_ref[...] + dwb_ref[...]
    mu = jnp.mean(v, axis=-1, keepdims=True)
    vc = v - mu
    var = jnp.mean(vc * vc, axis=-1, keepdims=True)
    y = vc * lax.rsqrt(var + EPS) * lng_ref[...] + lnb_ref[...]
    y = (y * jax.nn.sigmoid(y)).astype(jnp.bfloat16)
    mix = jnp.dot(y, wout_ref[...], preferred_element_type=jnp.float32) + bout_ref[...]
    _residual_norm_route(x + mix, g2_ref, wr_ref, br_ref, run_ref,
                         x1_ref, h2p_ref, rf_ref, rft_ref, cnt_ref, ts)


def _conv_layer(x, g1, w_in, b_in, dw, dw_b, ln_g, ln_b, w_out, b_out, g2, wr, br):
    ns = SEQ // CONV_TS
    tile = lambda b, s: b * ns + s
    return pl.pallas_call(
        _conv_kernel,
        grid=(BATCH, ns),
        in_specs=[
            pl.BlockSpec((CONV_TS, D_MODEL), lambda b, s: (tile(b, s), 0)),
            _full((1, D_MODEL)),
            _full((D_MODEL, 2 * D_MODEL)),
            _full((1, 2 * D_MODEL)),
            _full((HALO, D_MODEL)),
            _full((1, D_MODEL)), _full((1, D_MODEL)), _full((1, D_MODEL)),
            _full((D_MODEL, D_MODEL)),
            _full((1, D_MODEL)), _full((1, D_MODEL)),
            _full((D_MODEL, 2 * LANES)), _full((1, LANES)),
        ],
        out_specs=_route_out_specs(CONV_TS, tile),
        out_shape=_route_out_shapes(),
        scratch_shapes=[
            pltpu.VMEM((HALO + CONV_TS, D_MODEL), jnp.float32),
            pltpu.VMEM((CONV_TS, D_MODEL), jnp.float32),
            pltpu.VMEM((1, LANES), jnp.float32),
        ],
        compiler_params=_cparams(2),
        name="conv_mixer",
    )(x, g1, w_in, b_in, dw, dw_b, ln_g, ln_b, w_out, b_out, g2, wr, br)


def _half_norm(z, gain):
    lane = lax.broadcasted_iota(jnp.int32, z.shape, 1)
    first = lane < HEAD_DIM
    zz = z * z
    ss_a = jnp.sum(jnp.where(first, zz, 0.0), axis=1, keepdims=True)
    ss_b = jnp.sum(jnp.where(first, 0.0, zz), axis=1, keepdims=True)
    inv = jnp.where(first, lax.rsqrt(ss_a * (1.0 / HEAD_DIM) + EPS),
                    lax.rsqrt(ss_b * (1.0 / HEAD_DIM) + EPS))
    return z * inv * gain


def _attn_kernel(q_ref, k_ref, v_ref, qg_ref, kg_ref, lam_ref, sg_ref, o_ref,
                 qa_ref, qb_ref, kn_ref, v1_ref, s_ref, e_ref, part_ref, *, lambda_init):
    lp = lam_ref[...]
    lam = (jnp.exp(jnp.sum(lp[0:1, :] * lp[1:2, :], axis=1, keepdims=True))
           - jnp.exp(jnp.sum(lp[2:3, :] * lp[3:4, :], axis=1, keepdims=True))
           + lambda_init)

    kn_ref[...] = _half_norm(k_ref[...], kg_ref[...]).astype(jnp.bfloat16)
    qn = _half_norm(q_ref[...], qg_ref[...]) * (HEAD_DIM ** -0.5 * LOG2E)
    first = lax.broadcasted_iota(jnp.int32, (SEQ, V_DIM), 1) < HEAD_DIM
    qa_ref[...] = jnp.where(first, qn, 0.0).astype(jnp.bfloat16)
    qb_ref[...] = jnp.where(first, 0.0, qn).astype(jnp.bfloat16)

    ones_col = lax.broadcasted_iota(jnp.int32, (SEQ, V_DIM), 1) == 0
    v1_ref[:, 0:V_DIM] = v_ref[...]
    v1_ref[:, V_DIM:2 * V_DIM] = jnp.where(ones_col, 1.0, 0.0).astype(jnp.bfloat16)

    tq = ATT_TQ
    nt = (((1,), (1,)), ((), ()))
    visible = (lax.broadcasted_iota(jnp.int32, (tq, tq), 1) // CHUNK
               <= lax.broadcasted_iota(jnp.int32, (tq, tq), 0) // CHUNK)
    halves = (qa_ref, qb_ref)
    units = [(qi, h) for qi in range(SEQ // tq) for h in range(2)]

    def stage_scores(u, slot):
        qi, h = units[u]
        k0 = qi * tq
        q = halves[h][k0:k0 + tq, :]
        dg = lax.dot_general(q, kn_ref[k0:k0 + tq, :], nt, preferred_element_type=jnp.float32)
        s_ref[slot, :, k0:k0 + tq] = jnp.where(visible, dg, -jnp.inf)
        if qi:
            s_ref[slot, :, 0:k0] = lax.dot_general(q, kn_ref[0:k0, :], nt,
                                                   preferred_element_type=jnp.float32)

    def stage_numerators(u, slot):
        kend = (units[u][0] + 1) * tq
        sc = s_ref[slot, :, 0:kend]
        m = jnp.max(sc, axis=1, keepdims=True)
        e_ref[slot, :, 0:kend] = jnp.exp2(sc - m).astype(jnp.bfloat16)

    def stage_values(u, slot):
        qi, h = units[u]
        k0 = qi * tq
        kend = k0 + tq
        acc = jnp.dot(e_ref[slot, :, 0:kend], v1_ref[0:kend, :],
                      preferred_element_type=jnp.float32)
        attn = acc[:, 0:V_DIM] * (1.0 / acc[:, V_DIM:V_DIM + 1])
        if h == 0:
            part_ref[...] = attn
        else:
            o = part_ref[...] - lam * attn
            o = _rms(o, sg_ref[...]) * (1.0 - lambda_init)
            o_ref[k0:kend, :] = o.astype(jnp.bfloat16)

    for step in range(len(units) + 2):
        if step < len(units):
            stage_scores(step, step % 2)
        if 1 <= step <= len(units):
            stage_numerators(step - 1, (step - 1) % 2)
        if step >= 2:
            stage_values(step - 2, step % 2)


def _attention(qk, v, qg2, kg2, lam_rows, subln_g, lambda_init):
    return pl.pallas_call(
        functools.partial(_attn_kernel, lambda_init=lambda_init),
        grid=(BATCH, N_HEADS),
        in_specs=[
            pl.BlockSpec((SEQ, V_DIM), lambda b, h: (b, h)),
            pl.BlockSpec((SEQ, V_DIM), lambda b, h: (b, N_HEADS + h)),
            pl.BlockSpec((SEQ, V_DIM), lambda b, h: (b, h)),
            _full((1, V_DIM)), _full((1, V_DIM)),
            _full((SUBLANES, HEAD_DIM)),
            _full((1, V_DIM)),
        ],
        out_specs=pl.BlockSpec((SEQ, V_DIM), lambda b, h: (b, h)),
        out_shape=jax.ShapeDtypeStruct((N_TOK, D_MODEL), jnp.bfloat16),
        scratch_shapes=[pltpu.VMEM((SEQ, V_DIM), jnp.bfloat16)] * 3 + [
            pltpu.VMEM((SEQ, 2 * V_DIM), jnp.bfloat16),
            pltpu.VMEM((2, ATT_TQ, SEQ), jnp.float32),
            pltpu.VMEM((2, ATT_TQ, SEQ), jnp.bfloat16),
            pltpu.VMEM((ATT_TQ, V_DIM), jnp.float32),
        ],
        compiler_params=_cparams(2),
        name="diff_attention",
    )(qk, qk, v, qg2, kg2, lam_rows, subln_g)


def _post_kernel(x_ref, m_ref, w_ref, g2_ref, wr_ref, br_ref,
                 x1_ref, h2p_ref, rf_ref, rft_ref, cnt_ref, run_ref):
    @pl.when(pl.program_id(0) == 0)
    def _():
        run_ref[...] = jnp.zeros_like(run_ref)

    mix = jnp.dot(m_ref[...], w_ref[...], preferred_element_type=jnp.float32)
    _residual_norm_route(x_ref[...] + mix, g2_ref, wr_ref, br_ref, run_ref,
                         x1_ref, h2p_ref, rf_ref, rft_ref, cnt_ref, POST_TS)


def _attn_post(x, o, w_o, g2, wr, br):
    return pl.pallas_call(
        _post_kernel,
        grid=(N_TOK // POST_TS,),
        in_specs=[
            pl.BlockSpec((POST_TS, D_MODEL), lambda i: (i, 0)),
            pl.BlockSpec((POST_TS, D_MODEL), lambda i: (i, 0)),
            _full((D_MODEL, D_MODEL)),
            _full((1, D_MODEL)),
            _full((D_MODEL, 2 * LANES)), _full((1, LANES)),
        ],
        out_specs=_route_out_specs(POST_TS, lambda i: i),
        out_shape=_route_out_shapes(),
        scratch_shapes=[pltpu.VMEM((1, LANES), jnp.float32)],
        compiler_params=_cparams(1),
        name="attn_post",
    )(x, o, w_o, g2, wr, br)


def _unrolled(n, body):
    def group(g, carry):
        for u in range(DMA_UNROLL):
            body(g * DMA_UNROLL + u)
        return carry
    lax.fori_loop(0, n // DMA_UNROLL, group, 0)


def _expert_kernel(be_ref, nv_ref, dest_ref, pad_ref, next_ref,
                   h2p_ref, w1_hbm, w3_hbm, w2_hbm, yb_ref,
                   rowtok_ref, xbuf_ref, sem, w1f_ref, w3f_ref, w2f_ref, wsem,
                   w1b_ref, w3b_ref, w2b_ref, *, layer):
    b = pl.program_id(0)
    n_valid = nv_ref[0]
    w_hbm = (w1_hbm, w3_hbm, w2_hbm)
    w_f32 = (w1f_ref, w3f_ref, w2f_ref)

    def weight_copies(expert):
        return [pltpu.make_async_copy(w_hbm[i].at[layer, expert], w_f32[i], wsem.at[i])
                for i in range(3)]

    def row_copy(block, slot, r):
        tok = rowtok_ref[block * ROW_BLOCK + r]
        return pltpu.make_async_copy(
            h2p_ref.at[pl.ds(pl.multiple_of(tok * PACK, PACK), PACK)],
            xbuf_ref.at[slot, pl.ds(r * PACK, PACK)],
            sem.at[slot])

    def block_arrival(slot):
        return pltpu.make_async_copy(
            h2p_ref.at[pl.ds(0, ROW_BLOCK * PACK)], xbuf_ref.at[slot], sem.at[slot])

    @pl.when(b == 0)
    def _():
        for c in weight_copies(be_ref[0]):
            c.start()
        def clear_padding(e, carry):
            def clear(j, c):
                rowtok_ref[j] = 0
                return c
            return lax.fori_loop(pad_ref[e], pad_ref[N_EXPERTS + e], clear, carry)
        lax.fori_loop(0, N_EXPERTS, clear_padding, 0)

        def place(t):
            rowtok_ref[dest_ref[t]] = t
            rowtok_ref[dest_ref[N_TOK + t]] = t
        _unrolled(N_TOK, place)
        second = jnp.minimum(1, n_valid - 1)
        _unrolled(ROW_BLOCK, lambda r: row_copy(0, 0, r).start())
        _unrolled(ROW_BLOCK, lambda r: row_copy(second, 1, r).start())

    expert = be_ref[b]
    fresh = (b == 0) | (expert != be_ref[jnp.maximum(b - 1, 0)])

    @pl.when(fresh & (b < n_valid))
    def _():
        for c in weight_copies(expert):
            c.wait()
        w1b_ref[...] = w1f_ref[...].astype(jnp.bfloat16)
        w3b_ref[...] = w3f_ref[...].astype(jnp.bfloat16)
        w2b_ref[...] = w2f_ref[...].astype(jnp.bfloat16)
        following = next_ref[expert]

        @pl.when(following >= 0)
        def _():
            for c in weight_copies(following):
                c.start()

    @pl.when(b < n_valid)
    def _():
        slot = lax.rem(b, GATHER_SLOTS)
        block_arrival(slot).wait()
        x = _unpack_rows(xbuf_ref.at[slot], ROW_BLOCK).astype(jnp.bfloat16)
        ahead = jnp.minimum(b + 2, n_valid - 1)
        ahead_slot = lax.rem(b + 2, GATHER_SLOTS)
        for r in range(ROW_BLOCK):
            row_copy(ahead, ahead_slot, r).start()
        h1 = jnp.dot(x, w1b_ref[...], preferred_element_type=jnp.float32)
        h3 = jnp.dot(x, w3b_ref[...], preferred_element_type=jnp.float32)
        act = (h1 * jax.nn.sigmoid(h1) * h3).astype(jnp.bfloat16)
        y = jnp.dot(act, w2b_ref[...], preferred_element_type=jnp.float32)
        _pack_rows(yb_ref, y, ROW_BLOCK)

    @pl.when(b == n_valid - 1)
    def _():
        block_arrival(lax.rem(b + 1, GATHER_SLOTS)).wait()
        block_arrival(lax.rem(b + 2, GATHER_SLOTS)).wait()

    @pl.when(b >= n_valid)
    def _():
        yb_ref[...] = jnp.zeros_like(yb_ref)


def _experts(layer, block_e, n_valid, dest_flat, pad_rows, next_expert, h2p, w1, w3, w2):
    hbm = pl.BlockSpec(memory_space=pl.ANY)
    w_in = [pltpu.VMEM((D_MODEL, D_EXPERT), dt) for dt in (jnp.float32, jnp.bfloat16)]
    w_out = [pltpu.VMEM((D_EXPERT, D_MODEL), dt) for dt in (jnp.float32, jnp.bfloat16)]
    return pl.pallas_call(
        functools.partial(_expert_kernel, layer=layer),
        grid_spec=pltpu.PrefetchScalarGridSpec(
            num_scalar_prefetch=5,
            grid=(N_BLOCKS,),
            in_specs=[hbm, hbm, hbm, hbm],
            out_specs=pl.BlockSpec((ROW_BLOCK * PACK, LANES), lambda b, *_: (b, 0)),
            scratch_shapes=[
                pltpu.SMEM((N_ROWS,), jnp.int32),
                pltpu.VMEM((GATHER_SLOTS, ROW_BLOCK * PACK, LANES), jnp.float32),
                pltpu.SemaphoreType.DMA((GATHER_SLOTS,)),
                w_in[0], w_in[0], w_out[0],
                pltpu.SemaphoreType.DMA((3,)),
                w_in[1], w_in[1], w_out[1],
            ],
        ),
        out_shape=jax.ShapeDtypeStruct((N_ROWS * PACK, LANES), jnp.float32),
        compiler_params=_cparams(1),
        name="moe_experts",
    )(block_e, n_valid, dest_flat, pad_rows, next_expert, h2p, w1, w3, w2)


def _combine_kernel(pos_ref, x1_ref, rf_ref, yb_ref, out_ref, buf_ref, sem):
    i = pl.program_id(0)
    n = pl.num_programs(0)
    tm = COMB_TM

    def copy(tile, slot, j, k):
        p = pos_ref[k * N_TOK + tile * tm + j]
        return pltpu.make_async_copy(
            yb_ref.at[pl.ds(pl.multiple_of(p * PACK, PACK), PACK)],
            buf_ref.at[slot, k, pl.ds(pl.multiple_of(j * PACK, PACK), PACK)],
            sem.at[slot])

    def issue(tile, slot):
        def body(j):
            copy(tile, slot, j, 0).start()
            copy(tile, slot, j, 1).start()
        _unrolled(tm, body)

    def drain(tile, slot):
        def body(j):
            copy(tile, slot, j, 0).wait()
            copy(tile, slot, j, 1).wait()
        _unrolled(tm, body)

    slot = i % 2

    @pl.when(i == 0)
    def _():
        issue(0, 0)

    @pl.when(i + 1 < n)
    def _():
        issue(i + 1, 1 - slot)

    drain(i, slot)
    rf = rf_ref[...]
    y0 = _unpack_rows(buf_ref.at[slot, 0], tm)
    y1 = _unpack_rows(buf_ref.at[slot, 1], tm)
    out_ref[...] = x1_ref[...] + rf[:, 0:1] * y0 + rf[:, 1:2] * y1


def _combine(pos_flat, x1, rf, yb):
    return pl.pallas_call(
        _combine_kernel,
        grid_spec=pltpu.PrefetchScalarGridSpec(
            num_scalar_prefetch=1,
            grid=(N_TOK // COMB_TM,),
            in_specs=[
                pl.BlockSpec((COMB_TM, D_MODEL), lambda i, p: (i, 0)),
                pl.BlockSpec((COMB_TM, LANES), lambda i, p: (i, 0)),
                pl.BlockSpec(memory_space=pl.ANY),
            ],
            out_specs=pl.BlockSpec((COMB_TM, D_MODEL), lambda i, p: (i, 0)),
            scratch_shapes=[
                pltpu.VMEM((2, TOP_K, COMB_TM * PACK, LANES), jnp.float32),
                pltpu.SemaphoreType.DMA((2,)),
            ],
        ),
        out_shape=jax.ShapeDtypeStruct((N_TOK, D_MODEL), jnp.float32),
        compiler_params=_cparams(1),
        name="moe_combine",
    )(pos_flat, x1, rf, yb)


def _combine_qkv_kernel(pos_ref, x1_ref, rf_ref, yb_ref, g1_ref, w_ref,
                        x2_ref, qk_ref, v_ref, buf_ref, sem):
    i = pl.program_id(0)
    n = pl.num_programs(0)
    tm = QKV_TS

    def copy(tile, slot, j, k):
        p = pos_ref[k * N_TOK + tile * tm + j]
        return pltpu.make_async_copy(
            yb_ref.at[pl.ds(pl.multiple_of(p * PACK, PACK), PACK)],
            buf_ref.at[slot, k, pl.ds(j * PACK, PACK)],
            sem.at[slot])

    def plane_arrival(slot, k):
        return pltpu.make_async_copy(
            yb_ref.at[pl.ds(0, tm * PACK)], buf_ref.at[slot, k], sem.at[slot])

    @pl.when(i == 0)
    def _():
        def first(j):
            copy(0, 0, j, 0).start()
            copy(0, 0, j, 1).start()
        _unrolled(tm, first)

    slot = i % 2
    plane_arrival(slot, 0).wait()
    plane_arrival(slot, 1).wait()
    rf = rf_ref[...]
    y0 = _unpack_rows(buf_ref.at[slot, 0], tm)
    y1 = _unpack_rows(buf_ref.at[slot, 1], tm)
    x2 = x1_ref[...] + rf[:, 0:1] * y0 + rf[:, 1:2] * y1
    x2_ref[...] = x2
    ahead = jnp.minimum(i + 1, n - 1)
    for j in range(tm):
        copy(ahead, 1 - slot, j, 0).start()
        copy(ahead, 1 - slot, j, 1).start()
    h = _rms(x2, g1_ref[...]).astype(jnp.bfloat16)
    qkv = jnp.dot(h, w_ref[...], preferred_element_type=jnp.float32)
    qk_ref[...] = qkv[:, :2 * D_MODEL]
    v_ref[...] = qkv[:, 2 * D_MODEL:].astype(jnp.bfloat16)

    @pl.when(i == n - 1)
    def _():
        plane_arrival(1 - slot, 0).wait()
        plane_arrival(1 - slot, 1).wait()


def _combine_qkv(pos_flat, x1, rf, yb, g1, w_qkv):
    tile = lambda i, p: (i, 0)
    return pl.pallas_call(
        _combine_qkv_kernel,
        grid_spec=pltpu.PrefetchScalarGridSpec(
            num_scalar_prefetch=1,
            grid=(N_TOK // QKV_TS,),
            in_specs=[
                pl.BlockSpec((QKV_TS, D_MODEL), tile),
                pl.BlockSpec((QKV_TS, LANES), tile),
                pl.BlockSpec(memory_space=pl.ANY),
                _full((1, D_MODEL)),
                _full((D_MODEL, 3 * D_MODEL)),
            ],
            out_specs=(
                pl.BlockSpec((QKV_TS, D_MODEL), tile),
                pl.BlockSpec((QKV_TS, 2 * D_MODEL), tile),
                pl.BlockSpec((QKV_TS, D_MODEL), tile),
            ),
            scratch_shapes=[
                pltpu.VMEM((2, TOP_K, QKV_TS * PACK, LANES), jnp.float32),
                pltpu.SemaphoreType.DMA((2,)),
            ],
        ),
        out_shape=(
            jax.ShapeDtypeStruct((N_TOK, D_MODEL), jnp.float32),
            jax.ShapeDtypeStruct((N_TOK, 2 * D_MODEL), jnp.float32),
            jax.ShapeDtypeStruct((N_TOK, D_MODEL), jnp.bfloat16),
        ),
        compiler_params=_cparams(1),
        name="combine_qkv",
    )(pos_flat, x1, rf, yb, g1, w_qkv)


def _moe_experts(layer, h2p, rft, cnt, w1, w3, w2):
    experts = rft[2:4].astype(jnp.int32)
    rank = rft[4:6].astype(jnp.int32)
    counts = cnt[0, :N_EXPERTS].astype(jnp.int32)
    padded = (counts + ROW_BLOCK - 1) // ROW_BLOCK * ROW_BLOCK
    pad_ends = jnp.cumsum(padded)
    pad_starts = pad_ends - padded
    ids = jnp.arange(N_EXPERTS, dtype=jnp.int32)
    is_e = experts[None] == ids[:, None, None]
    dest = (jnp.sum(jnp.where(is_e, pad_starts[:, None, None], 0), axis=0) + rank).reshape(-1)
    n_valid = (pad_ends[-1:] // ROW_BLOCK).astype(jnp.int32)
    block_row0 = jnp.arange(N_BLOCKS, dtype=jnp.int32) * ROW_BLOCK
    block_e = jnp.minimum(
        jnp.sum((pad_ends[None, :] <= block_row0[:, None]).astype(jnp.int32), axis=1),
        N_EXPERTS - 1).astype(jnp.int32)
    pad_rows = jnp.concatenate([pad_starts + counts, pad_ends]).astype(jnp.int32)
    later_nonempty = (counts > 0)[None, :] & (ids[None, :] > ids[:, None])
    following = jnp.min(jnp.where(later_nonempty, ids[None, :], N_EXPERTS), axis=1)
    next_expert = jnp.where(following < N_EXPERTS, following, -1).astype(jnp.int32)
    yb = _experts(layer, block_e, n_valid, dest, pad_rows, next_expert, h2p, w1, w3, w2)
    return dest, yb


def _router_params(w_grp, b_grp, w_exp, b_exp):
    wr = jnp.zeros((D_MODEL, LANES), jnp.float32)
    wr = wr.at[:, :N_EXPERTS].set(w_exp).at[:, GRP_LANE0:GRP_LANE0 + N_GROUPS].set(w_grp)
    br = jnp.zeros((1, LANES), jnp.float32)
    br = br.at[0, :N_EXPERTS].set(b_exp).at[0, GRP_LANE0:GRP_LANE0 + N_GROUPS].set(b_grp)
    w_hi = wr.astype(jnp.bfloat16)
    w_lo = (wr - w_hi.astype(jnp.float32)).astype(jnp.bfloat16)
    return jnp.concatenate([w_hi, w_lo], axis=1), br


def kernel(x, norm1_g, norm2_g, conv_w_in, conv_b_in, conv_dw, conv_dw_b, conv_ln_g, conv_ln_b, conv_w_out, conv_b_out, attn_w_qkv, attn_q_g, attn_k_g, attn_lq1, attn_lk1, attn_lq2, attn_lk2, attn_subln_g, attn_w_o, moe_w_grp, moe_b_grp, moe_w_exp, moe_b_exp, moe_w1, moe_w3, moe_w2):
    bf16 = jnp.bfloat16
    row = lambda a: a.reshape(1, -1)
    xf = x.reshape(N_TOK, D_MODEL)

    wr, br = _router_params(moe_w_grp[0], moe_b_grp[0], moe_w_exp[0], moe_b_exp[0])
    dw = jnp.zeros((HALO, D_MODEL), jnp.float32).at[:CONV_WIDTH].set(conv_dw[0])
    x1, h2p, rf, rft, cnt = _conv_layer(
        xf, row(norm1_g[0]), conv_w_in[0].astype(bf16), row(conv_b_in[0]), dw,
        row(conv_dw_b[0]), row(conv_ln_g[0]), row(conv_ln_b[0]),
        conv_w_out[0].astype(bf16), row(conv_b_out[0]), row(norm2_g[0]), wr, br)
    dest, yb = _moe_experts(0, h2p, rft, cnt, moe_w1, moe_w3, moe_w2)

    lambda_init = 0.8 - 0.6 * math.exp(-0.3 * 1)
    wr, br = _router_params(moe_w_grp[1], moe_b_grp[1], moe_w_exp[1], moe_b_exp[1])
    xf, qk, v = _combine_qkv(dest, x1, rf, yb, row(norm1_g[1]), attn_w_qkv[0].astype(bf16))
    lam_rows = jnp.zeros((SUBLANES, HEAD_DIM), jnp.float32)
    lam_rows = lam_rows.at[0].set(attn_lq1[0]).at[1].set(attn_lk1[0])
    lam_rows = lam_rows.at[2].set(attn_lq2[0]).at[3].set(attn_lk2[0])
    two = lambda g: jnp.concatenate([g, g]).reshape(1, V_DIM)
    o = _attention(qk, v, two(attn_q_g[0]), two(attn_k_g[0]), lam_rows,
                   row(attn_subln_g[0]), lambda_init)
    x1, h2p, rf, rft, cnt = _attn_post(xf, o, attn_w_o[0].astype(bf16), row(norm2_g[1]), wr, br)
    dest, yb = _moe_experts(1, h2p, rft, cnt, moe_w1, moe_w3, moe_w2)
    xf = _combine(dest, x1, rf, yb)
    return xf.reshape(BATCH, SEQ, D_MODEL)
```

```python
import functools
import math

import jax
import jax.numpy as jnp
from jax import lax
from jax.experimental import pallas as pl
from jax.experimental.pallas import tpu as pltpu

D_MODEL = 1024
BATCH = 8
SEQ = 2048
N_TOK = BATCH * SEQ
CHUNK = 64
CONV_WIDTH = 31
N_HEADS = 8
HEAD_DIM = 64
V_DIM = 128
N_GROUPS = 4
EXPERTS_PER_GROUP = 8
N_EXPERTS = 32
TOP_K = 2
D_EXPERT = 512
EPS = 1e-6
LOG2E = math.log2(math.e)

LANES = 128
SUBLANES = 8
SLAB = D_MODEL // LANES
VMEM_LIMIT = 56 * 1024 * 1024

CONV_TS = 512
HALO = 32
CONV_RC = 64
POST_TS = 512
QKV_TS = 512
ATT_TQ = 256
ROW_BLOCK = 256
GATHER_SLOTS = 3
N_ASSIGN = N_TOK * TOP_K
N_BLOCKS = N_ASSIGN // ROW_BLOCK + N_EXPERTS
N_ROWS = N_BLOCKS * ROW_BLOCK
COMB_TM = 256
DMA_UNROLL = 8
GRP_LANE0 = N_EXPERTS


def _cparams(n_axes):
    return pltpu.CompilerParams(
        dimension_semantics=("arbitrary",) * n_axes, vmem_limit_bytes=VMEM_LIMIT)


def _rms(x, g):
    return x * lax.rsqrt(jnp.mean(x * x, axis=-1, keepdims=True) + EPS) * g


def _to_slab(ref, val, rows):
    for j in range(SLAB):
        ref[pl.ds(j, rows, stride=SLAB), :] = val[:, j * LANES:(j + 1) * LANES]


def _from_slab(ref, rows):
    return jnp.concatenate(
        [ref[pl.ds(j, rows, stride=SLAB), :] for j in range(SLAB)], axis=-1)


def _residual_norm_route(x1, g2_ref, wr_ref, br_ref, run_ref,
                         x1_ref, h2s_ref, rf_ref, rft_ref, cnt_ref, rows):
    x1_ref[...] = x1
    h2 = _rms(x1, g2_ref[...])
    _to_slab(h2s_ref, h2, rows)

    hi = h2.astype(jnp.bfloat16)
    lo = (h2 - hi.astype(jnp.float32)).astype(jnp.bfloat16)
    ab = jnp.dot(hi, wr_ref[...], preferred_element_type=jnp.float32)
    c = jnp.dot(lo, wr_ref[:, 0:LANES], preferred_element_type=jnp.float32)
    logits = ab[:, 0:LANES] + ab[:, LANES:2 * LANES] + c + br_ref[...]
    lane = lax.broadcasted_iota(jnp.int32, (rows, LANES), 1)
    lane_f = lane.astype(jnp.float32)
    neg = jnp.float32(-jnp.inf)
    big = jnp.float32(1e9)

    gmask = (lane >= GRP_LANE0) & (lane < GRP_LANE0 + N_GROUPS)
    gl = jnp.where(gmask, logits, neg)
    gmax = jnp.max(gl, axis=1, keepdims=True)
    gidx = jnp.min(jnp.where(gl == gmax, lane_f, big), axis=1, keepdims=True) - GRP_LANE0
    gsum = jnp.sum(jnp.where(gmask, jnp.exp(gl - gmax), 0.0), axis=1, keepdims=True)
    grp_p = 1.0 / gsum

    lane0 = gidx * EXPERTS_PER_GROUP
    emask = (lane_f >= lane0) & (lane_f < lane0 + EXPERTS_PER_GROUP)
    el = jnp.where(emask, logits, neg)
    m1 = jnp.max(el, axis=1, keepdims=True)
    i1 = jnp.min(jnp.where(el == m1, lane_f, big), axis=1, keepdims=True)
    el2 = jnp.where(lane_f == i1, neg, el)
    m2 = jnp.max(el2, axis=1, keepdims=True)
    i2 = jnp.min(jnp.where(el2 == m2, lane_f, big), axis=1, keepdims=True)
    t = jnp.exp(m2 - m1)
    inv = 1.0 / (1.0 + t)
    g_first = grp_p * inv
    g_second = grp_p * t * inv

    sel1 = lane_f == i1
    sel2 = lane_f == i2
    member = jnp.where(sel1 | sel2, 1.0, 0.0).astype(jnp.bfloat16)
    r_i = lax.broadcasted_iota(jnp.int32, (rows, rows), 0)
    c_i = lax.broadcasted_iota(jnp.int32, (rows, rows), 1)
    tri = jnp.where(c_i < r_i, 1.0, 0.0).astype(jnp.bfloat16)
    cum = jnp.dot(tri, member, preferred_element_type=jnp.float32) + run_ref[...]
    rank1 = jnp.sum(jnp.where(sel1, cum, 0.0), axis=1, keepdims=True)
    rank2 = jnp.sum(jnp.where(sel2, cum, 0.0), axis=1, keepdims=True)
    run_new = run_ref[...] + jnp.sum(member.astype(jnp.float32), axis=0, keepdims=True)
    run_ref[...] = run_new
    cnt_ref[...] = jnp.broadcast_to(run_new, (SUBLANES, LANES))

    out = jnp.where(lane == 0, g_first, 0.0)
    out = jnp.where(lane == 1, g_second, out)
    out = jnp.where(lane == 2, i1, out)
    out = jnp.where(lane == 3, i2, out)
    out = jnp.where(lane == 4, rank1, out)
    out = jnp.where(lane == 5, rank2, out)
    rf_ref[...] = out
    rft_ref[...] = jnp.transpose(out)[0:SUBLANES, :]


def _route_out_shapes():
    return (
        jax.ShapeDtypeStruct((N_TOK, D_MODEL), jnp.float32),
        jax.ShapeDtypeStruct((N_TOK * SLAB, LANES), jnp.float32),
        jax.ShapeDtypeStruct((N_TOK, LANES), jnp.float32),
        jax.ShapeDtypeStruct((SUBLANES, N_TOK), jnp.float32),
        jax.ShapeDtypeStruct((SUBLANES, LANES), jnp.float32),
    )


def _route_out_specs(ts, idx):
    return (
        pl.BlockSpec((ts, D_MODEL), lambda *a: (idx(*a), 0)),
        pl.BlockSpec((ts * SLAB, LANES), lambda *a: (idx(*a), 0)),
        pl.BlockSpec((ts, LANES), lambda *a: (idx(*a), 0)),
        pl.BlockSpec((SUBLANES, ts), lambda *a: (0, idx(*a))),
        pl.BlockSpec((SUBLANES, LANES), lambda *a: (0, 0)),
    )


def _full(shape):
    return pl.BlockSpec(shape, lambda *a: (0,) * len(shape))


def _conv_kernel(x_ref, g1_ref, win_ref, bin_ref, dw_ref, dwb_ref, lng_ref, lnb_ref,
                 wout_ref, bout_ref, g2_ref, wr_ref, br_ref,
                 x1_ref, h2s_ref, rf_ref, rft_ref, cnt_ref,
                 ext_ref, conv_ref, run_ref):
    b = pl.program_id(0)
    s = pl.program_id(1)
    ts = CONV_TS

    @pl.when((b == 0) & (s == 0))
    def _():
        run_ref[...] = jnp.zeros_like(run_ref)

    @pl.when(s == 0)
    def _():
        ext_ref[0:HALO, :] = jnp.zeros((HALO, D_MODEL), jnp.float32)

    x = x_ref[...]
    h = _rms(x, g1_ref[...]).astype(jnp.bfloat16)
    u = jnp.dot(h, win_ref[...], preferred_element_type=jnp.float32) + bin_ref[...]
    glu = u[:, :D_MODEL] * jax.nn.sigmoid(u[:, D_MODEL:])
    ext_ref[HALO:HALO + ts, :] = glu

    base = HALO - (CONV_WIDTH - 1)

    def lane_chunk(c, carry):
        cols = pl.ds(pl.multiple_of(c * LANES, LANES), LANES)
        taps = [dw_ref[pl.ds(k, 1), cols] for k in range(CONV_WIDTH)]
        for r0 in range(0, ts, CONV_RC):
            acc = None
            for r in range(SUBLANES):
                rows = CONV_RC + (SUBLANES if r else 0)
                part = None
                for q in range((base + CONV_WIDTH - 1) // SUBLANES + 1):
                    k = SUBLANES * q + r - base
                    if 0 <= k < CONV_WIDTH:
                        term = ext_ref[pl.ds(r0 + SUBLANES * q, rows), cols] * taps[k]
                        part = term if part is None else part + term
                part = part[r:r + CONV_RC, :] if r else part
                acc = part if acc is None else acc + part
            conv_ref[pl.ds(r0, CONV_RC), cols] = acc
        return carry

    lax.fori_loop(0, D_MODEL // LANES, lane_chunk, 0)
    ext_ref[0:HALO, :] = ext_ref[ts:ts + HALO, :]

    v = conv_ref[...] + dwb_ref[...]
    mu = jnp.mean(v, axis=-1, keepdims=True)
    vc = v - mu
    var = jnp.mean(vc * vc, axis=-1, keepdims=True)
    y = vc * lax.rsqrt(var + EPS) * lng_ref[...] + lnb_ref[...]
    y = (y * jax.nn.sigmoid(y)).astype(jnp.bfloat16)
    mix = jnp.dot(y, wout_ref[...], preferred_element_type=jnp.float32) + bout_ref[...]
    _residual_norm_route(x + mix, g2_ref, wr_ref, br_ref, run_ref,
                         x1_ref, h2s_ref, rf_ref, rft_ref, cnt_ref, ts)


def _conv_layer(x, g1, w_in, b_in, dw, dw_b, ln_g, ln_b, w_out, b_out, g2, wr, br):
    ns = SEQ // CONV_TS
    tile = lambda b, s: b * ns + s
    return pl.pallas_call(
        _conv_kernel,
        grid=(BATCH, ns),
        in_specs=[
            pl.BlockSpec((CONV_TS, D_MODEL), lambda b, s: (tile(b, s), 0)),
            _full((1, D_MODEL)),
            _full((D_MODEL, 2 * D_MODEL)),
            _full((1, 2 * D_MODEL)),
            _full((HALO, D_MODEL)),
            _full((1, D_MODEL)), _full((1, D_MODEL)), _full((1, D_MODEL)),
            _full((D_MODEL, D_MODEL)),
            _full((1, D_MODEL)), _full((1, D_MODEL)),
            _full((D_MODEL, 2 * LANES)), _full((1, LANES)),
        ],
        out_specs=_route_out_specs(CONV_TS, tile),
        out_shape=_route_out_shapes(),
        scratch_shapes=[
            pltpu.VMEM((HALO + CONV_TS, D_MODEL), jnp.float32),
            pltpu.VMEM((CONV_TS, D_MODEL), jnp.float32),
            pltpu.VMEM((1, LANES), jnp.float32),
        ],
        compiler_params=_cparams(2),
        name="conv_mixer",
    )(x, g1, w_in, b_in, dw, dw_b, ln_g, ln_b, w_out, b_out, g2, wr, br)


def _half_norm(z, gain):
    lane = lax.broadcasted_iota(jnp.int32, z.shape, 1)
    first = lane < HEAD_DIM
    zz = z * z
    ss_a = jnp.sum(jnp.where(first, zz, 0.0), axis=1, keepdims=True)
    ss_b = jnp.sum(jnp.where(first, 0.0, zz), axis=1, keepdims=True)
    inv = jnp.where(first, lax.rsqrt(ss_a * (1.0 / HEAD_DIM) + EPS),
                    lax.rsqrt(ss_b * (1.0 / HEAD_DIM) + EPS))
    return z * inv * gain


def _attn_kernel(q_ref, k_ref, v_ref, qg_ref, kg_ref, lam_ref, sg_ref, o_ref,
                 qa_ref, qb_ref, kn_ref, v1_ref, s_ref, e_ref, part_ref, *, lambda_init):
    lp = lam_ref[...]
    lam = (jnp.exp(jnp.sum(lp[0:1, :] * lp[1:2, :], axis=1, keepdims=True))
           - jnp.exp(jnp.sum(lp[2:3, :] * lp[3:4, :], axis=1, keepdims=True))
           + lambda_init)

    kn_ref[...] = _half_norm(k_ref[...], kg_ref[...]).astype(jnp.bfloat16)
    qn = _half_norm(q_ref[...], qg_ref[...]) * (HEAD_DIM ** -0.5 * LOG2E)
    first = lax.broadcasted_iota(jnp.int32, (SEQ, V_DIM), 1) < HEAD_DIM
    qa_ref[...] = jnp.where(first, qn, 0.0).astype(jnp.bfloat16)
    qb_ref[...] = jnp.where(first, 0.0, qn).astype(jnp.bfloat16)

    ones_col = lax.broadcasted_iota(jnp.int32, (SEQ, V_DIM), 1) == 0
    v1_ref[:, 0:V_DIM] = v_ref[...]
    v1_ref[:, V_DIM:2 * V_DIM] = jnp.where(ones_col, 1.0, 0.0).astype(jnp.bfloat16)

    tq = ATT_TQ
    nt = (((1,), (1,)), ((), ()))
    visible = (lax.broadcasted_iota(jnp.int32, (tq, tq), 1) // CHUNK
               <= lax.broadcasted_iota(jnp.int32, (tq, tq), 0) // CHUNK)
    halves = (qa_ref, qb_ref)
    units = [(qi, h) for qi in range(SEQ // tq) for h in range(2)]

    def stage_scores(u, slot):
        qi, h = units[u]
        k0 = qi * tq
        q = halves[h][k0:k0 + tq, :]
        dg = lax.dot_general(q, kn_ref[k0:k0 + tq, :], nt, preferred_element_type=jnp.float32)
        s_ref[slot, :, k0:k0 + tq] = jnp.where(visible, dg, -jnp.inf)
        if qi:
            s_ref[slot, :, 0:k0] = lax.dot_general(q, kn_ref[0:k0, :], nt,
                                                   preferred_element_type=jnp.float32)

    def stage_numerators(u, slot):
        kend = (units[u][0] + 1) * tq
        sc = s_ref[slot, :, 0:kend]
        m = jnp.max(sc, axis=1, keepdims=True)
        e_ref[slot, :, 0:kend] = jnp.exp2(sc - m).astype(jnp.bfloat16)

    def stage_values(u, slot):
        qi, h = units[u]
        k0 = qi * tq
        kend = k0 + tq
        acc = jnp.dot(e_ref[slot, :, 0:kend], v1_ref[0:kend, :],
                      preferred_element_type=jnp.float32)
        attn = acc[:, 0:V_DIM] * (1.0 / acc[:, V_DIM:V_DIM + 1])
        if h == 0:
            part_ref[...] = attn
        else:
            o = part_ref[...] - lam * attn
            o = _rms(o, sg_ref[...]) * (1.0 - lambda_init)
            o_ref[k0:kend, :] = o.astype(jnp.bfloat16)

    for step in range(len(units) + 2):
        if step < len(units):
            stage_scores(step, step % 2)
        if 1 <= step <= len(units):
            stage_numerators(step - 1, (step - 1) % 2)
        if step >= 2:
            stage_values(step - 2, step % 2)


def _attention(qk, v, qg2, kg2, lam_rows, subln_g, lambda_init):
    return pl.pallas_call(
        functools.partial(_attn_kernel, lambda_init=lambda_init),
        grid=(BATCH, N_HEADS),
        in_specs=[
            pl.BlockSpec((SEQ, V_DIM), lambda b, h: (b, h)),
            pl.BlockSpec((SEQ, V_DIM), lambda b, h: (b, N_HEADS + h)),
            pl.BlockSpec((SEQ, V_DIM), lambda b, h: (b, h)),
            _full((1, V_DIM)), _full((1, V_DIM)),
            _full((SUBLANES, HEAD_DIM)),
            _full((1, V_DIM)),
        ],
        out_specs=pl.BlockSpec((SEQ, V_DIM), lambda b, h: (b, h)),
        out_shape=jax.ShapeDtypeStruct((N_TOK, D_MODEL), jnp.bfloat16),
        scratch_shapes=[pltpu.VMEM((SEQ, V_DIM), jnp.bfloat16)] * 3 + [
            pltpu.VMEM((SEQ, 2 * V_DIM), jnp.bfloat16),
            pltpu.VMEM((2, ATT_TQ, SEQ), jnp.float32),
            pltpu.VMEM((2, ATT_TQ, SEQ), jnp.bfloat16),
            pltpu.VMEM((ATT_TQ, V_DIM), jnp.float32),
        ],
        compiler_params=_cparams(2),
        name="diff_attention",
    )(qk, qk, v, qg2, kg2, lam_rows, subln_g)


def _post_kernel(x_ref, m_ref, w_ref, g2_ref, wr_ref, br_ref,
                 x1_ref, h2s_ref, rf_ref, rft_ref, cnt_ref, run_ref):
    @pl.when(pl.program_id(0) == 0)
    def _():
        run_ref[...] = jnp.zeros_like(run_ref)

    mix = jnp.dot(m_ref[...], w_ref[...], preferred_element_type=jnp.float32)
    _residual_norm_route(x_ref[...] + mix, g2_ref, wr_ref, br_ref, run_ref,
                         x1_ref, h2s_ref, rf_ref, rft_ref, cnt_ref, POST_TS)


def _attn_post(x, o, w_o, g2, wr, br):
    return pl.pallas_call(
        _post_kernel,
        grid=(N_TOK // POST_TS,),
        in_specs=[
            pl.BlockSpec((POST_TS, D_MODEL), lambda i: (i, 0)),
            pl.BlockSpec((POST_TS, D_MODEL), lambda i: (i, 0)),
            _full((D_MODEL, D_MODEL)),
            _full((1, D_MODEL)),
            _full((D_MODEL, 2 * LANES)), _full((1, LANES)),
        ],
        out_specs=_route_out_specs(POST_TS, lambda i: i),
        out_shape=_route_out_shapes(),
        scratch_shapes=[pltpu.VMEM((1, LANES), jnp.float32)],
        compiler_params=_cparams(1),
        name="attn_post",
    )(x, o, w_o, g2, wr, br)


def _unrolled(n, body):
    def group(g, carry):
        for u in range(DMA_UNROLL):
            body(g * DMA_UNROLL + u)
        return carry
    lax.fori_loop(0, n // DMA_UNROLL, group, 0)


def _expert_kernel(be_ref, nv_ref, dest_ref, pad_ref, next_ref,
                   h2s_ref, w1_hbm, w3_hbm, w2_hbm, yb_ref,
                   rowtok_ref, xbuf_ref, sem, w1f_ref, w3f_ref, w2f_ref, wsem,
                   w1b_ref, w3b_ref, w2b_ref, *, layer):
    b = pl.program_id(0)
    n_valid = nv_ref[0]
    w_hbm = (w1_hbm, w3_hbm, w2_hbm)
    w_f32 = (w1f_ref, w3f_ref, w2f_ref)

    def weight_copies(expert):
        return [pltpu.make_async_copy(w_hbm[i].at[layer, expert], w_f32[i], wsem.at[i])
                for i in range(3)]

    def row_copy(block, slot, r):
        tok = rowtok_ref[block * ROW_BLOCK + r]
        return pltpu.make_async_copy(
            h2s_ref.at[pl.ds(pl.multiple_of(tok * SLAB, SLAB), SLAB)],
            xbuf_ref.at[slot, pl.ds(r * SLAB, SLAB)],
            sem.at[slot])

    def block_arrival(slot):
        return pltpu.make_async_copy(
            h2s_ref.at[pl.ds(0, ROW_BLOCK * SLAB)], xbuf_ref.at[slot], sem.at[slot])

    @pl.when(b == 0)
    def _():
        for c in weight_copies(be_ref[0]):
            c.start()
        def clear_padding(e, carry):
            def clear(j, c):
                rowtok_ref[j] = 0
                return c
            return lax.fori_loop(pad_ref[e], pad_ref[N_EXPERTS + e], clear, carry)
        lax.fori_loop(0, N_EXPERTS, clear_padding, 0)

        def place(t):
            rowtok_ref[dest_ref[t]] = t
            rowtok_ref[dest_ref[N_TOK + t]] = t
        _unrolled(N_TOK, place)
        second = jnp.minimum(1, n_valid - 1)
        _unrolled(ROW_BLOCK, lambda r: row_copy(0, 0, r).start())
        _unrolled(ROW_BLOCK, lambda r: row_copy(second, 1, r).start())

    expert = be_ref[b]
    fresh = (b == 0) | (expert != be_ref[jnp.maximum(b - 1, 0)])

    @pl.when(fresh & (b < n_valid))
    def _():
        for c in weight_copies(expert):
            c.wait()
        w1b_ref[...] = w1f_ref[...].astype(jnp.bfloat16)
        w3b_ref[...] = w3f_ref[...].astype(jnp.bfloat16)
        w2b_ref[...] = w2f_ref[...].astype(jnp.bfloat16)
        following = next_ref[expert]

        @pl.when(following >= 0)
        def _():
            for c in weight_copies(following):
                c.start()

    @pl.when(b < n_valid)
    def _():
        slot = lax.rem(b, GATHER_SLOTS)
        block_arrival(slot).wait()
        x = _from_slab(xbuf_ref.at[slot], ROW_BLOCK).astype(jnp.bfloat16)
        ahead = jnp.minimum(b + 2, n_valid - 1)
        ahead_slot = lax.rem(b + 2, GATHER_SLOTS)
        for r in range(ROW_BLOCK):
            row_copy(ahead, ahead_slot, r).start()
        h1 = jnp.dot(x, w1b_ref[...], preferred_element_type=jnp.float32)
        h3 = jnp.dot(x, w3b_ref[...], preferred_element_type=jnp.float32)
        act = (h1 * jax.nn.sigmoid(h1) * h3).astype(jnp.bfloat16)
        y = jnp.dot(act, w2b_ref[...], preferred_element_type=jnp.float32)
        _to_slab(yb_ref, y, ROW_BLOCK)

    @pl.when(b == n_valid - 1)
    def _():
        block_arrival(lax.rem(b + 1, GATHER_SLOTS)).wait()
        block_arrival(lax.rem(b + 2, GATHER_SLOTS)).wait()

    @pl.when(b >= n_valid)
    def _():
        yb_ref[...] = jnp.zeros_like(yb_ref)


def _experts(layer, block_e, n_valid, dest_flat, pad_rows, next_expert, h2s, w1, w3, w2):
    hbm = pl.BlockSpec(memory_space=pl.ANY)
    w_in = [pltpu.VMEM((D_MODEL, D_EXPERT), dt) for dt in (jnp.float32, jnp.bfloat16)]
    w_out = [pltpu.VMEM((D_EXPERT, D_MODEL), dt) for dt in (jnp.float32, jnp.bfloat16)]
    return pl.pallas_call(
        functools.partial(_expert_kernel, layer=layer),
        grid_spec=pltpu.PrefetchScalarGridSpec(
            num_scalar_prefetch=5,
            grid=(N_BLOCKS,),
            in_specs=[hbm, hbm, hbm, hbm],
            out_specs=pl.BlockSpec((ROW_BLOCK * SLAB, LANES), lambda b, *_: (b, 0)),
            scratch_shapes=[
                pltpu.SMEM((N_ROWS,), jnp.int32),
                pltpu.VMEM((GATHER_SLOTS, ROW_BLOCK * SLAB, LANES), jnp.float32),
                pltpu.SemaphoreType.DMA((GATHER_SLOTS,)),
                w_in[0], w_in[0], w_out[0],
                pltpu.SemaphoreType.DMA((3,)),
                w_in[1], w_in[1], w_out[1],
            ],
        ),
        out_shape=jax.ShapeDtypeStruct((N_ROWS * SLAB, LANES), jnp.float32),
        compiler_params=_cparams(1),
        name="moe_experts",
    )(block_e, n_valid, dest_flat, pad_rows, next_expert, h2s, w1, w3, w2)


def _combine_kernel(pos_ref, x1_ref, rf_ref, yb_ref, out_ref, buf_ref, sem):
    i = pl.program_id(0)
    n = pl.num_programs(0)
    tm = COMB_TM

    def copy(tile, slot, j, k):
        p = pos_ref[k * N_TOK + tile * tm + j]
        return pltpu.make_async_copy(
            yb_ref.at[pl.ds(pl.multiple_of(p * SLAB, SLAB), SLAB)],
            buf_ref.at[slot, k, pl.ds(pl.multiple_of(j * SLAB, SLAB), SLAB)],
            sem.at[slot])

    def issue(tile, slot):
        def body(j):
            copy(tile, slot, j, 0).start()
            copy(tile, slot, j, 1).start()
        _unrolled(tm, body)

    def drain(tile, slot):
        def body(j):
            copy(tile, slot, j, 0).wait()
            copy(tile, slot, j, 1).wait()
        _unrolled(tm, body)

    slot = i % 2

    @pl.when(i == 0)
    def _():
        issue(0, 0)

    @pl.when(i + 1 < n)
    def _():
        issue(i + 1, 1 - slot)

    drain(i, slot)
    rf = rf_ref[...]
    y0 = _from_slab(buf_ref.at[slot, 0], tm)
    y1 = _from_slab(buf_ref.at[slot, 1], tm)
    out_ref[...] = x1_ref[...] + rf[:, 0:1] * y0 + rf[:, 1:2] * y1


def _combine(pos_flat, x1, rf, yb):
    return pl.pallas_call(
        _combine_kernel,
        grid_spec=pltpu.PrefetchScalarGridSpec(
            num_scalar_prefetch=1,
            grid=(N_TOK // COMB_TM,),
            in_specs=[
                pl.BlockSpec((COMB_TM, D_MODEL), lambda i, p: (i, 0)),
                pl.BlockSpec((COMB_TM, LANES), lambda i, p: (i, 0)),
                pl.BlockSpec(memory_space=pl.ANY),
            ],
            out_specs=pl.BlockSpec((COMB_TM, D_MODEL), lambda i, p: (i, 0)),
            scratch_shapes=[
                pltpu.VMEM((2, TOP_K, COMB_TM * SLAB, LANES), jnp.float32),
                pltpu.SemaphoreType.DMA((2,)),
            ],
        ),
        out_shape=jax.ShapeDtypeStruct((N_TOK, D_MODEL), jnp.float32),
        compiler_params=_cparams(1),
        name="moe_combine",
    )(pos_flat, x1, rf, yb)


def _combine_qkv_kernel(pos_ref, x1_ref, rf_ref, yb_ref, g1_ref, w_ref,
                        x2_ref, qk_ref, v_ref, buf_ref, sem):
    i = pl.program_id(0)
    n = pl.num_programs(0)
    tm = QKV_TS

    def copy(tile, slot, j, k):
        p = pos_ref[k * N_TOK + tile * tm + j]
        return pltpu.make_async_copy(
            yb_ref.at[pl.ds(pl.multiple_of(p * SLAB, SLAB), SLAB)],
            buf_ref.at[slot, k, pl.ds(j * SLAB, SLAB)],
            sem.at[slot])

    def plane_arrival(slot, k):
        return pltpu.make_async_copy(
            yb_ref.at[pl.ds(0, tm * SLAB)], buf_ref.at[slot, k], sem.at[slot])

    @pl.when(i == 0)
    def _():
        def first(j):
            copy(0, 0, j, 0).start()
            copy(0, 0, j, 1).start()
        _unrolled(tm, first)

    slot = i % 2
    plane_arrival(slot, 0).wait()
    plane_arrival(slot, 1).wait()
    rf = rf_ref[...]
    y0 = _from_slab(buf_ref.at[slot, 0], tm)
    y1 = _from_slab(buf_ref.at[slot, 1], tm)
    x2 = x1_ref[...] + rf[:, 0:1] * y0 + rf[:, 1:2] * y1
    x2_ref[...] = x2
    ahead = jnp.minimum(i + 1, n - 1)
    for j in range(tm):
        copy(ahead, 1 - slot, j, 0).start()
        copy(ahead, 1 - slot, j, 1).start()
    h = _rms(x2, g1_ref[...]).astype(jnp.bfloat16)
    qkv = jnp.dot(h, w_ref[...], preferred_element_type=jnp.float32)
    qk_ref[...] = qkv[:, :2 * D_MODEL]
    v_ref[...] = qkv[:, 2 * D_MODEL:].astype(jnp.bfloat16)

    @pl.when(i == n - 1)
    def _():
        plane_arrival(1 - slot, 0).wait()
        plane_arrival(1 - slot, 1).wait()


def _combine_qkv(pos_flat, x1, rf, yb, g1, w_qkv):
    tile = lambda i, p: (i, 0)
    return pl.pallas_call(
        _combine_qkv_kernel,
        grid_spec=pltpu.PrefetchScalarGridSpec(
            num_scalar_prefetch=1,
            grid=(N_TOK // QKV_TS,),
            in_specs=[
                pl.BlockSpec((QKV_TS, D_MODEL), tile),
                pl.BlockSpec((QKV_TS, LANES), tile),
                pl.BlockSpec(memory_space=pl.ANY),
                _full((1, D_MODEL)),
                _full((D_MODEL, 3 * D_MODEL)),
            ],
            out_specs=(
                pl.BlockSpec((QKV_TS, D_MODEL), tile),
                pl.BlockSpec((QKV_TS, 2 * D_MODEL), tile),
                pl.BlockSpec((QKV_TS, D_MODEL), tile),
            ),
            scratch_shapes=[
                pltpu.VMEM((2, TOP_K, QKV_TS * SLAB, LANES), jnp.float32),
                pltpu.SemaphoreType.DMA((2,)),
            ],
        ),
        out_shape=(
            jax.ShapeDtypeStruct((N_TOK, D_MODEL), jnp.float32),
            jax.ShapeDtypeStruct((N_TOK, 2 * D_MODEL), jnp.float32),
            jax.ShapeDtypeStruct((N_TOK, D_MODEL), jnp.bfloat16),
        ),
        compiler_params=_cparams(1),
        name="combine_qkv",
    )(pos_flat, x1, rf, yb, g1, w_qkv)


def _moe_experts(layer, h2s, rft, cnt, w1, w3, w2):
    experts = rft[2:4].astype(jnp.int32)
    rank = rft[4:6].astype(jnp.int32)
    counts = cnt[0, :N_EXPERTS].astype(jnp.int32)
    padded = (counts + ROW_BLOCK - 1) // ROW_BLOCK * ROW_BLOCK
    pad_ends = jnp.cumsum(padded)
    pad_starts = pad_ends - padded
    ids = jnp.arange(N_EXPERTS, dtype=jnp.int32)
    is_e = experts[None] == ids[:, None, None]
    dest = (jnp.sum(jnp.where(is_e, pad_starts[:, None, None], 0), axis=0) + rank).reshape(-1)
    n_valid = (pad_ends[-1:] // ROW_BLOCK).astype(jnp.int32)
    block_row0 = jnp.arange(N_BLOCKS, dtype=jnp.int32) * ROW_BLOCK
    block_e = jnp.minimum(
        jnp.sum((pad_ends[None, :] <= block_row0[:, None]).astype(jnp.int32), axis=1),
        N_EXPERTS - 1).astype(jnp.int32)
    pad_rows = jnp.concatenate([pad_starts + counts, pad_ends]).astype(jnp.int32)
    later_nonempty = (counts > 0)[None, :] & (ids[None, :] > ids[:, None])
    following = jnp.min(jnp.where(later_nonempty, ids[None, :], N_EXPERTS), axis=1)
    next_expert = jnp.where(following < N_EXPERTS, following, -1).astype(jnp.int32)
    yb = _experts(layer, block_e, n_valid, dest, pad_rows, next_expert, h2s, w1, w3, w2)
    return dest, yb


def _router_params(w_grp, b_grp, w_exp, b_exp):
    wr = jnp.zeros((D_MODEL, LANES), jnp.float32)
    wr = wr.at[:, :N_EXPERTS].set(w_exp).at[:, GRP_LANE0:GRP_LANE0 + N_GROUPS].set(w_grp)
    br = jnp.zeros((1, LANES), jnp.float32)
    br = br.at[0, :N_EXPERTS].set(b_exp).at[0, GRP_LANE0:GRP_LANE0 + N_GROUPS].set(b_grp)
    w_hi = wr.astype(jnp.bfloat16)
    w_lo = (wr - w_hi.astype(jnp.float32)).astype(jnp.bfloat16)
    return jnp.concatenate([w_hi, w_lo], axis=1), br


def kernel(x, norm1_g, norm2_g, conv_w_in, conv_b_in, conv_dw, conv_dw_b, conv_ln_g, conv_ln_b, conv_w_out, conv_b_out, attn_w_qkv, attn_q_g, attn_k_g, attn_lq1, attn_lk1, attn_lq2, attn_lk2, attn_subln_g, attn_w_o, moe_w_grp, moe_b_grp, moe_w_exp, moe_b_exp, moe_w1, moe_w3, moe_w2):
    assert x.shape == (BATCH, SEQ, D_MODEL) and x.dtype == jnp.float32
    assert moe_w1.shape == (2, N_EXPERTS, D_MODEL, D_EXPERT) and conv_dw.shape[1] == CONV_WIDTH
    bf16 = jnp.bfloat16
    row = lambda a: a.reshape(1, -1)
    xf = x.reshape(N_TOK, D_MODEL)

    wr, br = _router_params(moe_w_grp[0], moe_b_grp[0], moe_w_exp[0], moe_b_exp[0])
    dw = jnp.zeros((HALO, D_MODEL), jnp.float32).at[:CONV_WIDTH].set(conv_dw[0])
    x1, h2s, rf, rft, cnt = _conv_layer(
        xf, row(norm1_g[0]), conv_w_in[0].astype(bf16), row(conv_b_in[0]), dw,
        row(conv_dw_b[0]), row(conv_ln_g[0]), row(conv_ln_b[0]),
        conv_w_out[0].astype(bf16), row(conv_b_out[0]), row(norm2_g[0]), wr, br)
    dest, yb = _moe_experts(0, h2s, rft, cnt, moe_w1, moe_w3, moe_w2)

    lambda_init = 0.8 - 0.6 * math.exp(-0.3 * 1)
    wr, br = _router_params(moe_w_grp[1], moe_b_grp[1], moe_w_exp[1], moe_b_exp[1])
    xf, qk, v = _combine_qkv(dest, x1, rf, yb, row(norm1_g[1]), attn_w_qkv[0].astype(bf16))
    lam_rows = jnp.zeros((SUBLANES, HEAD_DIM), jnp.float32)
    lam_rows = lam_rows.at[0].set(attn_lq1[0]).at[1].set(attn_lk1[0])
    lam_rows = lam_rows.at[2].set(attn_lq2[0]).at[3].set(attn_lk2[0])
    two = lambda g: jnp.concatenate([g, g]).reshape(1, V_DIM)
    o = _attention(qk, v, two(attn_q_g[0]), two(attn_k_g[0]), lam_rows,
                   row(attn_subln_g[0]), lambda_init)
    x1, h2s, rf, rft, cnt = _attn_post(xf, o, attn_w_o[0].astype(bf16), row(norm2_g[1]), wr, br)
    dest, yb = _moe_experts(1, h2s, rft, cnt, moe_w1, moe_w3, moe_w2)
    xf = _combine(dest, x1, rf, yb)
    return xf.reshape(BATCH, SEQ, D_MODEL)
```

```python
import functools
import math

import jax
import jax.numpy as jnp
from jax import lax
from jax.experimental import pallas as pl
from jax.experimental.pallas import tpu as pltpu

D_MODEL = 1024
BATCH = 8
SEQ = 2048
N_TOK = BATCH * SEQ
CHUNK = 64
CONV_WIDTH = 31
N_HEADS = 8
HEAD_DIM = 64
V_DIM = 128
N_GROUPS = 4
EXPERTS_PER_GROUP = 8
N_EXPERTS = 32
TOP_K = 2
D_EXPERT = 512
EPS = 1e-6
LOG2E = math.log2(math.e)

LANES = 128
SUBLANES = 8
SLAB = D_MODEL // LANES
VMEM_LIMIT = 56 * 1024 * 1024

CONV_TS = 512
HALO = 32
CONV_RC = 64
POST_TS = 512
QKV_TS = 512
ATT_TQ = 256
ROW_BLOCK = 256
GATHER_SLOTS = 3
N_ASSIGN = N_TOK * TOP_K
N_BLOCKS = N_ASSIGN // ROW_BLOCK + N_EXPERTS
N_ROWS = N_BLOCKS * ROW_BLOCK
COMB_TM = 256
DMA_UNROLL = 16
GRP_LANE0 = N_EXPERTS


def _cparams(n_axes):
    return pltpu.CompilerParams(
        dimension_semantics=("arbitrary",) * n_axes, vmem_limit_bytes=VMEM_LIMIT)


def _rms(x, g):
    return x * lax.rsqrt(jnp.mean(x * x, axis=-1, keepdims=True) + EPS) * g


def _to_slab(ref, val, rows):
    for j in range(SLAB):
        ref[pl.ds(j, rows, stride=SLAB), :] = val[:, j * LANES:(j + 1) * LANES]


def _from_slab(ref, rows):
    return jnp.concatenate(
        [ref[pl.ds(j, rows, stride=SLAB), :] for j in range(SLAB)], axis=-1)


def _residual_norm_route(x1, g2_ref, wr_ref, br_ref, run_ref,
                         x1_ref, h2s_ref, rf_ref, rft_ref, cnt_ref, rows):
    x1_ref[...] = x1
    h2 = _rms(x1, g2_ref[...])
    _to_slab(h2s_ref, h2, rows)

    hi = h2.astype(jnp.bfloat16)
    lo = (h2 - hi.astype(jnp.float32)).astype(jnp.bfloat16)
    ab = jnp.dot(hi, wr_ref[...], preferred_element_type=jnp.float32)
    c = jnp.dot(lo, wr_ref[:, 0:LANES], preferred_element_type=jnp.float32)
    logits = ab[:, 0:LANES] + ab[:, LANES:2 * LANES] + c + br_ref[...]
    lane = lax.broadcasted_iota(jnp.int32, (rows, LANES), 1)
    lane_f = lane.astype(jnp.float32)
    neg = jnp.float32(-jnp.inf)
    big = jnp.float32(1e9)

    gmask = (lane >= GRP_LANE0) & (lane < GRP_LANE0 + N_GROUPS)
    gl = jnp.where(gmask, logits, neg)
    gmax = jnp.max(gl, axis=1, keepdims=True)
    gidx = jnp.min(jnp.where(gl == gmax, lane_f, big), axis=1, keepdims=True) - GRP_LANE0
    gsum = jnp.sum(jnp.where(gmask, jnp.exp(gl - gmax), 0.0), axis=1, keepdims=True)
    grp_p = 1.0 / gsum

    lane0 = gidx * EXPERTS_PER_GROUP
    emask = (lane_f >= lane0) & (lane_f < lane0 + EXPERTS_PER_GROUP)
    el = jnp.where(emask, logits, neg)
    m1 = jnp.max(el, axis=1, keepdims=True)
    i1 = jnp.min(jnp.where(el == m1, lane_f, big), axis=1, keepdims=True)
    el2 = jnp.where(lane_f == i1, neg, el)
    m2 = jnp.max(el2, axis=1, keepdims=True)
    i2 = jnp.min(jnp.where(el2 == m2, lane_f, big), axis=1, keepdims=True)
    t = jnp.exp(m2 - m1)
    inv = 1.0 / (1.0 + t)
    g_first = grp_p * inv
    g_second = grp_p * t * inv

    sel1 = lane_f == i1
    sel2 = lane_f == i2
    member = jnp.where(sel1 | sel2, 1.0, 0.0).astype(jnp.bfloat16)
    r_i = lax.broadcasted_iota(jnp.int32, (rows, rows), 0)
    c_i = lax.broadcasted_iota(jnp.int32, (rows, rows), 1)
    tri = jnp.where(c_i < r_i, 1.0, 0.0).astype(jnp.bfloat16)
    cum = jnp.dot(tri, member, preferred_element_type=jnp.float32) + run_ref[...]
    rank1 = jnp.sum(jnp.where(sel1, cum, 0.0), axis=1, keepdims=True)
    rank2 = jnp.sum(jnp.where(sel2, cum, 0.0), axis=1, keepdims=True)
    run_new = run_ref[...] + jnp.sum(member.astype(jnp.float32), axis=0, keepdims=True)
    run_ref[...] = run_new
    cnt_ref[...] = jnp.broadcast_to(run_new, (SUBLANES, LANES))

    out = jnp.where(lane == 0, g_first, 0.0)
    out = jnp.where(lane == 1, g_second, out)
    out = jnp.where(lane == 2, i1, out)
    out = jnp.where(lane == 3, i2, out)
    out = jnp.where(lane == 4, rank1, out)
    out = jnp.where(lane == 5, rank2, out)
    rf_ref[...] = out
    rft_ref[...] = jnp.transpose(out)[0:SUBLANES, :]


def _route_out_shapes():
    return (
        jax.ShapeDtypeStruct((N_TOK, D_MODEL), jnp.float32),
        jax.ShapeDtypeStruct((N_TOK * SLAB, LANES), jnp.float32),
        jax.ShapeDtypeStruct((N_TOK, LANES), jnp.float32),
        jax.ShapeDtypeStruct((SUBLANES, N_TOK), jnp.float32),
        jax.ShapeDtypeStruct((SUBLANES, LANES), jnp.float32),
    )


def _route_out_specs(ts, idx):
    return (
        pl.BlockSpec((ts, D_MODEL), lambda *a: (idx(*a), 0)),
        pl.BlockSpec((ts * SLAB, LANES), lambda *a: (idx(*a), 0)),
        pl.BlockSpec((ts, LANES), lambda *a: (idx(*a), 0)),
        pl.BlockSpec((SUBLANES, ts), lambda *a: (0, idx(*a))),
        pl.BlockSpec((SUBLANES, LANES), lambda *a: (0, 0)),
    )


def _full(shape):
    return pl.BlockSpec(shape, lambda *a: (0,) * len(shape))


def _conv_kernel(x_ref, g1_ref, win_ref, bin_ref, dw_ref, dwb_ref, lng_ref, lnb_ref,
                 wout_ref, bout_ref, g2_ref, wr_ref, br_ref,
                 x1_ref, h2s_ref, rf_ref, rft_ref, cnt_ref,
                 ext_ref, conv_ref, run_ref):
    b = pl.program_id(0)
    s = pl.program_id(1)
    ts = CONV_TS

    @pl.when((b == 0) & (s == 0))
    def _():
        run_ref[...] = jnp.zeros_like(run_ref)

    @pl.when(s == 0)
    def _():
        ext_ref[0:HALO, :] = jnp.zeros((HALO, D_MODEL), jnp.float32)

    x = x_ref[...]
    h = _rms(x, g1_ref[...]).astype(jnp.bfloat16)
    u = jnp.dot(h, win_ref[...], preferred_element_type=jnp.float32) + bin_ref[...]
    glu = u[:, :D_MODEL] * jax.nn.sigmoid(u[:, D_MODEL:])
    ext_ref[HALO:HALO + ts, :] = glu

    base = HALO - (CONV_WIDTH - 1)

    def lane_chunk(c, carry):
        cols = pl.ds(pl.multiple_of(c * LANES, LANES), LANES)
        taps = [dw_ref[pl.ds(k, 1), cols] for k in range(CONV_WIDTH)]
        for r0 in range(0, ts, CONV_RC):
            acc = None
            for r in range(SUBLANES):
                rows = CONV_RC + (SUBLANES if r else 0)
                part = None
                for q in range((base + CONV_WIDTH - 1) // SUBLANES + 1):
                    k = SUBLANES * q + r - base
                    if 0 <= k < CONV_WIDTH:
                        term = ext_ref[pl.ds(r0 + SUBLANES * q, rows), cols] * taps[k]
                        part = term if part is None else part + term
                part = part[r:r + CONV_RC, :] if r else part
                acc = part if acc is None else acc + part
            conv_ref[pl.ds(r0, CONV_RC), cols] = acc
        return carry

    lax.fori_loop(0, D_MODEL // LANES, lane_chunk, 0)
    ext_ref[0:HALO, :] = ext_ref[ts:ts + HALO, :]

    v = conv_ref[...] + dwb_ref[...]
    mu = jnp.mean(v, axis=-1, keepdims=True)
    vc = v - mu
    var = jnp.mean(vc * vc, axis=-1, keepdims=True)
    y = vc * lax.rsqrt(var + EPS) * lng_ref[...] + lnb_ref[...]
    y = (y * jax.nn.sigmoid(y)).astype(jnp.bfloat16)
    mix = jnp.dot(y, wout_ref[...], preferred_element_type=jnp.float32) + bout_ref[...]
    _residual_norm_route(x + mix, g2_ref, wr_ref, br_ref, run_ref,
                         x1_ref, h2s_ref, rf_ref, rft_ref, cnt_ref, ts)


def _conv_layer(x, g1, w_in, b_in, dw, dw_b, ln_g, ln_b, w_out, b_out, g2, wr, br):
    ns = SEQ // CONV_TS
    tile = lambda b, s: b * ns + s
    return pl.pallas_call(
        _conv_kernel,
        grid=(BATCH, ns),
        in_specs=[
            pl.BlockSpec((CONV_TS, D_MODEL), lambda b, s: (tile(b, s), 0)),
            _full((1, D_MODEL)),
            _full((D_MODEL, 2 * D_MODEL)),
            _full((1, 2 * D_MODEL)),
            _full((HALO, D_MODEL)),
            _full((1, D_MODEL)), _full((1, D_MODEL)), _full((1, D_MODEL)),
            _full((D_MODEL, D_MODEL)),
            _full((1, D_MODEL)), _full((1, D_MODEL)),
            _full((D_MODEL, 2 * LANES)), _full((1, LANES)),
        ],
        out_specs=_route_out_specs(CONV_TS, tile),
        out_shape=_route_out_shapes(),
        scratch_shapes=[
            pltpu.VMEM((HALO + CONV_TS, D_MODEL), jnp.float32),
            pltpu.VMEM((CONV_TS, D_MODEL), jnp.float32),
            pltpu.VMEM((1, LANES), jnp.float32),
        ],
        compiler_params=_cparams(2),
        name="conv_mixer",
    )(x, g1, w_in, b_in, dw, dw_b, ln_g, ln_b, w_out, b_out, g2, wr, br)


def _half_norm(z, gain):
    lane = lax.broadcasted_iota(jnp.int32, z.shape, 1)
    first = lane < HEAD_DIM
    zz = z * z
    ss_a = jnp.sum(jnp.where(first, zz, 0.0), axis=1, keepdims=True)
    ss_b = jnp.sum(jnp.where(first, 0.0, zz), axis=1, keepdims=True)
    inv = jnp.where(first, lax.rsqrt(ss_a * (1.0 / HEAD_DIM) + EPS),
                    lax.rsqrt(ss_b * (1.0 / HEAD_DIM) + EPS))
    return z * inv * gain


def _attn_kernel(q_ref, k_ref, v_ref, qg_ref, kg_ref, lam_ref, sg_ref, o_ref,
                 qa_ref, qb_ref, kn_ref, v1_ref, s_ref, e_ref, part_ref, *, lambda_init):
    lp = lam_ref[...]
    lam = (jnp.exp(jnp.sum(lp[0:1, :] * lp[1:2, :], axis=1, keepdims=True))
           - jnp.exp(jnp.sum(lp[2:3, :] * lp[3:4, :], axis=1, keepdims=True))
           + lambda_init)

    kn_ref[...] = _half_norm(k_ref[...], kg_ref[...]).astype(jnp.bfloat16)
    qn = _half_norm(q_ref[...], qg_ref[...]) * (HEAD_DIM ** -0.5 * LOG2E)
    first = lax.broadcasted_iota(jnp.int32, (SEQ, V_DIM), 1) < HEAD_DIM
    qa_ref[...] = jnp.where(first, qn, 0.0).astype(jnp.bfloat16)
    qb_ref[...] = jnp.where(first, 0.0, qn).astype(jnp.bfloat16)

    ones_col = lax.broadcasted_iota(jnp.int32, (SEQ, V_DIM), 1) == 0
    v1_ref[:, 0:V_DIM] = v_ref[...]
    v1_ref[:, V_DIM:2 * V_DIM] = jnp.where(ones_col, 1.0, 0.0).astype(jnp.bfloat16)

    tq = ATT_TQ
    nt = (((1,), (1,)), ((), ()))
    visible = (lax.broadcasted_iota(jnp.int32, (tq, tq), 1) // CHUNK
               <= lax.broadcasted_iota(jnp.int32, (tq, tq), 0) // CHUNK)
    halves = (qa_ref, qb_ref)
    units = [(qi, h) for qi in range(SEQ // tq) for h in range(2)]

    def stage_scores(u, slot):
        qi, h = units[u]
        k0 = qi * tq
        q = halves[h][k0:k0 + tq, :]
        dg = lax.dot_general(q, kn_ref[k0:k0 + tq, :], nt, preferred_element_type=jnp.float32)
        s_ref[slot, :, k0:k0 + tq] = jnp.where(visible, dg, -jnp.inf)
        if qi:
            s_ref[slot, :, 0:k0] = lax.dot_general(q, kn_ref[0:k0, :], nt,
                                                   preferred_element_type=jnp.float32)

    def stage_numerators(u, slot):
        kend = (units[u][0] + 1) * tq
        sc = s_ref[slot, :, 0:kend]
        m = jnp.max(sc, axis=1, keepdims=True)
        e_ref[slot, :, 0:kend] = jnp.exp2(sc - m).astype(jnp.bfloat16)

    def stage_values(u, slot):
        qi, h = units[u]
        k0 = qi * tq
        kend = k0 + tq
        acc = jnp.dot(e_ref[slot, :, 0:kend], v1_ref[0:kend, :],
                      preferred_element_type=jnp.float32)
        attn = acc[:, 0:V_DIM] * (1.0 / acc[:, V_DIM:V_DIM + 1])
        if h == 0:
            part_ref[...] = attn
        else:
            o = part_ref[...] - lam * attn
            o = _rms(o, sg_ref[...]) * (1.0 - lambda_init)
            o_ref[k0:kend, :] = o.astype(jnp.bfloat16)

    for step in range(len(units) + 2):
        if step < len(units):
            stage_scores(step, step % 2)
        if 1 <= step <= len(units):
            stage_numerators(step - 1, (step - 1) % 2)
        if step >= 2:
            stage_values(step - 2, step % 2)


def _attention(qk, v, qg2, kg2, lam_rows, subln_g, lambda_init):
    return pl.pallas_call(
        functools.partial(_attn_kernel, lambda_init=lambda_init),
        grid=(BATCH, N_HEADS),
        in_specs=[
            pl.BlockSpec((SEQ, V_DIM), lambda b, h: (b, h)),
            pl.BlockSpec((SEQ, V_DIM), lambda b, h: (b, N_HEADS + h)),
            pl.BlockSpec((SEQ, V_DIM), lambda b, h: (b, h)),
            _full((1, V_DIM)), _full((1, V_DIM)),
            _full((SUBLANES, HEAD_DIM)),
            _full((1, V_DIM)),
        ],
        out_specs=pl.BlockSpec((SEQ, V_DIM), lambda b, h: (b, h)),
        out_shape=jax.ShapeDtypeStruct((N_TOK, D_MODEL), jnp.bfloat16),
        scratch_shapes=[pltpu.VMEM((SEQ, V_DIM), jnp.bfloat16)] * 3 + [
            pltpu.VMEM((SEQ, 2 * V_DIM), jnp.bfloat16),
            pltpu.VMEM((2, ATT_TQ, SEQ), jnp.float32),
            pltpu.VMEM((2, ATT_TQ, SEQ), jnp.bfloat16),
            pltpu.VMEM((ATT_TQ, V_DIM), jnp.float32),
        ],
        compiler_params=_cparams(2),
        name="diff_attention",
    )(qk, qk, v, qg2, kg2, lam_rows, subln_g)


def _post_kernel(x_ref, m_ref, w_ref, g2_ref, wr_ref, br_ref,
                 x1_ref, h2s_ref, rf_ref, rft_ref, cnt_ref, run_ref):
    @pl.when(pl.program_id(0) == 0)
    def _():
        run_ref[...] = jnp.zeros_like(run_ref)

    mix = jnp.dot(m_ref[...], w_ref[...], preferred_element_type=jnp.float32)
    _residual_norm_route(x_ref[...] + mix, g2_ref, wr_ref, br_ref, run_ref,
                         x1_ref, h2s_ref, rf_ref, rft_ref, cnt_ref, POST_TS)


def _attn_post(x, o, w_o, g2, wr, br):
    return pl.pallas_call(
        _post_kernel,
        grid=(N_TOK // POST_TS,),
        in_specs=[
            pl.BlockSpec((POST_TS, D_MODEL), lambda i: (i, 0)),
            pl.BlockSpec((POST_TS, D_MODEL), lambda i: (i, 0)),
            _full((D_MODEL, D_MODEL)),
            _full((1, D_MODEL)),
            _full((D_MODEL, 2 * LANES)), _full((1, LANES)),
        ],
        out_specs=_route_out_specs(POST_TS, lambda i: i),
        out_shape=_route_out_shapes(),
        scratch_shapes=[pltpu.VMEM((1, LANES), jnp.float32)],
        compiler_params=_cparams(1),
        name="attn_post",
    )(x, o, w_o, g2, wr, br)


def _unrolled(n, body):
    def group(g, carry):
        for u in range(DMA_UNROLL):
            body(g * DMA_UNROLL + u)
        return carry
    lax.fori_loop(0, n // DMA_UNROLL, group, 0)


def _expert_kernel(be_ref, nv_ref, dest_ref, pad_ref, next_ref,
                   h2s_ref, w1_hbm, w3_hbm, w2_hbm, yb_ref,
                   rowtok_ref, xbuf_ref, sem, w1f_ref, w3f_ref, w2f_ref, wsem,
                   w1b_ref, w3b_ref, w2b_ref, *, layer):
    b = pl.program_id(0)
    n_valid = nv_ref[0]
    w_hbm = (w1_hbm, w3_hbm, w2_hbm)
    w_f32 = (w1f_ref, w3f_ref, w2f_ref)

    def weight_copies(expert):
        return [pltpu.make_async_copy(w_hbm[i].at[layer, expert], w_f32[i], wsem.at[i])
                for i in range(3)]

    def row_copy(block, slot, r):
        tok = rowtok_ref[block * ROW_BLOCK + r]
        return pltpu.make_async_copy(
            h2s_ref.at[pl.ds(pl.multiple_of(tok * SLAB, SLAB), SLAB)],
            xbuf_ref.at[slot, pl.ds(r * SLAB, SLAB)],
            sem.at[slot])

    def block_arrival(slot):
        return pltpu.make_async_copy(
            h2s_ref.at[pl.ds(0, ROW_BLOCK * SLAB)], xbuf_ref.at[slot], sem.at[slot])

    @pl.when(b == 0)
    def _():
        for c in weight_copies(be_ref[0]):
            c.start()
        def clear_padding(e, carry):
            def clear(j, c):
                rowtok_ref[j] = 0
                return c
            return lax.fori_loop(pad_ref[e], pad_ref[N_EXPERTS + e], clear, carry)
        lax.fori_loop(0, N_EXPERTS, clear_padding, 0)

        def place(t):
            rowtok_ref[dest_ref[t]] = t
            rowtok_ref[dest_ref[N_TOK + t]] = t
        _unrolled(N_TOK, place)
        second = jnp.minimum(1, n_valid - 1)
        _unrolled(ROW_BLOCK, lambda r: row_copy(0, 0, r).start())
        _unrolled(ROW_BLOCK, lambda r: row_copy(second, 1, r).start())

    expert = be_ref[b]
    fresh = (b == 0) | (expert != be_ref[jnp.maximum(b - 1, 0)])

    @pl.when(fresh & (b < n_valid))
    def _():
        for c in weight_copies(expert):
            c.wait()
        w1b_ref[...] = w1f_ref[...].astype(jnp.bfloat16)
        w3b_ref[...] = w3f_ref[...].astype(jnp.bfloat16)
        w2b_ref[...] = w2f_ref[...].astype(jnp.bfloat16)
        following = next_ref[expert]

        @pl.when(following >= 0)
        def _():
            for c in weight_copies(following):
                c.start()

    @pl.when(b < n_valid)
    def _():
        slot = lax.rem(b, GATHER_SLOTS)
        block_arrival(slot).wait()
        x = _from_slab(xbuf_ref.at[slot], ROW_BLOCK).astype(jnp.bfloat16)
        ahead = jnp.minimum(b + 2, n_valid - 1)
        ahead_slot = lax.rem(b + 2, GATHER_SLOTS)
        for r in range(ROW_BLOCK):
            row_copy(ahead, ahead_slot, r).start()
        h1 = jnp.dot(x, w1b_ref[...], preferred_element_type=jnp.float32)
        h3 = jnp.dot(x, w3b_ref[...], preferred_element_type=jnp.float32)
        act = (h1 * jax.nn.sigmoid(h1) * h3).astype(jnp.bfloat16)
        y = jnp.dot(act, w2b_ref[...], preferred_element_type=jnp.float32)
        _to_slab(yb_ref, y, ROW_BLOCK)

    @pl.when(b == n_valid - 1)
    def _():
        block_arrival(lax.rem(b + 1, GATHER_SLOTS)).wait()
        block_arrival(lax.rem(b + 2, GATHER_SLOTS)).wait()

    @pl.when(b >= n_valid)
    def _():
        yb_ref[...] = jnp.zeros_like(yb_ref)


def _experts(layer, block_e, n_valid, dest_flat, pad_rows, next_expert, h2s, w1, w3, w2):
    hbm = pl.BlockSpec(memory_space=pl.ANY)
    w_in = [pltpu.VMEM((D_MODEL, D_EXPERT), dt) for dt in (jnp.float32, jnp.bfloat16)]
    w_out = [pltpu.VMEM((D_EXPERT, D_MODEL), dt) for dt in (jnp.float32, jnp.bfloat16)]
    return pl.pallas_call(
        functools.partial(_expert_kernel, layer=layer),
        grid_spec=pltpu.PrefetchScalarGridSpec(
            num_scalar_prefetch=5,
            grid=(N_BLOCKS,),
            in_specs=[hbm, hbm, hbm, hbm],
            out_specs=pl.BlockSpec((ROW_BLOCK * SLAB, LANES), lambda b, *_: (b, 0)),
            scratch_shapes=[
                pltpu.SMEM((N_ROWS,), jnp.int32),
                pltpu.VMEM((GATHER_SLOTS, ROW_BLOCK * SLAB, LANES), jnp.float32),
                pltpu.SemaphoreType.DMA((GATHER_SLOTS,)),
                w_in[0], w_in[0], w_out[0],
                pltpu.SemaphoreType.DMA((3,)),
                w_in[1], w_in[1], w_out[1],
            ],
        ),
        out_shape=jax.ShapeDtypeStruct((N_ROWS * SLAB, LANES), jnp.float32),
        compiler_params=_cparams(1),
        name="moe_experts",
    )(block_e, n_valid, dest_flat, pad_rows, next_expert, h2s, w1, w3, w2)


def _combine_kernel(pos_ref, x1_ref, rf_ref, yb_ref, out_ref, buf_ref, sem):
    i = pl.program_id(0)
    n = pl.num_programs(0)
    tm = COMB_TM

    def copy(tile, slot, j, k):
        p = pos_ref[k * N_TOK + tile * tm + j]
        return pltpu.make_async_copy(
            yb_ref.at[pl.ds(pl.multiple_of(p * SLAB, SLAB), SLAB)],
            buf_ref.at[slot, k, pl.ds(pl.multiple_of(j * SLAB, SLAB), SLAB)],
            sem.at[slot])

    def issue(tile, slot):
        def body(j):
            copy(tile, slot, j, 0).start()
            copy(tile, slot, j, 1).start()
        _unrolled(tm, body)

    def drain(tile, slot):
        def body(j):
            copy(tile, slot, j, 0).wait()
            copy(tile, slot, j, 1).wait()
        _unrolled(tm, body)

    slot = i % 2

    @pl.when(i == 0)
    def _():
        issue(0, 0)

    @pl.when(i + 1 < n)
    def _():
        issue(i + 1, 1 - slot)

    drain(i, slot)
    rf = rf_ref[...]
    y0 = _from_slab(buf_ref.at[slot, 0], tm)
    y1 = _from_slab(buf_ref.at[slot, 1], tm)
    out_ref[...] = x1_ref[...] + rf[:, 0:1] * y0 + rf[:, 1:2] * y1


def _combine(pos_flat, x1, rf, yb):
    return pl.pallas_call(
        _combine_kernel,
        grid_spec=pltpu.PrefetchScalarGridSpec(
            num_scalar_prefetch=1,
            grid=(N_TOK // COMB_TM,),
            in_specs=[
                pl.BlockSpec((COMB_TM, D_MODEL), lambda i, p: (i, 0)),
                pl.BlockSpec((COMB_TM, LANES), lambda i, p: (i, 0)),
                pl.BlockSpec(memory_space=pl.ANY),
            ],
            out_specs=pl.BlockSpec((COMB_TM, D_MODEL), lambda i, p: (i, 0)),
            scratch_shapes=[
                pltpu.VMEM((2, TOP_K, COMB_TM * SLAB, LANES), jnp.float32),
                pltpu.SemaphoreType.DMA((2,)),
            ],
        ),
        out_shape=jax.ShapeDtypeStruct((N_TOK, D_MODEL), jnp.float32),
        compiler_params=_cparams(1),
        name="moe_combine",
    )(pos_flat, x1, rf, yb)


def _combine_qkv_kernel(pos_ref, x1_ref, rf_ref, yb_ref, g1_ref, w_ref,
                        x2_ref, qk_ref, v_ref, buf_ref, sem):
    i = pl.program_id(0)
    n = pl.num_programs(0)
    tm = QKV_TS

    def copy(tile, slot, j, k):
        p = pos_ref[k * N_TOK + tile * tm + j]
        return pltpu.make_async_copy(
            yb_ref.at[pl.ds(pl.multiple_of(p * SLAB, SLAB), SLAB)],
            buf_ref.at[slot, k, pl.ds(j * SLAB, SLAB)],
            sem.at[slot])

    def plane_arrival(slot, k):
        return pltpu.make_async_copy(
            yb_ref.at[pl.ds(0, tm * SLAB)], buf_ref.at[slot, k], sem.at[slot])

    @pl.when(i == 0)
    def _():
        def first(j):
            copy(0, 0, j, 0).start()
            copy(0, 0, j, 1).start()
        _unrolled(tm, first)

    slot = i % 2
    plane_arrival(slot, 0).wait()
    plane_arrival(slot, 1).wait()
    rf = rf_ref[...]
    y0 = _from_slab(buf_ref.at[slot, 0], tm)
    y1 = _from_slab(buf_ref.at[slot, 1], tm)
    x2 = x1_ref[...] + rf[:, 0:1] * y0 + rf[:, 1:2] * y1
    x2_ref[...] = x2
    ahead = jnp.minimum(i + 1, n - 1)
    for j in range(tm):
        copy(ahead, 1 - slot, j, 0).start()
        copy(ahead, 1 - slot, j, 1).start()
    h = _rms(x2, g1_ref[...]).astype(jnp.bfloat16)
    qkv = jnp.dot(h, w_ref[...], preferred_element_type=jnp.float32)
    qk_ref[...] = qkv[:, :2 * D_MODEL]
    v_ref[...] = qkv[:, 2 * D_MODEL:].astype(jnp.bfloat16)

    @pl.when(i == n - 1)
    def _():
        plane_arrival(1 - slot, 0).wait()
        plane_arrival(1 - slot, 1).wait()


def _combine_qkv(pos_flat, x1, rf, yb, g1, w_qkv):
    tile = lambda i, p: (i, 0)
    return pl.pallas_call(
        _combine_qkv_kernel,
        grid_spec=pltpu.PrefetchScalarGridSpec(
            num_scalar_prefetch=1,
            grid=(N_TOK // QKV_TS,),
            in_specs=[
                pl.BlockSpec((QKV_TS, D_MODEL), tile),
                pl.BlockSpec((QKV_TS, LANES), tile),
                pl.BlockSpec(memory_space=pl.ANY),
                _full((1, D_MODEL)),
                _full((D_MODEL, 3 * D_MODEL)),
            ],
            out_specs=(
                pl.BlockSpec((QKV_TS, D_MODEL), tile),
                pl.BlockSpec((QKV_TS, 2 * D_MODEL), tile),
                pl.BlockSpec((QKV_TS, D_MODEL), tile),
            ),
            scratch_shapes=[
                pltpu.VMEM((2, TOP_K, QKV_TS * SLAB, LANES), jnp.float32),
                pltpu.SemaphoreType.DMA((2,)),
            ],
        ),
        out_shape=(
            jax.ShapeDtypeStruct((N_TOK, D_MODEL), jnp.float32),
            jax.ShapeDtypeStruct((N_TOK, 2 * D_MODEL), jnp.float32),
            jax.ShapeDtypeStruct((N_TOK, D_MODEL), jnp.bfloat16),
        ),
        compiler_params=_cparams(1),
        name="combine_qkv",
    )(pos_flat, x1, rf, yb, g1, w_qkv)


def _moe_experts(layer, h2s, rft, cnt, w1, w3, w2):
    experts = rft[2:4].astype(jnp.int32)
    rank = rft[4:6].astype(jnp.int32)
    counts = cnt[0, :N_EXPERTS].astype(jnp.int32)
    padded = (counts + ROW_BLOCK - 1) // ROW_BLOCK * ROW_BLOCK
    pad_ends = jnp.cumsum(padded)
    pad_starts = pad_ends - padded
    ids = jnp.arange(N_EXPERTS, dtype=jnp.int32)
    is_e = experts[None] == ids[:, None, None]
    dest = (jnp.sum(jnp.where(is_e, pad_starts[:, None, None], 0), axis=0) + rank).reshape(-1)
    n_valid = (pad_ends[-1:] // ROW_BLOCK).astype(jnp.int32)
    block_row0 = jnp.arange(N_BLOCKS, dtype=jnp.int32) * ROW_BLOCK
    block_e = jnp.minimum(
        jnp.sum((pad_ends[None, :] <= block_row0[:, None]).astype(jnp.int32), axis=1),
        N_EXPERTS - 1).astype(jnp.int32)
    pad_rows = jnp.concatenate([pad_starts + counts, pad_ends]).astype(jnp.int32)
    later_nonempty = (counts > 0)[None, :] & (ids[None, :] > ids[:, None])
    following = jnp.min(jnp.where(later_nonempty, ids[None, :], N_EXPERTS), axis=1)
    next_expert = jnp.where(following < N_EXPERTS, following, -1).astype(jnp.int32)
    yb = _experts(layer, block_e, n_valid, dest, pad_rows, next_expert, h2s, w1, w3, w2)
    return dest, yb


def _router_params(w_grp, b_grp, w_exp, b_exp):
    wr = jnp.zeros((D_MODEL, LANES), jnp.float32)
    wr = wr.at[:, :N_EXPERTS].set(w_exp).at[:, GRP_LANE0:GRP_LANE0 + N_GROUPS].set(w_grp)
    br = jnp.zeros((1, LANES), jnp.float32)
    br = br.at[0, :N_EXPERTS].set(b_exp).at[0, GRP_LANE0:GRP_LANE0 + N_GROUPS].set(b_grp)
    w_hi = wr.astype(jnp.bfloat16)
    w_lo = (wr - w_hi.astype(jnp.float32)).astype(jnp.bfloat16)
    return jnp.concatenate([w_hi, w_lo], axis=1), br


def kernel(x, norm1_g, norm2_g, conv_w_in, conv_b_in, conv_dw, conv_dw_b, conv_ln_g, conv_ln_b, conv_w_out, conv_b_out, attn_w_qkv, attn_q_g, attn_k_g, attn_lq1, attn_lk1, attn_lq2, attn_lk2, attn_subln_g, attn_w_o, moe_w_grp, moe_b_grp, moe_w_exp, moe_b_exp, moe_w1, moe_w3, moe_w2):
    assert x.shape == (BATCH, SEQ, D_MODEL) and x.dtype == jnp.float32
    assert moe_w1.shape == (2, N_EXPERTS, D_MODEL, D_EXPERT) and conv_dw.shape[1] == CONV_WIDTH
    bf16 = jnp.bfloat16
    row = lambda a: a.reshape(1, -1)
    xf = x.reshape(N_TOK, D_MODEL)

    wr, br = _router_params(moe_w_grp[0], moe_b_grp[0], moe_w_exp[0], moe_b_exp[0])
    dw = jnp.zeros((HALO, D_MODEL), jnp.float32).at[:CONV_WIDTH].set(conv_dw[0])
    x1, h2s, rf, rft, cnt = _conv_layer(
        xf, row(norm1_g[0]), conv_w_in[0].astype(bf16), row(conv_b_in[0]), dw,
        row(conv_dw_b[0]), row(conv_ln_g[0]), row(conv_ln_b[0]),
        conv_w_out[0].astype(bf16), row(conv_b_out[0]), row(norm2_g[0]), wr, br)
    dest, yb = _moe_experts(0, h2s, rft, cnt, moe_w1, moe_w3, moe_w2)

    lambda_init = 0.8 - 0.6 * math.exp(-0.3 * 1)
    wr, br = _router_params(moe_w_grp[1], moe_b_grp[1], moe_w_exp[1], moe_b_exp[1])
    xf, qk, v = _combine_qkv(dest, x1, rf, yb, row(norm1_g[1]), attn_w_qkv[0].astype(bf16))
    lam_rows = jnp.zeros((SUBLANES, HEAD_DIM), jnp.float32)
    lam_rows = lam_rows.at[0].set(attn_lq1[0]).at[1].set(attn_lk1[0])
    lam_rows = lam_rows.at[2].set(attn_lq2[0]).at[3].set(attn_lk2[0])
    two = lambda g: jnp.concatenate([g, g]).reshape(1, V_DIM)
    o = _attention(qk, v, two(attn_q_g[0]), two(attn_k_g[0]), lam_rows,
                   row(attn_subln_g[0]), lambda_init)
    x1, h2s, rf, rft, cnt = _attn_post(xf, o, attn_w_o[0].astype(bf16), row(norm2_g[1]), wr, br)
    dest, yb = _moe_experts(1, h2s, rft, cnt, moe_w1, moe_w3, moe_w2)
    xf = _combine(dest, x1, rf, yb)
    return xf.reshape(BATCH, SEQ, D_MODEL)
```

```python
import functools
import math

import jax
import jax.numpy as jnp
from jax import lax
from jax.experimental import pallas as pl
from jax.experimental.pallas import tpu as pltpu

D_MODEL = 1024
BATCH = 8
SEQ = 2048
N_TOK = BATCH * SEQ
CHUNK = 64
CONV_WIDTH = 31
N_HEADS = 8
HEAD_DIM = 64
V_DIM = 128
N_GROUPS = 4
EXPERTS_PER_GROUP = 8
N_EXPERTS = 32
TOP_K = 2
D_EXPERT = 512
EPS = 1e-6
LOG2E = math.log2(math.e)

LANES = 128
SUBLANES = 8
SLAB = D_MODEL // LANES
VMEM_LIMIT = 56 * 1024 * 1024

CONV_TS = 512
HALO = 32
CONV_RC = 64
POST_TS = 512
QKV_TS = 512
ATT_TQ = 256
ROW_BLOCK = 256
GATHER_SLOTS = 3
N_ASSIGN = N_TOK * TOP_K
N_BLOCKS = N_ASSIGN // ROW_BLOCK + N_EXPERTS
N_ROWS = N_BLOCKS * ROW_BLOCK
COMB_TM = 256
DMA_UNROLL = 16
GRP_LANE0 = N_EXPERTS


def _cparams(n_axes):
    return pltpu.CompilerParams(
        dimension_semantics=("arbitrary",) * n_axes, vmem_limit_bytes=VMEM_LIMIT)


def _rms(x, g):
    return x * lax.rsqrt(jnp.mean(x * x, axis=-1, keepdims=True) + EPS) * g


def _to_slab(ref, val, rows):
    for j in range(SLAB):
        ref[pl.ds(j, rows, stride=SLAB), :] = val[:, j * LANES:(j + 1) * LANES]


def _from_slab(ref, rows):
    return jnp.concatenate(
        [ref[pl.ds(j, rows, stride=SLAB), :] for j in range(SLAB)], axis=-1)


def _residual_norm_route(x1, g2_ref, wr_ref, br_ref, run_ref,
                         x1_ref, h2s_ref, rf_ref, rft_ref, cnt_ref, rows):
    x1_ref[...] = x1
    h2 = _rms(x1, g2_ref[...])
    _to_slab(h2s_ref, h2, rows)

    hi = h2.astype(jnp.bfloat16)
    lo = (h2 - hi.astype(jnp.float32)).astype(jnp.bfloat16)
    ab = jnp.dot(hi, wr_ref[...], preferred_element_type=jnp.float32)
    c = jnp.dot(lo, wr_ref[:, 0:LANES], preferred_element_type=jnp.float32)
    logits = ab[:, 0:LANES] + ab[:, LANES:2 * LANES] + c + br_ref[...]
    lane = lax.broadcasted_iota(jnp.int32, (rows, LANES), 1)
    lane_f = lane.astype(jnp.float32)
    neg = jnp.float32(-jnp.inf)
    big = jnp.float32(1e9)

    gmask = (lane >= GRP_LANE0) & (lane < GRP_LANE0 + N_GROUPS)
    gl = jnp.where(gmask, logits, neg)
    gmax = jnp.max(gl, axis=1, keepdims=True)
    gidx = jnp.min(jnp.where(gl == gmax, lane_f, big), axis=1, keepdims=True) - GRP_LANE0
    gsum = jnp.sum(jnp.where(gmask, jnp.exp(gl - gmax), 0.0), axis=1, keepdims=True)
    grp_p = 1.0 / gsum

    lane0 = gidx * EXPERTS_PER_GROUP
    emask = (lane_f >= lane0) & (lane_f < lane0 + EXPERTS_PER_GROUP)
    el = jnp.where(emask, logits, neg)
    m1 = jnp.max(el, axis=1, keepdims=True)
    i1 = jnp.min(jnp.where(el == m1, lane_f, big), axis=1, keepdims=True)
    el2 = jnp.where(lane_f == i1, neg, el)
    m2 = jnp.max(el2, axis=1, keepdims=True)
    i2 = jnp.min(jnp.where(el2 == m2, lane_f, big), axis=1, keepdims=True)
    t = jnp.exp(m2 - m1)
    inv = 1.0 / (1.0 + t)
    g_first = grp_p * inv
    g_second = grp_p * t * inv

    sel1 = lane_f == i1
    sel2 = lane_f == i2
    member = jnp.where(sel1 | sel2, 1.0, 0.0).astype(jnp.bfloat16)
    r_i = lax.broadcasted_iota(jnp.int32, (rows, rows), 0)
    c_i = lax.broadcasted_iota(jnp.int32, (rows, rows), 1)
    tri = jnp.where(c_i < r_i, 1.0, 0.0).astype(jnp.bfloat16)
    cum = jnp.dot(tri, member, preferred_element_type=jnp.float32) + run_ref[...]
    rank1 = jnp.sum(jnp.where(sel1, cum, 0.0), axis=1, keepdims=True)
    rank2 = jnp.sum(jnp.where(sel2, cum, 0.0), axis=1, keepdims=True)
    run_new = run_ref[...] + jnp.sum(member.astype(jnp.float32), axis=0, keepdims=True)
    run_ref[...] = run_new
    cnt_ref[...] = jnp.broadcast_to(run_new, (SUBLANES, LANES))

    out = jnp.where(lane == 0, g_first, 0.0)
    out = jnp.where(lane == 1, g_second, out)
    out = jnp.where(lane == 2, i1, out)
    out = jnp.where(lane == 3, i2, out)
    out = jnp.where(lane == 4, rank1, out)
    out = jnp.where(lane == 5, rank2, out)
    rf_ref[...] = out
    rft_ref[...] = jnp.transpose(out)[0:SUBLANES, :]


def _route_out_shapes():
    return (
        jax.ShapeDtypeStruct((N_TOK, D_MODEL), jnp.float32),
        jax.ShapeDtypeStruct((N_TOK * SLAB, LANES), jnp.float32),
        jax.ShapeDtypeStruct((N_TOK, LANES), jnp.float32),
        jax.ShapeDtypeStruct((SUBLANES, N_TOK), jnp.float32),
        jax.ShapeDtypeStruct((SUBLANES, LANES), jnp.float32),
    )


def _route_out_specs(ts, idx):
    return (
        pl.BlockSpec((ts, D_MODEL), lambda *a: (idx(*a), 0)),
        pl.BlockSpec((ts * SLAB, LANES), lambda *a: (idx(*a), 0)),
        pl.BlockSpec((ts, LANES), lambda *a: (idx(*a), 0)),
        pl.BlockSpec((SUBLANES, ts), lambda *a: (0, idx(*a))),
        pl.BlockSpec((SUBLANES, LANES), lambda *a: (0, 0)),
    )


def _full(shape):
    return pl.BlockSpec(shape, lambda *a: (0,) * len(shape))


def _conv_kernel(x_ref, g1_ref, win_ref, bin_ref, dw_ref, dwb_ref, lng_ref, lnb_ref,
                 wout_ref, bout_ref, g2_ref, wr_ref, br_ref,
                 x1_ref, h2s_ref, rf_ref, rft_ref, cnt_ref,
                 ext_ref, conv_ref, run_ref):
    b = pl.program_id(0)
    s = pl.program_id(1)
    ts = CONV_TS

    @pl.when((b == 0) & (s == 0))
    def _():
        run_ref[...] = jnp.zeros_like(run_ref)

    @pl.when(s == 0)
    def _():
        ext_ref[0:HALO, :] = jnp.zeros((HALO, D_MODEL), jnp.float32)

    x = x_ref[...]
    h = _rms(x, g1_ref[...]).astype(jnp.bfloat16)
    u = jnp.dot(h, win_ref[...], preferred_element_type=jnp.float32) + bin_ref[...]
    glu = u[:, :D_MODEL] * jax.nn.sigmoid(u[:, D_MODEL:])
    ext_ref[HALO:HALO + ts, :] = glu

    base = HALO - (CONV_WIDTH - 1)

    def lane_chunk(c, carry):
        cols = pl.ds(pl.multiple_of(c * LANES, LANES), LANES)
        taps = [dw_ref[pl.ds(k, 1), cols] for k in range(CONV_WIDTH)]
        for r0 in range(0, ts, CONV_RC):
            acc = None
            for r in range(SUBLANES):
                rows = CONV_RC + (SUBLANES if r else 0)
                part = None
                for q in range((base + CONV_WIDTH - 1) // SUBLANES + 1):
                    k = SUBLANES * q + r - base
                    if 0 <= k < CONV_WIDTH:
                        term = ext_ref[pl.ds(r0 + SUBLANES * q, rows), cols] * taps[k]
                        part = term if part is None else part + term
                part = part[r:r + CONV_RC, :] if r else part
                acc = part if acc is None else acc + part
            conv_ref[pl.ds(r0, CONV_RC), cols] = acc
        return carry

    lax.fori_loop(0, D_MODEL // LANES, lane_chunk, 0)
    ext_ref[0:HALO, :] = ext_ref[ts:ts + HALO, :]

    v = conv_ref[...] + dwb_ref[...]
    mu = jnp.mean(v, axis=-1, keepdims=True)
    vc = v - mu
    var = jnp.mean(vc * vc, axis=-1, keepdims=True)
    y = vc * lax.rsqrt(var + EPS) * lng_ref[...] + lnb_ref[...]
    y = (y * jax.nn.sigmoid(y)).astype(jnp.bfloat16)
    mix = jnp.dot(y, wout_ref[...], preferred_element_type=jnp.float32) + bout_ref[...]
    _residual_norm_route(x + mix, g2_ref, wr_ref, br_ref, run_ref,
                         x1_ref, h2s_ref, rf_ref, rft_ref, cnt_ref, ts)


def _conv_layer(x, g1, w_in, b_in, dw, dw_b, ln_g, ln_b, w_out, b_out, g2, wr, br):
    ns = SEQ // CONV_TS
    tile = lambda b, s: b * ns + s
    return pl.pallas_call(
        _conv_kernel,
        grid=(BATCH, ns),
        in_specs=[
            pl.BlockSpec((CONV_TS, D_MODEL), lambda b, s: (tile(b, s), 0)),
            _full((1, D_MODEL)),
            _full((D_MODEL, 2 * D_MODEL)),
            _full((1, 2 * D_MODEL)),
            _full((HALO, D_MODEL)),
            _full((1, D_MODEL)), _full((1, D_MODEL)), _full((1, D_MODEL)),
            _full((D_MODEL, D_MODEL)),
            _full((1, D_MODEL)), _full((1, D_MODEL)),
            _full((D_MODEL, 2 * LANES)), _full((1, LANES)),
        ],
        out_specs=_route_out_specs(CONV_TS, tile),
        out_shape=_route_out_shapes(),
        scratch_shapes=[
            pltpu.VMEM((HALO + CONV_TS, D_MODEL), jnp.float32),
            pltpu.VMEM((CONV_TS, D_MODEL), jnp.float32),
            pltpu.VMEM((1, LANES), jnp.float32),
        ],
        compiler_params=_cparams(2),
        name="conv_mixer",
    )(x, g1, w_in, b_in, dw, dw_b, ln_g, ln_b, w_out, b_out, g2, wr, br)


def _half_norm(z, gain):
    lane = lax.broadcasted_iota(jnp.int32, z.shape, 1)
    first = lane < HEAD_DIM
    zz = z * z
    ss_a = jnp.sum(jnp.where(first, zz, 0.0), axis=1, keepdims=True)
    ss_b = jnp.sum(jnp.where(first, 0.0, zz), axis=1, keepdims=True)
    inv = jnp.where(first, lax.rsqrt(ss_a * (1.0 / HEAD_DIM) + EPS),
                    lax.rsqrt(ss_b * (1.0 / HEAD_DIM) + EPS))
    return z * inv * gain


def _attn_kernel(q_ref, k_ref, v_ref, lam_ref, sg_ref, o_ref,
                 qa_ref, qb_ref, v1_ref, s_ref, e_ref, part_ref, *, lambda_init):
    lp = lam_ref[...]
    lam = (jnp.exp(jnp.sum(lp[0:1, :] * lp[1:2, :], axis=1, keepdims=True))
           - jnp.exp(jnp.sum(lp[2:3, :] * lp[3:4, :], axis=1, keepdims=True))
           + lambda_init)

    q = q_ref[...]
    first = lax.broadcasted_iota(jnp.int32, (SEQ, V_DIM), 1) < HEAD_DIM
    qa_ref[...] = jnp.where(first, q, jnp.zeros_like(q))
    qb_ref[...] = jnp.where(first, jnp.zeros_like(q), q)

    ones_col = lax.broadcasted_iota(jnp.int32, (SEQ, V_DIM), 1) == 0
    v1_ref[:, 0:V_DIM] = v_ref[...]
    v1_ref[:, V_DIM:2 * V_DIM] = jnp.where(ones_col, 1.0, 0.0).astype(jnp.bfloat16)

    tq = ATT_TQ
    nt = (((1,), (1,)), ((), ()))
    visible = (lax.broadcasted_iota(jnp.int32, (tq, tq), 1) // CHUNK
               <= lax.broadcasted_iota(jnp.int32, (tq, tq), 0) // CHUNK)
    halves = (qa_ref, qb_ref)
    units = [(qi, h) for qi in range(SEQ // tq) for h in range(2)]

    def stage_scores(u, slot):
        qi, h = units[u]
        k0 = qi * tq
        q = halves[h][k0:k0 + tq, :]
        dg = lax.dot_general(q, k_ref[k0:k0 + tq, :], nt, preferred_element_type=jnp.float32)
        s_ref[slot, :, k0:k0 + tq] = jnp.where(visible, dg, -jnp.inf)
        if qi:
            s_ref[slot, :, 0:k0] = lax.dot_general(q, k_ref[0:k0, :], nt,
                                                   preferred_element_type=jnp.float32)

    def stage_numerators(u, slot):
        kend = (units[u][0] + 1) * tq
        sc = s_ref[slot, :, 0:kend]
        m = jnp.max(sc, axis=1, keepdims=True)
        e_ref[slot, :, 0:kend] = jnp.exp2(sc - m).astype(jnp.bfloat16)

    def stage_values(u, slot):
        qi, h = units[u]
        k0 = qi * tq
        kend = k0 + tq
        acc = jnp.dot(e_ref[slot, :, 0:kend], v1_ref[0:kend, :],
                      preferred_element_type=jnp.float32)
        attn = acc[:, 0:V_DIM] * (1.0 / acc[:, V_DIM:V_DIM + 1])
        if h == 0:
            part_ref[...] = attn
        else:
            o = part_ref[...] - lam * attn
            o = _rms(o, sg_ref[...]) * (1.0 - lambda_init)
            o_ref[k0:kend, :] = o.astype(jnp.bfloat16)

    for step in range(len(units) + 2):
        if step < len(units):
            stage_scores(step, step % 2)
        if 1 <= step <= len(units):
            stage_numerators(step - 1, (step - 1) % 2)
        if step >= 2:
            stage_values(step - 2, step % 2)


def _attention(qk, v, lam_rows, subln_g, lambda_init):
    return pl.pallas_call(
        functools.partial(_attn_kernel, lambda_init=lambda_init),
        grid=(BATCH, N_HEADS),
        in_specs=[
            pl.BlockSpec((SEQ, V_DIM), lambda b, h: (b, h)),
            pl.BlockSpec((SEQ, V_DIM), lambda b, h: (b, N_HEADS + h)),
            pl.BlockSpec((SEQ, V_DIM), lambda b, h: (b, h)),
            _full((SUBLANES, HEAD_DIM)),
            _full((1, V_DIM)),
        ],
        out_specs=pl.BlockSpec((SEQ, V_DIM), lambda b, h: (b, h)),
        out_shape=jax.ShapeDtypeStruct((N_TOK, D_MODEL), jnp.bfloat16),
        scratch_shapes=[pltpu.VMEM((SEQ, V_DIM), jnp.bfloat16)] * 2 + [
            pltpu.VMEM((SEQ, 2 * V_DIM), jnp.bfloat16),
            pltpu.VMEM((2, ATT_TQ, SEQ), jnp.float32),
            pltpu.VMEM((2, ATT_TQ, SEQ), jnp.bfloat16),
            pltpu.VMEM((ATT_TQ, V_DIM), jnp.float32),
        ],
        compiler_params=_cparams(2),
        name="diff_attention",
    )(qk, qk, v, lam_rows, subln_g)


def _post_kernel(x_ref, m_ref, w_ref, g2_ref, wr_ref, br_ref,
                 x1_ref, h2s_ref, rf_ref, rft_ref, cnt_ref, run_ref):
    @pl.when(pl.program_id(0) == 0)
    def _():
        run_ref[...] = jnp.zeros_like(run_ref)

    mix = jnp.dot(m_ref[...], w_ref[...], preferred_element_type=jnp.float32)
    _residual_norm_route(x_ref[...] + mix, g2_ref, wr_ref, br_ref, run_ref,
                         x1_ref, h2s_ref, rf_ref, rft_ref, cnt_ref, POST_TS)


def _attn_post(x, o, w_o, g2, wr, br):
    return pl.pallas_call(
        _post_kernel,
        grid=(N_TOK // POST_TS,),
        in_specs=[
            pl.BlockSpec((POST_TS, D_MODEL), lambda i: (i, 0)),
            pl.BlockSpec((POST_TS, D_MODEL), lambda i: (i, 0)),
            _full((D_MODEL, D_MODEL)),
            _full((1, D_MODEL)),
            _full((D_MODEL, 2 * LANES)), _full((1, LANES)),
        ],
        out_specs=_route_out_specs(POST_TS, lambda i: i),
        out_shape=_route_out_shapes(),
        scratch_shapes=[pltpu.VMEM((1, LANES), jnp.float32)],
        compiler_params=_cparams(1),
        name="attn_post",
    )(x, o, w_o, g2, wr, br)


def _unrolled(n, body):
    def group(g, carry):
        for u in range(DMA_UNROLL):
            body(g * DMA_UNROLL + u)
        return carry
    lax.fori_loop(0, n // DMA_UNROLL, group, 0)


def _expert_kernel(be_ref, nv_ref, dest_ref, pad_ref, next_ref,
                   h2s_ref, w1_hbm, w3_hbm, w2_hbm, yb_ref,
                   rowtok_ref, xbuf_ref, sem, w1f_ref, w3f_ref, w2f_ref, wsem,
                   w1b_ref, w3b_ref, w2b_ref, *, layer):
    b = pl.program_id(0)
    n_valid = nv_ref[0]
    w_hbm = (w1_hbm, w3_hbm, w2_hbm)
    w_f32 = (w1f_ref, w3f_ref, w2f_ref)

    def weight_copies(expert):
        return [pltpu.make_async_copy(w_hbm[i].at[layer, expert], w_f32[i], wsem.at[i])
                for i in range(3)]

    def row_copy(block, slot, r):
        tok = rowtok_ref[block * ROW_BLOCK + r]
        return pltpu.make_async_copy(
            h2s_ref.at[pl.ds(pl.multiple_of(tok * SLAB, SLAB), SLAB)],
            xbuf_ref.at[slot, pl.ds(r * SLAB, SLAB)],
            sem.at[slot])

    def block_arrival(slot):
        return pltpu.make_async_copy(
            h2s_ref.at[pl.ds(0, ROW_BLOCK * SLAB)], xbuf_ref.at[slot], sem.at[slot])

    @pl.when(b == 0)
    def _():
        for c in weight_copies(be_ref[0]):
            c.start()
        def clear_padding(e, carry):
            def clear(j, c):
                rowtok_ref[j] = 0
                return c
            return lax.fori_loop(pad_ref[e], pad_ref[N_EXPERTS + e], clear, carry)
        lax.fori_loop(0, N_EXPERTS, clear_padding, 0)

        def place(t):
            rowtok_ref[dest_ref[t]] = t
            rowtok_ref[dest_ref[N_TOK + t]] = t
        _unrolled(N_TOK, place)
        second = jnp.minimum(1, n_valid - 1)
        _unrolled(ROW_BLOCK, lambda r: row_copy(0, 0, r).start())
        _unrolled(ROW_BLOCK, lambda r: row_copy(second, 1, r).start())

    expert = be_ref[b]
    fresh = (b == 0) | (expert != be_ref[jnp.maximum(b - 1, 0)])

    @pl.when(fresh & (b < n_valid))
    def _():
        for c in weight_copies(expert):
            c.wait()
        w1b_ref[...] = w1f_ref[...].astype(jnp.bfloat16)
        w3b_ref[...] = w3f_ref[...].astype(jnp.bfloat16)
        w2b_ref[...] = w2f_ref[...].astype(jnp.bfloat16)
        following = next_ref[expert]

        @pl.when(following >= 0)
        def _():
            for c in weight_copies(following):
                c.start()

    @pl.when(b < n_valid)
    def _():
        slot = lax.rem(b, GATHER_SLOTS)
        block_arrival(slot).wait()
        x = _from_slab(xbuf_ref.at[slot], ROW_BLOCK).astype(jnp.bfloat16)
        ahead = jnp.minimum(b + 2, n_valid - 1)
        ahead_slot = lax.rem(b + 2, GATHER_SLOTS)
        for r in range(ROW_BLOCK):
            row_copy(ahead, ahead_slot, r).start()
        h1 = jnp.dot(x, w1b_ref[...], preferred_element_type=jnp.float32)
        h3 = jnp.dot(x, w3b_ref[...], preferred_element_type=jnp.float32)
        act = (h1 * jax.nn.sigmoid(h1) * h3).astype(jnp.bfloat16)
        y = jnp.dot(act, w2b_ref[...], preferred_element_type=jnp.float32)
        _to_slab(yb_ref, y, ROW_BLOCK)

    @pl.when(b == n_valid - 1)
    def _():
        block_arrival(lax.rem(b + 1, GATHER_SLOTS)).wait()
        block_arrival(lax.rem(b + 2, GATHER_SLOTS)).wait()

    @pl.when(b >= n_valid)
    def _():
        yb_ref[...] = jnp.zeros_like(yb_ref)


def _experts(layer, block_e, n_valid, dest_flat, pad_rows, next_expert, h2s, w1, w3, w2):
    hbm = pl.BlockSpec(memory_space=pl.ANY)
    w_in = [pltpu.VMEM((D_MODEL, D_EXPERT), dt) for dt in (jnp.float32, jnp.bfloat16)]
    w_out = [pltpu.VMEM((D_EXPERT, D_MODEL), dt) for dt in (jnp.float32, jnp.bfloat16)]
    return pl.pallas_call(
        functools.partial(_expert_kernel, layer=layer),
        grid_spec=pltpu.PrefetchScalarGridSpec(
            num_scalar_prefetch=5,
            grid=(N_BLOCKS,),
            in_specs=[hbm, hbm, hbm, hbm],
            out_specs=pl.BlockSpec((ROW_BLOCK * SLAB, LANES), lambda b, *_: (b, 0)),
            scratch_shapes=[
                pltpu.SMEM((N_ROWS,), jnp.int32),
                pltpu.VMEM((GATHER_SLOTS, ROW_BLOCK * SLAB, LANES), jnp.float32),
                pltpu.SemaphoreType.DMA((GATHER_SLOTS,)),
                w_in[0], w_in[0], w_out[0],
                pltpu.SemaphoreType.DMA((3,)),
                w_in[1], w_in[1], w_out[1],
            ],
        ),
        out_shape=jax.ShapeDtypeStruct((N_ROWS * SLAB, LANES), jnp.float32),
        compiler_params=_cparams(1),
        name="moe_experts",
    )(block_e, n_valid, dest_flat, pad_rows, next_expert, h2s, w1, w3, w2)


def _combine_kernel(pos_ref, x1_ref, rf_ref, yb_ref, out_ref, buf_ref, sem):
    i = pl.program_id(0)
    n = pl.num_programs(0)
    tm = COMB_TM

    def copy(tile, slot, j, k):
        p = pos_ref[k * N_TOK + tile * tm + j]
        return pltpu.make_async_copy(
            yb_ref.at[pl.ds(pl.multiple_of(p * SLAB, SLAB), SLAB)],
            buf_ref.at[slot, k, pl.ds(pl.multiple_of(j * SLAB, SLAB), SLAB)],
            sem.at[slot])

    def issue(tile, slot):
        def body(j):
            copy(tile, slot, j, 0).start()
            copy(tile, slot, j, 1).start()
        _unrolled(tm, body)

    def drain(tile, slot):
        def body(j):
            copy(tile, slot, j, 0).wait()
            copy(tile, slot, j, 1).wait()
        _unrolled(tm, body)

    slot = i % 2

    @pl.when(i == 0)
    def _():
        issue(0, 0)

    @pl.when(i + 1 < n)
    def _():
        issue(i + 1, 1 - slot)

    drain(i, slot)
    rf = rf_ref[...]
    y0 = _from_slab(buf_ref.at[slot, 0], tm)
    y1 = _from_slab(buf_ref.at[slot, 1], tm)
    out_ref[...] = x1_ref[...] + rf[:, 0:1] * y0 + rf[:, 1:2] * y1


def _combine(pos_flat, x1, rf, yb):
    return pl.pallas_call(
        _combine_kernel,
        grid_spec=pltpu.PrefetchScalarGridSpec(
            num_scalar_prefetch=1,
            grid=(N_TOK // COMB_TM,),
            in_specs=[
                pl.BlockSpec((COMB_TM, D_MODEL), lambda i, p: (i, 0)),
                pl.BlockSpec((COMB_TM, LANES), lambda i, p: (i, 0)),
                pl.BlockSpec(memory_space=pl.ANY),
            ],
            out_specs=pl.BlockSpec((COMB_TM, D_MODEL), lambda i, p: (i, 0)),
            scratch_shapes=[
                pltpu.VMEM((2, TOP_K, COMB_TM * SLAB, LANES), jnp.float32),
                pltpu.SemaphoreType.DMA((2,)),
            ],
        ),
        out_shape=jax.ShapeDtypeStruct((N_TOK, D_MODEL), jnp.float32),
        compiler_params=_cparams(1),
        name="moe_combine",
    )(pos_flat, x1, rf, yb)


def _combine_qkv_kernel(pos_ref, x1_ref, rf_ref, yb_ref, g1_ref, w_ref, qg_ref, kg_ref,
                        x2_ref, qk_ref, v_ref, buf_ref, sem):
    i = pl.program_id(0)
    n = pl.num_programs(0)
    tm = QKV_TS

    def copy(tile, slot, j, k):
        p = pos_ref[k * N_TOK + tile * tm + j]
        return pltpu.make_async_copy(
            yb_ref.at[pl.ds(pl.multiple_of(p * SLAB, SLAB), SLAB)],
            buf_ref.at[slot, k, pl.ds(j * SLAB, SLAB)],
            sem.at[slot])

    def plane_arrival(slot, k):
        return pltpu.make_async_copy(
            yb_ref.at[pl.ds(0, tm * SLAB)], buf_ref.at[slot, k], sem.at[slot])

    @pl.when(i == 0)
    def _():
        def first(j):
            copy(0, 0, j, 0).start()
            copy(0, 0, j, 1).start()
        _unrolled(tm, first)

    slot = i % 2
    plane_arrival(slot, 0).wait()
    plane_arrival(slot, 1).wait()
    rf = rf_ref[...]
    y0 = _from_slab(buf_ref.at[slot, 0], tm)
    y1 = _from_slab(buf_ref.at[slot, 1], tm)
    x2 = x1_ref[...] + rf[:, 0:1] * y0 + rf[:, 1:2] * y1
    x2_ref[...] = x2
    ahead = jnp.minimum(i + 1, n - 1)
    for j in range(tm):
        copy(ahead, 1 - slot, j, 0).start()
        copy(ahead, 1 - slot, j, 1).start()
    h = _rms(x2, g1_ref[...]).astype(jnp.bfloat16)
    qkv = jnp.dot(h, w_ref[...], preferred_element_type=jnp.float32)
    q_scale = HEAD_DIM ** -0.5 * LOG2E
    for c in range(N_HEADS):
        q_cols = slice(c * V_DIM, (c + 1) * V_DIM)
        k_cols = slice(D_MODEL + c * V_DIM, D_MODEL + (c + 1) * V_DIM)
        qk_ref[:, q_cols] = (_half_norm(qkv[:, q_cols], qg_ref[...]) * q_scale).astype(jnp.bfloat16)
        qk_ref[:, k_cols] = _half_norm(qkv[:, k_cols], kg_ref[...]).astype(jnp.bfloat16)
    v_ref[...] = qkv[:, 2 * D_MODEL:].astype(jnp.bfloat16)

    @pl.when(i == n - 1)
    def _():
        plane_arrival(1 - slot, 0).wait()
        plane_arrival(1 - slot, 1).wait()


def _combine_qkv(pos_flat, x1, rf, yb, g1, w_qkv, qg2, kg2):
    tile = lambda i, p: (i, 0)
    return pl.pallas_call(
        _combine_qkv_kernel,
        grid_spec=pltpu.PrefetchScalarGridSpec(
            num_scalar_prefetch=1,
            grid=(N_TOK // QKV_TS,),
            in_specs=[
                pl.BlockSpec((QKV_TS, D_MODEL), tile),
                pl.BlockSpec((QKV_TS, LANES), tile),
                pl.BlockSpec(memory_space=pl.ANY),
                _full((1, D_MODEL)),
                _full((D_MODEL, 3 * D_MODEL)),
                _full((1, V_DIM)), _full((1, V_DIM)),
            ],
            out_specs=(
                pl.BlockSpec((QKV_TS, D_MODEL), tile),
                pl.BlockSpec((QKV_TS, 2 * D_MODEL), tile),
                pl.BlockSpec((QKV_TS, D_MODEL), tile),
            ),
            scratch_shapes=[
                pltpu.VMEM((2, TOP_K, QKV_TS * SLAB, LANES), jnp.float32),
                pltpu.SemaphoreType.DMA((2,)),
            ],
        ),
        out_shape=(
            jax.ShapeDtypeStruct((N_TOK, D_MODEL), jnp.float32),
            jax.ShapeDtypeStruct((N_TOK, 2 * D_MODEL), jnp.bfloat16),
            jax.ShapeDtypeStruct((N_TOK, D_MODEL), jnp.bfloat16),
        ),
        compiler_params=_cparams(1),
        name="combine_qkv",
    )(pos_flat, x1, rf, yb, g1, w_qkv, qg2, kg2)


def _moe_experts(layer, h2s, rft, cnt, w1, w3, w2):
    experts = rft[2:4].astype(jnp.int32)
    rank = rft[4:6].astype(jnp.int32)
    counts = cnt[0, :N_EXPERTS].astype(jnp.int32)
    padded = (counts + ROW_BLOCK - 1) // ROW_BLOCK * ROW_BLOCK
    pad_ends = jnp.cumsum(padded)
    pad_starts = pad_ends - padded
    ids = jnp.arange(N_EXPERTS, dtype=jnp.int32)
    is_e = experts[None] == ids[:, None, None]
    dest = (jnp.sum(jnp.where(is_e, pad_starts[:, None, None], 0), axis=0) + rank).reshape(-1)
    n_valid = (pad_ends[-1:] // ROW_BLOCK).astype(jnp.int32)
    block_row0 = jnp.arange(N_BLOCKS, dtype=jnp.int32) * ROW_BLOCK
    block_e = jnp.minimum(
        jnp.sum((pad_ends[None, :] <= block_row0[:, None]).astype(jnp.int32), axis=1),
        N_EXPERTS - 1).astype(jnp.int32)
    pad_rows = jnp.concatenate([pad_starts + counts, pad_ends]).astype(jnp.int32)
    later_nonempty = (counts > 0)[None, :] & (ids[None, :] > ids[:, None])
    following = jnp.min(jnp.where(later_nonempty, ids[None, :], N_EXPERTS), axis=1)
    next_expert = jnp.where(following < N_EXPERTS, following, -1).astype(jnp.int32)
    yb = _experts(layer, block_e, n_valid, dest, pad_rows, next_expert, h2s, w1, w3, w2)
    return dest, yb


def _router_params(w_grp, b_grp, w_exp, b_exp):
    wr = jnp.zeros((D_MODEL, LANES), jnp.float32)
    wr = wr.at[:, :N_EXPERTS].set(w_exp).at[:, GRP_LANE0:GRP_LANE0 + N_GROUPS].set(w_grp)
    br = jnp.zeros((1, LANES), jnp.float32)
    br = br.at[0, :N_EXPERTS].set(b_exp).at[0, GRP_LANE0:GRP_LANE0 + N_GROUPS].set(b_grp)
    w_hi = wr.astype(jnp.bfloat16)
    w_lo = (wr - w_hi.astype(jnp.float32)).astype(jnp.bfloat16)
    return jnp.concatenate([w_hi, w_lo], axis=1), br


def kernel(x, norm1_g, norm2_g, conv_w_in, conv_b_in, conv_dw, conv_dw_b, conv_ln_g, conv_ln_b, conv_w_out, conv_b_out, attn_w_qkv, attn_q_g, attn_k_g, attn_lq1, attn_lk1, attn_lq2, attn_lk2, attn_subln_g, attn_w_o, moe_w_grp, moe_b_grp, moe_w_exp, moe_b_exp, moe_w1, moe_w3, moe_w2):
    assert x.shape == (BATCH, SEQ, D_MODEL) and x.dtype == jnp.float32
    assert moe_w1.shape == (2, N_EXPERTS, D_MODEL, D_EXPERT) and conv_dw.shape[1] == CONV_WIDTH
    bf16 = jnp.bfloat16
    row = lambda a: a.reshape(1, -1)
    xf = x.reshape(N_TOK, D_MODEL)

    wr, br = _router_params(moe_w_grp[0], moe_b_grp[0], moe_w_exp[0], moe_b_exp[0])
    dw = jnp.zeros((HALO, D_MODEL), jnp.float32).at[:CONV_WIDTH].set(conv_dw[0])
    x1, h2s, rf, rft, cnt = _conv_layer(
        xf, row(norm1_g[0]), conv_w_in[0].astype(bf16), row(conv_b_in[0]), dw,
        row(conv_dw_b[0]), row(conv_ln_g[0]), row(conv_ln_b[0]),
        conv_w_out[0].astype(bf16), row(conv_b_out[0]), row(norm2_g[0]), wr, br)
    dest, yb = _moe_experts(0, h2s, rft, cnt, moe_w1, moe_w3, moe_w2)

    lambda_init = 0.8 - 0.6 * math.exp(-0.3 * 1)
    wr, br = _router_params(moe_w_grp[1], moe_b_grp[1], moe_w_exp[1], moe_b_exp[1])
    two = lambda g: jnp.concatenate([g, g]).reshape(1, V_DIM)
    xf, qk, v = _combine_qkv(dest, x1, rf, yb, row(norm1_g[1]), attn_w_qkv[0].astype(bf16),
                             two(attn_q_g[0]), two(attn_k_g[0]))
    lam_rows = jnp.zeros((SUBLANES, HEAD_DIM), jnp.float32)
    lam_rows = lam_rows.at[0].set(attn_lq1[0]).at[1].set(attn_lk1[0])
    lam_rows = lam_rows.at[2].set(attn_lq2[0]).at[3].set(attn_lk2[0])
    o = _attention(qk, v, lam_rows, row(attn_subln_g[0]), lambda_init)
    x1, h2s, rf, rft, cnt = _attn_post(xf, o, attn_w_o[0].astype(bf16), row(norm2_g[1]), wr, br)
    dest, yb = _moe_experts(1, h2s, rft, cnt, moe_w1, moe_w3, moe_w2)
    xf = _combine(dest, x1, rf, yb)
    return xf.reshape(BATCH, SEQ, D_MODEL)
```

```python
import functools
import math

import jax
import jax.numpy as jnp
from jax import lax
from jax.experimental import pallas as pl
from jax.experimental.pallas import tpu as pltpu

D_MODEL = 1024
BATCH = 8
SEQ = 2048
N_TOK = BATCH * SEQ
CHUNK = 64
CONV_WIDTH = 31
N_HEADS = 8
HEAD_DIM = 64
V_DIM = 128
N_GROUPS = 4
EXPERTS_PER_GROUP = 8
N_EXPERTS = 32
TOP_K = 2
D_EXPERT = 512
EPS = 1e-6
LOG2E = math.log2(math.e)

LANES = 128
SUBLANES = 8
SLAB = D_MODEL // LANES
VMEM_LIMIT = 56 * 1024 * 1024

CONV_TS = 512
HALO = 32
CONV_RC = 64
POST_TS = 512
QKV_TS = 512
ATT_TQ = 256
ROW_BLOCK = 256
GATHER_SLOTS = 3
N_ASSIGN = N_TOK * TOP_K
N_BLOCKS = N_ASSIGN // ROW_BLOCK + N_EXPERTS
N_ROWS = N_BLOCKS * ROW_BLOCK
COMB_TM = 256
DMA_UNROLL = 16
GRP_LANE0 = N_EXPERTS
ROUTE_ROWS = 40


def _cparams(n_axes):
    return pltpu.CompilerParams(
        dimension_semantics=("arbitrary",) * n_axes, vmem_limit_bytes=VMEM_LIMIT)


def _rms(x, g):
    return x * lax.rsqrt(jnp.mean(x * x, axis=-1, keepdims=True) + EPS) * g


def _to_slab(ref, val, rows):
    for j in range(SLAB):
        ref[pl.ds(j, rows, stride=SLAB), :] = val[:, j * LANES:(j + 1) * LANES]


def _from_slab(ref, rows):
    return jnp.concatenate(
        [ref[pl.ds(j, rows, stride=SLAB), :] for j in range(SLAB)], axis=-1)


def _residual_norm_route(x1, g2_ref, wr_ref, br_ref, run_ref,
                         x1_ref, h2s_ref, rf_ref, rft_ref, cnt_ref, rows):
    x1_ref[...] = x1
    h2 = _rms(x1, g2_ref[...])
    _to_slab(h2s_ref, h2, rows)

    hi = h2.astype(jnp.bfloat16)
    lo = (h2 - hi.astype(jnp.float32)).astype(jnp.bfloat16)
    ab = jnp.dot(hi, wr_ref[...], preferred_element_type=jnp.float32)
    c = jnp.dot(lo, wr_ref[:, 0:LANES], preferred_element_type=jnp.float32)
    logits = ab[:, 0:LANES] + ab[:, LANES:2 * LANES] + c + br_ref[...]
    lt = jnp.transpose(logits)[0:ROUTE_ROWS, :]
    sub = lax.broadcasted_iota(jnp.int32, (ROUTE_ROWS, rows), 0).astype(jnp.float32)
    neg = jnp.float32(-jnp.inf)
    big = jnp.float32(1e9)

    gmask = (sub >= GRP_LANE0) & (sub < GRP_LANE0 + N_GROUPS)
    gl = jnp.where(gmask, lt, neg)
    gmax = jnp.max(gl, axis=0, keepdims=True)
    gidx = jnp.min(jnp.where(gl == gmax, sub, big), axis=0, keepdims=True) - GRP_LANE0
    gsum = jnp.sum(jnp.where(gmask, jnp.exp(gl - gmax), 0.0), axis=0, keepdims=True)
    grp_p = 1.0 / gsum

    first_e = gidx * EXPERTS_PER_GROUP
    emask = (sub >= first_e) & (sub < first_e + EXPERTS_PER_GROUP)
    el = jnp.where(emask, lt, neg)
    m1 = jnp.max(el, axis=0, keepdims=True)
    i1 = jnp.min(jnp.where(el == m1, sub, big), axis=0, keepdims=True)
    el2 = jnp.where(sub == i1, neg, el)
    m2 = jnp.max(el2, axis=0, keepdims=True)
    i2 = jnp.min(jnp.where(el2 == m2, sub, big), axis=0, keepdims=True)
    t = jnp.exp(m2 - m1)
    inv = 1.0 / (1.0 + t)
    g_first = grp_p * inv
    g_second = grp_p * t * inv

    sel1 = sub == i1
    sel2 = sub == i2
    member = jnp.where(sel1 | sel2, 1.0, 0.0).astype(jnp.bfloat16)
    c_i = lax.broadcasted_iota(jnp.int32, (rows, rows), 0)
    r_i = lax.broadcasted_iota(jnp.int32, (rows, rows), 1)
    earlier = jnp.where(c_i < r_i, 1.0, 0.0).astype(jnp.bfloat16)
    run = run_ref[...]
    cum = (jnp.dot(member, earlier, preferred_element_type=jnp.float32)
           + jnp.concatenate([run] * (rows // LANES), axis=1))
    rank1 = jnp.sum(jnp.where(sel1, cum, 0.0), axis=0, keepdims=True)
    rank2 = jnp.sum(jnp.where(sel2, cum, 0.0), axis=0, keepdims=True)
    run_new = run + jnp.sum(member.astype(jnp.float32), axis=1, keepdims=True)
    run_ref[...] = run_new
    cnt_ref[...] = run_new

    sub8 = lax.broadcasted_iota(jnp.int32, (SUBLANES, rows), 0)
    out = jnp.where(sub8 == 0, g_first, 0.0)
    out = jnp.where(sub8 == 1, g_second, out)
    out = jnp.where(sub8 == 2, i1, out)
    out = jnp.where(sub8 == 3, i2, out)
    out = jnp.where(sub8 == 4, rank1, out)
    out = jnp.where(sub8 == 5, rank2, out)
    rft_ref[...] = out
    padded = jnp.concatenate([out, jnp.zeros((LANES - SUBLANES, rows), jnp.float32)], axis=0)
    rf_ref[...] = jnp.transpose(padded)


def _route_out_shapes():
    return (
        jax.ShapeDtypeStruct((N_TOK, D_MODEL), jnp.float32),
        jax.ShapeDtypeStruct((N_TOK * SLAB, LANES), jnp.float32),
        jax.ShapeDtypeStruct((N_TOK, LANES), jnp.float32),
        jax.ShapeDtypeStruct((SUBLANES, N_TOK), jnp.float32),
        jax.ShapeDtypeStruct((ROUTE_ROWS, LANES), jnp.float32),
    )


def _route_out_specs(ts, idx):
    return (
        pl.BlockSpec((ts, D_MODEL), lambda *a: (idx(*a), 0)),
        pl.BlockSpec((ts * SLAB, LANES), lambda *a: (idx(*a), 0)),
        pl.BlockSpec((ts, LANES), lambda *a: (idx(*a), 0)),
        pl.BlockSpec((SUBLANES, ts), lambda *a: (0, idx(*a))),
        pl.BlockSpec((ROUTE_ROWS, LANES), lambda *a: (0, 0)),
    )


def _full(shape):
    return pl.BlockSpec(shape, lambda *a: (0,) * len(shape))


def _conv_kernel(x_ref, g1_ref, win_ref, bin_ref, dw_ref, dwb_ref, lng_ref, lnb_ref,
                 wout_ref, bout_ref, g2_ref, wr_ref, br_ref,
                 x1_ref, h2s_ref, rf_ref, rft_ref, cnt_ref,
                 ext_ref, conv_ref, run_ref):
    b = pl.program_id(0)
    s = pl.program_id(1)
    ts = CONV_TS

    @pl.when((b == 0) & (s == 0))
    def _():
        run_ref[...] = jnp.zeros_like(run_ref)

    @pl.when(s == 0)
    def _():
        ext_ref[0:HALO, :] = jnp.zeros((HALO, D_MODEL), jnp.float32)

    x = x_ref[...]
    h = _rms(x, g1_ref[...]).astype(jnp.bfloat16)
    u = jnp.dot(h, win_ref[...], preferred_element_type=jnp.float32) + bin_ref[...]
    glu = u[:, :D_MODEL] * jax.nn.sigmoid(u[:, D_MODEL:])
    ext_ref[HALO:HALO + ts, :] = glu

    base = HALO - (CONV_WIDTH - 1)

    def lane_chunk(c, carry):
        cols = pl.ds(pl.multiple_of(c * LANES, LANES), LANES)
        taps = [dw_ref[pl.ds(k, 1), cols] for k in range(CONV_WIDTH)]
        for r0 in range(0, ts, CONV_RC):
            acc = None
            for r in range(SUBLANES):
                rows = CONV_RC + (SUBLANES if r else 0)
                part = None
                for q in range((base + CONV_WIDTH - 1) // SUBLANES + 1):
                    k = SUBLANES * q + r - base
                    if 0 <= k < CONV_WIDTH:
                        term = ext_ref[pl.ds(r0 + SUBLANES * q, rows), cols] * taps[k]
                        part = term if part is None else part + term
                part = part[r:r + CONV_RC, :] if r else part
                acc = part if acc is None else acc + part
            conv_ref[pl.ds(r0, CONV_RC), cols] = acc
        return carry

    lax.fori_loop(0, D_MODEL // LANES, lane_chunk, 0)
    ext_ref[0:HALO, :] = ext_ref[ts:ts + HALO, :]

    v = conv_ref[...] + dwb_ref[...]
    mu = jnp.mean(v, axis=-1, keepdims=True)
    vc = v - mu
    var = jnp.mean(vc * vc, axis=-1, keepdims=True)
    y = vc * lax.rsqrt(var + EPS) * lng_ref[...] + lnb_ref[...]
    y = (y * jax.nn.sigmoid(y)).astype(jnp.bfloat16)
    mix = jnp.dot(y, wout_ref[...], preferred_element_type=jnp.float32) + bout_ref[...]
    _residual_norm_route(x + mix, g2_ref, wr_ref, br_ref, run_ref,
                         x1_ref, h2s_ref, rf_ref, rft_ref, cnt_ref, ts)


def _conv_layer(x, g1, w_in, b_in, dw, dw_b, ln_g, ln_b, w_out, b_out, g2, wr, br):
    ns = SEQ // CONV_TS
    tile = lambda b, s: b * ns + s
    return pl.pallas_call(
        _conv_kernel,
        grid=(BATCH, ns),
        in_specs=[
            pl.BlockSpec((CONV_TS, D_MODEL), lambda b, s: (tile(b, s), 0)),
            _full((1, D_MODEL)),
            _full((D_MODEL, 2 * D_MODEL)),
            _full((1, 2 * D_MODEL)),
            _full((HALO, D_MODEL)),
            _full((1, D_MODEL)), _full((1, D_MODEL)), _full((1, D_MODEL)),
            _full((D_MODEL, D_MODEL)),
            _full((1, D_MODEL)), _full((1, D_MODEL)),
            _full((D_MODEL, 2 * LANES)), _full((1, LANES)),
        ],
        out_specs=_route_out_specs(CONV_TS, tile),
        out_shape=_route_out_shapes(),
        scratch_shapes=[
            pltpu.VMEM((HALO + CONV_TS, D_MODEL), jnp.float32),
            pltpu.VMEM((CONV_TS, D_MODEL), jnp.float32),
            pltpu.VMEM((ROUTE_ROWS, LANES), jnp.float32),
        ],
        compiler_params=_cparams(2),
        name="conv_mixer",
    )(x, g1, w_in, b_in, dw, dw_b, ln_g, ln_b, w_out, b_out, g2, wr, br)


def _half_norm(z, gain):
    lane = lax.broadcasted_iota(jnp.int32, z.shape, 1)
    first = lane < HEAD_DIM
    zz = z * z
    ss_a = jnp.sum(jnp.where(first, zz, 0.0), axis=1, keepdims=True)
    ss_b = jnp.sum(jnp.where(first, 0.0, zz), axis=1, keepdims=True)
    inv = jnp.where(first, lax.rsqrt(ss_a * (1.0 / HEAD_DIM) + EPS),
                    lax.rsqrt(ss_b * (1.0 / HEAD_DIM) + EPS))
    return z * inv * gain


def _attn_kernel(q_ref, k_ref, v_ref, lam_ref, sg_ref, o_ref,
                 qa_ref, qb_ref, v1_ref, s_ref, e_ref, part_ref, *, lambda_init):
    lp = lam_ref[...]
    lam = (jnp.exp(jnp.sum(lp[0:1, :] * lp[1:2, :], axis=1, keepdims=True))
           - jnp.exp(jnp.sum(lp[2:3, :] * lp[3:4, :], axis=1, keepdims=True))
           + lambda_init)

    q = q_ref[...]
    first = lax.broadcasted_iota(jnp.int32, (SEQ, V_DIM), 1) < HEAD_DIM
    qa_ref[...] = jnp.where(first, q, jnp.zeros_like(q))
    qb_ref[...] = jnp.where(first, jnp.zeros_like(q), q)

    ones_col = lax.broadcasted_iota(jnp.int32, (SEQ, V_DIM), 1) == 0
    v1_ref[:, 0:V_DIM] = v_ref[...]
    v1_ref[:, V_DIM:2 * V_DIM] = jnp.where(ones_col, 1.0, 0.0).astype(jnp.bfloat16)

    tq = ATT_TQ
    nt = (((1,), (1,)), ((), ()))
    visible = (lax.broadcasted_iota(jnp.int32, (tq, tq), 1) // CHUNK
               <= lax.broadcasted_iota(jnp.int32, (tq, tq), 0) // CHUNK)
    halves = (qa_ref, qb_ref)
    units = [(qi, h) for qi in range(SEQ // tq) for h in range(2)]

    def stage_scores(u, slot):
        qi, h = units[u]
        k0 = qi * tq
        q = halves[h][k0:k0 + tq, :]
        dg = lax.dot_general(q, k_ref[k0:k0 + tq, :], nt, preferred_element_type=jnp.float32)
        s_ref[slot, :, k0:k0 + tq] = jnp.where(visible, dg, -jnp.inf)
        if qi:
            s_ref[slot, :, 0:k0] = lax.dot_general(q, k_ref[0:k0, :], nt,
                                                   preferred_element_type=jnp.float32)

    def stage_numerators(u, slot):
        kend = (units[u][0] + 1) * tq
        sc = s_ref[slot, :, 0:kend]
        m = jnp.max(sc, axis=1, keepdims=True)
        e_ref[slot, :, 0:kend] = jnp.exp2(sc - m).astype(jnp.bfloat16)

    def stage_values(u, slot):
        qi, h = units[u]
        k0 = qi * tq
        kend = k0 + tq
        acc = jnp.dot(e_ref[slot, :, 0:kend], v1_ref[0:kend, :],
                      preferred_element_type=jnp.float32)
        attn = acc[:, 0:V_DIM] * (1.0 / acc[:, V_DIM:V_DIM + 1])
        if h == 0:
            part_ref[...] = attn
        else:
            o = part_ref[...] - lam * attn
            o = _rms(o, sg_ref[...]) * (1.0 - lambda_init)
            o_ref[k0:kend, :] = o.astype(jnp.bfloat16)

    for step in range(len(units) + 2):
        if step < len(units):
            stage_scores(step, step % 2)
        if 1 <= step <= len(units):
            stage_numerators(step - 1, (step - 1) % 2)
        if step >= 2:
            stage_values(step - 2, step % 2)


def _attention(qk, v, lam_rows, subln_g, lambda_init):
    return pl.pallas_call(
        functools.partial(_attn_kernel, lambda_init=lambda_init),
        grid=(BATCH, N_HEADS),
        in_specs=[
            pl.BlockSpec((SEQ, V_DIM), lambda b, h: (b, h)),
            pl.BlockSpec((SEQ, V_DIM), lambda b, h: (b, N_HEADS + h)),
            pl.BlockSpec((SEQ, V_DIM), lambda b, h: (b, h)),
            _full((SUBLANES, HEAD_DIM)),
            _full((1, V_DIM)),
        ],
        out_specs=pl.BlockSpec((SEQ, V_DIM), lambda b, h: (b, h)),
        out_shape=jax.ShapeDtypeStruct((N_TOK, D_MODEL), jnp.bfloat16),
        scratch_shapes=[pltpu.VMEM((SEQ, V_DIM), jnp.bfloat16)] * 2 + [
            pltpu.VMEM((SEQ, 2 * V_DIM), jnp.bfloat16),
            pltpu.VMEM((2, ATT_TQ, SEQ), jnp.float32),
            pltpu.VMEM((2, ATT_TQ, SEQ), jnp.bfloat16),
            pltpu.VMEM((ATT_TQ, V_DIM), jnp.float32),
        ],
        compiler_params=_cparams(2),
        name="diff_attention",
    )(qk, qk, v, lam_rows, subln_g)


def _post_kernel(x_ref, m_ref, w_ref, g2_ref, wr_ref, br_ref,
                 x1_ref, h2s_ref, rf_ref, rft_ref, cnt_ref, run_ref):
    @pl.when(pl.program_id(0) == 0)
    def _():
        run_ref[...] = jnp.zeros_like(run_ref)

    mix = jnp.dot(m_ref[...], w_ref[...], preferred_element_type=jnp.float32)
    _residual_norm_route(x_ref[...] + mix, g2_ref, wr_ref, br_ref, run_ref,
                         x1_ref, h2s_ref, rf_ref, rft_ref, cnt_ref, POST_TS)


def _attn_post(x, o, w_o, g2, wr, br):
    return pl.pallas_call(
        _post_kernel,
        grid=(N_TOK // POST_TS,),
        in_specs=[
            pl.BlockSpec((POST_TS, D_MODEL), lambda i: (i, 0)),
            pl.BlockSpec((POST_TS, D_MODEL), lambda i: (i, 0)),
            _full((D_MODEL, D_MODEL)),
            _full((1, D_MODEL)),
            _full((D_MODEL, 2 * LANES)), _full((1, LANES)),
        ],
        out_specs=_route_out_specs(POST_TS, lambda i: i),
        out_shape=_route_out_shapes(),
        scratch_shapes=[pltpu.VMEM((ROUTE_ROWS, LANES), jnp.float32)],
        compiler_params=_cparams(1),
        name="attn_post",
    )(x, o, w_o, g2, wr, br)


def _unrolled(n, body):
    def group(g, carry):
        for u in range(DMA_UNROLL):
            body(g * DMA_UNROLL + u)
        return carry
    lax.fori_loop(0, n // DMA_UNROLL, group, 0)


def _expert_kernel(be_ref, nv_ref, dest_ref, pad_ref, next_ref,
                   h2s_ref, w1_hbm, w3_hbm, w2_hbm, yb_ref,
                   rowtok_ref, xbuf_ref, sem, w1f_ref, w3f_ref, w2f_ref, wsem,
                   w1b_ref, w3b_ref, w2b_ref, *, layer):
    b = pl.program_id(0)
    n_valid = nv_ref[0]
    w_hbm = (w1_hbm, w3_hbm, w2_hbm)
    w_f32 = (w1f_ref, w3f_ref, w2f_ref)

    def weight_copies(expert):
        return [pltpu.make_async_copy(w_hbm[i].at[layer, expert], w_f32[i], wsem.at[i])
                for i in range(3)]

    def row_copy(block, slot, r):
        tok = rowtok_ref[block * ROW_BLOCK + r]
        return pltpu.make_async_copy(
            h2s_ref.at[pl.ds(pl.multiple_of(tok * SLAB, SLAB), SLAB)],
            xbuf_ref.at[slot, pl.ds(r * SLAB, SLAB)],
            sem.at[slot])

    def block_arrival(slot):
        return pltpu.make_async_copy(
            h2s_ref.at[pl.ds(0, ROW_BLOCK * SLAB)], xbuf_ref.at[slot], sem.at[slot])

    @pl.when(b == 0)
    def _():
        for c in weight_copies(be_ref[0]):
            c.start()
        def clear_padding(e, carry):
            def clear(j, c):
                rowtok_ref[j] = 0
                return c
            return lax.fori_loop(pad_ref[e], pad_ref[N_EXPERTS + e], clear, carry)
        lax.fori_loop(0, N_EXPERTS, clear_padding, 0)

        def place(t):
            rowtok_ref[dest_ref[t]] = t
            rowtok_ref[dest_ref[N_TOK + t]] = t
        _unrolled(N_TOK, place)
        second = jnp.minimum(1, n_valid - 1)
        _unrolled(ROW_BLOCK, lambda r: row_copy(0, 0, r).start())
        _unrolled(ROW_BLOCK, lambda r: row_copy(second, 1, r).start())

    expert = be_ref[b]
    fresh = (b == 0) | (expert != be_ref[jnp.maximum(b - 1, 0)])

    @pl.when(fresh & (b < n_valid))
    def _():
        for c in weight_copies(expert):
            c.wait()
        w1b_ref[...] = w1f_ref[...].astype(jnp.bfloat16)
        w3b_ref[...] = w3f_ref[...].astype(jnp.bfloat16)
        w2b_ref[...] = w2f_ref[...].astype(jnp.bfloat16)
        following = next_ref[expert]

        @pl.when(following >= 0)
        def _():
            for c in weight_copies(following):
                c.start()

    @pl.when(b < n_valid)
    def _():
        slot = lax.rem(b, GATHER_SLOTS)
        block_arrival(slot).wait()
        x = _from_slab(xbuf_ref.at[slot], ROW_BLOCK).astype(jnp.bfloat16)
        ahead = jnp.minimum(b + 2, n_valid - 1)
        ahead_slot = lax.rem(b + 2, GATHER_SLOTS)
        for r in range(ROW_BLOCK):
            row_copy(ahead, ahead_slot, r).start()
        h1 = jnp.dot(x, w1b_ref[...], preferred_element_type=jnp.float32)
        h3 = jnp.dot(x, w3b_ref[...], preferred_element_type=jnp.float32)
        act = (h1 * jax.nn.sigmoid(h1) * h3).astype(jnp.bfloat16)
        y = jnp.dot(act, w2b_ref[...], preferred_element_type=jnp.float32)
        _to_slab(yb_ref, y, ROW_BLOCK)

    @pl.when(b == n_valid - 1)
    def _():
        block_arrival(lax.rem(b + 1, GATHER_SLOTS)).wait()
        block_arrival(lax.rem(b + 2, GATHER_SLOTS)).wait()

    @pl.when(b >= n_valid)
    def _():
        yb_ref[...] = jnp.zeros_like(yb_ref)


def _experts(layer, block_e, n_valid, dest_flat, pad_rows, next_expert, h2s, w1, w3, w2):
    hbm = pl.BlockSpec(memory_space=pl.ANY)
    w_in = [pltpu.VMEM((D_MODEL, D_EXPERT), dt) for dt in (jnp.float32, jnp.bfloat16)]
    w_out = [pltpu.VMEM((D_EXPERT, D_MODEL), dt) for dt in (jnp.float32, jnp.bfloat16)]
    return pl.pallas_call(
        functools.partial(_expert_kernel, layer=layer),
        grid_spec=pltpu.PrefetchScalarGridSpec(
            num_scalar_prefetch=5,
            grid=(N_BLOCKS,),
            in_specs=[hbm, hbm, hbm, hbm],
            out_specs=pl.BlockSpec((ROW_BLOCK * SLAB, LANES), lambda b, *_: (b, 0)),
            scratch_shapes=[
                pltpu.SMEM((N_ROWS,), jnp.int32),
                pltpu.VMEM((GATHER_SLOTS, ROW_BLOCK * SLAB, LANES), jnp.float32),
                pltpu.SemaphoreType.DMA((GATHER_SLOTS,)),
                w_in[0], w_in[0], w_out[0],
                pltpu.SemaphoreType.DMA((3,)),
                w_in[1], w_in[1], w_out[1],
            ],
        ),
        out_shape=jax.ShapeDtypeStruct((N_ROWS * SLAB, LANES), jnp.float32),
        compiler_params=_cparams(1),
        name="moe_experts",
    )(block_e, n_valid, dest_flat, pad_rows, next_expert, h2s, w1, w3, w2)


def _combine_kernel(pos_ref, x1_ref, rf_ref, yb_ref, out_ref, buf_ref, sem):
    i = pl.program_id(0)
    n = pl.num_programs(0)
    tm = COMB_TM

    def copy(tile, slot, j, k):
        p = pos_ref[k * N_TOK + tile * tm + j]
        return pltpu.make_async_copy(
            yb_ref.at[pl.ds(pl.multiple_of(p * SLAB, SLAB), SLAB)],
            buf_ref.at[slot, k, pl.ds(pl.multiple_of(j * SLAB, SLAB), SLAB)],
            sem.at[slot])

    def issue(tile, slot):
        def body(j):
            copy(tile, slot, j, 0).start()
            copy(tile, slot, j, 1).start()
        _unrolled(tm, body)

    def drain(tile, slot):
        def body(j):
            copy(tile, slot, j, 0).wait()
            copy(tile, slot, j, 1).wait()
        _unrolled(tm, body)

    slot = i % 2

    @pl.when(i == 0)
    def _():
        issue(0, 0)

    @pl.when(i + 1 < n)
    def _():
        issue(i + 1, 1 - slot)

    drain(i, slot)
    rf = rf_ref[...]
    y0 = _from_slab(buf_ref.at[slot, 0], tm)
    y1 = _from_slab(buf_ref.at[slot, 1], tm)
    out_ref[...] = x1_ref[...] + rf[:, 0:1] * y0 + rf[:, 1:2] * y1


def _combine(pos_flat, x1, rf, yb):
    return pl.pallas_call(
        _combine_kernel,
        grid_spec=pltpu.PrefetchScalarGridSpec(
            num_scalar_prefetch=1,
            grid=(N_TOK // COMB_TM,),
            in_specs=[
                pl.BlockSpec((COMB_TM, D_MODEL), lambda i, p: (i, 0)),
                pl.BlockSpec((COMB_TM, LANES), lambda i, p: (i, 0)),
                pl.BlockSpec(memory_space=pl.ANY),
            ],
            out_specs=pl.BlockSpec((COMB_TM, D_MODEL), lambda i, p: (i, 0)),
            scratch_shapes=[
                pltpu.VMEM((2, TOP_K, COMB_TM * SLAB, LANES), jnp.float32),
                pltpu.SemaphoreType.DMA((2,)),
            ],
        ),
        out_shape=jax.ShapeDtypeStruct((N_TOK, D_MODEL), jnp.float32),
        compiler_params=_cparams(1),
        name="moe_combine",
    )(pos_flat, x1, rf, yb)


def _combine_qkv_kernel(pos_ref, x1_ref, rf_ref, yb_ref, g1_ref, w_ref, qg_ref, kg_ref,
                        x2_ref, qk_ref, v_ref, buf_ref, sem):
    i = pl.program_id(0)
    n = pl.num_programs(0)
    tm = QKV_TS

    def copy(tile, slot, j, k):
        p = pos_ref[k * N_TOK + tile * tm + j]
        return pltpu.make_async_copy(
            yb_ref.at[pl.ds(pl.multiple_of(p * SLAB, SLAB), SLAB)],
            buf_ref.at[slot, k, pl.ds(j * SLAB, SLAB)],
            sem.at[slot])

    def plane_arrival(slot, k):
        return pltpu.make_async_copy(
            yb_ref.at[pl.ds(0, tm * SLAB)], buf_ref.at[slot, k], sem.at[slot])

    @pl.when(i == 0)
    def _():
        def first(j):
            copy(0, 0, j, 0).start()
            copy(0, 0, j, 1).start()
        _unrolled(tm, first)

    slot = i % 2
    plane_arrival(slot, 0).wait()
    plane_arrival(slot, 1).wait()
    rf = rf_ref[...]
    y0 = _from_slab(buf_ref.at[slot, 0], tm)
    y1 = _from_slab(buf_ref.at[slot, 1], tm)
    x2 = x1_ref[...] + rf[:, 0:1] * y0 + rf[:, 1:2] * y1
    x2_ref[...] = x2
    ahead = jnp.minimum(i + 1, n - 1)
    for j in range(tm):
        copy(ahead, 1 - slot, j, 0).start()
        copy(ahead, 1 - slot, j, 1).start()
    h = _rms(x2, g1_ref[...]).astype(jnp.bfloat16)
    qkv = jnp.dot(h, w_ref[...], preferred_element_type=jnp.float32)
    q_scale = HEAD_DIM ** -0.5 * LOG2E
    for c in range(N_HEADS):
        q_cols = slice(c * V_DIM, (c + 1) * V_DIM)
        k_cols = slice(D_MODEL + c * V_DIM, D_MODEL + (c + 1) * V_DIM)
        qk_ref[:, q_cols] = (_half_norm(qkv[:, q_cols], qg_ref[...]) * q_scale).astype(jnp.bfloat16)
        qk_ref[:, k_cols] = _half_norm(qkv[:, k_cols], kg_ref[...]).astype(jnp.bfloat16)
    v_ref[...] = qkv[:, 2 * D_MODEL:].astype(jnp.bfloat16)

    @pl.when(i == n - 1)
    def _():
        plane_arrival(1 - slot, 0).wait()
        plane_arrival(1 - slot, 1).wait()


def _combine_qkv(pos_flat, x1, rf, yb, g1, w_qkv, qg2, kg2):
    tile = lambda i, p: (i, 0)
    return pl.pallas_call(
        _combine_qkv_kernel,
        grid_spec=pltpu.PrefetchScalarGridSpec(
            num_scalar_prefetch=1,
            grid=(N_TOK // QKV_TS,),
            in_specs=[
                pl.BlockSpec((QKV_TS, D_MODEL), tile),
                pl.BlockSpec((QKV_TS, LANES), tile),
                pl.BlockSpec(memory_space=pl.ANY),
                _full((1, D_MODEL)),
                _full((D_MODEL, 3 * D_MODEL)),
                _full((1, V_DIM)), _full((1, V_DIM)),
            ],
            out_specs=(
                pl.BlockSpec((QKV_TS, D_MODEL), tile),
                pl.BlockSpec((QKV_TS, 2 * D_MODEL), tile),
                pl.BlockSpec((QKV_TS, D_MODEL), tile),
            ),
            scratch_shapes=[
                pltpu.VMEM((2, TOP_K, QKV_TS * SLAB, LANES), jnp.float32),
                pltpu.SemaphoreType.DMA((2,)),
            ],
        ),
        out_shape=(
            jax.ShapeDtypeStruct((N_TOK, D_MODEL), jnp.float32),
            jax.ShapeDtypeStruct((N_TOK, 2 * D_MODEL), jnp.bfloat16),
            jax.ShapeDtypeStruct((N_TOK, D_MODEL), jnp.bfloat16),
        ),
        compiler_params=_cparams(1),
        name="combine_qkv",
    )(pos_flat, x1, rf, yb, g1, w_qkv, qg2, kg2)


def _moe_experts(layer, h2s, rft, cnt, w1, w3, w2):
    experts = rft[2:4].astype(jnp.int32)
    rank = rft[4:6].astype(jnp.int32)
    counts = cnt[:N_EXPERTS, 0].astype(jnp.int32)
    padded = (counts + ROW_BLOCK - 1) // ROW_BLOCK * ROW_BLOCK
    pad_ends = jnp.cumsum(padded)
    pad_starts = pad_ends - padded
    ids = jnp.arange(N_EXPERTS, dtype=jnp.int32)
    is_e = experts[None] == ids[:, None, None]
    dest = (jnp.sum(jnp.where(is_e, pad_starts[:, None, None], 0), axis=0) + rank).reshape(-1)
    n_valid = (pad_ends[-1:] // ROW_BLOCK).astype(jnp.int32)
    block_row0 = jnp.arange(N_BLOCKS, dtype=jnp.int32) * ROW_BLOCK
    block_e = jnp.minimum(
        jnp.sum((pad_ends[None, :] <= block_row0[:, None]).astype(jnp.int32), axis=1),
        N_EXPERTS - 1).astype(jnp.int32)
    pad_rows = jnp.concatenate([pad_starts + counts, pad_ends]).astype(jnp.int32)
    later_nonempty = (counts > 0)[None, :] & (ids[None, :] > ids[:, None])
    following = jnp.min(jnp.where(later_nonempty, ids[None, :], N_EXPERTS), axis=1)
    next_expert = jnp.where(following < N_EXPERTS, following, -1).astype(jnp.int32)
    yb = _experts(layer, block_e, n_valid, dest, pad_rows, next_expert, h2s, w1, w3, w2)
    return dest, yb


def _router_params(w_grp, b_grp, w_exp, b_exp):
    wr = jnp.zeros((D_MODEL, LANES), jnp.float32)
    wr = wr.at[:, :N_EXPERTS].set(w_exp).at[:, GRP_LANE0:GRP_LANE0 + N_GROUPS].set(w_grp)
    br = jnp.zeros((1, LANES), jnp.float32)
    br = br.at[0, :N_EXPERTS].set(b_exp).at[0, GRP_LANE0:GRP_LANE0 + N_GROUPS].set(b_grp)
    w_hi = wr.astype(jnp.bfloat16)
    w_lo = (wr - w_hi.astype(jnp.float32)).astype(jnp.bfloat16)
    return jnp.concatenate([w_hi, w_lo], axis=1), br


def kernel(x, norm1_g, norm2_g, conv_w_in, conv_b_in, conv_dw, conv_dw_b, conv_ln_g, conv_ln_b, conv_w_out, conv_b_out, attn_w_qkv, attn_q_g, attn_k_g, attn_lq1, attn_lk1, attn_lq2, attn_lk2, attn_subln_g, attn_w_o, moe_w_grp, moe_b_grp, moe_w_exp, moe_b_exp, moe_w1, moe_w3, moe_w2):
    assert x.shape == (BATCH, SEQ, D_MODEL) and x.dtype == jnp.float32
    assert moe_w1.shape == (2, N_EXPERTS, D_MODEL, D_EXPERT) and conv_dw.shape[1] == CONV_WIDTH
    bf16 = jnp.bfloat16
    row = lambda a: a.reshape(1, -1)
    xf = x.reshape(N_TOK, D_MODEL)

    wr, br = _router_params(moe_w_grp[0], moe_b_grp[0], moe_w_exp[0], moe_b_exp[0])
    dw = jnp.zeros((HALO, D_MODEL), jnp.float32).at[:CONV_WIDTH].set(conv_dw[0])
    x1, h2s, rf, rft, cnt = _conv_layer(
        xf, row(norm1_g[0]), conv_w_in[0].astype(bf16), row(conv_b_in[0]), dw,
        row(conv_dw_b[0]), row(conv_ln_g[0]), row(conv_ln_b[0]),
        conv_w_out[0].astype(bf16), row(conv_b_out[0]), row(norm2_g[0]), wr, br)
    dest, yb = _moe_experts(0, h2s, rft, cnt, moe_w1, moe_w3, moe_w2)

    lambda_init = 0.8 - 0.6 * math.exp(-0.3 * 1)
    wr, br = _router_params(moe_w_grp[1], moe_b_grp[1], moe_w_exp[1], moe_b_exp[1])
    two = lambda g: jnp.concatenate([g, g]).reshape(1, V_DIM)
    xf, qk, v = _combine_qkv(dest, x1, rf, yb, row(norm1_g[1]), attn_w_qkv[0].astype(bf16),
                             two(attn_q_g[0]), two(attn_k_g[0]))
    lam_rows = jnp.zeros((SUBLANES, HEAD_DIM), jnp.float32)
    lam_rows = lam_rows.at[0].set(attn_lq1[0]).at[1].set(attn_lk1[0])
    lam_rows = lam_rows.at[2].set(attn_lq2[0]).at[3].set(attn_lk2[0])
    o = _attention(qk, v, lam_rows, row(attn_subln_g[0]), lambda_init)
    x1, h2s, rf, rft, cnt = _attn_post(xf, o, attn_w_o[0].astype(bf16), row(norm2_g[1]), wr, br)
    dest, yb = _moe_experts(1, h2s, rft, cnt, moe_w1, moe_w3, moe_w2)
    xf = _combine(dest, x1, rf, yb)
    return xf.reshape(BATCH, SEQ, D_MODEL)
```

```python
import functools
import math

import jax
import jax.numpy as jnp
from jax import lax
from jax.experimental import pallas as pl
from jax.experimental.pallas import tpu as pltpu

D_MODEL = 1024
BATCH = 8
SEQ = 2048
N_TOK = BATCH * SEQ
CHUNK = 64
CONV_WIDTH = 31
N_HEADS = 8
HEAD_DIM = 64
V_DIM = 128
N_GROUPS = 4
EXPERTS_PER_GROUP = 8
N_EXPERTS = 32
TOP_K = 2
D_EXPERT = 512
EPS = 1e-6
LOG2E = math.log2(math.e)

LANES = 128
SUBLANES = 8
SLAB = D_MODEL // LANES
VMEM_LIMIT = 56 * 1024 * 1024

CONV_TS = 512
HALO = 32
CONV_RC = 64
POST_TS = 512
QKV_TS = 512
ATT_TQ = 256
ROW_BLOCK = 256
GATHER_SLOTS = 3
N_ASSIGN = N_TOK * TOP_K
N_BLOCKS = N_ASSIGN // ROW_BLOCK + N_EXPERTS
N_ROWS = N_BLOCKS * ROW_BLOCK
COMB_TM = 512
DMA_UNROLL = 32
GRP_LANE0 = N_EXPERTS
ROUTE_ROWS = 40


def _cparams(n_axes):
    return pltpu.CompilerParams(
        dimension_semantics=("arbitrary",) * n_axes, vmem_limit_bytes=VMEM_LIMIT)


def _rms(x, g):
    return x * lax.rsqrt(jnp.mean(x * x, axis=-1, keepdims=True) + EPS) * g


def _to_slab(ref, val, rows):
    for j in range(SLAB):
        ref[pl.ds(j, rows, stride=SLAB), :] = val[:, j * LANES:(j + 1) * LANES]


def _from_slab(ref, rows):
    return jnp.concatenate(
        [ref[pl.ds(j, rows, stride=SLAB), :] for j in range(SLAB)], axis=-1)


def _residual_norm_route(x1, g2_ref, wr_ref, br_ref, run_ref,
                         x1_ref, h2s_ref, rf_ref, rft_ref, cnt_ref, rows):
    x1_ref[...] = x1
    h2 = _rms(x1, g2_ref[...])
    _to_slab(h2s_ref, h2, rows)

    hi = h2.astype(jnp.bfloat16)
    lo = (h2 - hi.astype(jnp.float32)).astype(jnp.bfloat16)
    ab = jnp.dot(hi, wr_ref[...], preferred_element_type=jnp.float32)
    c = jnp.dot(lo, wr_ref[:, 0:LANES], preferred_element_type=jnp.float32)
    logits = ab[:, 0:LANES] + ab[:, LANES:2 * LANES] + c + br_ref[...]
    lt = jnp.transpose(logits)[0:ROUTE_ROWS, :]
    sub = lax.broadcasted_iota(jnp.int32, (ROUTE_ROWS, rows), 0).astype(jnp.float32)
    neg = jnp.float32(-jnp.inf)
    big = jnp.float32(1e9)

    gmask = (sub >= GRP_LANE0) & (sub < GRP_LANE0 + N_GROUPS)
    gl = jnp.where(gmask, lt, neg)
    gmax = jnp.max(gl, axis=0, keepdims=True)
    gidx = jnp.min(jnp.where(gl == gmax, sub, big), axis=0, keepdims=True) - GRP_LANE0
    gsum = jnp.sum(jnp.where(gmask, jnp.exp(gl - gmax), 0.0), axis=0, keepdims=True)
    grp_p = 1.0 / gsum

    first_e = gidx * EXPERTS_PER_GROUP
    emask = (sub >= first_e) & (sub < first_e + EXPERTS_PER_GROUP)
    el = jnp.where(emask, lt, neg)
    m1 = jnp.max(el, axis=0, keepdims=True)
    i1 = jnp.min(jnp.where(el == m1, sub, big), axis=0, keepdims=True)
    el2 = jnp.where(sub == i1, neg, el)
    m2 = jnp.max(el2, axis=0, keepdims=True)
    i2 = jnp.min(jnp.where(el2 == m2, sub, big), axis=0, keepdims=True)
    t = jnp.exp(m2 - m1)
    inv = 1.0 / (1.0 + t)
    g_first = grp_p * inv
    g_second = grp_p * t * inv

    sel1 = sub == i1
    sel2 = sub == i2
    member = jnp.where(sel1 | sel2, 1.0, 0.0).astype(jnp.bfloat16)
    c_i = lax.broadcasted_iota(jnp.int32, (rows, rows), 0)
    r_i = lax.broadcasted_iota(jnp.int32, (rows, rows), 1)
    earlier = jnp.where(c_i < r_i, 1.0, 0.0).astype(jnp.bfloat16)
    run = run_ref[...]
    cum = (jnp.dot(member, earlier, preferred_element_type=jnp.float32)
           + jnp.concatenate([run] * (rows // LANES), axis=1))
    rank1 = jnp.sum(jnp.where(sel1, cum, 0.0), axis=0, keepdims=True)
    rank2 = jnp.sum(jnp.where(sel2, cum, 0.0), axis=0, keepdims=True)
    run_new = run + jnp.sum(member.astype(jnp.float32), axis=1, keepdims=True)
    run_ref[...] = run_new
    cnt_ref[...] = run_new

    sub8 = lax.broadcasted_iota(jnp.int32, (SUBLANES, rows), 0)
    out = jnp.where(sub8 == 0, g_first, 0.0)
    out = jnp.where(sub8 == 1, g_second, out)
    out = jnp.where(sub8 == 2, i1, out)
    out = jnp.where(sub8 == 3, i2, out)
    out = jnp.where(sub8 == 4, rank1, out)
    out = jnp.where(sub8 == 5, rank2, out)
    rft_ref[...] = out
    padded = jnp.concatenate([out, jnp.zeros((LANES - SUBLANES, rows), jnp.float32)], axis=0)
    rf_ref[...] = jnp.transpose(padded)


def _route_out_shapes():
    return (
        jax.ShapeDtypeStruct((N_TOK, D_MODEL), jnp.float32),
        jax.ShapeDtypeStruct((N_TOK * SLAB, LANES), jnp.float32),
        jax.ShapeDtypeStruct((N_TOK, LANES), jnp.float32),
        jax.ShapeDtypeStruct((SUBLANES, N_TOK), jnp.float32),
        jax.ShapeDtypeStruct((ROUTE_ROWS, LANES), jnp.float32),
    )


def _route_out_specs(ts, idx):
    return (
        pl.BlockSpec((ts, D_MODEL), lambda *a: (idx(*a), 0)),
        pl.BlockSpec((ts * SLAB, LANES), lambda *a: (idx(*a), 0)),
        pl.BlockSpec((ts, LANES), lambda *a: (idx(*a), 0)),
        pl.BlockSpec((SUBLANES, ts), lambda *a: (0, idx(*a))),
        pl.BlockSpec((ROUTE_ROWS, LANES), lambda *a: (0, 0)),
    )


def _full(shape):
    return pl.BlockSpec(shape, lambda *a: (0,) * len(shape))


def _conv_kernel(x_ref, g1_ref, win_ref, bin_ref, dw_ref, dwb_ref, lng_ref, lnb_ref,
                 wout_ref, bout_ref, g2_ref, wr_ref, br_ref,
                 x1_ref, h2s_ref, rf_ref, rft_ref, cnt_ref,
                 ext_ref, conv_ref, run_ref):
    b = pl.program_id(0)
    s = pl.program_id(1)
    ts = CONV_TS

    @pl.when((b == 0) & (s == 0))
    def _():
        run_ref[...] = jnp.zeros_like(run_ref)

    @pl.when(s == 0)
    def _():
        ext_ref[0:HALO, :] = jnp.zeros((HALO, D_MODEL), jnp.float32)

    x = x_ref[...]
    h = _rms(x, g1_ref[...]).astype(jnp.bfloat16)
    u = jnp.dot(h, win_ref[...], preferred_element_type=jnp.float32) + bin_ref[...]
    glu = u[:, :D_MODEL] * jax.nn.sigmoid(u[:, D_MODEL:])
    ext_ref[HALO:HALO + ts, :] = glu

    base = HALO - (CONV_WIDTH - 1)

    def lane_chunk(c, carry):
        cols = pl.ds(pl.multiple_of(c * LANES, LANES), LANES)
        taps = [dw_ref[pl.ds(k, 1), cols] for k in range(CONV_WIDTH)]
        for r0 in range(0, ts, CONV_RC):
            acc = None
            for r in range(SUBLANES):
                rows = CONV_RC + (SUBLANES if r else 0)
                part = None
                for q in range((base + CONV_WIDTH - 1) // SUBLANES + 1):
                    k = SUBLANES * q + r - base
                    if 0 <= k < CONV_WIDTH:
                        term = ext_ref[pl.ds(r0 + SUBLANES * q, rows), cols] * taps[k]
                        part = term if part is None else part + term
                part = part[r:r + CONV_RC, :] if r else part
                acc = part if acc is None else acc + part
            conv_ref[pl.ds(r0, CONV_RC), cols] = acc
        return carry

    lax.fori_loop(0, D_MODEL // LANES, lane_chunk, 0)
    ext_ref[0:HALO, :] = ext_ref[ts:ts + HALO, :]

    v = conv_ref[...] + dwb_ref[...]
    mu = jnp.mean(v, axis=-1, keepdims=True)
    vc = v - mu
    var = jnp.mean(vc * vc, axis=-1, keepdims=True)
    y = vc * lax.rsqrt(var + EPS) * lng_ref[...] + lnb_ref[...]
    y = (y * jax.nn.sigmoid(y)).astype(jnp.bfloat16)
    mix = jnp.dot(y, wout_ref[...], preferred_element_type=jnp.float32) + bout_ref[...]
    _residual_norm_route(x + mix, g2_ref, wr_ref, br_ref, run_ref,
                         x1_ref, h2s_ref, rf_ref, rft_ref, cnt_ref, ts)


def _conv_layer(x, g1, w_in, b_in, dw, dw_b, ln_g, ln_b, w_out, b_out, g2, wr, br):
    ns = SEQ // CONV_TS
    tile = lambda b, s: b * ns + s
    return pl.pallas_call(
        _conv_kernel,
        grid=(BATCH, ns),
        in_specs=[
            pl.BlockSpec((CONV_TS, D_MODEL), lambda b, s: (tile(b, s), 0)),
            _full((1, D_MODEL)),
            _full((D_MODEL, 2 * D_MODEL)),
            _full((1, 2 * D_MODEL)),
            _full((HALO, D_MODEL)),
            _full((1, D_MODEL)), _full((1, D_MODEL)), _full((1, D_MODEL)),
            _full((D_MODEL, D_MODEL)),
            _full((1, D_MODEL)), _full((1, D_MODEL)),
            _full((D_MODEL, 2 * LANES)), _full((1, LANES)),
        ],
        out_specs=_route_out_specs(CONV_TS, tile),
        out_shape=_route_out_shapes(),
        scratch_shapes=[
            pltpu.VMEM((HALO + CONV_TS, D_MODEL), jnp.float32),
            pltpu.VMEM((CONV_TS, D_MODEL), jnp.float32),
            pltpu.VMEM((ROUTE_ROWS, LANES), jnp.float32),
        ],
        compiler_params=_cparams(2),
        name="conv_mixer",
    )(x, g1, w_in, b_in, dw, dw_b, ln_g, ln_b, w_out, b_out, g2, wr, br)


def _half_norm(z, gain):
    lane = lax.broadcasted_iota(jnp.int32, z.shape, 1)
    first = lane < HEAD_DIM
    zz = z * z
    ss_a = jnp.sum(jnp.where(first, zz, 0.0), axis=1, keepdims=True)
    ss_b = jnp.sum(jnp.where(first, 0.0, zz), axis=1, keepdims=True)
    inv = jnp.where(first, lax.rsqrt(ss_a * (1.0 / HEAD_DIM) + EPS),
                    lax.rsqrt(ss_b * (1.0 / HEAD_DIM) + EPS))
    return z * inv * gain


def _attn_kernel(q_ref, k_ref, v_ref, lam_ref, sg_ref, o_ref,
                 qa_ref, qb_ref, v1_ref, s_ref, e_ref, part_ref, *, lambda_init):
    lp = lam_ref[...]
    lam = (jnp.exp(jnp.sum(lp[0:1, :] * lp[1:2, :], axis=1, keepdims=True))
           - jnp.exp(jnp.sum(lp[2:3, :] * lp[3:4, :], axis=1, keepdims=True))
           + lambda_init)

    q = q_ref[...]
    first = lax.broadcasted_iota(jnp.int32, (SEQ, V_DIM), 1) < HEAD_DIM
    qa_ref[...] = jnp.where(first, q, jnp.zeros_like(q))
    qb_ref[...] = jnp.where(first, jnp.zeros_like(q), q)

    ones_col = lax.broadcasted_iota(jnp.int32, (SEQ, V_DIM), 1) == 0
    v1_ref[:, 0:V_DIM] = v_ref[...]
    v1_ref[:, V_DIM:2 * V_DIM] = jnp.where(ones_col, 1.0, 0.0).astype(jnp.bfloat16)

    tq = ATT_TQ
    nt = (((1,), (1,)), ((), ()))
    visible = (lax.broadcasted_iota(jnp.int32, (tq, tq), 1) // CHUNK
               <= lax.broadcasted_iota(jnp.int32, (tq, tq), 0) // CHUNK)
    halves = (qa_ref, qb_ref)
    units = [(qi, h) for qi in range(SEQ // tq) for h in range(2)]

    def stage_scores(u, slot):
        qi, h = units[u]
        k0 = qi * tq
        q = halves[h][k0:k0 + tq, :]
        dg = lax.dot_general(q, k_ref[k0:k0 + tq, :], nt, preferred_element_type=jnp.float32)
        s_ref[slot, :, k0:k0 + tq] = jnp.where(visible, dg, -jnp.inf)
        if qi:
            s_ref[slot, :, 0:k0] = lax.dot_general(q, k_ref[0:k0, :], nt,
                                                   preferred_element_type=jnp.float32)

    def stage_numerators(u, slot):
        kend = (units[u][0] + 1) * tq
        sc = s_ref[slot, :, 0:kend]
        m = jnp.max(sc, axis=1, keepdims=True)
        e_ref[slot, :, 0:kend] = jnp.exp2(sc - m).astype(jnp.bfloat16)

    def stage_values(u, slot):
        qi, h = units[u]
        k0 = qi * tq
        kend = k0 + tq
        acc = jnp.dot(e_ref[slot, :, 0:kend], v1_ref[0:kend, :],
                      preferred_element_type=jnp.float32)
        attn = acc[:, 0:V_DIM] * (1.0 / acc[:, V_DIM:V_DIM + 1])
        if h == 0:
            part_ref[...] = attn
        else:
            o = part_ref[...] - lam * attn
            o = _rms(o, sg_ref[...]) * (1.0 - lambda_init)
            o_ref[k0:kend, :] = o.astype(jnp.bfloat16)

    for step in range(len(units) + 2):
        if step < len(units):
            stage_scores(step, step % 2)
        if 1 <= step <= len(units):
            stage_numerators(step - 1, (step - 1) % 2)
        if step >= 2:
            stage_values(step - 2, step % 2)


def _attention(qk, v, lam_rows, subln_g, lambda_init):
    return pl.pallas_call(
        functools.partial(_attn_kernel, lambda_init=lambda_init),
        grid=(BATCH, N_HEADS),
        in_specs=[
            pl.BlockSpec((SEQ, V_DIM), lambda b, h: (b, h)),
            pl.BlockSpec((SEQ, V_DIM), lambda b, h: (b, N_HEADS + h)),
            pl.BlockSpec((SEQ, V_DIM), lambda b, h: (b, h)),
            _full((SUBLANES, HEAD_DIM)),
            _full((1, V_DIM)),
        ],
        out_specs=pl.BlockSpec((SEQ, V_DIM), lambda b, h: (b, h)),
        out_shape=jax.ShapeDtypeStruct((N_TOK, D_MODEL), jnp.bfloat16),
        scratch_shapes=[pltpu.VMEM((SEQ, V_DIM), jnp.bfloat16)] * 2 + [
            pltpu.VMEM((SEQ, 2 * V_DIM), jnp.bfloat16),
            pltpu.VMEM((2, ATT_TQ, SEQ), jnp.float32),
            pltpu.VMEM((2, ATT_TQ, SEQ), jnp.bfloat16),
            pltpu.VMEM((ATT_TQ, V_DIM), jnp.float32),
        ],
        compiler_params=_cparams(2),
        name="diff_attention",
    )(qk, qk, v, lam_rows, subln_g)


def _post_kernel(x_ref, m_ref, w_ref, g2_ref, wr_ref, br_ref,
                 x1_ref, h2s_ref, rf_ref, rft_ref, cnt_ref, run_ref):
    @pl.when(pl.program_id(0) == 0)
    def _():
        run_ref[...] = jnp.zeros_like(run_ref)

    mix = jnp.dot(m_ref[...], w_ref[...], preferred_element_type=jnp.float32)
    _residual_norm_route(x_ref[...] + mix, g2_ref, wr_ref, br_ref, run_ref,
                         x1_ref, h2s_ref, rf_ref, rft_ref, cnt_ref, POST_TS)


def _attn_post(x, o, w_o, g2, wr, br):
    return pl.pallas_call(
        _post_kernel,
        grid=(N_TOK // POST_TS,),
        in_specs=[
            pl.BlockSpec((POST_TS, D_MODEL), lambda i: (i, 0)),
            pl.BlockSpec((POST_TS, D_MODEL), lambda i: (i, 0)),
            _full((D_MODEL, D_MODEL)),
            _full((1, D_MODEL)),
            _full((D_MODEL, 2 * LANES)), _full((1, LANES)),
        ],
        out_specs=_route_out_specs(POST_TS, lambda i: i),
        out_shape=_route_out_shapes(),
        scratch_shapes=[pltpu.VMEM((ROUTE_ROWS, LANES), jnp.float32)],
        compiler_params=_cparams(1),
        name="attn_post",
    )(x, o, w_o, g2, wr, br)


def _unrolled(n, body):
    def group(g, carry):
        for u in range(DMA_UNROLL):
            body(g * DMA_UNROLL + u)
        return carry
    lax.fori_loop(0, n // DMA_UNROLL, group, 0)


def _expert_kernel(be_ref, nv_ref, dest_ref, pad_ref, next_ref,
                   h2s_ref, w1_hbm, w3_hbm, w2_hbm, yb_ref,
                   rowtok_ref, xbuf_ref, sem, w1f_ref, w3f_ref, w2f_ref, wsem,
                   w1b_ref, w3b_ref, w2b_ref, *, layer):
    b = pl.program_id(0)
    n_valid = nv_ref[0]
    w_hbm = (w1_hbm, w3_hbm, w2_hbm)
    w_f32 = (w1f_ref, w3f_ref, w2f_ref)

    def weight_copies(expert):
        return [pltpu.make_async_copy(w_hbm[i].at[layer, expert], w_f32[i], wsem.at[i])
                for i in range(3)]

    def row_copy(block, slot, r):
        tok = rowtok_ref[block * ROW_BLOCK + r]
        return pltpu.make_async_copy(
            h2s_ref.at[pl.ds(pl.multiple_of(tok * SLAB, SLAB), SLAB)],
            xbuf_ref.at[slot, pl.ds(r * SLAB, SLAB)],
            sem.at[slot])

    def block_arrival(slot):
        return pltpu.make_async_copy(
            h2s_ref.at[pl.ds(0, ROW_BLOCK * SLAB)], xbuf_ref.at[slot], sem.at[slot])

    @pl.when(b == 0)
    def _():
        for c in weight_copies(be_ref[0]):
            c.start()
        def clear_padding(e, carry):
            def clear(j, c):
                rowtok_ref[j] = 0
                return c
            return lax.fori_loop(pad_ref[e], pad_ref[N_EXPERTS + e], clear, carry)
        lax.fori_loop(0, N_EXPERTS, clear_padding, 0)

        def place(t):
            rowtok_ref[dest_ref[t]] = t
            rowtok_ref[dest_ref[N_TOK + t]] = t
        _unrolled(N_TOK, place)
        second = jnp.minimum(1, n_valid - 1)
        _unrolled(ROW_BLOCK, lambda r: row_copy(0, 0, r).start())
        _unrolled(ROW_BLOCK, lambda r: row_copy(second, 1, r).start())

    expert = be_ref[b]
    fresh = (b == 0) | (expert != be_ref[jnp.maximum(b - 1, 0)])

    @pl.when(fresh & (b < n_valid))
    def _():
        for c in weight_copies(expert):
            c.wait()
        w1b_ref[...] = w1f_ref[...].astype(jnp.bfloat16)
        w3b_ref[...] = w3f_ref[...].astype(jnp.bfloat16)
        w2b_ref[...] = w2f_ref[...].astype(jnp.bfloat16)
        following = next_ref[expert]

        @pl.when(following >= 0)
        def _():
            for c in weight_copies(following):
                c.start()

    @pl.when(b < n_valid)
    def _():
        slot = lax.rem(b, GATHER_SLOTS)
        block_arrival(slot).wait()
        x = _from_slab(xbuf_ref.at[slot], ROW_BLOCK).astype(jnp.bfloat16)
        ahead = jnp.minimum(b + 2, n_valid - 1)
        ahead_slot = lax.rem(b + 2, GATHER_SLOTS)
        for r in range(ROW_BLOCK):
            row_copy(ahead, ahead_slot, r).start()
        h1 = jnp.dot(x, w1b_ref[...], preferred_element_type=jnp.float32)
        h3 = jnp.dot(x, w3b_ref[...], preferred_element_type=jnp.float32)
        act = (h1 * jax.nn.sigmoid(h1) * h3).astype(jnp.bfloat16)
        y = jnp.dot(act, w2b_ref[...], preferred_element_type=jnp.float32)
        _to_slab(yb_ref, y, ROW_BLOCK)

    @pl.when(b == n_valid - 1)
    def _():
        block_arrival(lax.rem(b + 1, GATHER_SLOTS)).wait()
        block_arrival(lax.rem(b + 2, GATHER_SLOTS)).wait()

    @pl.when(b >= n_valid)
    def _():
        yb_ref[...] = jnp.zeros_like(yb_ref)


def _experts(layer, block_e, n_valid, dest_flat, pad_rows, next_expert, h2s, w1, w3, w2):
    hbm = pl.BlockSpec(memory_space=pl.ANY)
    w_in = [pltpu.VMEM((D_MODEL, D_EXPERT), dt) for dt in (jnp.float32, jnp.bfloat16)]
    w_out = [pltpu.VMEM((D_EXPERT, D_MODEL), dt) for dt in (jnp.float32, jnp.bfloat16)]
    return pl.pallas_call(
        functools.partial(_expert_kernel, layer=layer),
        grid_spec=pltpu.PrefetchScalarGridSpec(
            num_scalar_prefetch=5,
            grid=(N_BLOCKS,),
            in_specs=[hbm, hbm, hbm, hbm],
            out_specs=pl.BlockSpec((ROW_BLOCK * SLAB, LANES), lambda b, *_: (b, 0)),
            scratch_shapes=[
                pltpu.SMEM((N_ROWS,), jnp.int32),
                pltpu.VMEM((GATHER_SLOTS, ROW_BLOCK * SLAB, LANES), jnp.float32),
                pltpu.SemaphoreType.DMA((GATHER_SLOTS,)),
                w_in[0], w_in[0], w_out[0],
                pltpu.SemaphoreType.DMA((3,)),
                w_in[1], w_in[1], w_out[1],
            ],
        ),
        out_shape=jax.ShapeDtypeStruct((N_ROWS * SLAB, LANES), jnp.float32),
        compiler_params=_cparams(1),
        name="moe_experts",
    )(block_e, n_valid, dest_flat, pad_rows, next_expert, h2s, w1, w3, w2)


def _combine_kernel(pos_ref, x1_ref, rf_ref, yb_ref, out_ref, buf_ref, sem):
    i = pl.program_id(0)
    n = pl.num_programs(0)
    tm = COMB_TM

    def copy(tile, slot, j, k):
        p = pos_ref[k * N_TOK + tile * tm + j]
        return pltpu.make_async_copy(
            yb_ref.at[pl.ds(pl.multiple_of(p * SLAB, SLAB), SLAB)],
            buf_ref.at[slot, k, pl.ds(pl.multiple_of(j * SLAB, SLAB), SLAB)],
            sem.at[slot])

    def issue(tile, slot):
        def body(j):
            copy(tile, slot, j, 0).start()
            copy(tile, slot, j, 1).start()
        _unrolled(tm, body)

    def drain(tile, slot):
        def body(j):
            copy(tile, slot, j, 0).wait()
            copy(tile, slot, j, 1).wait()
        _unrolled(tm, body)

    slot = i % 2

    @pl.when(i == 0)
    def _():
        issue(0, 0)

    @pl.when(i + 1 < n)
    def _():
        issue(i + 1, 1 - slot)

    drain(i, slot)
    rf = rf_ref[...]
    y0 = _from_slab(buf_ref.at[slot, 0], tm)
    y1 = _from_slab(buf_ref.at[slot, 1], tm)
    out_ref[...] = x1_ref[...] + rf[:, 0:1] * y0 + rf[:, 1:2] * y1


def _combine(pos_flat, x1, rf, yb):
    return pl.pallas_call(
        _combine_kernel,
        grid_spec=pltpu.PrefetchScalarGridSpec(
            num_scalar_prefetch=1,
            grid=(N_TOK // COMB_TM,),
            in_specs=[
                pl.BlockSpec((COMB_TM, D_MODEL), lambda i, p: (i, 0)),
                pl.BlockSpec((COMB_TM, LANES), lambda i, p: (i, 0)),
                pl.BlockSpec(memory_space=pl.ANY),
            ],
            out_specs=pl.BlockSpec((COMB_TM, D_MODEL), lambda i, p: (i, 0)),
            scratch_shapes=[
                pltpu.VMEM((2, TOP_K, COMB_TM * SLAB, LANES), jnp.float32),
                pltpu.SemaphoreType.DMA((2,)),
            ],
        ),
        out_shape=jax.ShapeDtypeStruct((N_TOK, D_MODEL), jnp.float32),
        compiler_params=_cparams(1),
        name="moe_combine",
    )(pos_flat, x1, rf, yb)


def _combine_qkv_kernel(pos_ref, x1_ref, rf_ref, yb_ref, g1_ref, w_ref, qg_ref, kg_ref,
                        x2_ref, qk_ref, v_ref, buf_ref, sem):
    i = pl.program_id(0)
    n = pl.num_programs(0)
    tm = QKV_TS

    def copy(tile, slot, j, k):
        p = pos_ref[k * N_TOK + tile * tm + j]
        return pltpu.make_async_copy(
            yb_ref.at[pl.ds(pl.multiple_of(p * SLAB, SLAB), SLAB)],
            buf_ref.at[slot, k, pl.ds(j * SLAB, SLAB)],
            sem.at[slot])

    def plane_arrival(slot, k):
        return pltpu.make_async_copy(
            yb_ref.at[pl.ds(0, tm * SLAB)], buf_ref.at[slot, k], sem.at[slot])

    @pl.when(i == 0)
    def _():
        def first(j):
            copy(0, 0, j, 0).start()
            copy(0, 0, j, 1).start()
        _unrolled(tm, first)

    slot = i % 2
    plane_arrival(slot, 0).wait()
    plane_arrival(slot, 1).wait()
    rf = rf_ref[...]
    y0 = _from_slab(buf_ref.at[slot, 0], tm)
    y1 = _from_slab(buf_ref.at[slot, 1], tm)
    x2 = x1_ref[...] + rf[:, 0:1] * y0 + rf[:, 1:2] * y1
    x2_ref[...] = x2
    ahead = jnp.minimum(i + 1, n - 1)
    for j in range(tm):
        copy(ahead, 1 - slot, j, 0).start()
        copy(ahead, 1 - slot, j, 1).start()
    h = _rms(x2, g1_ref[...]).astype(jnp.bfloat16)
    qkv = jnp.dot(h, w_ref[...], preferred_element_type=jnp.float32)
    q_scale = HEAD_DIM ** -0.5 * LOG2E
    for c in range(N_HEADS):
        q_cols = slice(c * V_DIM, (c + 1) * V_DIM)
        k_cols = slice(D_MODEL + c * V_DIM, D_MODEL + (c + 1) * V_DIM)
        qk_ref[:, q_cols] = (_half_norm(qkv[:, q_cols], qg_ref[...]) * q_scale).astype(jnp.bfloat16)
        qk_ref[:, k_cols] = _half_norm(qkv[:, k_cols], kg_ref[...]).astype(jnp.bfloat16)
    v_ref[...] = qkv[:, 2 * D_MODEL:].astype(jnp.bfloat16)

    @pl.when(i == n - 1)
    def _():
        plane_arrival(1 - slot, 0).wait()
        plane_arrival(1 - slot, 1).wait()


def _combine_qkv(pos_flat, x1, rf, yb, g1, w_qkv, qg2, kg2):
    tile = lambda i, p: (i, 0)
    return pl.pallas_call(
        _combine_qkv_kernel,
        grid_spec=pltpu.PrefetchScalarGridSpec(
            num_scalar_prefetch=1,
            grid=(N_TOK // QKV_TS,),
            in_specs=[
                pl.BlockSpec((QKV_TS, D_MODEL), tile),
                pl.BlockSpec((QKV_TS, LANES), tile),
                pl.BlockSpec(memory_space=pl.ANY),
                _full((1, D_MODEL)),
                _full((D_MODEL, 3 * D_MODEL)),
                _full((1, V_DIM)), _full((1, V_DIM)),
            ],
            out_specs=(
                pl.BlockSpec((QKV_TS, D_MODEL), tile),
                pl.BlockSpec((QKV_TS, 2 * D_MODEL), tile),
                pl.BlockSpec((QKV_TS, D_MODEL), tile),
            ),
            scratch_shapes=[
                pltpu.VMEM((2, TOP_K, QKV_TS * SLAB, LANES), jnp.float32),
                pltpu.SemaphoreType.DMA((2,)),
            ],
        ),
        out_shape=(
            jax.ShapeDtypeStruct((N_TOK, D_MODEL), jnp.float32),
            jax.ShapeDtypeStruct((N_TOK, 2 * D_MODEL), jnp.bfloat16),
            jax.ShapeDtypeStruct((N_TOK, D_MODEL), jnp.bfloat16),
        ),
        compiler_params=_cparams(1),
        name="combine_qkv",
    )(pos_flat, x1, rf, yb, g1, w_qkv, qg2, kg2)


def _moe_experts(layer, h2s, rft, cnt, w1, w3, w2):
    experts = rft[2:4].astype(jnp.int32)
    rank = rft[4:6].astype(jnp.int32)
    counts = cnt[:N_EXPERTS, 0].astype(jnp.int32)
    padded = (counts + ROW_BLOCK - 1) // ROW_BLOCK * ROW_BLOCK
    pad_ends = jnp.cumsum(padded)
    pad_starts = pad_ends - padded
    ids = jnp.arange(N_EXPERTS, dtype=jnp.int32)
    is_e = experts[None] == ids[:, None, None]
    dest = (jnp.sum(jnp.where(is_e, pad_starts[:, None, None], 0), axis=0) + rank).reshape(-1)
    n_valid = (pad_ends[-1:] // ROW_BLOCK).astype(jnp.int32)
    block_row0 = jnp.arange(N_BLOCKS, dtype=jnp.int32) * ROW_BLOCK
    block_e = jnp.minimum(
        jnp.sum((pad_ends[None, :] <= block_row0[:, None]).astype(jnp.int32), axis=1),
        N_EXPERTS - 1).astype(jnp.int32)
    pad_rows = jnp.concatenate([pad_starts + counts, pad_ends]).astype(jnp.int32)
    later_nonempty = (counts > 0)[None, :] & (ids[None, :] > ids[:, None])
    following = jnp.min(jnp.where(later_nonempty, ids[None, :], N_EXPERTS), axis=1)
    next_expert = jnp.where(following < N_EXPERTS, following, -1).astype(jnp.int32)
    yb = _experts(layer, block_e, n_valid, dest, pad_rows, next_expert, h2s, w1, w3, w2)
    return dest, yb


def _router_params(w_grp, b_grp, w_exp, b_exp):
    wr = jnp.zeros((D_MODEL, LANES), jnp.float32)
    wr = wr.at[:, :N_EXPERTS].set(w_exp).at[:, GRP_LANE0:GRP_LANE0 + N_GROUPS].set(w_grp)
    br = jnp.zeros((1, LANES), jnp.float32)
    br = br.at[0, :N_EXPERTS].set(b_exp).at[0, GRP_LANE0:GRP_LANE0 + N_GROUPS].set(b_grp)
    w_hi = wr.astype(jnp.bfloat16)
    w_lo = (wr - w_hi.astype(jnp.float32)).astype(jnp.bfloat16)
    return jnp.concatenate([w_hi, w_lo], axis=1), br


def kernel(x, norm1_g, norm2_g, conv_w_in, conv_b_in, conv_dw, conv_dw_b, conv_ln_g, conv_ln_b, conv_w_out, conv_b_out, attn_w_qkv, attn_q_g, attn_k_g, attn_lq1, attn_lk1, attn_lq2, attn_lk2, attn_subln_g, attn_w_o, moe_w_grp, moe_b_grp, moe_w_exp, moe_b_exp, moe_w1, moe_w3, moe_w2):
    assert x.shape == (BATCH, SEQ, D_MODEL) and x.dtype == jnp.float32
    assert moe_w1.shape == (2, N_EXPERTS, D_MODEL, D_EXPERT) and conv_dw.shape[1] == CONV_WIDTH
    bf16 = jnp.bfloat16
    row = lambda a: a.reshape(1, -1)
    xf = x.reshape(N_TOK, D_MODEL)

    wr, br = _router_params(moe_w_grp[0], moe_b_grp[0], moe_w_exp[0], moe_b_exp[0])
    dw = jnp.zeros((HALO, D_MODEL), jnp.float32).at[:CONV_WIDTH].set(conv_dw[0])
    x1, h2s, rf, rft, cnt = _conv_layer(
        xf, row(norm1_g[0]), conv_w_in[0].astype(bf16), row(conv_b_in[0]), dw,
        row(conv_dw_b[0]), row(conv_ln_g[0]), row(conv_ln_b[0]),
        conv_w_out[0].astype(bf16), row(conv_b_out[0]), row(norm2_g[0]), wr, br)
    dest, yb = _moe_experts(0, h2s, rft, cnt, moe_w1, moe_w3, moe_w2)

    lambda_init = 0.8 - 0.6 * math.exp(-0.3 * 1)
    wr, br = _router_params(moe_w_grp[1], moe_b_grp[1], moe_w_exp[1], moe_b_exp[1])
    two = lambda g: jnp.concatenate([g, g]).reshape(1, V_DIM)
    xf, qk, v = _combine_qkv(dest, x1, rf, yb, row(norm1_g[1]), attn_w_qkv[0].astype(bf16),
                             two(attn_q_g[0]), two(attn_k_g[0]))
    lam_rows = jnp.zeros((SUBLANES, HEAD_DIM), jnp.float32)
    lam_rows = lam_rows.at[0].set(attn_lq1[0]).at[1].set(attn_lk1[0])
    lam_rows = lam_rows.at[2].set(attn_lq2[0]).at[3].set(attn_lk2[0])
    o = _attention(qk, v, lam_rows, row(attn_subln_g[0]), lambda_init)
    x1, h2s, rf, rft, cnt = _attn_post(xf, o, attn_w_o[0].astype(bf16), row(norm2_g[1]), wr, br)
    dest, yb = _moe_experts(1, h2s, rft, cnt, moe_w1, moe_w3, moe_w2)
    xf = _combine(dest, x1, rf, yb)
    return xf.reshape(BATCH, SEQ, D_MODEL)
```

```python
import functools
import math

import jax
import jax.numpy as jnp
from jax import lax
from jax.experimental import pallas as pl
from jax.experimental.pallas import tpu as pltpu

D_MODEL = 1024
BATCH = 8
SEQ = 2048
N_TOK = BATCH * SEQ
CHUNK = 64
CONV_WIDTH = 31
N_HEADS = 8
HEAD_DIM = 64
V_DIM = 128
N_GROUPS = 4
EXPERTS_PER_GROUP = 8
N_EXPERTS = 32
TOP_K = 2
D_EXPERT = 512
EPS = 1e-6
LOG2E = math.log2(math.e)

LANES = 128
SUBLANES = 8
SLAB = D_MODEL // LANES
VMEM_LIMIT = 56 * 1024 * 1024

CONV_TS = 512
HALO = 32
CONV_RC = 64
POST_TS = 512
QKV_TS = 512
ATT_TQ = 256
ROW_BLOCK = 256
GATHER_SLOTS = 3
N_ASSIGN = N_TOK * TOP_K
N_BLOCKS = N_ASSIGN // ROW_BLOCK + N_EXPERTS
N_ROWS = N_BLOCKS * ROW_BLOCK
COMB_TM = 256
DMA_UNROLL = 16
GRP_LANE0 = N_EXPERTS
ROUTE_ROWS = 40


def _cparams(n_axes):
    return pltpu.CompilerParams(
        dimension_semantics=("arbitrary",) * n_axes, vmem_limit_bytes=VMEM_LIMIT)


def _rms(x, g):
    return x * lax.rsqrt(jnp.mean(x * x, axis=-1, keepdims=True) + EPS) * g


def _to_slab(ref, val, rows):
    for j in range(SLAB):
        ref[pl.ds(j, rows, stride=SLAB), :] = val[:, j * LANES:(j + 1) * LANES]


def _from_slab(ref, rows):
    return jnp.concatenate(
        [ref[pl.ds(j, rows, stride=SLAB), :] for j in range(SLAB)], axis=-1)


def _residual_norm_route(x1, g2_ref, wr_ref, br_ref, run_ref,
                         x1_ref, h2s_ref, rf_ref, rft_ref, cnt_ref, rows):
    x1_ref[...] = x1
    h2 = _rms(x1, g2_ref[...])
    _to_slab(h2s_ref, h2, rows)

    hi = h2.astype(jnp.bfloat16)
    lo = (h2 - hi.astype(jnp.float32)).astype(jnp.bfloat16)
    ab = jnp.dot(hi, wr_ref[...], preferred_element_type=jnp.float32)
    c = jnp.dot(lo, wr_ref[:, 0:LANES], preferred_element_type=jnp.float32)
    logits = ab[:, 0:LANES] + ab[:, LANES:2 * LANES] + c + br_ref[...]
    lt = jnp.transpose(logits)[0:ROUTE_ROWS, :]
    sub = lax.broadcasted_iota(jnp.int32, (ROUTE_ROWS, rows), 0).astype(jnp.float32)
    neg = jnp.float32(-jnp.inf)
    big = jnp.float32(1e9)

    gmask = (sub >= GRP_LANE0) & (sub < GRP_LANE0 + N_GROUPS)
    gl = jnp.where(gmask, lt, neg)
    gmax = jnp.max(gl, axis=0, keepdims=True)
    gidx = jnp.min(jnp.where(gl == gmax, sub, big), axis=0, keepdims=True) - GRP_LANE0
    gsum = jnp.sum(jnp.where(gmask, jnp.exp(gl - gmax), 0.0), axis=0, keepdims=True)
    grp_p = 1.0 / gsum

    first_e = gidx * EXPERTS_PER_GROUP
    emask = (sub >= first_e) & (sub < first_e + EXPERTS_PER_GROUP)
    el = jnp.where(emask, lt, neg)
    m1 = jnp.max(el, axis=0, keepdims=True)
    i1 = jnp.min(jnp.where(el == m1, sub, big), axis=0, keepdims=True)
    el2 = jnp.where(sub == i1, neg, el)
    m2 = jnp.max(el2, axis=0, keepdims=True)
    i2 = jnp.min(jnp.where(el2 == m2, sub, big), axis=0, keepdims=True)
    t = jnp.exp(m2 - m1)
    inv = 1.0 / (1.0 + t)
    g_first = grp_p * inv
    g_second = grp_p * t * inv

    sel1 = sub == i1
    sel2 = sub == i2
    member = jnp.where(sel1 | sel2, 1.0, 0.0).astype(jnp.bfloat16)
    c_i = lax.broadcasted_iota(jnp.int32, (rows, rows), 0)
    r_i = lax.broadcasted_iota(jnp.int32, (rows, rows), 1)
    earlier = jnp.where(c_i < r_i, 1.0, 0.0).astype(jnp.bfloat16)
    run = run_ref[...]
    cum = (jnp.dot(member, earlier, preferred_element_type=jnp.float32)
           + jnp.concatenate([run] * (rows // LANES), axis=1))
    rank1 = jnp.sum(jnp.where(sel1, cum, 0.0), axis=0, keepdims=True)
    rank2 = jnp.sum(jnp.where(sel2, cum, 0.0), axis=0, keepdims=True)
    run_new = run + jnp.sum(member.astype(jnp.float32), axis=1, keepdims=True)
    run_ref[...] = run_new
    cnt_ref[...] = run_new

    sub8 = lax.broadcasted_iota(jnp.int32, (SUBLANES, rows), 0)
    out = jnp.where(sub8 == 0, g_first, 0.0)
    out = jnp.where(sub8 == 1, g_second, out)
    out = jnp.where(sub8 == 2, i1, out)
    out = jnp.where(sub8 == 3, i2, out)
    out = jnp.where(sub8 == 4, rank1, out)
    out = jnp.where(sub8 == 5, rank2, out)
    rft_ref[...] = out
    padded = jnp.concatenate([out, jnp.zeros((LANES - SUBLANES, rows), jnp.float32)], axis=0)
    rf_ref[...] = jnp.transpose(padded)


def _route_out_shapes():
    return (
        jax.ShapeDtypeStruct((N_TOK, D_MODEL), jnp.float32),
        jax.ShapeDtypeStruct((N_TOK * SLAB, LANES), jnp.float32),
        jax.ShapeDtypeStruct((N_TOK, LANES), jnp.float32),
        jax.ShapeDtypeStruct((SUBLANES, N_TOK), jnp.float32),
        jax.ShapeDtypeStruct((ROUTE_ROWS, LANES), jnp.float32),
    )


def _route_out_specs(ts, idx):
    return (
        pl.BlockSpec((ts, D_MODEL), lambda *a: (idx(*a), 0)),
        pl.BlockSpec((ts * SLAB, LANES), lambda *a: (idx(*a), 0)),
        pl.BlockSpec((ts, LANES), lambda *a: (idx(*a), 0)),
        pl.BlockSpec((SUBLANES, ts), lambda *a: (0, idx(*a))),
        pl.BlockSpec((ROUTE_ROWS, LANES), lambda *a: (0, 0)),
    )


def _full(shape):
    return pl.BlockSpec(shape, lambda *a: (0,) * len(shape))


def _conv_kernel(x_ref, g1_ref, win_ref, bin_ref, dw_ref, dwb_ref, lng_ref, lnb_ref,
                 wout_ref, bout_ref, g2_ref, wr_ref, br_ref,
                 x1_ref, h2s_ref, rf_ref, rft_ref, cnt_ref,
                 ext_ref, conv_ref, run_ref):
    b = pl.program_id(0)
    s = pl.program_id(1)
    ts = CONV_TS

    @pl.when((b == 0) & (s == 0))
    def _():
        run_ref[...] = jnp.zeros_like(run_ref)

    @pl.when(s == 0)
    def _():
        ext_ref[0:HALO, :] = jnp.zeros((HALO, D_MODEL), jnp.float32)

    x = x_ref[...]
    h = _rms(x, g1_ref[...]).astype(jnp.bfloat16)
    u = jnp.dot(h, win_ref[...], preferred_element_type=jnp.float32) + bin_ref[...]
    glu = u[:, :D_MODEL] * jax.nn.sigmoid(u[:, D_MODEL:])
    ext_ref[HALO:HALO + ts, :] = glu

    base = HALO - (CONV_WIDTH - 1)

    def lane_chunk(c, carry):
        cols = pl.ds(pl.multiple_of(c * LANES, LANES), LANES)
        taps = [dw_ref[pl.ds(k, 1), cols] for k in range(CONV_WIDTH)]
        for r0 in range(0, ts, CONV_RC):
            acc = None
            for r in range(SUBLANES):
                rows = CONV_RC + (SUBLANES if r else 0)
                part = None
                for q in range((base + CONV_WIDTH - 1) // SUBLANES + 1):
                    k = SUBLANES * q + r - base
                    if 0 <= k < CONV_WIDTH:
                        term = ext_ref[pl.ds(r0 + SUBLANES * q, rows), cols] * taps[k]
                        part = term if part is None else part + term
                part = part[r:r + CONV_RC, :] if r else part
                acc = part if acc is None else acc + part
            conv_ref[pl.ds(r0, CONV_RC), cols] = acc
        return carry

    lax.fori_loop(0, D_MODEL // LANES, lane_chunk, 0)
    ext_ref[0:HALO, :] = ext_ref[ts:ts + HALO, :]

    v = conv_ref[...] + dwb_ref[...]
    mu = jnp.mean(v, axis=-1, keepdims=True)
    vc = v - mu
    var = jnp.mean(vc * vc, axis=-1, keepdims=True)
    y = vc * lax.rsqrt(var + EPS) * lng_ref[...] + lnb_ref[...]
    y = (y * jax.nn.sigmoid(y)).astype(jnp.bfloat16)
    mix = jnp.dot(y, wout_ref[...], preferred_element_type=jnp.float32) + bout_ref[...]
    _residual_norm_route(x + mix, g2_ref, wr_ref, br_ref, run_ref,
                         x1_ref, h2s_ref, rf_ref, rft_ref, cnt_ref, ts)


def _conv_layer(x, g1, w_in, b_in, dw, dw_b, ln_g, ln_b, w_out, b_out, g2, wr, br):
    ns = SEQ // CONV_TS
    tile = lambda b, s: b * ns + s
    return pl.pallas_call(
        _conv_kernel,
        grid=(BATCH, ns),
        in_specs=[
            pl.BlockSpec((CONV_TS, D_MODEL), lambda b, s: (tile(b, s), 0)),
            _full((1, D_MODEL)),
            _full((D_MODEL, 2 * D_MODEL)),
            _full((1, 2 * D_MODEL)),
            _full((HALO, D_MODEL)),
            _full((1, D_MODEL)), _full((1, D_MODEL)), _full((1, D_MODEL)),
            _full((D_MODEL, D_MODEL)),
            _full((1, D_MODEL)), _full((1, D_MODEL)),
            _full((D_MODEL, 2 * LANES)), _full((1, LANES)),
        ],
        out_specs=_route_out_specs(CONV_TS, tile),
        out_shape=_route_out_shapes(),
        scratch_shapes=[
            pltpu.VMEM((HALO + CONV_TS, D_MODEL), jnp.float32),
            pltpu.VMEM((CONV_TS, D_MODEL), jnp.float32),
            pltpu.VMEM((ROUTE_ROWS, LANES), jnp.float32),
        ],
        compiler_params=_cparams(2),
        name="conv_mixer",
    )(x, g1, w_in, b_in, dw, dw_b, ln_g, ln_b, w_out, b_out, g2, wr, br)


def _half_norm(z, gain):
    lane = lax.broadcasted_iota(jnp.int32, z.shape, 1)
    first = lane < HEAD_DIM
    zz = z * z
    ss_a = jnp.sum(jnp.where(first, zz, 0.0), axis=1, keepdims=True)
    ss_b = jnp.sum(jnp.where(first, 0.0, zz), axis=1, keepdims=True)
    inv = jnp.where(first, lax.rsqrt(ss_a * (1.0 / HEAD_DIM) + EPS),
                    lax.rsqrt(ss_b * (1.0 / HEAD_DIM) + EPS))
    return z * inv * gain


def _attn_kernel(q_ref, k_ref, v_ref, lam_ref, sg_ref, w1_ref, w3_ref, w2_ref,
                 o_ref, w1b_ref, w3b_ref, w2b_ref,
                 qa_ref, qb_ref, v1_ref, s_ref, e_ref, part_ref, *, lambda_init):
    w1b_ref[...] = w1_ref[...].astype(jnp.bfloat16)
    w3b_ref[...] = w3_ref[...].astype(jnp.bfloat16)
    w2b_ref[...] = w2_ref[...].astype(jnp.bfloat16)

    lp = lam_ref[...]
    lam = (jnp.exp(jnp.sum(lp[0:1, :] * lp[1:2, :], axis=1, keepdims=True))
           - jnp.exp(jnp.sum(lp[2:3, :] * lp[3:4, :], axis=1, keepdims=True))
           + lambda_init)

    q = q_ref[...]
    first = lax.broadcasted_iota(jnp.int32, (SEQ, V_DIM), 1) < HEAD_DIM
    qa_ref[...] = jnp.where(first, q, jnp.zeros_like(q))
    qb_ref[...] = jnp.where(first, jnp.zeros_like(q), q)

    ones_col = lax.broadcasted_iota(jnp.int32, (SEQ, V_DIM), 1) == 0
    v1_ref[:, 0:V_DIM] = v_ref[...]
    v1_ref[:, V_DIM:2 * V_DIM] = jnp.where(ones_col, 1.0, 0.0).astype(jnp.bfloat16)

    tq = ATT_TQ
    nt = (((1,), (1,)), ((), ()))
    visible = (lax.broadcasted_iota(jnp.int32, (tq, tq), 1) // CHUNK
               <= lax.broadcasted_iota(jnp.int32, (tq, tq), 0) // CHUNK)
    halves = (qa_ref, qb_ref)
    units = [(qi, h) for qi in range(SEQ // tq) for h in range(2)]

    def stage_scores(u, slot):
        qi, h = units[u]
        k0 = qi * tq
        q = halves[h][k0:k0 + tq, :]
        dg = lax.dot_general(q, k_ref[k0:k0 + tq, :], nt, preferred_element_type=jnp.float32)
        s_ref[slot, :, k0:k0 + tq] = jnp.where(visible, dg, -jnp.inf)
        if qi:
            s_ref[slot, :, 0:k0] = lax.dot_general(q, k_ref[0:k0, :], nt,
                                                   preferred_element_type=jnp.float32)

    def stage_numerators(u, slot):
        kend = (units[u][0] + 1) * tq
        sc = s_ref[slot, :, 0:kend]
        m = jnp.max(sc, axis=1, keepdims=True)
        e_ref[slot, :, 0:kend] = jnp.exp2(sc - m).astype(jnp.bfloat16)

    def stage_values(u, slot):
        qi, h = units[u]
        k0 = qi * tq
        kend = k0 + tq
        acc = jnp.dot(e_ref[slot, :, 0:kend], v1_ref[0:kend, :],
                      preferred_element_type=jnp.float32)
        attn = acc[:, 0:V_DIM] * (1.0 / acc[:, V_DIM:V_DIM + 1])
        if h == 0:
            part_ref[...] = attn
        else:
            o = part_ref[...] - lam * attn
            o = _rms(o, sg_ref[...]) * (1.0 - lambda_init)
            o_ref[k0:kend, :] = o.astype(jnp.bfloat16)

    for step in range(len(units) + 2):
        if step < len(units):
            stage_scores(step, step % 2)
        if 1 <= step <= len(units):
            stage_numerators(step - 1, (step - 1) % 2)
        if step >= 2:
            stage_values(step - 2, step % 2)


def _attention(qk, v, lam_rows, subln_g, lambda_init, w1, w3, w2, first_expert):
    halves = BATCH * N_HEADS // N_EXPERTS
    assert halves == 2
    step = lambda b, h: b * N_HEADS + h
    src = lambda b, h: (first_expert + step(b, h) // halves, step(b, h) % halves, 0)
    dst = lambda b, h: (step(b, h) // halves, step(b, h) % halves, 0)
    w_in_block = (None, D_MODEL // halves, D_EXPERT)
    w_out_block = (None, D_EXPERT // halves, D_MODEL)
    return pl.pallas_call(
        functools.partial(_attn_kernel, lambda_init=lambda_init),
        grid=(BATCH, N_HEADS),
        in_specs=[
            pl.BlockSpec((SEQ, V_DIM), lambda b, h: (b, h)),
            pl.BlockSpec((SEQ, V_DIM), lambda b, h: (b, N_HEADS + h)),
            pl.BlockSpec((SEQ, V_DIM), lambda b, h: (b, h)),
            _full((SUBLANES, HEAD_DIM)),
            _full((1, V_DIM)),
            pl.BlockSpec(w_in_block, src), pl.BlockSpec(w_in_block, src),
            pl.BlockSpec(w_out_block, src),
        ],
        out_specs=(
            pl.BlockSpec((SEQ, V_DIM), lambda b, h: (b, h)),
            pl.BlockSpec(w_in_block, dst), pl.BlockSpec(w_in_block, dst),
            pl.BlockSpec(w_out_block, dst),
        ),
        out_shape=(
            jax.ShapeDtypeStruct((N_TOK, D_MODEL), jnp.bfloat16),
            jax.ShapeDtypeStruct((N_EXPERTS, D_MODEL, D_EXPERT), jnp.bfloat16),
            jax.ShapeDtypeStruct((N_EXPERTS, D_MODEL, D_EXPERT), jnp.bfloat16),
            jax.ShapeDtypeStruct((N_EXPERTS, D_EXPERT, D_MODEL), jnp.bfloat16),
        ),
        scratch_shapes=[pltpu.VMEM((SEQ, V_DIM), jnp.bfloat16)] * 2 + [
            pltpu.VMEM((SEQ, 2 * V_DIM), jnp.bfloat16),
            pltpu.VMEM((2, ATT_TQ, SEQ), jnp.float32),
            pltpu.VMEM((2, ATT_TQ, SEQ), jnp.bfloat16),
            pltpu.VMEM((ATT_TQ, V_DIM), jnp.float32),
        ],
        compiler_params=_cparams(2),
        name="diff_attention",
    )(qk, qk, v, lam_rows, subln_g, w1, w3, w2)


def _post_kernel(x_ref, m_ref, w_ref, g2_ref, wr_ref, br_ref,
                 x1_ref, h2s_ref, rf_ref, rft_ref, cnt_ref, run_ref):
    @pl.when(pl.program_id(0) == 0)
    def _():
        run_ref[...] = jnp.zeros_like(run_ref)

    mix = jnp.dot(m_ref[...], w_ref[...], preferred_element_type=jnp.float32)
    _residual_norm_route(x_ref[...] + mix, g2_ref, wr_ref, br_ref, run_ref,
                         x1_ref, h2s_ref, rf_ref, rft_ref, cnt_ref, POST_TS)


def _attn_post(x, o, w_o, g2, wr, br):
    return pl.pallas_call(
        _post_kernel,
        grid=(N_TOK // POST_TS,),
        in_specs=[
            pl.BlockSpec((POST_TS, D_MODEL), lambda i: (i, 0)),
            pl.BlockSpec((POST_TS, D_MODEL), lambda i: (i, 0)),
            _full((D_MODEL, D_MODEL)),
            _full((1, D_MODEL)),
            _full((D_MODEL, 2 * LANES)), _full((1, LANES)),
        ],
        out_specs=_route_out_specs(POST_TS, lambda i: i),
        out_shape=_route_out_shapes(),
        scratch_shapes=[pltpu.VMEM((ROUTE_ROWS, LANES), jnp.float32)],
        compiler_params=_cparams(1),
        name="attn_post",
    )(x, o, w_o, g2, wr, br)


def _unrolled(n, body):
    def group(g, carry):
        for u in range(DMA_UNROLL):
            body(g * DMA_UNROLL + u)
        return carry
    lax.fori_loop(0, n // DMA_UNROLL, group, 0)


def _expert_kernel(be_ref, nv_ref, dest_ref, pad_ref, next_ref,
                   h2s_ref, w1_hbm, w3_hbm, w2_hbm, yb_ref,
                   rowtok_ref, xbuf_ref, sem, w1f_ref, w3f_ref, w2f_ref, wsem,
                   w1b_ref, w3b_ref, w2b_ref, *, first_expert):
    b = pl.program_id(0)
    n_valid = nv_ref[0]
    w_hbm = (w1_hbm, w3_hbm, w2_hbm)
    w_f32 = (w1f_ref, w3f_ref, w2f_ref)

    def weight_copies(expert):
        return [pltpu.make_async_copy(w_hbm[i].at[first_expert + expert], w_f32[i], wsem.at[i])
                for i in range(3)]

    def row_copy(block, slot, r):
        tok = rowtok_ref[block * ROW_BLOCK + r]
        return pltpu.make_async_copy(
            h2s_ref.at[pl.ds(pl.multiple_of(tok * SLAB, SLAB), SLAB)],
            xbuf_ref.at[slot, pl.ds(r * SLAB, SLAB)],
            sem.at[slot])

    def block_arrival(slot):
        return pltpu.make_async_copy(
            h2s_ref.at[pl.ds(0, ROW_BLOCK * SLAB)], xbuf_ref.at[slot], sem.at[slot])

    @pl.when(b == 0)
    def _():
        for c in weight_copies(be_ref[0]):
            c.start()
        def clear_padding(e, carry):
            def clear(j, c):
                rowtok_ref[j] = 0
                return c
            return lax.fori_loop(pad_ref[e], pad_ref[N_EXPERTS + e], clear, carry)
        lax.fori_loop(0, N_EXPERTS, clear_padding, 0)

        def place(t):
            rowtok_ref[dest_ref[t]] = t
            rowtok_ref[dest_ref[N_TOK + t]] = t
        _unrolled(N_TOK, place)
        second = jnp.minimum(1, n_valid - 1)
        _unrolled(ROW_BLOCK, lambda r: row_copy(0, 0, r).start())
        _unrolled(ROW_BLOCK, lambda r: row_copy(second, 1, r).start())

    expert = be_ref[b]
    fresh = (b == 0) | (expert != be_ref[jnp.maximum(b - 1, 0)])

    @pl.when(fresh & (b < n_valid))
    def _():
        for c in weight_copies(expert):
            c.wait()
        w1b_ref[...] = w1f_ref[...].astype(jnp.bfloat16)
        w3b_ref[...] = w3f_ref[...].astype(jnp.bfloat16)
        w2b_ref[...] = w2f_ref[...].astype(jnp.bfloat16)
        following = next_ref[expert]

        @pl.when(following >= 0)
        def _():
            for c in weight_copies(following):
                c.start()

    @pl.when(b < n_valid)
    def _():
        slot = lax.rem(b, GATHER_SLOTS)
        block_arrival(slot).wait()
        x = _from_slab(xbuf_ref.at[slot], ROW_BLOCK).astype(jnp.bfloat16)
        ahead = jnp.minimum(b + 2, n_valid - 1)
        ahead_slot = lax.rem(b + 2, GATHER_SLOTS)
        for r in range(ROW_BLOCK):
            row_copy(ahead, ahead_slot, r).start()
        h1 = jnp.dot(x, w1b_ref[...], preferred_element_type=jnp.float32)
        h3 = jnp.dot(x, w3b_ref[...], preferred_element_type=jnp.float32)
        act = (h1 * jax.nn.sigmoid(h1) * h3).astype(jnp.bfloat16)
        y = jnp.dot(act, w2b_ref[...], preferred_element_type=jnp.float32)
        _to_slab(yb_ref, y, ROW_BLOCK)

    @pl.when(b == n_valid - 1)
    def _():
        block_arrival(lax.rem(b + 1, GATHER_SLOTS)).wait()
        block_arrival(lax.rem(b + 2, GATHER_SLOTS)).wait()

    @pl.when(b >= n_valid)
    def _():
        yb_ref[...] = jnp.zeros_like(yb_ref)


def _experts(first_expert, block_e, n_valid, dest_flat, pad_rows, next_expert, h2s, w1, w3, w2):
    hbm = pl.BlockSpec(memory_space=pl.ANY)
    w_in = [pltpu.VMEM((D_MODEL, D_EXPERT), dt) for dt in (w1.dtype, jnp.bfloat16)]
    w_out = [pltpu.VMEM((D_EXPERT, D_MODEL), dt) for dt in (w2.dtype, jnp.bfloat16)]
    return pl.pallas_call(
        functools.partial(_expert_kernel, first_expert=first_expert),
        grid_spec=pltpu.PrefetchScalarGridSpec(
            num_scalar_prefetch=5,
            grid=(N_BLOCKS,),
            in_specs=[hbm, hbm, hbm, hbm],
            out_specs=pl.BlockSpec((ROW_BLOCK * SLAB, LANES), lambda b, *_: (b, 0)),
            scratch_shapes=[
                pltpu.SMEM((N_ROWS,), jnp.int32),
                pltpu.VMEM((GATHER_SLOTS, ROW_BLOCK * SLAB, LANES), jnp.float32),
                pltpu.SemaphoreType.DMA((GATHER_SLOTS,)),
                w_in[0], w_in[0], w_out[0],
                pltpu.SemaphoreType.DMA((3,)),
                w_in[1], w_in[1], w_out[1],
            ],
        ),
        out_shape=jax.ShapeDtypeStruct((N_ROWS * SLAB, LANES), jnp.float32),
        compiler_params=_cparams(1),
        name="moe_experts",
    )(block_e, n_valid, dest_flat, pad_rows, next_expert, h2s, w1, w3, w2)


def _combine_kernel(pos_ref, x1_ref, rf_ref, yb_ref, out_ref, buf_ref, sem):
    i = pl.program_id(0)
    n = pl.num_programs(0)
    tm = COMB_TM

    def copy(tile, slot, j, k):
        p = pos_ref[k * N_TOK + tile * tm + j]
        return pltpu.make_async_copy(
            yb_ref.at[pl.ds(pl.multiple_of(p * SLAB, SLAB), SLAB)],
            buf_ref.at[slot, k, pl.ds(pl.multiple_of(j * SLAB, SLAB), SLAB)],
            sem.at[slot])

    def issue(tile, slot):
        def body(j):
            copy(tile, slot, j, 0).start()
            copy(tile, slot, j, 1).start()
        _unrolled(tm, body)

    def drain(tile, slot):
        def body(j):
            copy(tile, slot, j, 0).wait()
            copy(tile, slot, j, 1).wait()
        _unrolled(tm, body)

    slot = i % 2

    @pl.when(i == 0)
    def _():
        issue(0, 0)

    @pl.when(i + 1 < n)
    def _():
        issue(i + 1, 1 - slot)

    drain(i, slot)
    rf = rf_ref[...]
    y0 = _from_slab(buf_ref.at[slot, 0], tm)
    y1 = _from_slab(buf_ref.at[slot, 1], tm)
    out_ref[...] = x1_ref[...] + rf[:, 0:1] * y0 + rf[:, 1:2] * y1


def _combine(pos_flat, x1, rf, yb):
    return pl.pallas_call(
        _combine_kernel,
        grid_spec=pltpu.PrefetchScalarGridSpec(
            num_scalar_prefetch=1,
            grid=(N_TOK // COMB_TM,),
            in_specs=[
                pl.BlockSpec((COMB_TM, D_MODEL), lambda i, p: (i, 0)),
                pl.BlockSpec((COMB_TM, LANES), lambda i, p: (i, 0)),
                pl.BlockSpec(memory_space=pl.ANY),
            ],
            out_specs=pl.BlockSpec((COMB_TM, D_MODEL), lambda i, p: (i, 0)),
            scratch_shapes=[
                pltpu.VMEM((2, TOP_K, COMB_TM * SLAB, LANES), jnp.float32),
                pltpu.SemaphoreType.DMA((2,)),
            ],
        ),
        out_shape=jax.ShapeDtypeStruct((N_TOK, D_MODEL), jnp.float32),
        compiler_params=_cparams(1),
        name="moe_combine",
    )(pos_flat, x1, rf, yb)


def _combine_qkv_kernel(pos_ref, x1_ref, rf_ref, yb_ref, g1_ref, w_ref, qg_ref, kg_ref,
                        x2_ref, qk_ref, v_ref, buf_ref, sem):
    i = pl.program_id(0)
    n = pl.num_programs(0)
    tm = QKV_TS

    def copy(tile, slot, j, k):
        p = pos_ref[k * N_TOK + tile * tm + j]
        return pltpu.make_async_copy(
            yb_ref.at[pl.ds(pl.multiple_of(p * SLAB, SLAB), SLAB)],
            buf_ref.at[slot, k, pl.ds(j * SLAB, SLAB)],
            sem.at[slot])

    def plane_arrival(slot, k):
        return pltpu.make_async_copy(
            yb_ref.at[pl.ds(0, tm * SLAB)], buf_ref.at[slot, k], sem.at[slot])

    @pl.when(i == 0)
    def _():
        def first(j):
            copy(0, 0, j, 0).start()
            copy(0, 0, j, 1).start()
        _unrolled(tm, first)

    slot = i % 2
    plane_arrival(slot, 0).wait()
    plane_arrival(slot, 1).wait()
    rf = rf_ref[...]
    y0 = _from_slab(buf_ref.at[slot, 0], tm)
    y1 = _from_slab(buf_ref.at[slot, 1], tm)
    x2 = x1_ref[...] + rf[:, 0:1] * y0 + rf[:, 1:2] * y1
    x2_ref[...] = x2
    ahead = jnp.minimum(i + 1, n - 1)
    for j in range(tm):
        copy(ahead, 1 - slot, j, 0).start()
        copy(ahead, 1 - slot, j, 1).start()
    h = _rms(x2, g1_ref[...]).astype(jnp.bfloat16)
    qkv = jnp.dot(h, w_ref[...], preferred_element_type=jnp.float32)
    q_scale = HEAD_DIM ** -0.5 * LOG2E
    for c in range(N_HEADS):
        q_cols = slice(c * V_DIM, (c + 1) * V_DIM)
        k_cols = slice(D_MODEL + c * V_DIM, D_MODEL + (c + 1) * V_DIM)
        qk_ref[:, q_cols] = (_half_norm(qkv[:, q_cols], qg_ref[...]) * q_scale).astype(jnp.bfloat16)
        qk_ref[:, k_cols] = _half_norm(qkv[:, k_cols], kg_ref[...]).astype(jnp.bfloat16)
    v_ref[...] = qkv[:, 2 * D_MODEL:].astype(jnp.bfloat16)

    @pl.when(i == n - 1)
    def _():
        plane_arrival(1 - slot, 0).wait()
        plane_arrival(1 - slot, 1).wait()


def _combine_qkv(pos_flat, x1, rf, yb, g1, w_qkv, qg2, kg2):
    tile = lambda i, p: (i, 0)
    return pl.pallas_call(
        _combine_qkv_kernel,
        grid_spec=pltpu.PrefetchScalarGridSpec(
            num_scalar_prefetch=1,
            grid=(N_TOK // QKV_TS,),
            in_specs=[
                pl.BlockSpec((QKV_TS, D_MODEL), tile),
                pl.BlockSpec((QKV_TS, LANES), tile),
                pl.BlockSpec(memory_space=pl.ANY),
                _full((1, D_MODEL)),
                _full((D_MODEL, 3 * D_MODEL)),
                _full((1, V_DIM)), _full((1, V_DIM)),
            ],
            out_specs=(
                pl.BlockSpec((QKV_TS, D_MODEL), tile),
                pl.BlockSpec((QKV_TS, 2 * D_MODEL), tile),
                pl.BlockSpec((QKV_TS, D_MODEL), tile),
            ),
            scratch_shapes=[
                pltpu.VMEM((2, TOP_K, QKV_TS * SLAB, LANES), jnp.float32),
                pltpu.SemaphoreType.DMA((2,)),
            ],
        ),
        out_shape=(
            jax.ShapeDtypeStruct((N_TOK, D_MODEL), jnp.float32),
            jax.ShapeDtypeStruct((N_TOK, 2 * D_MODEL), jnp.bfloat16),
            jax.ShapeDtypeStruct((N_TOK, D_MODEL), jnp.bfloat16),
        ),
        compiler_params=_cparams(1),
        name="combine_qkv",
    )(pos_flat, x1, rf, yb, g1, w_qkv, qg2, kg2)


def _moe_experts(first_expert, h2s, rft, cnt, w1, w3, w2):
    experts = rft[2:4].astype(jnp.int32)
    rank = rft[4:6].astype(jnp.int32)
    counts = cnt[:N_EXPERTS, 0].astype(jnp.int32)
    padded = (counts + ROW_BLOCK - 1) // ROW_BLOCK * ROW_BLOCK
    pad_ends = jnp.cumsum(padded)
    pad_starts = pad_ends - padded
    ids = jnp.arange(N_EXPERTS, dtype=jnp.int32)
    is_e = experts[None] == ids[:, None, None]
    dest = (jnp.sum(jnp.where(is_e, pad_starts[:, None, None], 0), axis=0) + rank).reshape(-1)
    n_valid = (pad_ends[-1:] // ROW_BLOCK).astype(jnp.int32)
    block_row0 = jnp.arange(N_BLOCKS, dtype=jnp.int32) * ROW_BLOCK
    block_e = jnp.minimum(
        jnp.sum((pad_ends[None, :] <= block_row0[:, None]).astype(jnp.int32), axis=1),
        N_EXPERTS - 1).astype(jnp.int32)
    pad_rows = jnp.concatenate([pad_starts + counts, pad_ends]).astype(jnp.int32)
    later_nonempty = (counts > 0)[None, :] & (ids[None, :] > ids[:, None])
    following = jnp.min(jnp.where(later_nonempty, ids[None, :], N_EXPERTS), axis=1)
    next_expert = jnp.where(following < N_EXPERTS, following, -1).astype(jnp.int32)
    yb = _experts(first_expert, block_e, n_valid, dest, pad_rows, next_expert, h2s, w1, w3, w2)
    return dest, yb


def _router_params(w_grp, b_grp, w_exp, b_exp):
    wr = jnp.zeros((D_MODEL, LANES), jnp.float32)
    wr = wr.at[:, :N_EXPERTS].set(w_exp).at[:, GRP_LANE0:GRP_LANE0 + N_GROUPS].set(w_grp)
    br = jnp.zeros((1, LANES), jnp.float32)
    br = br.at[0, :N_EXPERTS].set(b_exp).at[0, GRP_LANE0:GRP_LANE0 + N_GROUPS].set(b_grp)
    w_hi = wr.astype(jnp.bfloat16)
    w_lo = (wr - w_hi.astype(jnp.float32)).astype(jnp.bfloat16)
    return jnp.concatenate([w_hi, w_lo], axis=1), br


def kernel(x, norm1_g, norm2_g, conv_w_in, conv_b_in, conv_dw, conv_dw_b, conv_ln_g, conv_ln_b, conv_w_out, conv_b_out, attn_w_qkv, attn_q_g, attn_k_g, attn_lq1, attn_lk1, attn_lq2, attn_lk2, attn_subln_g, attn_w_o, moe_w_grp, moe_b_grp, moe_w_exp, moe_b_exp, moe_w1, moe_w3, moe_w2):
    assert x.shape == (BATCH, SEQ, D_MODEL) and x.dtype == jnp.float32
    assert moe_w1.shape == (2, N_EXPERTS, D_MODEL, D_EXPERT) and conv_dw.shape[1] == CONV_WIDTH
    bf16 = jnp.bfloat16
    row = lambda a: a.reshape(1, -1)
    xf = x.reshape(N_TOK, D_MODEL)

    wr, br = _router_params(moe_w_grp[0], moe_b_grp[0], moe_w_exp[0], moe_b_exp[0])
    dw = jnp.zeros((HALO, D_MODEL), jnp.float32).at[:CONV_WIDTH].set(conv_dw[0])
    x1, h2s, rf, rft, cnt = _conv_layer(
        xf, row(norm1_g[0]), conv_w_in[0].astype(bf16), row(conv_b_in[0]), dw,
        row(conv_dw_b[0]), row(conv_ln_g[0]), row(conv_ln_b[0]),
        conv_w_out[0].astype(bf16), row(conv_b_out[0]), row(norm2_g[0]), wr, br)
    w1 = moe_w1.reshape(-1, D_MODEL, D_EXPERT)
    w3 = moe_w3.reshape(-1, D_MODEL, D_EXPERT)
    w2 = moe_w2.reshape(-1, D_EXPERT, D_MODEL)
    dest, yb = _moe_experts(0, h2s, rft, cnt, w1, w3, w2)

    lambda_init = 0.8 - 0.6 * math.exp(-0.3 * 1)
    wr, br = _router_params(moe_w_grp[1], moe_b_grp[1], moe_w_exp[1], moe_b_exp[1])
    two = lambda g: jnp.concatenate([g, g]).reshape(1, V_DIM)
    xf, qk, v = _combine_qkv(dest, x1, rf, yb, row(norm1_g[1]), attn_w_qkv[0].astype(bf16),
                             two(attn_q_g[0]), two(attn_k_g[0]))
    lam_rows = jnp.zeros((SUBLANES, HEAD_DIM), jnp.float32)
    lam_rows = lam_rows.at[0].set(attn_lq1[0]).at[1].set(attn_lk1[0])
    lam_rows = lam_rows.at[2].set(attn_lq2[0]).at[3].set(attn_lk2[0])
    o, w1b, w3b, w2b = _attention(qk, v, lam_rows, row(attn_subln_g[0]), lambda_init,
                                  w1, w3, w2, N_EXPERTS)
    x1, h2s, rf, rft, cnt = _attn_post(xf, o, attn_w_o[0].astype(bf16), row(norm2_g[1]), wr, br)
    dest, yb = _moe_experts(0, h2s, rft, cnt, w1b, w3b, w2b)
    xf = _combine(dest, x1, rf, yb)
    return xf.reshape(BATCH, SEQ, D_MODEL)
```

```python
import functools
import math

import jax
import jax.numpy as jnp
from jax import lax
from jax.experimental import pallas as pl
from jax.experimental.pallas import tpu as pltpu

D_MODEL = 1024
BATCH = 8
SEQ = 2048
N_TOK = BATCH * SEQ
CHUNK = 64
CONV_WIDTH = 31
N_HEADS = 8
HEAD_DIM = 64
V_DIM = 128
N_GROUPS = 4
EXPERTS_PER_GROUP = 8
N_EXPERTS = 32
TOP_K = 2
D_EXPERT = 512
EPS = 1e-6
LOG2E = math.log2(math.e)

LANES = 128
SUBLANES = 8
SLAB = D_MODEL // LANES
VMEM_LIMIT = 56 * 1024 * 1024

CONV_TS = 512
HALO = 32
CONV_RC = 64
POST_TS = 512
QKV_TS = 512
ATT_TQ = 256
ROW_BLOCK = 256
GATHER_SLOTS = 5
N_ASSIGN = N_TOK * TOP_K
N_BLOCKS = N_ASSIGN // ROW_BLOCK + N_EXPERTS
N_ROWS = N_BLOCKS * ROW_BLOCK
COMB_TM = 256
DMA_UNROLL = 16
GRP_LANE0 = N_EXPERTS
ROUTE_ROWS = 40


def _cparams(n_axes):
    return pltpu.CompilerParams(
        dimension_semantics=("arbitrary",) * n_axes, vmem_limit_bytes=VMEM_LIMIT)


def _rms(x, g):
    return x * lax.rsqrt(jnp.mean(x * x, axis=-1, keepdims=True) + EPS) * g


def _to_slab(ref, val, rows):
    for j in range(SLAB):
        ref[pl.ds(j, rows, stride=SLAB), :] = val[:, j * LANES:(j + 1) * LANES]


def _from_slab(ref, rows):
    return jnp.concatenate(
        [ref[pl.ds(j, rows, stride=SLAB), :] for j in range(SLAB)], axis=-1)


def _residual_norm_route(x1, g2_ref, wr_ref, br_ref, run_ref,
                         x1_ref, h2s_ref, rf_ref, rft_ref, cnt_ref, rows):
    x1_ref[...] = x1
    h2 = _rms(x1, g2_ref[...])
    _to_slab(h2s_ref, h2, rows)

    hi = h2.astype(jnp.bfloat16)
    lo = (h2 - hi.astype(jnp.float32)).astype(jnp.bfloat16)
    ab = jnp.dot(hi, wr_ref[...], preferred_element_type=jnp.float32)
    c = jnp.dot(lo, wr_ref[:, 0:LANES], preferred_element_type=jnp.float32)
    logits = ab[:, 0:LANES] + ab[:, LANES:2 * LANES] + c + br_ref[...]
    lt = jnp.transpose(logits)[0:ROUTE_ROWS, :]
    sub = lax.broadcasted_iota(jnp.int32, (ROUTE_ROWS, rows), 0).astype(jnp.float32)
    neg = jnp.float32(-jnp.inf)
    big = jnp.float32(1e9)

    gmask = (sub >= GRP_LANE0) & (sub < GRP_LANE0 + N_GROUPS)
    gl = jnp.where(gmask, lt, neg)
    gmax = jnp.max(gl, axis=0, keepdims=True)
    gidx = jnp.min(jnp.where(gl == gmax, sub, big), axis=0, keepdims=True) - GRP_LANE0
    gsum = jnp.sum(jnp.where(gmask, jnp.exp(gl - gmax), 0.0), axis=0, keepdims=True)
    grp_p = 1.0 / gsum

    first_e = gidx * EXPERTS_PER_GROUP
    emask = (sub >= first_e) & (sub < first_e + EXPERTS_PER_GROUP)
    el = jnp.where(emask, lt, neg)
    m1 = jnp.max(el, axis=0, keepdims=True)
    i1 = jnp.min(jnp.where(el == m1, sub, big), axis=0, keepdims=True)
    el2 = jnp.where(sub == i1, neg, el)
    m2 = jnp.max(el2, axis=0, keepdims=True)
    i2 = jnp.min(jnp.where(el2 == m2, sub, big), axis=0, keepdims=True)
    t = jnp.exp(m2 - m1)
    inv = 1.0 / (1.0 + t)
    g_first = grp_p * inv
    g_second = grp_p * t * inv

    sel1 = sub == i1
    sel2 = sub == i2
    member = jnp.where(sel1 | sel2, 1.0, 0.0).astype(jnp.bfloat16)
    c_i = lax.broadcasted_iota(jnp.int32, (rows, rows), 0)
    r_i = lax.broadcasted_iota(jnp.int32, (rows, rows), 1)
    earlier = jnp.where(c_i < r_i, 1.0, 0.0).astype(jnp.bfloat16)
    run = run_ref[...]
    cum = (jnp.dot(member, earlier, preferred_element_type=jnp.float32)
           + jnp.concatenate([run] * (rows // LANES), axis=1))
    rank1 = jnp.sum(jnp.where(sel1, cum, 0.0), axis=0, keepdims=True)
    rank2 = jnp.sum(jnp.where(sel2, cum, 0.0), axis=0, keepdims=True)
    run_new = run + jnp.sum(member.astype(jnp.float32), axis=1, keepdims=True)
    run_ref[...] = run_new
    cnt_ref[...] = run_new

    sub8 = lax.broadcasted_iota(jnp.int32, (SUBLANES, rows), 0)
    out = jnp.where(sub8 == 0, g_first, 0.0)
    out = jnp.where(sub8 == 1, g_second, out)
    out = jnp.where(sub8 == 2, i1, out)
    out = jnp.where(sub8 == 3, i2, out)
    out = jnp.where(sub8 == 4, rank1, out)
    out = jnp.where(sub8 == 5, rank2, out)
    rft_ref[...] = out
    padded = jnp.concatenate([out, jnp.zeros((LANES - SUBLANES, rows), jnp.float32)], axis=0)
    rf_ref[...] = jnp.transpose(padded)


def _route_out_shapes():
    return (
        jax.ShapeDtypeStruct((N_TOK, D_MODEL), jnp.float32),
        jax.ShapeDtypeStruct((N_TOK * SLAB, LANES), jnp.float32),
        jax.ShapeDtypeStruct((N_TOK, LANES), jnp.float32),
        jax.ShapeDtypeStruct((SUBLANES, N_TOK), jnp.float32),
        jax.ShapeDtypeStruct((ROUTE_ROWS, LANES), jnp.float32),
    )


def _route_out_specs(ts, idx):
    return (
        pl.BlockSpec((ts, D_MODEL), lambda *a: (idx(*a), 0)),
        pl.BlockSpec((ts * SLAB, LANES), lambda *a: (idx(*a), 0)),
        pl.BlockSpec((ts, LANES), lambda *a: (idx(*a), 0)),
        pl.BlockSpec((SUBLANES, ts), lambda *a: (0, idx(*a))),
        pl.BlockSpec((ROUTE_ROWS, LANES), lambda *a: (0, 0)),
    )


def _full(shape):
    return pl.BlockSpec(shape, lambda *a: (0,) * len(shape))


def _conv_kernel(x_ref, g1_ref, win_ref, bin_ref, dw_ref, dwb_ref, lng_ref, lnb_ref,
                 wout_ref, bout_ref, g2_ref, wr_ref, br_ref,
                 x1_ref, h2s_ref, rf_ref, rft_ref, cnt_ref,
                 ext_ref, conv_ref, run_ref):
    b = pl.program_id(0)
    s = pl.program_id(1)
    ts = CONV_TS

    @pl.when((b == 0) & (s == 0))
    def _():
        run_ref[...] = jnp.zeros_like(run_ref)

    @pl.when(s == 0)
    def _():
        ext_ref[0:HALO, :] = jnp.zeros((HALO, D_MODEL), jnp.float32)

    x = x_ref[...]
    h = _rms(x, g1_ref[...]).astype(jnp.bfloat16)
    u = jnp.dot(h, win_ref[...], preferred_element_type=jnp.float32) + bin_ref[...]
    glu = u[:, :D_MODEL] * jax.nn.sigmoid(u[:, D_MODEL:])
    ext_ref[HALO:HALO + ts, :] = glu

    base = HALO - (CONV_WIDTH - 1)

    def lane_chunk(c, carry):
        cols = pl.ds(pl.multiple_of(c * LANES, LANES), LANES)
        taps = [dw_ref[pl.ds(k, 1), cols] for k in range(CONV_WIDTH)]
        for r0 in range(0, ts, CONV_RC):
            acc = None
            for r in range(SUBLANES):
                rows = CONV_RC + (SUBLANES if r else 0)
                part = None
                for q in range((base + CONV_WIDTH - 1) // SUBLANES + 1):
                    k = SUBLANES * q + r - base
                    if 0 <= k < CONV_WIDTH:
                        term = ext_ref[pl.ds(r0 + SUBLANES * q, rows), cols] * taps[k]
                        part = term if part is None else part + term
                part = part[r:r + CONV_RC, :] if r else part
                acc = part if acc is None else acc + part
            conv_ref[pl.ds(r0, CONV_RC), cols] = acc
        return carry

    lax.fori_loop(0, D_MODEL // LANES, lane_chunk, 0)
    ext_ref[0:HALO, :] = ext_ref[ts:ts + HALO, :]

    v = conv_ref[...] + dwb_ref[...]
    mu = jnp.mean(v, axis=-1, keepdims=True)
    vc = v - mu
    var = jnp.mean(vc * vc, axis=-1, keepdims=True)
    y = vc * lax.rsqrt(var + EPS) * lng_ref[...] + lnb_ref[...]
    y = (y * jax.nn.sigmoid(y)).astype(jnp.bfloat16)
    mix = jnp.dot(y, wout_ref[...], preferred_element_type=jnp.float32) + bout_ref[...]
    _residual_norm_route(x + mix, g2_ref, wr_ref, br_ref, run_ref,
                         x1_ref, h2s_ref, rf_ref, rft_ref, cnt_ref, ts)


def _conv_layer(x, g1, w_in, b_in, dw, dw_b, ln_g, ln_b, w_out, b_out, g2, wr, br):
    ns = SEQ // CONV_TS
    tile = lambda b, s: b * ns + s
    return pl.pallas_call(
        _conv_kernel,
        grid=(BATCH, ns),
        in_specs=[
            pl.BlockSpec((CONV_TS, D_MODEL), lambda b, s: (tile(b, s), 0)),
            _full((1, D_MODEL)),
            _full((D_MODEL, 2 * D_MODEL)),
            _full((1, 2 * D_MODEL)),
            _full((HALO, D_MODEL)),
            _full((1, D_MODEL)), _full((1, D_MODEL)), _full((1, D_MODEL)),
            _full((D_MODEL, D_MODEL)),
            _full((1, D_MODEL)), _full((1, D_MODEL)),
            _full((D_MODEL, 2 * LANES)), _full((1, LANES)),
        ],
        out_specs=_route_out_specs(CONV_TS, tile),
        out_shape=_route_out_shapes(),
        scratch_shapes=[
            pltpu.VMEM((HALO + CONV_TS, D_MODEL), jnp.float32),
            pltpu.VMEM((CONV_TS, D_MODEL), jnp.float32),
            pltpu.VMEM((ROUTE_ROWS, LANES), jnp.float32),
        ],
        compiler_params=_cparams(2),
        name="conv_mixer",
    )(x, g1, w_in, b_in, dw, dw_b, ln_g, ln_b, w_out, b_out, g2, wr, br)


def _half_norm(z, gain):
    lane = lax.broadcasted_iota(jnp.int32, z.shape, 1)
    first = lane < HEAD_DIM
    zz = z * z
    ss_a = jnp.sum(jnp.where(first, zz, 0.0), axis=1, keepdims=True)
    ss_b = jnp.sum(jnp.where(first, 0.0, zz), axis=1, keepdims=True)
    inv = jnp.where(first, lax.rsqrt(ss_a * (1.0 / HEAD_DIM) + EPS),
                    lax.rsqrt(ss_b * (1.0 / HEAD_DIM) + EPS))
    return z * inv * gain


def _attn_kernel(q_ref, k_ref, v_ref, lam_ref, sg_ref, o_ref,
                 qa_ref, qb_ref, v1_ref, s_ref, e_ref, part_ref, *, lambda_init):
    lp = lam_ref[...]
    lam = (jnp.exp(jnp.sum(lp[0:1, :] * lp[1:2, :], axis=1, keepdims=True))
           - jnp.exp(jnp.sum(lp[2:3, :] * lp[3:4, :], axis=1, keepdims=True))
           + lambda_init)

    q = q_ref[...]
    first = lax.broadcasted_iota(jnp.int32, (SEQ, V_DIM), 1) < HEAD_DIM
    qa_ref[...] = jnp.where(first, q, jnp.zeros_like(q))
    qb_ref[...] = jnp.where(first, jnp.zeros_like(q), q)

    ones_col = lax.broadcasted_iota(jnp.int32, (SEQ, V_DIM), 1) == 0
    v1_ref[:, 0:V_DIM] = v_ref[...]
    v1_ref[:, V_DIM:2 * V_DIM] = jnp.where(ones_col, 1.0, 0.0).astype(jnp.bfloat16)

    tq = ATT_TQ
    nt = (((1,), (1,)), ((), ()))
    visible = (lax.broadcasted_iota(jnp.int32, (tq, tq), 1) // CHUNK
               <= lax.broadcasted_iota(jnp.int32, (tq, tq), 0) // CHUNK)
    halves = (qa_ref, qb_ref)
    units = [(qi, h) for qi in range(SEQ // tq) for h in range(2)]

    def stage_scores(u, slot):
        qi, h = units[u]
        k0 = qi * tq
        q = halves[h][k0:k0 + tq, :]
        dg = lax.dot_general(q, k_ref[k0:k0 + tq, :], nt, preferred_element_type=jnp.float32)
        s_ref[slot, :, k0:k0 + tq] = jnp.where(visible, dg, -jnp.inf)
        if qi:
            s_ref[slot, :, 0:k0] = lax.dot_general(q, k_ref[0:k0, :], nt,
                                                   preferred_element_type=jnp.float32)

    def stage_numerators(u, slot):
        kend = (units[u][0] + 1) * tq
        sc = s_ref[slot, :, 0:kend]
        m = jnp.max(sc, axis=1, keepdims=True)
        e_ref[slot, :, 0:kend] = jnp.exp2(sc - m).astype(jnp.bfloat16)

    def stage_values(u, slot):
        qi, h = units[u]
        k0 = qi * tq
        kend = k0 + tq
        acc = jnp.dot(e_ref[slot, :, 0:kend], v1_ref[0:kend, :],
                      preferred_element_type=jnp.float32)
        attn = acc[:, 0:V_DIM] * (1.0 / acc[:, V_DIM:V_DIM + 1])
        if h == 0:
            part_ref[...] = attn
        else:
            o = part_ref[...] - lam * attn
            o = _rms(o, sg_ref[...]) * (1.0 - lambda_init)
            o_ref[k0:kend, :] = o.astype(jnp.bfloat16)

    for step in range(len(units) + 2):
        if step < len(units):
            stage_scores(step, step % 2)
        if 1 <= step <= len(units):
            stage_numerators(step - 1, (step - 1) % 2)
        if step >= 2:
            stage_values(step - 2, step % 2)


def _attention(qk, v, lam_rows, subln_g, lambda_init):
    return pl.pallas_call(
        functools.partial(_attn_kernel, lambda_init=lambda_init),
        grid=(BATCH, N_HEADS),
        in_specs=[
            pl.BlockSpec((SEQ, V_DIM), lambda b, h: (b, h)),
            pl.BlockSpec((SEQ, V_DIM), lambda b, h: (b, N_HEADS + h)),
            pl.BlockSpec((SEQ, V_DIM), lambda b, h: (b, h)),
            _full((SUBLANES, HEAD_DIM)),
            _full((1, V_DIM)),
        ],
        out_specs=pl.BlockSpec((SEQ, V_DIM), lambda b, h: (b, h)),
        out_shape=jax.ShapeDtypeStruct((N_TOK, D_MODEL), jnp.bfloat16),
        scratch_shapes=[pltpu.VMEM((SEQ, V_DIM), jnp.bfloat16)] * 2 + [
            pltpu.VMEM((SEQ, 2 * V_DIM), jnp.bfloat16),
            pltpu.VMEM((2, ATT_TQ, SEQ), jnp.float32),
            pltpu.VMEM((2, ATT_TQ, SEQ), jnp.bfloat16),
            pltpu.VMEM((ATT_TQ, V_DIM), jnp.float32),
        ],
        compiler_params=_cparams(2),
        name="diff_attention",
    )(qk, qk, v, lam_rows, subln_g)


def _post_kernel(x_ref, m_ref, w_ref, g2_ref, wr_ref, br_ref,
                 x1_ref, h2s_ref, rf_ref, rft_ref, cnt_ref, run_ref):
    @pl.when(pl.program_id(0) == 0)
    def _():
        run_ref[...] = jnp.zeros_like(run_ref)

    mix = jnp.dot(m_ref[...], w_ref[...], preferred_element_type=jnp.float32)
    _residual_norm_route(x_ref[...] + mix, g2_ref, wr_ref, br_ref, run_ref,
                         x1_ref, h2s_ref, rf_ref, rft_ref, cnt_ref, POST_TS)


def _attn_post(x, o, w_o, g2, wr, br):
    return pl.pallas_call(
        _post_kernel,
        grid=(N_TOK // POST_TS,),
        in_specs=[
            pl.BlockSpec((POST_TS, D_MODEL), lambda i: (i, 0)),
            pl.BlockSpec((POST_TS, D_MODEL), lambda i: (i, 0)),
            _full((D_MODEL, D_MODEL)),
            _full((1, D_MODEL)),
            _full((D_MODEL, 2 * LANES)), _full((1, LANES)),
        ],
        out_specs=_route_out_specs(POST_TS, lambda i: i),
        out_shape=_route_out_shapes(),
        scratch_shapes=[pltpu.VMEM((ROUTE_ROWS, LANES), jnp.float32)],
        compiler_params=_cparams(1),
        name="attn_post",
    )(x, o, w_o, g2, wr, br)


def _unrolled(n, body):
    def group(g, carry):
        for u in range(DMA_UNROLL):
            body(g * DMA_UNROLL + u)
        return carry
    lax.fori_loop(0, n // DMA_UNROLL, group, 0)


def _expert_kernel(be_ref, nv_ref, dest_ref, pad_ref, next_ref,
                   h2s_ref, w1_hbm, w3_hbm, w2_hbm, yb_ref,
                   rowtok_ref, xbuf_ref, sem, w1f_ref, w3f_ref, w2f_ref, wsem,
                   w1b_ref, w3b_ref, w2b_ref, *, layer):
    b = pl.program_id(0)
    n_valid = nv_ref[0]
    w_hbm = (w1_hbm, w3_hbm, w2_hbm)
    w_f32 = (w1f_ref, w3f_ref, w2f_ref)

    def weight_copies(expert):
        return [pltpu.make_async_copy(w_hbm[i].at[layer, expert], w_f32[i], wsem.at[i])
                for i in range(3)]

    def row_copy(block, slot, r):
        tok = rowtok_ref[block * ROW_BLOCK + r]
        return pltpu.make_async_copy(
            h2s_ref.at[pl.ds(pl.multiple_of(tok * SLAB, SLAB), SLAB)],
            xbuf_ref.at[slot, pl.ds(r * SLAB, SLAB)],
            sem.at[slot])

    def block_arrival(slot):
        return pltpu.make_async_copy(
            h2s_ref.at[pl.ds(0, ROW_BLOCK * SLAB)], xbuf_ref.at[slot], sem.at[slot])

    @pl.when(b == 0)
    def _():
        for c in weight_copies(be_ref[0]):
            c.start()
        def clear_padding(e, carry):
            def clear(j, c):
                rowtok_ref[j] = 0
                return c
            return lax.fori_loop(pad_ref[e], pad_ref[N_EXPERTS + e], clear, carry)
        lax.fori_loop(0, N_EXPERTS, clear_padding, 0)

        def place(t):
            rowtok_ref[dest_ref[t]] = t
            rowtok_ref[dest_ref[N_TOK + t]] = t
        _unrolled(N_TOK, place)
        for s in range(GATHER_SLOTS - 1):
            early = jnp.minimum(s, n_valid - 1)
            _unrolled(ROW_BLOCK, lambda r, early=early, s=s: row_copy(early, s, r).start())

    expert = be_ref[b]
    fresh = (b == 0) | (expert != be_ref[jnp.maximum(b - 1, 0)])

    @pl.when(fresh & (b < n_valid))
    def _():
        for c in weight_copies(expert):
            c.wait()
        w1b_ref[...] = w1f_ref[...].astype(jnp.bfloat16)
        w3b_ref[...] = w3f_ref[...].astype(jnp.bfloat16)
        w2b_ref[...] = w2f_ref[...].astype(jnp.bfloat16)
        following = next_ref[expert]

        @pl.when(following >= 0)
        def _():
            for c in weight_copies(following):
                c.start()

    @pl.when(b < n_valid)
    def _():
        slot = lax.rem(b, GATHER_SLOTS)
        block_arrival(slot).wait()
        x = _from_slab(xbuf_ref.at[slot], ROW_BLOCK).astype(jnp.bfloat16)
        ahead = jnp.minimum(b + GATHER_SLOTS - 1, n_valid - 1)
        ahead_slot = lax.rem(b + GATHER_SLOTS - 1, GATHER_SLOTS)
        for r in range(ROW_BLOCK):
            row_copy(ahead, ahead_slot, r).start()
        h1 = jnp.dot(x, w1b_ref[...], preferred_element_type=jnp.float32)
        h3 = jnp.dot(x, w3b_ref[...], preferred_element_type=jnp.float32)
        act = (h1 * jax.nn.sigmoid(h1) * h3).astype(jnp.bfloat16)
        y = jnp.dot(act, w2b_ref[...], preferred_element_type=jnp.float32)
        _to_slab(yb_ref, y, ROW_BLOCK)

    @pl.when(b == n_valid - 1)
    def _():
        for s in range(1, GATHER_SLOTS):
            block_arrival(lax.rem(b + s, GATHER_SLOTS)).wait()

    @pl.when(b >= n_valid)
    def _():
        yb_ref[...] = jnp.zeros_like(yb_ref)


def _experts(layer, block_e, n_valid, dest_flat, pad_rows, next_expert, h2s, w1, w3, w2):
    hbm = pl.BlockSpec(memory_space=pl.ANY)
    w_in = [pltpu.VMEM((D_MODEL, D_EXPERT), dt) for dt in (jnp.float32, jnp.bfloat16)]
    w_out = [pltpu.VMEM((D_EXPERT, D_MODEL), dt) for dt in (jnp.float32, jnp.bfloat16)]
    return pl.pallas_call(
        functools.partial(_expert_kernel, layer=layer),
        grid_spec=pltpu.PrefetchScalarGridSpec(
            num_scalar_prefetch=5,
            grid=(N_BLOCKS,),
            in_specs=[hbm, hbm, hbm, hbm],
            out_specs=pl.BlockSpec((ROW_BLOCK * SLAB, LANES), lambda b, *_: (b, 0)),
            scratch_shapes=[
                pltpu.SMEM((N_ROWS,), jnp.int32),
                pltpu.VMEM((GATHER_SLOTS, ROW_BLOCK * SLAB, LANES), jnp.float32),
                pltpu.SemaphoreType.DMA((GATHER_SLOTS,)),
                w_in[0], w_in[0], w_out[0],
                pltpu.SemaphoreType.DMA((3,)),
                w_in[1], w_in[1], w_out[1],
            ],
        ),
        out_shape=jax.ShapeDtypeStruct((N_ROWS * SLAB, LANES), jnp.float32),
        compiler_params=_cparams(1),
        name="moe_experts",
    )(block_e, n_valid, dest_flat, pad_rows, next_expert, h2s, w1, w3, w2)


def _combine_kernel(pos_ref, x1_ref, rf_ref, yb_ref, out_ref, buf_ref, sem):
    i = pl.program_id(0)
    n = pl.num_programs(0)
    tm = COMB_TM

    def copy(tile, slot, j, k):
        p = pos_ref[k * N_TOK + tile * tm + j]
        return pltpu.make_async_copy(
            yb_ref.at[pl.ds(pl.multiple_of(p * SLAB, SLAB), SLAB)],
            buf_ref.at[slot, k, pl.ds(pl.multiple_of(j * SLAB, SLAB), SLAB)],
            sem.at[slot])

    def issue(tile, slot):
        def body(j):
            copy(tile, slot, j, 0).start()
            copy(tile, slot, j, 1).start()
        _unrolled(tm, body)

    def drain(tile, slot):
        def body(j):
            copy(tile, slot, j, 0).wait()
            copy(tile, slot, j, 1).wait()
        _unrolled(tm, body)

    slot = i % 2

    @pl.when(i == 0)
    def _():
        issue(0, 0)

    @pl.when(i + 1 < n)
    def _():
        issue(i + 1, 1 - slot)

    drain(i, slot)
    rf = rf_ref[...]
    y0 = _from_slab(buf_ref.at[slot, 0], tm)
    y1 = _from_slab(buf_ref.at[slot, 1], tm)
    out_ref[...] = x1_ref[...] + rf[:, 0:1] * y0 + rf[:, 1:2] * y1


def _combine(pos_flat, x1, rf, yb):
    return pl.pallas_call(
        _combine_kernel,
        grid_spec=pltpu.PrefetchScalarGridSpec(
            num_scalar_prefetch=1,
            grid=(N_TOK // COMB_TM,),
            in_specs=[
                pl.BlockSpec((COMB_TM, D_MODEL), lambda i, p: (i, 0)),
                pl.BlockSpec((COMB_TM, LANES), lambda i, p: (i, 0)),
                pl.BlockSpec(memory_space=pl.ANY),
            ],
            out_specs=pl.BlockSpec((COMB_TM, D_MODEL), lambda i, p: (i, 0)),
            scratch_shapes=[
                pltpu.VMEM((2, TOP_K, COMB_TM * SLAB, LANES), jnp.float32),
                pltpu.SemaphoreType.DMA((2,)),
            ],
        ),
        out_shape=jax.ShapeDtypeStruct((N_TOK, D_MODEL), jnp.float32),
        compiler_params=_cparams(1),
        name="moe_combine",
    )(pos_flat, x1, rf, yb)


def _combine_qkv_kernel(pos_ref, x1_ref, rf_ref, yb_ref, g1_ref, w_ref, qg_ref, kg_ref,
                        x2_ref, qk_ref, v_ref, buf_ref, sem):
    i = pl.program_id(0)
    n = pl.num_programs(0)
    tm = QKV_TS

    def copy(tile, slot, j, k):
        p = pos_ref[k * N_TOK + tile * tm + j]
        return pltpu.make_async_copy(
            yb_ref.at[pl.ds(pl.multiple_of(p * SLAB, SLAB), SLAB)],
            buf_ref.at[slot, k, pl.ds(j * SLAB, SLAB)],
            sem.at[slot])

    def plane_arrival(slot, k):
        return pltpu.make_async_copy(
            yb_ref.at[pl.ds(0, tm * SLAB)], buf_ref.at[slot, k], sem.at[slot])

    @pl.when(i == 0)
    def _():
        def first(j):
            copy(0, 0, j, 0).start()
            copy(0, 0, j, 1).start()
        _unrolled(tm, first)

    slot = i % 2
    plane_arrival(slot, 0).wait()
    plane_arrival(slot, 1).wait()
    rf = rf_ref[...]
    y0 = _from_slab(buf_ref.at[slot, 0], tm)
    y1 = _from_slab(buf_ref.at[slot, 1], tm)
    x2 = x1_ref[...] + rf[:, 0:1] * y0 + rf[:, 1:2] * y1
    x2_ref[...] = x2
    ahead = jnp.minimum(i + 1, n - 1)
    for j in range(tm):
        copy(ahead, 1 - slot, j, 0).start()
        copy(ahead, 1 - slot, j, 1).start()
    h = _rms(x2, g1_ref[...]).astype(jnp.bfloat16)
    qkv = jnp.dot(h, w_ref[...], preferred_element_type=jnp.float32)
    q_scale = HEAD_DIM ** -0.5 * LOG2E
    for c in range(N_HEADS):
        q_cols = slice(c * V_DIM, (c + 1) * V_DIM)
        k_cols = slice(D_MODEL + c * V_DIM, D_MODEL + (c + 1) * V_DIM)
        qk_ref[:, q_cols] = (_half_norm(qkv[:, q_cols], qg_ref[...]) * q_scale).astype(jnp.bfloat16)
        qk_ref[:, k_cols] = _half_norm(qkv[:, k_cols], kg_ref[...]).astype(jnp.bfloat16)
    v_ref[...] = qkv[:, 2 * D_MODEL:].astype(jnp.bfloat16)

    @pl.when(i == n - 1)
    def _():
        plane_arrival(1 - slot, 0).wait()
        plane_arrival(1 - slot, 1).wait()


def _combine_qkv(pos_flat, x1, rf, yb, g1, w_qkv, qg2, kg2):
    tile = lambda i, p: (i, 0)
    return pl.pallas_call(
        _combine_qkv_kernel,
        grid_spec=pltpu.PrefetchScalarGridSpec(
            num_scalar_prefetch=1,
            grid=(N_TOK // QKV_TS,),
            in_specs=[
                pl.BlockSpec((QKV_TS, D_MODEL), tile),
                pl.BlockSpec((QKV_TS, LANES), tile),
                pl.BlockSpec(memory_space=pl.ANY),
                _full((1, D_MODEL)),
                _full((D_MODEL, 3 * D_MODEL)),
                _full((1, V_DIM)), _full((1, V_DIM)),
            ],
            out_specs=(
                pl.BlockSpec((QKV_TS, D_MODEL), tile),
                pl.BlockSpec((QKV_TS, 2 * D_MODEL), tile),
                pl.BlockSpec((QKV_TS, D_MODEL), tile),
            ),
            scratch_shapes=[
                pltpu.VMEM((2, TOP_K, QKV_TS * SLAB, LANES), jnp.float32),
                pltpu.SemaphoreType.DMA((2,)),
            ],
        ),
        out_shape=(
            jax.ShapeDtypeStruct((N_TOK, D_MODEL), jnp.float32),
            jax.ShapeDtypeStruct((N_TOK, 2 * D_MODEL), jnp.bfloat16),
            jax.ShapeDtypeStruct((N_TOK, D_MODEL), jnp.bfloat16),
        ),
        compiler_params=_cparams(1),
        name="combine_qkv",
    )(pos_flat, x1, rf, yb, g1, w_qkv, qg2, kg2)


def _moe_experts(layer, h2s, rft, cnt, w1, w3, w2):
    experts = rft[2:4].astype(jnp.int32)
    rank = rft[4:6].astype(jnp.int32)
    counts = cnt[:N_EXPERTS, 0].astype(jnp.int32)
    padded = (counts + ROW_BLOCK - 1) // ROW_BLOCK * ROW_BLOCK
    pad_ends = jnp.cumsum(padded)
    pad_starts = pad_ends - padded
    ids = jnp.arange(N_EXPERTS, dtype=jnp.int32)
    is_e = experts[None] == ids[:, None, None]
    dest = (jnp.sum(jnp.where(is_e, pad_starts[:, None, None], 0), axis=0) + rank).reshape(-1)
    n_valid = (pad_ends[-1:] // ROW_BLOCK).astype(jnp.int32)
    block_row0 = jnp.arange(N_BLOCKS, dtype=jnp.int32) * ROW_BLOCK
    block_e = jnp.minimum(
        jnp.sum((pad_ends[None, :] <= block_row0[:, None]).astype(jnp.int32), axis=1),
        N_EXPERTS - 1).astype(jnp.int32)
    pad_rows = jnp.concatenate([pad_starts + counts, pad_ends]).astype(jnp.int32)
    later_nonempty = (counts > 0)[None, :] & (ids[None, :] > ids[:, None])
    following = jnp.min(jnp.where(later_nonempty, ids[None, :], N_EXPERTS), axis=1)
    next_expert = jnp.where(following < N_EXPERTS, following, -1).astype(jnp.int32)
    yb = _experts(layer, block_e, n_valid, dest, pad_rows, next_expert, h2s, w1, w3, w2)
    return dest, yb


def _router_params(w_grp, b_grp, w_exp, b_exp):
    wr = jnp.zeros((D_MODEL, LANES), jnp.float32)
    wr = wr.at[:, :N_EXPERTS].set(w_exp).at[:, GRP_LANE0:GRP_LANE0 + N_GROUPS].set(w_grp)
    br = jnp.zeros((1, LANES), jnp.float32)
    br = br.at[0, :N_EXPERTS].set(b_exp).at[0, GRP_LANE0:GRP_LANE0 + N_GROUPS].set(b_grp)
    w_hi = wr.astype(jnp.bfloat16)
    w_lo = (wr - w_hi.astype(jnp.float32)).astype(jnp.bfloat16)
    return jnp.concatenate([w_hi, w_lo], axis=1), br


def kernel(x, norm1_g, norm2_g, conv_w_in, conv_b_in, conv_dw, conv_dw_b, conv_ln_g, conv_ln_b, conv_w_out, conv_b_out, attn_w_qkv, attn_q_g, attn_k_g, attn_lq1, attn_lk1, attn_lq2, attn_lk2, attn_subln_g, attn_w_o, moe_w_grp, moe_b_grp, moe_w_exp, moe_b_exp, moe_w1, moe_w3, moe_w2):
    assert x.shape == (BATCH, SEQ, D_MODEL) and x.dtype == jnp.float32
    assert moe_w1.shape == (2, N_EXPERTS, D_MODEL, D_EXPERT) and conv_dw.shape[1] == CONV_WIDTH
    bf16 = jnp.bfloat16
    row = lambda a: a.reshape(1, -1)
    xf = x.reshape(N_TOK, D_MODEL)

    wr, br = _router_params(moe_w_grp[0], moe_b_grp[0], moe_w_exp[0], moe_b_exp[0])
    dw = jnp.zeros((HALO, D_MODEL), jnp.float32).at[:CONV_WIDTH].set(conv_dw[0])
    x1, h2s, rf, rft, cnt = _conv_layer(
        xf, row(norm1_g[0]), conv_w_in[0].astype(bf16), row(conv_b_in[0]), dw,
        row(conv_dw_b[0]), row(conv_ln_g[0]), row(conv_ln_b[0]),
        conv_w_out[0].astype(bf16), row(conv_b_out[0]), row(norm2_g[0]), wr, br)
    dest, yb = _moe_experts(0, h2s, rft, cnt, moe_w1, moe_w3, moe_w2)

    lambda_init = 0.8 - 0.6 * math.exp(-0.3 * 1)
    wr, br = _router_params(moe_w_grp[1], moe_b_grp[1], moe_w_exp[1], moe_b_exp[1])
    two = lambda g: jnp.concatenate([g, g]).reshape(1, V_DIM)
    xf, qk, v = _combine_qkv(dest, x1, rf, yb, row(norm1_g[1]), attn_w_qkv[0].astype(bf16),
                             two(attn_q_g[0]), two(attn_k_g[0]))
    lam_rows = jnp.zeros((SUBLANES, HEAD_DIM), jnp.float32)
    lam_rows = lam_rows.at[0].set(attn_lq1[0]).at[1].set(attn_lk1[0])
    lam_rows = lam_rows.at[2].set(attn_lq2[0]).at[3].set(attn_lk2[0])
    o = _attention(qk, v, lam_rows, row(attn_subln_g[0]), lambda_init)
    x1, h2s, rf, rft, cnt = _attn_post(xf, o, attn_w_o[0].astype(bf16), row(norm2_g[1]), wr, br)
    dest, yb = _moe_experts(1, h2s, rft, cnt, moe_w1, moe_w3, moe_w2)
    xf = _combine(dest, x1, rf, yb)
    return xf.reshape(BATCH, SEQ, D_MODEL)
```

```python
import functools
import math

import jax
import jax.numpy as jnp
from jax import lax
from jax.experimental import pallas as pl
from jax.experimental.pallas import tpu as pltpu

D_MODEL = 1024
BATCH = 8
SEQ = 2048
N_TOK = BATCH * SEQ
CHUNK = 64
CONV_WIDTH = 31
N_HEADS = 8
HEAD_DIM = 64
V_DIM = 128
N_GROUPS = 4
EXPERTS_PER_GROUP = 8
N_EXPERTS = 32
TOP_K = 2
D_EXPERT = 512
EPS = 1e-6
LOG2E = math.log2(math.e)

LANES = 128
SUBLANES = 8
SLAB = D_MODEL // LANES
VMEM_LIMIT = 56 * 1024 * 1024

CONV_TS = 512
HALO = 32
CONV_RC = 64
POST_TS = 512
QKV_TS = 512
ATT_TQ = 256
ROW_BLOCK = 256
GATHER_SLOTS = 9
N_ASSIGN = N_TOK * TOP_K
N_BLOCKS = N_ASSIGN // ROW_BLOCK + N_EXPERTS
N_ROWS = N_BLOCKS * ROW_BLOCK
COMB_TM = 256
DMA_UNROLL = 16
GRP_LANE0 = N_EXPERTS
ROUTE_ROWS = 40


def _cparams(n_axes):
    return pltpu.CompilerParams(
        dimension_semantics=("arbitrary",) * n_axes, vmem_limit_bytes=VMEM_LIMIT)


def _rms(x, g):
    return x * lax.rsqrt(jnp.mean(x * x, axis=-1, keepdims=True) + EPS) * g


def _to_slab(ref, val, rows):
    for j in range(SLAB):
        ref[pl.ds(j, rows, stride=SLAB), :] = val[:, j * LANES:(j + 1) * LANES]


def _from_slab(ref, rows):
    return jnp.concatenate(
        [ref[pl.ds(j, rows, stride=SLAB), :] for j in range(SLAB)], axis=-1)


def _residual_norm_route(x1, g2_ref, wr_ref, br_ref, run_ref,
                         x1_ref, h2s_ref, rf_ref, rft_ref, cnt_ref, rows):
    x1_ref[...] = x1
    h2 = _rms(x1, g2_ref[...])
    _to_slab(h2s_ref, h2, rows)

    hi = h2.astype(jnp.bfloat16)
    lo = (h2 - hi.astype(jnp.float32)).astype(jnp.bfloat16)
    ab = jnp.dot(hi, wr_ref[...], preferred_element_type=jnp.float32)
    c = jnp.dot(lo, wr_ref[:, 0:LANES], preferred_element_type=jnp.float32)
    logits = ab[:, 0:LANES] + ab[:, LANES:2 * LANES] + c + br_ref[...]
    lt = jnp.transpose(logits)[0:ROUTE_ROWS, :]
    sub = lax.broadcasted_iota(jnp.int32, (ROUTE_ROWS, rows), 0).astype(jnp.float32)
    neg = jnp.float32(-jnp.inf)
    big = jnp.float32(1e9)

    gmask = (sub >= GRP_LANE0) & (sub < GRP_LANE0 + N_GROUPS)
    gl = jnp.where(gmask, lt, neg)
    gmax = jnp.max(gl, axis=0, keepdims=True)
    gidx = jnp.min(jnp.where(gl == gmax, sub, big), axis=0, keepdims=True) - GRP_LANE0
    gsum = jnp.sum(jnp.where(gmask, jnp.exp(gl - gmax), 0.0), axis=0, keepdims=True)
    grp_p = 1.0 / gsum

    first_e = gidx * EXPERTS_PER_GROUP
    emask = (sub >= first_e) & (sub < first_e + EXPERTS_PER_GROUP)
    el = jnp.where(emask, lt, neg)
    m1 = jnp.max(el, axis=0, keepdims=True)
    i1 = jnp.min(jnp.where(el == m1, sub, big), axis=0, keepdims=True)
    el2 = jnp.where(sub == i1, neg, el)
    m2 = jnp.max(el2, axis=0, keepdims=True)
    i2 = jnp.min(jnp.where(el2 == m2, sub, big), axis=0, keepdims=True)
    t = jnp.exp(m2 - m1)
    inv = 1.0 / (1.0 + t)
    g_first = grp_p * inv
    g_second = grp_p * t * inv

    sel1 = sub == i1
    sel2 = sub == i2
    member = jnp.where(sel1 | sel2, 1.0, 0.0).astype(jnp.bfloat16)
    c_i = lax.broadcasted_iota(jnp.int32, (rows, rows), 0)
    r_i = lax.broadcasted_iota(jnp.int32, (rows, rows), 1)
    earlier = jnp.where(c_i < r_i, 1.0, 0.0).astype(jnp.bfloat16)
    run = run_ref[...]
    cum = (jnp.dot(member, earlier, preferred_element_type=jnp.float32)
           + jnp.concatenate([run] * (rows // LANES), axis=1))
    rank1 = jnp.sum(jnp.where(sel1, cum, 0.0), axis=0, keepdims=True)
    rank2 = jnp.sum(jnp.where(sel2, cum, 0.0), axis=0, keepdims=True)
    run_new = run + jnp.sum(member.astype(jnp.float32), axis=1, keepdims=True)
    run_ref[...] = run_new
    cnt_ref[...] = run_new

    sub8 = lax.broadcasted_iota(jnp.int32, (SUBLANES, rows), 0)
    out = jnp.where(sub8 == 0, g_first, 0.0)
    out = jnp.where(sub8 == 1, g_second, out)
    out = jnp.where(sub8 == 2, i1, out)
    out = jnp.where(sub8 == 3, i2, out)
    out = jnp.where(sub8 == 4, rank1, out)
    out = jnp.where(sub8 == 5, rank2, out)
    rft_ref[...] = out
    padded = jnp.concatenate([out, jnp.zeros((LANES - SUBLANES, rows), jnp.float32)], axis=0)
    rf_ref[...] = jnp.transpose(padded)


def _route_out_shapes():
    return (
        jax.ShapeDtypeStruct((N_TOK, D_MODEL), jnp.float32),
        jax.ShapeDtypeStruct((N_TOK * SLAB, LANES), jnp.float32),
        jax.ShapeDtypeStruct((N_TOK, LANES), jnp.float32),
        jax.ShapeDtypeStruct((SUBLANES, N_TOK), jnp.float32),
        jax.ShapeDtypeStruct((ROUTE_ROWS, LANES), jnp.float32),
    )


def _route_out_specs(ts, idx):
    return (
        pl.BlockSpec((ts, D_MODEL), lambda *a: (idx(*a), 0)),
        pl.BlockSpec((ts * SLAB, LANES), lambda *a: (idx(*a), 0)),
        pl.BlockSpec((ts, LANES), lambda *a: (idx(*a), 0)),
        pl.BlockSpec((SUBLANES, ts), lambda *a: (0, idx(*a))),
        pl.BlockSpec((ROUTE_ROWS, LANES), lambda *a: (0, 0)),
    )


def _full(shape):
    return pl.BlockSpec(shape, lambda *a: (0,) * len(shape))


def _conv_kernel(x_ref, g1_ref, win_ref, bin_ref, dw_ref, dwb_ref, lng_ref, lnb_ref,
                 wout_ref, bout_ref, g2_ref, wr_ref, br_ref,
                 x1_ref, h2s_ref, rf_ref, rft_ref, cnt_ref,
                 ext_ref, conv_ref, run_ref):
    b = pl.program_id(0)
    s = pl.program_id(1)
    ts = CONV_TS

    @pl.when((b == 0) & (s == 0))
    def _():
        run_ref[...] = jnp.zeros_like(run_ref)

    @pl.when(s == 0)
    def _():
        ext_ref[0:HALO, :] = jnp.zeros((HALO, D_MODEL), jnp.float32)

    x = x_ref[...]
    h = _rms(x, g1_ref[...]).astype(jnp.bfloat16)
    u = jnp.dot(h, win_ref[...], preferred_element_type=jnp.float32) + bin_ref[...]
    glu = u[:, :D_MODEL] * jax.nn.sigmoid(u[:, D_MODEL:])
    ext_ref[HALO:HALO + ts, :] = glu

    base = HALO - (CONV_WIDTH - 1)

    def lane_chunk(c, carry):
        cols = pl.ds(pl.multiple_of(c * LANES, LANES), LANES)
        taps = [dw_ref[pl.ds(k, 1), cols] for k in range(CONV_WIDTH)]
        for r0 in range(0, ts, CONV_RC):
            acc = None
            for r in range(SUBLANES):
                rows = CONV_RC + (SUBLANES if r else 0)
                part = None
                for q in range((base + CONV_WIDTH - 1) // SUBLANES + 1):
                    k = SUBLANES * q + r - base
                    if 0 <= k < CONV_WIDTH:
                        term = ext_ref[pl.ds(r0 + SUBLANES * q, rows), cols] * taps[k]
                        part = term if part is None else part + term
                part = part[r:r + CONV_RC, :] if r else part
                acc = part if acc is None else acc + part
            conv_ref[pl.ds(r0, CONV_RC), cols] = acc
        return carry

    lax.fori_loop(0, D_MODEL // LANES, lane_chunk, 0)
    ext_ref[0:HALO, :] = ext_ref[ts:ts + HALO, :]

    v = conv_ref[...] + dwb_ref[...]
    mu = jnp.mean(v, axis=-1, keepdims=True)
    vc = v - mu
    var = jnp.mean(vc * vc, axis=-1, keepdims=True)
    y = vc * lax.rsqrt(var + EPS) * lng_ref[...] + lnb_ref[...]
    y = (y * jax.nn.sigmoid(y)).astype(jnp.bfloat16)
    mix = jnp.dot(y, wout_ref[...], preferred_element_type=jnp.float32) + bout_ref[...]
    _residual_norm_route(x + mix, g2_ref, wr_ref, br_ref, run_ref,
                         x1_ref, h2s_ref, rf_ref, rft_ref, cnt_ref, ts)


def _conv_layer(x, g1, w_in, b_in, dw, dw_b, ln_g, ln_b, w_out, b_out, g2, wr, br):
    ns = SEQ // CONV_TS
    tile = lambda b, s: b * ns + s
    return pl.pallas_call(
        _conv_kernel,
        grid=(BATCH, ns),
        in_specs=[
            pl.BlockSpec((CONV_TS, D_MODEL), lambda b, s: (tile(b, s), 0)),
            _full((1, D_MODEL)),
            _full((D_MODEL, 2 * D_MODEL)),
            _full((1, 2 * D_MODEL)),
            _full((HALO, D_MODEL)),
            _full((1, D_MODEL)), _full((1, D_MODEL)), _full((1, D_MODEL)),
            _full((D_MODEL, D_MODEL)),
            _full((1, D_MODEL)), _full((1, D_MODEL)),
            _full((D_MODEL, 2 * LANES)), _full((1, LANES)),
        ],
        out_specs=_route_out_specs(CONV_TS, tile),
        out_shape=_route_out_shapes(),
        scratch_shapes=[
            pltpu.VMEM((HALO + CONV_TS, D_MODEL), jnp.float32),
            pltpu.VMEM((CONV_TS, D_MODEL), jnp.float32),
            pltpu.VMEM((ROUTE_ROWS, LANES), jnp.float32),
        ],
        compiler_params=_cparams(2),
        name="conv_mixer",
    )(x, g1, w_in, b_in, dw, dw_b, ln_g, ln_b, w_out, b_out, g2, wr, br)


def _half_norm(z, gain):
    lane = lax.broadcasted_iota(jnp.int32, z.shape, 1)
    first = lane < HEAD_DIM
    zz = z * z
    ss_a = jnp.sum(jnp.where(first, zz, 0.0), axis=1, keepdims=True)
    ss_b = jnp.sum(jnp.where(first, 0.0, zz), axis=1, keepdims=True)
    inv = jnp.where(first, lax.rsqrt(ss_a * (1.0 / HEAD_DIM) + EPS),
                    lax.rsqrt(ss_b * (1.0 / HEAD_DIM) + EPS))
    return z * inv * gain


def _attn_kernel(q_ref, k_ref, v_ref, lam_ref, sg_ref, o_ref,
                 qa_ref, qb_ref, v1_ref, s_ref, e_ref, part_ref, *, lambda_init):
    lp = lam_ref[...]
    lam = (jnp.exp(jnp.sum(lp[0:1, :] * lp[1:2, :], axis=1, keepdims=True))
           - jnp.exp(jnp.sum(lp[2:3, :] * lp[3:4, :], axis=1, keepdims=True))
           + lambda_init)

    q = q_ref[...]
    first = lax.broadcasted_iota(jnp.int32, (SEQ, V_DIM), 1) < HEAD_DIM
    qa_ref[...] = jnp.where(first, q, jnp.zeros_like(q))
    qb_ref[...] = jnp.where(first, jnp.zeros_like(q), q)

    ones_col = lax.broadcasted_iota(jnp.int32, (SEQ, V_DIM), 1) == 0
    v1_ref[:, 0:V_DIM] = v_ref[...]
    v1_ref[:, V_DIM:2 * V_DIM] = jnp.where(ones_col, 1.0, 0.0).astype(jnp.bfloat16)

    tq = ATT_TQ
    nt = (((1,), (1,)), ((), ()))
    visible = (lax.broadcasted_iota(jnp.int32, (tq, tq), 1) // CHUNK
               <= lax.broadcasted_iota(jnp.int32, (tq, tq), 0) // CHUNK)
    halves = (qa_ref, qb_ref)
    units = [(qi, h) for qi in range(SEQ // tq) for h in range(2)]

    def stage_scores(u, slot):
        qi, h = units[u]
        k0 = qi * tq
        q = halves[h][k0:k0 + tq, :]
        dg = lax.dot_general(q, k_ref[k0:k0 + tq, :], nt, preferred_element_type=jnp.float32)
        s_ref[slot, :, k0:k0 + tq] = jnp.where(visible, dg, -jnp.inf)
        if qi:
            s_ref[slot, :, 0:k0] = lax.dot_general(q, k_ref[0:k0, :], nt,
                                                   preferred_element_type=jnp.float32)

    def stage_numerators(u, slot):
        kend = (units[u][0] + 1) * tq
        sc = s_ref[slot, :, 0:kend]
        m = jnp.max(sc, axis=1, keepdims=True)
        e_ref[slot, :, 0:kend] = jnp.exp2(sc - m).astype(jnp.bfloat16)

    def stage_values(u, slot):
        qi, h = units[u]
        k0 = qi * tq
        kend = k0 + tq
        acc = jnp.dot(e_ref[slot, :, 0:kend], v1_ref[0:kend, :],
                      preferred_element_type=jnp.float32)
        attn = acc[:, 0:V_DIM] * (1.0 / acc[:, V_DIM:V_DIM + 1])
        if h == 0:
            part_ref[...] = attn
        else:
            o = part_ref[...] - lam * attn
            o = _rms(o, sg_ref[...]) * (1.0 - lambda_init)
            o_ref[k0:kend, :] = o.astype(jnp.bfloat16)

    for step in range(len(units) + 2):
        if step < len(units):
            stage_scores(step, step % 2)
        if 1 <= step <= len(units):
            stage_numerators(step - 1, (step - 1) % 2)
        if step >= 2:
            stage_values(step - 2, step % 2)


def _attention(qk, v, lam_rows, subln_g, lambda_init):
    return pl.pallas_call(
        functools.partial(_attn_kernel, lambda_init=lambda_init),
        grid=(BATCH, N_HEADS),
        in_specs=[
            pl.BlockSpec((SEQ, V_DIM), lambda b, h: (b, h)),
            pl.BlockSpec((SEQ, V_DIM), lambda b, h: (b, N_HEADS + h)),
            pl.BlockSpec((SEQ, V_DIM), lambda b, h: (b, h)),
            _full((SUBLANES, HEAD_DIM)),
            _full((1, V_DIM)),
        ],
        out_specs=pl.BlockSpec((SEQ, V_DIM), lambda b, h: (b, h)),
        out_shape=jax.ShapeDtypeStruct((N_TOK, D_MODEL), jnp.bfloat16),
        scratch_shapes=[pltpu.VMEM((SEQ, V_DIM), jnp.bfloat16)] * 2 + [
            pltpu.VMEM((SEQ, 2 * V_DIM), jnp.bfloat16),
            pltpu.VMEM((2, ATT_TQ, SEQ), jnp.float32),
            pltpu.VMEM((2, ATT_TQ, SEQ), jnp.bfloat16),
            pltpu.VMEM((ATT_TQ, V_DIM), jnp.float32),
        ],
        compiler_params=_cparams(2),
        name="diff_attention",
    )(qk, qk, v, lam_rows, subln_g)


def _post_kernel(x_ref, m_ref, w_ref, g2_ref, wr_ref, br_ref,
                 x1_ref, h2s_ref, rf_ref, rft_ref, cnt_ref, run_ref):
    @pl.when(pl.program_id(0) == 0)
    def _():
        run_ref[...] = jnp.zeros_like(run_ref)

    mix = jnp.dot(m_ref[...], w_ref[...], preferred_element_type=jnp.float32)
    _residual_norm_route(x_ref[...] + mix, g2_ref, wr_ref, br_ref, run_ref,
                         x1_ref, h2s_ref, rf_ref, rft_ref, cnt_ref, POST_TS)


def _attn_post(x, o, w_o, g2, wr, br):
    return pl.pallas_call(
        _post_kernel,
        grid=(N_TOK // POST_TS,),
        in_specs=[
            pl.BlockSpec((POST_TS, D_MODEL), lambda i: (i, 0)),
            pl.BlockSpec((POST_TS, D_MODEL), lambda i: (i, 0)),
            _full((D_MODEL, D_MODEL)),
            _full((1, D_MODEL)),
            _full((D_MODEL, 2 * LANES)), _full((1, LANES)),
        ],
        out_specs=_route_out_specs(POST_TS, lambda i: i),
        out_shape=_route_out_shapes(),
        scratch_shapes=[pltpu.VMEM((ROUTE_ROWS, LANES), jnp.float32)],
        compiler_params=_cparams(1),
        name="attn_post",
    )(x, o, w_o, g2, wr, br)


def _unrolled(n, body):
    def group(g, carry):
        for u in range(DMA_UNROLL):
            body(g * DMA_UNROLL + u)
        return carry
    lax.fori_loop(0, n // DMA_UNROLL, group, 0)


def _expert_kernel(be_ref, nv_ref, dest_ref, pad_ref, next_ref,
                   h2s_ref, w1_hbm, w3_hbm, w2_hbm, yb_ref,
                   rowtok_ref, xbuf_ref, sem, w1f_ref, w3f_ref, w2f_ref, wsem,
                   w1b_ref, w3b_ref, w2b_ref, *, layer):
    b = pl.program_id(0)
    n_valid = nv_ref[0]
    w_hbm = (w1_hbm, w3_hbm, w2_hbm)
    w_f32 = (w1f_ref, w3f_ref, w2f_ref)

    def weight_copies(expert):
        return [pltpu.make_async_copy(w_hbm[i].at[layer, expert], w_f32[i], wsem.at[i])
                for i in range(3)]

    def row_copy(block, slot, r):
        tok = rowtok_ref[block * ROW_BLOCK + r]
        return pltpu.make_async_copy(
            h2s_ref.at[pl.ds(pl.multiple_of(tok * SLAB, SLAB), SLAB)],
            xbuf_ref.at[slot, pl.ds(r * SLAB, SLAB)],
            sem.at[slot])

    def block_arrival(slot):
        return pltpu.make_async_copy(
            h2s_ref.at[pl.ds(0, ROW_BLOCK * SLAB)], xbuf_ref.at[slot], sem.at[slot])

    @pl.when(b == 0)
    def _():
        for c in weight_copies(be_ref[0]):
            c.start()
        def clear_padding(e, carry):
            def clear(j, c):
                rowtok_ref[j] = 0
                return c
            return lax.fori_loop(pad_ref[e], pad_ref[N_EXPERTS + e], clear, carry)
        lax.fori_loop(0, N_EXPERTS, clear_padding, 0)

        def place(t):
            rowtok_ref[dest_ref[t]] = t
            rowtok_ref[dest_ref[N_TOK + t]] = t
        _unrolled(N_TOK, place)
        for s in range(GATHER_SLOTS - 1):
            early = jnp.minimum(s, n_valid - 1)
            _unrolled(ROW_BLOCK, lambda r, early=early, s=s: row_copy(early, s, r).start())

    expert = be_ref[b]
    fresh = (b == 0) | (expert != be_ref[jnp.maximum(b - 1, 0)])

    @pl.when(fresh & (b < n_valid))
    def _():
        for c in weight_copies(expert):
            c.wait()
        w1b_ref[...] = w1f_ref[...].astype(jnp.bfloat16)
        w3b_ref[...] = w3f_ref[...].astype(jnp.bfloat16)
        w2b_ref[...] = w2f_ref[...].astype(jnp.bfloat16)
        following = next_ref[expert]

        @pl.when(following >= 0)
        def _():
            for c in weight_copies(following):
                c.start()

    @pl.when(b < n_valid)
    def _():
        slot = lax.rem(b, GATHER_SLOTS)
        block_arrival(slot).wait()
        x = _from_slab(xbuf_ref.at[slot], ROW_BLOCK).astype(jnp.bfloat16)
        ahead = jnp.minimum(b + GATHER_SLOTS - 1, n_valid - 1)
        ahead_slot = lax.rem(b + GATHER_SLOTS - 1, GATHER_SLOTS)
        for r in range(ROW_BLOCK):
            row_copy(ahead, ahead_slot, r).start()
        h1 = jnp.dot(x, w1b_ref[...], preferred_element_type=jnp.float32)
        h3 = jnp.dot(x, w3b_ref[...], preferred_element_type=jnp.float32)
        act = (h1 * jax.nn.sigmoid(h1) * h3).astype(jnp.bfloat16)
        y = jnp.dot(act, w2b_ref[...], preferred_element_type=jnp.float32)
        _to_slab(yb_ref, y, ROW_BLOCK)

    @pl.when(b == n_valid - 1)
    def _():
        for s in range(1, GATHER_SLOTS):
            block_arrival(lax.rem(b + s, GATHER_SLOTS)).wait()

    @pl.when(b >= n_valid)
    def _():
        yb_ref[...] = jnp.zeros_like(yb_ref)


def _experts(layer, block_e, n_valid, dest_flat, pad_rows, next_expert, h2s, w1, w3, w2):
    hbm = pl.BlockSpec(memory_space=pl.ANY)
    w_in = [pltpu.VMEM((D_MODEL, D_EXPERT), dt) for dt in (jnp.float32, jnp.bfloat16)]
    w_out = [pltpu.VMEM((D_EXPERT, D_MODEL), dt) for dt in (jnp.float32, jnp.bfloat16)]
    return pl.pallas_call(
        functools.partial(_expert_kernel, layer=layer),
        grid_spec=pltpu.PrefetchScalarGridSpec(
            num_scalar_prefetch=5,
            grid=(N_BLOCKS,),
            in_specs=[hbm, hbm, hbm, hbm],
            out_specs=pl.BlockSpec((ROW_BLOCK * SLAB, LANES), lambda b, *_: (b, 0)),
            scratch_shapes=[
                pltpu.SMEM((N_ROWS,), jnp.int32),
                pltpu.VMEM((GATHER_SLOTS, ROW_BLOCK * SLAB, LANES), jnp.float32),
                pltpu.SemaphoreType.DMA((GATHER_SLOTS,)),
                w_in[0], w_in[0], w_out[0],
                pltpu.SemaphoreType.DMA((3,)),
                w_in[1], w_in[1], w_out[1],
            ],
        ),
        out_shape=jax.ShapeDtypeStruct((N_ROWS * SLAB, LANES), jnp.float32),
        compiler_params=_cparams(1),
        name="moe_experts",
    )(block_e, n_valid, dest_flat, pad_rows, next_expert, h2s, w1, w3, w2)


def _combine_kernel(pos_ref, x1_ref, rf_ref, yb_ref, out_ref, buf_ref, sem):
    i = pl.program_id(0)
    n = pl.num_programs(0)
    tm = COMB_TM

    def copy(tile, slot, j, k):
        p = pos_ref[k * N_TOK + tile * tm + j]
        return pltpu.make_async_copy(
            yb_ref.at[pl.ds(pl.multiple_of(p * SLAB, SLAB), SLAB)],
            buf_ref.at[slot, k, pl.ds(pl.multiple_of(j * SLAB, SLAB), SLAB)],
            sem.at[slot])

    def issue(tile, slot):
        def body(j):
            copy(tile, slot, j, 0).start()
            copy(tile, slot, j, 1).start()
        _unrolled(tm, body)

    def drain(tile, slot):
        def body(j):
            copy(tile, slot, j, 0).wait()
            copy(tile, slot, j, 1).wait()
        _unrolled(tm, body)

    slot = i % 2

    @pl.when(i == 0)
    def _():
        issue(0, 0)

    @pl.when(i + 1 < n)
    def _():
        issue(i + 1, 1 - slot)

    drain(i, slot)
    rf = rf_ref[...]
    y0 = _from_slab(buf_ref.at[slot, 0], tm)
    y1 = _from_slab(buf_ref.at[slot, 1], tm)
    out_ref[...] = x1_ref[...] + rf[:, 0:1] * y0 + rf[:, 1:2] * y1


def _combine(pos_flat, x1, rf, yb):
    return pl.pallas_call(
        _combine_kernel,
        grid_spec=pltpu.PrefetchScalarGridSpec(
            num_scalar_prefetch=1,
            grid=(N_TOK // COMB_TM,),
            in_specs=[
                pl.BlockSpec((COMB_TM, D_MODEL), lambda i, p: (i, 0)),
                pl.BlockSpec((COMB_TM, LANES), lambda i, p: (i, 0)),
                pl.BlockSpec(memory_space=pl.ANY),
            ],
            out_specs=pl.BlockSpec((COMB_TM, D_MODEL), lambda i, p: (i, 0)),
            scratch_shapes=[
                pltpu.VMEM((2, TOP_K, COMB_TM * SLAB, LANES), jnp.float32),
                pltpu.SemaphoreType.DMA((2,)),
            ],
        ),
        out_shape=jax.ShapeDtypeStruct((N_TOK, D_MODEL), jnp.float32),
        compiler_params=_cparams(1),
        name="moe_combine",
    )(pos_flat, x1, rf, yb)


def _combine_qkv_kernel(pos_ref, x1_ref, rf_ref, yb_ref, g1_ref, w_ref, qg_ref, kg_ref,
                        x2_ref, qk_ref, v_ref, buf_ref, sem):
    i = pl.program_id(0)
    n = pl.num_programs(0)
    tm = QKV_TS

    def copy(tile, slot, j, k):
        p = pos_ref[k * N_TOK + tile * tm + j]
        return pltpu.make_async_copy(
            yb_ref.at[pl.ds(pl.multiple_of(p * SLAB, SLAB), SLAB)],
            buf_ref.at[slot, k, pl.ds(j * SLAB, SLAB)],
            sem.at[slot])

    def plane_arrival(slot, k):
        return pltpu.make_async_copy(
            yb_ref.at[pl.ds(0, tm * SLAB)], buf_ref.at[slot, k], sem.at[slot])

    @pl.when(i == 0)
    def _():
        def first(j):
            copy(0, 0, j, 0).start()
            copy(0, 0, j, 1).start()
        _unrolled(tm, first)

    slot = i % 2
    plane_arrival(slot, 0).wait()
    plane_arrival(slot, 1).wait()
    rf = rf_ref[...]
    y0 = _from_slab(buf_ref.at[slot, 0], tm)
    y1 = _from_slab(buf_ref.at[slot, 1], tm)
    x2 = x1_ref[...] + rf[:, 0:1] * y0 + rf[:, 1:2] * y1
    x2_ref[...] = x2
    ahead = jnp.minimum(i + 1, n - 1)
    for j in range(tm):
        copy(ahead, 1 - slot, j, 0).start()
        copy(ahead, 1 - slot, j, 1).start()
    h = _rms(x2, g1_ref[...]).astype(jnp.bfloat16)
    qkv = jnp.dot(h, w_ref[...], preferred_element_type=jnp.float32)
    q_scale = HEAD_DIM ** -0.5 * LOG2E
    for c in range(N_HEADS):
        q_cols = slice(c * V_DIM, (c + 1) * V_DIM)
        k_cols = slice(D_MODEL + c * V_DIM, D_MODEL + (c + 1) * V_DIM)
        qk_ref[:, q_cols] = (_half_norm(qkv[:, q_cols], qg_ref[...]) * q_scale).astype(jnp.bfloat16)
        qk_ref[:, k_cols] = _half_norm(qkv[:, k_cols], kg_ref[...]).astype(jnp.bfloat16)
    v_ref[...] = qkv[:, 2 * D_MODEL:].astype(jnp.bfloat16)

    @pl.when(i == n - 1)
    def _():
        plane_arrival(1 - slot, 0).wait()
        plane_arrival(1 - slot, 1).wait()


def _combine_qkv(pos_flat, x1, rf, yb, g1, w_qkv, qg2, kg2):
    tile = lambda i, p: (i, 0)
    return pl.pallas_call(
        _combine_qkv_kernel,
        grid_spec=pltpu.PrefetchScalarGridSpec(
            num_scalar_prefetch=1,
            grid=(N_TOK // QKV_TS,),
            in_specs=[
                pl.BlockSpec((QKV_TS, D_MODEL), tile),
                pl.BlockSpec((QKV_TS, LANES), tile),
                pl.BlockSpec(memory_space=pl.ANY),
                _full((1, D_MODEL)),
                _full((D_MODEL, 3 * D_MODEL)),
                _full((1, V_DIM)), _full((1, V_DIM)),
            ],
            out_specs=(
                pl.BlockSpec((QKV_TS, D_MODEL), tile),
                pl.BlockSpec((QKV_TS, 2 * D_MODEL), tile),
                pl.BlockSpec((QKV_TS, D_MODEL), tile),
            ),
            scratch_shapes=[
                pltpu.VMEM((2, TOP_K, QKV_TS * SLAB, LANES), jnp.float32),
                pltpu.SemaphoreType.DMA((2,)),
            ],
        ),
        out_shape=(
            jax.ShapeDtypeStruct((N_TOK, D_MODEL), jnp.float32),
            jax.ShapeDtypeStruct((N_TOK, 2 * D_MODEL), jnp.bfloat16),
            jax.ShapeDtypeStruct((N_TOK, D_MODEL), jnp.bfloat16),
        ),
        compiler_params=_cparams(1),
        name="combine_qkv",
    )(pos_flat, x1, rf, yb, g1, w_qkv, qg2, kg2)


def _moe_experts(layer, h2s, rft, cnt, w1, w3, w2):
    experts = rft[2:4].astype(jnp.int32)
    rank = rft[4:6].astype(jnp.int32)
    counts = cnt[:N_EXPERTS, 0].astype(jnp.int32)
    padded = (counts + ROW_BLOCK - 1) // ROW_BLOCK * ROW_BLOCK
    pad_ends = jnp.cumsum(padded)
    pad_starts = pad_ends - padded
    ids = jnp.arange(N_EXPERTS, dtype=jnp.int32)
    is_e = experts[None] == ids[:, None, None]
    dest = (jnp.sum(jnp.where(is_e, pad_starts[:, None, None], 0), axis=0) + rank).reshape(-1)
    n_valid = (pad_ends[-1:] // ROW_BLOCK).astype(jnp.int32)
    block_row0 = jnp.arange(N_BLOCKS, dtype=jnp.int32) * ROW_BLOCK
    block_e = jnp.minimum(
        jnp.sum((pad_ends[None, :] <= block_row0[:, None]).astype(jnp.int32), axis=1),
        N_EXPERTS - 1).astype(jnp.int32)
    pad_rows = jnp.concatenate([pad_starts + counts, pad_ends]).astype(jnp.int32)
    later_nonempty = (counts > 0)[None, :] & (ids[None, :] > ids[:, None])
    following = jnp.min(jnp.where(later_nonempty, ids[None, :], N_EXPERTS), axis=1)
    next_expert = jnp.where(following < N_EXPERTS, following, -1).astype(jnp.int32)
    yb = _experts(layer, block_e, n_valid, dest, pad_rows, next_expert, h2s, w1, w3, w2)
    return dest, yb


def _router_params(w_grp, b_grp, w_exp, b_exp):
    wr = jnp.zeros((D_MODEL, LANES), jnp.float32)
    wr = wr.at[:, :N_EXPERTS].set(w_exp).at[:, GRP_LANE0:GRP_LANE0 + N_GROUPS].set(w_grp)
    br = jnp.zeros((1, LANES), jnp.float32)
    br = br.at[0, :N_EXPERTS].set(b_exp).at[0, GRP_LANE0:GRP_LANE0 + N_GROUPS].set(b_grp)
    w_hi = wr.astype(jnp.bfloat16)
    w_lo = (wr - w_hi.astype(jnp.float32)).astype(jnp.bfloat16)
    return jnp.concatenate([w_hi, w_lo], axis=1), br


def kernel(x, norm1_g, norm2_g, conv_w_in, conv_b_in, conv_dw, conv_dw_b, conv_ln_g, conv_ln_b, conv_w_out, conv_b_out, attn_w_qkv, attn_q_g, attn_k_g, attn_lq1, attn_lk1, attn_lq2, attn_lk2, attn_subln_g, attn_w_o, moe_w_grp, moe_b_grp, moe_w_exp, moe_b_exp, moe_w1, moe_w3, moe_w2):
    assert x.shape == (BATCH, SEQ, D_MODEL) and x.dtype == jnp.float32
    assert moe_w1.shape == (2, N_EXPERTS, D_MODEL, D_EXPERT) and conv_dw.shape[1] == CONV_WIDTH
    bf16 = jnp.bfloat16
    row = lambda a: a.reshape(1, -1)
    xf = x.reshape(N_TOK, D_MODEL)

    wr, br = _router_params(moe_w_grp[0], moe_b_grp[0], moe_w_exp[0], moe_b_exp[0])
    dw = jnp.zeros((HALO, D_MODEL), jnp.float32).at[:CONV_WIDTH].set(conv_dw[0])
    x1, h2s, rf, rft, cnt = _conv_layer(
        xf, row(norm1_g[0]), conv_w_in[0].astype(bf16), row(conv_b_in[0]), dw,
        row(conv_dw_b[0]), row(conv_ln_g[0]), row(conv_ln_b[0]),
        conv_w_out[0].astype(bf16), row(conv_b_out[0]), row(norm2_g[0]), wr, br)
    dest, yb = _moe_experts(0, h2s, rft, cnt, moe_w1, moe_w3, moe_w2)

    lambda_init = 0.8 - 0.6 * math.exp(-0.3 * 1)
    wr, br = _router_params(moe_w_grp[1], moe_b_grp[1], moe_w_exp[1], moe_b_exp[1])
    two = lambda g: jnp.concatenate([g, g]).reshape(1, V_DIM)
    xf, qk, v = _combine_qkv(dest, x1, rf, yb, row(norm1_g[1]), attn_w_qkv[0].astype(bf16),
                             two(attn_q_g[0]), two(attn_k_g[0]))
    lam_rows = jnp.zeros((SUBLANES, HEAD_DIM), jnp.float32)
    lam_rows = lam_rows.at[0].set(attn_lq1[0]).at[1].set(attn_lk1[0])
    lam_rows = lam_rows.at[2].set(attn_lq2[0]).at[3].set(attn_lk2[0])
    o = _attention(qk, v, lam_rows, row(attn_subln_g[0]), lambda_init)
    x1, h2s, rf, rft, cnt = _attn_post(xf, o, attn_w_o[0].astype(bf16), row(norm2_g[1]), wr, br)
    dest, yb = _moe_experts(1, h2s, rft, cnt, moe_w1, moe_w3, moe_w2)
    xf = _combine(dest, x1, rf, yb)
    return xf.reshape(BATCH, SEQ, D_MODEL)
```

```python
import functools
import math

import jax
import jax.numpy as jnp
from jax import lax
from jax.experimental import pallas as pl
from jax.experimental.pallas import tpu as pltpu

D_MODEL = 1024
BATCH = 8
SEQ = 2048
N_TOK = BATCH * SEQ
CHUNK = 64
CONV_WIDTH = 31
N_HEADS = 8
HEAD_DIM = 64
V_DIM = 128
N_GROUPS = 4
EXPERTS_PER_GROUP = 8
N_EXPERTS = 32
TOP_K = 2
D_EXPERT = 512
EPS = 1e-6
LOG2E = math.log2(math.e)

LANES = 128
SUBLANES = 8
SLAB = D_MODEL // LANES
VMEM_LIMIT = 56 * 1024 * 1024

CONV_TS = 512
HALO = 32
CONV_RC = 64
POST_TS = 512
QKV_TS = 512
ATT_TQ = 256
ROW_BLOCK = 256
GATHER_SLOTS = 17
N_ASSIGN = N_TOK * TOP_K
N_BLOCKS = N_ASSIGN // ROW_BLOCK + N_EXPERTS
N_ROWS = N_BLOCKS * ROW_BLOCK
COMB_TM = 256
DMA_UNROLL = 16
GRP_LANE0 = N_EXPERTS
ROUTE_ROWS = 40


def _cparams(n_axes):
    return pltpu.CompilerParams(
        dimension_semantics=("arbitrary",) * n_axes, vmem_limit_bytes=VMEM_LIMIT)


def _rms(x, g):
    return x * lax.rsqrt(jnp.mean(x * x, axis=-1, keepdims=True) + EPS) * g


def _to_slab(ref, val, rows):
    for j in range(SLAB):
        ref[pl.ds(j, rows, stride=SLAB), :] = val[:, j * LANES:(j + 1) * LANES]


def _from_slab(ref, rows):
    return jnp.concatenate(
        [ref[pl.ds(j, rows, stride=SLAB), :] for j in range(SLAB)], axis=-1)


def _residual_norm_route(x1, g2_ref, wr_ref, br_ref, run_ref,
                         x1_ref, h2s_ref, rf_ref, rft_ref, cnt_ref, rows):
    x1_ref[...] = x1
    h2 = _rms(x1, g2_ref[...])
    _to_slab(h2s_ref, h2, rows)

    hi = h2.astype(jnp.bfloat16)
    lo = (h2 - hi.astype(jnp.float32)).astype(jnp.bfloat16)
    ab = jnp.dot(hi, wr_ref[...], preferred_element_type=jnp.float32)
    c = jnp.dot(lo, wr_ref[:, 0:LANES], preferred_element_type=jnp.float32)
    logits = ab[:, 0:LANES] + ab[:, LANES:2 * LANES] + c + br_ref[...]
    lt = jnp.transpose(logits)[0:ROUTE_ROWS, :]
    sub = lax.broadcasted_iota(jnp.int32, (ROUTE_ROWS, rows), 0).astype(jnp.float32)
    neg = jnp.float32(-jnp.inf)
    big = jnp.float32(1e9)

    gmask = (sub >= GRP_LANE0) & (sub < GRP_LANE0 + N_GROUPS)
    gl = jnp.where(gmask, lt, neg)
    gmax = jnp.max(gl, axis=0, keepdims=True)
    gidx = jnp.min(jnp.where(gl == gmax, sub, big), axis=0, keepdims=True) - GRP_LANE0
    gsum = jnp.sum(jnp.where(gmask, jnp.exp(gl - gmax), 0.0), axis=0, keepdims=True)
    grp_p = 1.0 / gsum

    first_e = gidx * EXPERTS_PER_GROUP
    emask = (sub >= first_e) & (sub < first_e + EXPERTS_PER_GROUP)
    el = jnp.where(emask, lt, neg)
    m1 = jnp.max(el, axis=0, keepdims=True)
    i1 = jnp.min(jnp.where(el == m1, sub, big), axis=0, keepdims=True)
    el2 = jnp.where(sub == i1, neg, el)
    m2 = jnp.max(el2, axis=0, keepdims=True)
    i2 = jnp.min(jnp.where(el2 == m2, sub, big), axis=0, keepdims=True)
    t = jnp.exp(m2 - m1)
    inv = 1.0 / (1.0 + t)
    g_first = grp_p * inv
    g_second = grp_p * t * inv

    sel1 = sub == i1
    sel2 = sub == i2
    member = jnp.where(sel1 | sel2, 1.0, 0.0).astype(jnp.bfloat16)
    c_i = lax.broadcasted_iota(jnp.int32, (rows, rows), 0)
    r_i = lax.broadcasted_iota(jnp.int32, (rows, rows), 1)
    earlier = jnp.where(c_i < r_i, 1.0, 0.0).astype(jnp.bfloat16)
    run = run_ref[...]
    cum = (jnp.dot(member, earlier, preferred_element_type=jnp.float32)
           + jnp.concatenate([run] * (rows // LANES), axis=1))
    rank1 = jnp.sum(jnp.where(sel1, cum, 0.0), axis=0, keepdims=True)
    rank2 = jnp.sum(jnp.where(sel2, cum, 0.0), axis=0, keepdims=True)
    run_new = run + jnp.sum(member.astype(jnp.float32), axis=1, keepdims=True)
    run_ref[...] = run_new
    cnt_ref[...] = run_new

    sub8 = lax.broadcasted_iota(jnp.int32, (SUBLANES, rows), 0)
    out = jnp.where(sub8 == 0, g_first, 0.0)
    out = jnp.where(sub8 == 1, g_second, out)
    out = jnp.where(sub8 == 2, i1, out)
    out = jnp.where(sub8 == 3, i2, out)
    out = jnp.where(sub8 == 4, rank1, out)
    out = jnp.where(sub8 == 5, rank2, out)
    rft_ref[...] = out
    padded = jnp.concatenate([out, jnp.zeros((LANES - SUBLANES, rows), jnp.float32)], axis=0)
    rf_ref[...] = jnp.transpose(padded)


def _route_out_shapes():
    return (
        jax.ShapeDtypeStruct((N_TOK, D_MODEL), jnp.float32),
        jax.ShapeDtypeStruct((N_TOK * SLAB, LANES), jnp.float32),
        jax.ShapeDtypeStruct((N_TOK, LANES), jnp.float32),
        jax.ShapeDtypeStruct((SUBLANES, N_TOK), jnp.float32),
        jax.ShapeDtypeStruct((ROUTE_ROWS, LANES), jnp.float32),
    )


def _route_out_specs(ts, idx):
    return (
        pl.BlockSpec((ts, D_MODEL), lambda *a: (idx(*a), 0)),
        pl.BlockSpec((ts * SLAB, LANES), lambda *a: (idx(*a), 0)),
        pl.BlockSpec((ts, LANES), lambda *a: (idx(*a), 0)),
        pl.BlockSpec((SUBLANES, ts), lambda *a: (0, idx(*a))),
        pl.BlockSpec((ROUTE_ROWS, LANES), lambda *a: (0, 0)),
    )


def _full(shape):
    return pl.BlockSpec(shape, lambda *a: (0,) * len(shape))


def _conv_kernel(x_ref, g1_ref, win_ref, bin_ref, dw_ref, dwb_ref, lng_ref, lnb_ref,
                 wout_ref, bout_ref, g2_ref, wr_ref, br_ref,
                 x1_ref, h2s_ref, rf_ref, rft_ref, cnt_ref,
                 ext_ref, conv_ref, run_ref):
    b = pl.program_id(0)
    s = pl.program_id(1)
    ts = CONV_TS

    @pl.when((b == 0) & (s == 0))
    def _():
        run_ref[...] = jnp.zeros_like(run_ref)

    @pl.when(s == 0)
    def _():
        ext_ref[0:HALO, :] = jnp.zeros((HALO, D_MODEL), jnp.float32)

    x = x_ref[...]
    h = _rms(x, g1_ref[...]).astype(jnp.bfloat16)
    u = jnp.dot(h, win_ref[...], preferred_element_type=jnp.float32) + bin_ref[...]
    glu = u[:, :D_MODEL] * jax.nn.sigmoid(u[:, D_MODEL:])
    ext_ref[HALO:HALO + ts, :] = glu

    base = HALO - (CONV_WIDTH - 1)

    def lane_chunk(c, carry):
        cols = pl.ds(pl.multiple_of(c * LANES, LANES), LANES)
        taps = [dw_ref[pl.ds(k, 1), cols] for k in range(CONV_WIDTH)]
        for r0 in range(0, ts, CONV_RC):
            acc = None
            for r in range(SUBLANES):
                rows = CONV_RC + (SUBLANES if r else 0)
                part = None
                for q in range((base + CONV_WIDTH - 1) // SUBLANES + 1):
                    k = SUBLANES * q + r - base
                    if 0 <= k < CONV_WIDTH:
                        term = ext_ref[pl.ds(r0 + SUBLANES * q, rows), cols] * taps[k]
                        part = term if part is None else part + term
                part = part[r:r + CONV_RC, :] if r else part
                acc = part if acc is None else acc + part
            conv_ref[pl.ds(r0, CONV_RC), cols] = acc
        return carry

    lax.fori_loop(0, D_MODEL // LANES, lane_chunk, 0)
    ext_ref[0:HALO, :] = ext_ref[ts:ts + HALO, :]

    v = conv_ref[...] + dwb_ref[...]
    mu = jnp.mean(v, axis=-1, keepdims=True)
    vc = v - mu
    var = jnp.mean(vc * vc, axis=-1, keepdims=True)
    y = vc * lax.rsqrt(var + EPS) * lng_ref[...] + lnb_ref[...]
    y = (y * jax.nn.sigmoid(y)).astype(jnp.bfloat16)
    mix = jnp.dot(y, wout_ref[...], preferred_element_type=jnp.float32) + bout_ref[...]
    _residual_norm_route(x + mix, g2_ref, wr_ref, br_ref, run_ref,
                         x1_ref, h2s_ref, rf_ref, rft_ref, cnt_ref, ts)


def _conv_layer(x, g1, w_in, b_in, dw, dw_b, ln_g, ln_b, w_out, b_out, g2, wr, br):
    ns = SEQ // CONV_TS
    tile = lambda b, s: b * ns + s
    return pl.pallas_call(
        _conv_kernel,
        grid=(BATCH, ns),
        in_specs=[
            pl.BlockSpec((CONV_TS, D_MODEL), lambda b, s: (tile(b, s), 0)),
            _full((1, D_MODEL)),
            _full((D_MODEL, 2 * D_MODEL)),
            _full((1, 2 * D_MODEL)),
            _full((HALO, D_MODEL)),
            _full((1, D_MODEL)), _full((1, D_MODEL)), _full((1, D_MODEL)),
            _full((D_MODEL, D_MODEL)),
            _full((1, D_MODEL)), _full((1, D_MODEL)),
            _full((D_MODEL, 2 * LANES)), _full((1, LANES)),
        ],
        out_specs=_route_out_specs(CONV_TS, tile),
        out_shape=_route_out_shapes(),
        scratch_shapes=[
            pltpu.VMEM((HALO + CONV_TS, D_MODEL), jnp.float32),
            pltpu.VMEM((CONV_TS, D_MODEL), jnp.float32),
            pltpu.VMEM((ROUTE_ROWS, LANES), jnp.float32),
        ],
        compiler_params=_cparams(2),
        name="conv_mixer",
    )(x, g1, w_in, b_in, dw, dw_b, ln_g, ln_b, w_out, b_out, g2, wr, br)


def _half_norm(z, gain):
    lane = lax.broadcasted_iota(jnp.int32, z.shape, 1)
    first = lane < HEAD_DIM
    zz = z * z
    ss_a = jnp.sum(jnp.where(first, zz, 0.0), axis=1, keepdims=True)
    ss_b = jnp.sum(jnp.where(first, 0.0, zz), axis=1, keepdims=True)
    inv = jnp.where(first, lax.rsqrt(ss_a * (1.0 / HEAD_DIM) + EPS),
                    lax.rsqrt(ss_b * (1.0 / HEAD_DIM) + EPS))
    return z * inv * gain


def _attn_kernel(q_ref, k_ref, v_ref, lam_ref, sg_ref, o_ref,
                 qa_ref, qb_ref, v1_ref, s_ref, e_ref, part_ref, *, lambda_init):
    lp = lam_ref[...]
    lam = (jnp.exp(jnp.sum(lp[0:1, :] * lp[1:2, :], axis=1, keepdims=True))
           - jnp.exp(jnp.sum(lp[2:3, :] * lp[3:4, :], axis=1, keepdims=True))
           + lambda_init)

    q = q_ref[...]
    first = lax.broadcasted_iota(jnp.int32, (SEQ, V_DIM), 1) < HEAD_DIM
    qa_ref[...] = jnp.where(first, q, jnp.zeros_like(q))
    qb_ref[...] = jnp.where(first, jnp.zeros_like(q), q)

    ones_col = lax.broadcasted_iota(jnp.int32, (SEQ, V_DIM), 1) == 0
    v1_ref[:, 0:V_DIM] = v_ref[...]
    v1_ref[:, V_DIM:2 * V_DIM] = jnp.where(ones_col, 1.0, 0.0).astype(jnp.bfloat16)

    tq = ATT_TQ
    nt = (((1,), (1,)), ((), ()))
    visible = (lax.broadcasted_iota(jnp.int32, (tq, tq), 1) // CHUNK
               <= lax.broadcasted_iota(jnp.int32, (tq, tq), 0) // CHUNK)
    halves = (qa_ref, qb_ref)
    units = [(qi, h) for qi in range(SEQ // tq) for h in range(2)]

    def stage_scores(u, slot):
        qi, h = units[u]
        k0 = qi * tq
        q = halves[h][k0:k0 + tq, :]
        dg = lax.dot_general(q, k_ref[k0:k0 + tq, :], nt, preferred_element_type=jnp.float32)
        s_ref[slot, :, k0:k0 + tq] = jnp.where(visible, dg, -jnp.inf)
        if qi:
            s_ref[slot, :, 0:k0] = lax.dot_general(q, k_ref[0:k0, :], nt,
                                                   preferred_element_type=jnp.float32)

    def stage_numerators(u, slot):
        kend = (units[u][0] + 1) * tq
        sc = s_ref[slot, :, 0:kend]
        m = jnp.max(sc, axis=1, keepdims=True)
        e_ref[slot, :, 0:kend] = jnp.exp2(sc - m).astype(jnp.bfloat16)

    def stage_values(u, slot):
        qi, h = units[u]
        k0 = qi * tq
        kend = k0 + tq
        acc = jnp.dot(e_ref[slot, :, 0:kend], v1_ref[0:kend, :],
                      preferred_element_type=jnp.float32)
        attn = acc[:, 0:V_DIM] * (1.0 / acc[:, V_DIM:V_DIM + 1])
        if h == 0:
            part_ref[...] = attn
        else:
            o = part_ref[...] - lam * attn
            o = _rms(o, sg_ref[...]) * (1.0 - lambda_init)
            o_ref[k0:kend, :] = o.astype(jnp.bfloat16)

    for step in range(len(units) + 2):
        if step < len(units):
            stage_scores(step, step % 2)
        if 1 <= step <= len(units):
            stage_numerators(step - 1, (step - 1) % 2)
        if step >= 2:
            stage_values(step - 2, step % 2)


def _attention(qk, v, lam_rows, subln_g, lambda_init):
    return pl.pallas_call(
        functools.partial(_attn_kernel, lambda_init=lambda_init),
        grid=(BATCH, N_HEADS),
        in_specs=[
            pl.BlockSpec((SEQ, V_DIM), lambda b, h: (b, h)),
            pl.BlockSpec((SEQ, V_DIM), lambda b, h: (b, N_HEADS + h)),
            pl.BlockSpec((SEQ, V_DIM), lambda b, h: (b, h)),
            _full((SUBLANES, HEAD_DIM)),
            _full((1, V_DIM)),
        ],
        out_specs=pl.BlockSpec((SEQ, V_DIM), lambda b, h: (b, h)),
        out_shape=jax.ShapeDtypeStruct((N_TOK, D_MODEL), jnp.bfloat16),
        scratch_shapes=[pltpu.VMEM((SEQ, V_DIM), jnp.bfloat16)] * 2 + [
            pltpu.VMEM((SEQ, 2 * V_DIM), jnp.bfloat16),
            pltpu.VMEM((2, ATT_TQ, SEQ), jnp.float32),
            pltpu.VMEM((2, ATT_TQ, SEQ), jnp.bfloat16),
            pltpu.VMEM((ATT_TQ, V_DIM), jnp.float32),
        ],
        compiler_params=_cparams(2),
        name="diff_attention",
    )(qk, qk, v, lam_rows, subln_g)


def _post_kernel(x_ref, m_ref, w_ref, g2_ref, wr_ref, br_ref,
                 x1_ref, h2s_ref, rf_ref, rft_ref, cnt_ref, run_ref):
    @pl.when(pl.program_id(0) == 0)
    def _():
        run_ref[...] = jnp.zeros_like(run_ref)

    mix = jnp.dot(m_ref[...], w_ref[...], preferred_element_type=jnp.float32)
    _residual_norm_route(x_ref[...] + mix, g2_ref, wr_ref, br_ref, run_ref,
                         x1_ref, h2s_ref, rf_ref, rft_ref, cnt_ref, POST_TS)


def _attn_post(x, o, w_o, g2, wr, br):
    return pl.pallas_call(
        _post_kernel,
        grid=(N_TOK // POST_TS,),
        in_specs=[
            pl.BlockSpec((POST_TS, D_MODEL), lambda i: (i, 0)),
            pl.BlockSpec((POST_TS, D_MODEL), lambda i: (i, 0)),
            _full((D_MODEL, D_MODEL)),
            _full((1, D_MODEL)),
            _full((D_MODEL, 2 * LANES)), _full((1, LANES)),
        ],
        out_specs=_route_out_specs(POST_TS, lambda i: i),
        out_shape=_route_out_shapes(),
        scratch_shapes=[pltpu.VMEM((ROUTE_ROWS, LANES), jnp.float32)],
        compiler_params=_cparams(1),
        name="attn_post",
    )(x, o, w_o, g2, wr, br)


def _unrolled(n, body):
    def group(g, carry):
        for u in range(DMA_UNROLL):
            body(g * DMA_UNROLL + u)
        return carry
    lax.fori_loop(0, n // DMA_UNROLL, group, 0)


def _expert_kernel(be_ref, nv_ref, dest_ref, pad_ref, next_ref,
                   h2s_ref, w1_hbm, w3_hbm, w2_hbm, yb_ref,
                   rowtok_ref, xbuf_ref, sem, w1f_ref, w3f_ref, w2f_ref, wsem,
                   w1b_ref, w3b_ref, w2b_ref, *, layer):
    b = pl.program_id(0)
    n_valid = nv_ref[0]
    w_hbm = (w1_hbm, w3_hbm, w2_hbm)
    w_f32 = (w1f_ref, w3f_ref, w2f_ref)

    def weight_copies(expert):
        return [pltpu.make_async_copy(w_hbm[i].at[layer, expert], w_f32[i], wsem.at[i])
                for i in range(3)]

    def row_copy(block, slot, r):
        tok = rowtok_ref[block * ROW_BLOCK + r]
        return pltpu.make_async_copy(
            h2s_ref.at[pl.ds(pl.multiple_of(tok * SLAB, SLAB), SLAB)],
            xbuf_ref.at[slot, pl.ds(r * SLAB, SLAB)],
            sem.at[slot])

    def block_arrival(slot):
        return pltpu.make_async_copy(
            h2s_ref.at[pl.ds(0, ROW_BLOCK * SLAB)], xbuf_ref.at[slot], sem.at[slot])

    @pl.when(b == 0)
    def _():
        for c in weight_copies(be_ref[0]):
            c.start()
        def clear_padding(e, carry):
            def clear(j, c):
                rowtok_ref[j] = 0
                return c
            return lax.fori_loop(pad_ref[e], pad_ref[N_EXPERTS + e], clear, carry)
        lax.fori_loop(0, N_EXPERTS, clear_padding, 0)

        def place(t):
            rowtok_ref[dest_ref[t]] = t
            rowtok_ref[dest_ref[N_TOK + t]] = t
        _unrolled(N_TOK, place)
        for s in range(GATHER_SLOTS - 1):
            early = jnp.minimum(s, n_valid - 1)
            _unrolled(ROW_BLOCK, lambda r, early=early, s=s: row_copy(early, s, r).start())

    expert = be_ref[b]
    fresh = (b == 0) | (expert != be_ref[jnp.maximum(b - 1, 0)])

    @pl.when(fresh & (b < n_valid))
    def _():
        for c in weight_copies(expert):
            c.wait()
        w1b_ref[...] = w1f_ref[...].astype(jnp.bfloat16)
        w3b_ref[...] = w3f_ref[...].astype(jnp.bfloat16)
        w2b_ref[...] = w2f_ref[...].astype(jnp.bfloat16)
        following = next_ref[expert]

        @pl.when(following >= 0)
        def _():
            for c in weight_copies(following):
                c.start()

    @pl.when(b < n_valid)
    def _():
        slot = lax.rem(b, GATHER_SLOTS)
        block_arrival(slot).wait()
        x = _from_slab(xbuf_ref.at[slot], ROW_BLOCK).astype(jnp.bfloat16)
        ahead = jnp.minimum(b + GATHER_SLOTS - 1, n_valid - 1)
        ahead_slot = lax.rem(b + GATHER_SLOTS - 1, GATHER_SLOTS)
        for r in range(ROW_BLOCK):
            row_copy(ahead, ahead_slot, r).start()
        h1 = jnp.dot(x, w1b_ref[...], preferred_element_type=jnp.float32)
        h3 = jnp.dot(x, w3b_ref[...], preferred_element_type=jnp.float32)
        act = (h1 * jax.nn.sigmoid(h1) * h3).astype(jnp.bfloat16)
        y = jnp.dot(act, w2b_ref[...], preferred_element_type=jnp.float32)
        _to_slab(yb_ref, y, ROW_BLOCK)

    @pl.when(b == n_valid - 1)
    def _():
        for s in range(1, GATHER_SLOTS):
            block_arrival(lax.rem(b + s, GATHER_SLOTS)).wait()

    @pl.when(b >= n_valid)
    def _():
        yb_ref[...] = jnp.zeros_like(yb_ref)


def _experts(layer, block_e, n_valid, dest_flat, pad_rows, next_expert, h2s, w1, w3, w2):
    hbm = pl.BlockSpec(memory_space=pl.ANY)
    w_in = [pltpu.VMEM((D_MODEL, D_EXPERT), dt) for dt in (jnp.float32, jnp.bfloat16)]
    w_out = [pltpu.VMEM((D_EXPERT, D_MODEL), dt) for dt in (jnp.float32, jnp.bfloat16)]
    return pl.pallas_call(
        functools.partial(_expert_kernel, layer=layer),
        grid_spec=pltpu.PrefetchScalarGridSpec(
            num_scalar_prefetch=5,
            grid=(N_BLOCKS,),
            in_specs=[hbm, hbm, hbm, hbm],
            out_specs=pl.BlockSpec((ROW_BLOCK * SLAB, LANES), lambda b, *_: (b, 0)),
            scratch_shapes=[
                pltpu.SMEM((N_ROWS,), jnp.int32),
                pltpu.VMEM((GATHER_SLOTS, ROW_BLOCK * SLAB, LANES), jnp.float32),
                pltpu.SemaphoreType.DMA((GATHER_SLOTS,)),
                w_in[0], w_in[0], w_out[0],
                pltpu.SemaphoreType.DMA((3,)),
                w_in[1], w_in[1], w_out[1],
            ],
        ),
        out_shape=jax.ShapeDtypeStruct((N_ROWS * SLAB, LANES), jnp.float32),
        compiler_params=_cparams(1),
        name="moe_experts",
    )(block_e, n_valid, dest_flat, pad_rows, next_expert, h2s, w1, w3, w2)


def _combine_kernel(pos_ref, x1_ref, rf_ref, yb_ref, out_ref, buf_ref, sem):
    i = pl.program_id(0)
    n = pl.num_programs(0)
    tm = COMB_TM

    def copy(tile, slot, j, k):
        p = pos_ref[k * N_TOK + tile * tm + j]
        return pltpu.make_async_copy(
            yb_ref.at[pl.ds(pl.multiple_of(p * SLAB, SLAB), SLAB)],
            buf_ref.at[slot, k, pl.ds(pl.multiple_of(j * SLAB, SLAB), SLAB)],
            sem.at[slot])

    def issue(tile, slot):
        def body(j):
            copy(tile, slot, j, 0).start()
            copy(tile, slot, j, 1).start()
        _unrolled(tm, body)

    def drain(tile, slot):
        def body(j):
            copy(tile, slot, j, 0).wait()
            copy(tile, slot, j, 1).wait()
        _unrolled(tm, body)

    slot = i % 2

    @pl.when(i == 0)
    def _():
        issue(0, 0)

    @pl.when(i + 1 < n)
    def _():
        issue(i + 1, 1 - slot)

    drain(i, slot)
    rf = rf_ref[...]
    y0 = _from_slab(buf_ref.at[slot, 0], tm)
    y1 = _from_slab(buf_ref.at[slot, 1], tm)
    out_ref[...] = x1_ref[...] + rf[:, 0:1] * y0 + rf[:, 1:2] * y1


def _combine(pos_flat, x1, rf, yb):
    return pl.pallas_call(
        _combine_kernel,
        grid_spec=pltpu.PrefetchScalarGridSpec(
            num_scalar_prefetch=1,
            grid=(N_TOK // COMB_TM,),
            in_specs=[
                pl.BlockSpec((COMB_TM, D_MODEL), lambda i, p: (i, 0)),
                pl.BlockSpec((COMB_TM, LANES), lambda i, p: (i, 0)),
                pl.BlockSpec(memory_space=pl.ANY),
            ],
            out_specs=pl.BlockSpec((COMB_TM, D_MODEL), lambda i, p: (i, 0)),
            scratch_shapes=[
                pltpu.VMEM((2, TOP_K, COMB_TM * SLAB, LANES), jnp.float32),
                pltpu.SemaphoreType.DMA((2,)),
            ],
        ),
        out_shape=jax.ShapeDtypeStruct((N_TOK, D_MODEL), jnp.float32),
        compiler_params=_cparams(1),
        name="moe_combine",
    )(pos_flat, x1, rf, yb)


def _combine_qkv_kernel(pos_ref, x1_ref, rf_ref, yb_ref, g1_ref, w_ref, qg_ref, kg_ref,
                        x2_ref, qk_ref, v_ref, buf_ref, sem):
    i = pl.program_id(0)
    n = pl.num_programs(0)
    tm = QKV_TS

    def copy(tile, slot, j, k):
        p = pos_ref[k * N_TOK + tile * tm + j]
        return pltpu.make_async_copy(
            yb_ref.at[pl.ds(pl.multiple_of(p * SLAB, SLAB), SLAB)],
            buf_ref.at[slot, k, pl.ds(j * SLAB, SLAB)],
            sem.at[slot])

    def plane_arrival(slot, k):
        return pltpu.make_async_copy(
            yb_ref.at[pl.ds(0, tm * SLAB)], buf_ref.at[slot, k], sem.at[slot])

    @pl.when(i == 0)
    def _():
        def first(j):
            copy(0, 0, j, 0).start()
            copy(0, 0, j, 1).start()
        _unrolled(tm, first)

    slot = i % 2
    plane_arrival(slot, 0).wait()
    plane_arrival(slot, 1).wait()
    rf = rf_ref[...]
    y0 = _from_slab(buf_ref.at[slot, 0], tm)
    y1 = _from_slab(buf_ref.at[slot, 1], tm)
    x2 = x1_ref[...] + rf[:, 0:1] * y0 + rf[:, 1:2] * y1
    x2_ref[...] = x2
    ahead = jnp.minimum(i + 1, n - 1)
    for j in range(tm):
        copy(ahead, 1 - slot, j, 0).start()
        copy(ahead, 1 - slot, j, 1).start()
    h = _rms(x2, g1_ref[...]).astype(jnp.bfloat16)
    qkv = jnp.dot(h, w_ref[...], preferred_element_type=jnp.float32)
    q_scale = HEAD_DIM ** -0.5 * LOG2E
    for c in range(N_HEADS):
        q_cols = slice(c * V_DIM, (c + 1) * V_DIM)
        k_cols = slice(D_MODEL + c * V_DIM, D_MODEL + (c + 1) * V_DIM)
        qk_ref[:, q_cols] = (_half_norm(qkv[:, q_cols], qg_ref[...]) * q_scale).astype(jnp.bfloat16)
        qk_ref[:, k_cols] = _half_norm(qkv[:, k_cols], kg_ref[...]).astype(jnp.bfloat16)
    v_ref[...] = qkv[:, 2 * D_MODEL:].astype(jnp.bfloat16)

    @pl.when(i == n - 1)
    def _():
        plane_arrival(1 - slot, 0).wait()
        plane_arrival(1 - slot, 1).wait()


def _combine_qkv(pos_flat, x1, rf, yb, g1, w_qkv, qg2, kg2):
    tile = lambda i, p: (i, 0)
    return pl.pallas_call(
        _combine_qkv_kernel,
        grid_spec=pltpu.PrefetchScalarGridSpec(
            num_scalar_prefetch=1,
            grid=(N_TOK // QKV_TS,),
            in_specs=[
                pl.BlockSpec((QKV_TS, D_MODEL), tile),
                pl.BlockSpec((QKV_TS, LANES), tile),
                pl.BlockSpec(memory_space=pl.ANY),
                _full((1, D_MODEL)),
                _full((D_MODEL, 3 * D_MODEL)),
                _full((1, V_DIM)), _full((1, V_DIM)),
            ],
            out_specs=(
                pl.BlockSpec((QKV_TS, D_MODEL), tile),
                pl.BlockSpec((QKV_TS, 2 * D_MODEL), tile),
                pl.BlockSpec((QKV_TS, D_MODEL), tile),
            ),
            scratch_shapes=[
                pltpu.VMEM((2, TOP_K, QKV_TS * SLAB, LANES), jnp.float32),
                pltpu.SemaphoreType.DMA((2,)),
            ],
        ),
        out_shape=(
            jax.ShapeDtypeStruct((N_TOK, D_MODEL), jnp.float32),
            jax.ShapeDtypeStruct((N_TOK, 2 * D_MODEL), jnp.bfloat16),
            jax.ShapeDtypeStruct((N_TOK, D_MODEL), jnp.bfloat16),
        ),
        compiler_params=_cparams(1),
        name="combine_qkv",
    )(pos_flat, x1, rf, yb, g1, w_qkv, qg2, kg2)


def _moe_experts(layer, h2s, rft, cnt, w1, w3, w2):
    experts = rft[2:4].astype(jnp.int32)
    rank = rft[4:6].astype(jnp.int32)
    counts = cnt[:N_EXPERTS, 0].astype(jnp.int32)
    padded = (counts + ROW_BLOCK - 1) // ROW_BLOCK * ROW_BLOCK
    pad_ends = jnp.cumsum(padded)
    pad_starts = pad_ends - padded
    ids = jnp.arange(N_EXPERTS, dtype=jnp.int32)
    is_e = experts[None] == ids[:, None, None]
    dest = (jnp.sum(jnp.where(is_e, pad_starts[:, None, None], 0), axis=0) + rank).reshape(-1)
    n_valid = (pad_ends[-1:] // ROW_BLOCK).astype(jnp.int32)
    block_row0 = jnp.arange(N_BLOCKS, dtype=jnp.int32) * ROW_BLOCK
    block_e = jnp.minimum(
        jnp.sum((pad_ends[None, :] <= block_row0[:, None]).astype(jnp.int32), axis=1),
        N_EXPERTS - 1).astype(jnp.int32)
    pad_rows = jnp.concatenate([pad_starts + counts, pad_ends]).astype(jnp.int32)
    later_nonempty = (counts > 0)[None, :] & (ids[None, :] > ids[:, None])
    following = jnp.min(jnp.where(later_nonempty, ids[None, :], N_EXPERTS), axis=1)
    next_expert = jnp.where(following < N_EXPERTS, following, -1).astype(jnp.int32)
    yb = _experts(layer, block_e, n_valid, dest, pad_rows, next_expert, h2s, w1, w3, w2)
    return dest, yb


def _router_params(w_grp, b_grp, w_exp, b_exp):
    wr = jnp.zeros((D_MODEL, LANES), jnp.float32)
    wr = wr.at[:, :N_EXPERTS].set(w_exp).at[:, GRP_LANE0:GRP_LANE0 + N_GROUPS].set(w_grp)
    br = jnp.zeros((1, LANES), jnp.float32)
    br = br.at[0, :N_EXPERTS].set(b_exp).at[0, GRP_LANE0:GRP_LANE0 + N_GROUPS].set(b_grp)
    w_hi = wr.astype(jnp.bfloat16)
    w_lo = (wr - w_hi.astype(jnp.float32)).astype(jnp.bfloat16)
    return jnp.concatenate([w_hi, w_lo], axis=1), br


def kernel(x, norm1_g, norm2_g, conv_w_in, conv_b_in, conv_dw, conv_dw_b, conv_ln_g, conv_ln_b, conv_w_out, conv_b_out, attn_w_qkv, attn_q_g, attn_k_g, attn_lq1, attn_lk1, attn_lq2, attn_lk2, attn_subln_g, attn_w_o, moe_w_grp, moe_b_grp, moe_w_exp, moe_b_exp, moe_w1, moe_w3, moe_w2):
    assert x.shape == (BATCH, SEQ, D_MODEL) and x.dtype == jnp.float32
    assert moe_w1.shape == (2, N_EXPERTS, D_MODEL, D_EXPERT) and conv_dw.shape[1] == CONV_WIDTH
    bf16 = jnp.bfloat16
    row = lambda a: a.reshape(1, -1)
    xf = x.reshape(N_TOK, D_MODEL)

    wr, br = _router_params(moe_w_grp[0], moe_b_grp[0], moe_w_exp[0], moe_b_exp[0])
    dw = jnp.zeros((HALO, D_MODEL), jnp.float32).at[:CONV_WIDTH].set(conv_dw[0])
    x1, h2s, rf, rft, cnt = _conv_layer(
        xf, row(norm1_g[0]), conv_w_in[0].astype(bf16), row(conv_b_in[0]), dw,
        row(conv_dw_b[0]), row(conv_ln_g[0]), row(conv_ln_b[0]),
        conv_w_out[0].astype(bf16), row(conv_b_out[0]), row(norm2_g[0]), wr, br)
    dest, yb = _moe_experts(0, h2s, rft, cnt, moe_w1, moe_w3, moe_w2)

    lambda_init = 0.8 - 0.6 * math.exp(-0.3 * 1)
    wr, br = _router_params(moe_w_grp[1], moe_b_grp[1], moe_w_exp[1], moe_b_exp[1])
    two = lambda g: jnp.concatenate([g, g]).reshape(1, V_DIM)
    xf, qk, v = _combine_qkv(dest, x1, rf, yb, row(norm1_g[1]), attn_w_qkv[0].astype(bf16),
                             two(attn_q_g[0]), two(attn_k_g[0]))
    lam_rows = jnp.zeros((SUBLANES, HEAD_DIM), jnp.float32)
    lam_rows = lam_rows.at[0].set(attn_lq1[0]).at[1].set(attn_lk1[0])
    lam_rows = lam_rows.at[2].set(attn_lq2[0]).at[3].set(attn_lk2[0])
    o = _attention(qk, v, lam_rows, row(attn_subln_g[0]), lambda_init)
    x1, h2s, rf, rft, cnt = _attn_post(xf, o, attn_w_o[0].astype(bf16), row(norm2_g[1]), wr, br)
    dest, yb = _moe_experts(1, h2s, rft, cnt, moe_w1, moe_w3, moe_w2)
    xf = _combine(dest, x1, rf, yb)
    return xf.reshape(BATCH, SEQ, D_MODEL)
```

```python
import functools
import math

import jax
import jax.numpy as jnp
from jax import lax
from jax.experimental import pallas as pl
from jax.experimental.pallas import tpu as pltpu

D_MODEL = 1024
BATCH = 8
SEQ = 2048
N_TOK = BATCH * SEQ
CHUNK = 64
CONV_WIDTH = 31
N_HEADS = 8
HEAD_DIM = 64
V_DIM = 128
N_GROUPS = 4
EXPERTS_PER_GROUP = 8
N_EXPERTS = 32
TOP_K = 2
D_EXPERT = 512
EPS = 1e-6
LOG2E = math.log2(math.e)

LANES = 128
SUBLANES = 8
SLAB = D_MODEL // LANES
VMEM_LIMIT = 56 * 1024 * 1024

CONV_TS = 512
HALO = 32
CONV_RC = 64
POST_TS = 512
QKV_TS = 512
ATT_TQ = 256
ROW_BLOCK = 256
GATHER_SLOTS = 9
N_ASSIGN = N_TOK * TOP_K
N_BLOCKS = N_ASSIGN // ROW_BLOCK + N_EXPERTS
N_ROWS = N_BLOCKS * ROW_BLOCK
COMB_TM = 256
COMBINE_SLOTS = 3
DMA_UNROLL = 16
GRP_LANE0 = N_EXPERTS
ROUTE_ROWS = 40


def _cparams(n_axes):
    return pltpu.CompilerParams(
        dimension_semantics=("arbitrary",) * n_axes, vmem_limit_bytes=VMEM_LIMIT)


def _rms(x, g):
    return x * lax.rsqrt(jnp.mean(x * x, axis=-1, keepdims=True) + EPS) * g


def _to_slab(ref, val, rows):
    for j in range(SLAB):
        ref[pl.ds(j, rows, stride=SLAB), :] = val[:, j * LANES:(j + 1) * LANES]


def _from_slab(ref, rows):
    return jnp.concatenate(
        [ref[pl.ds(j, rows, stride=SLAB), :] for j in range(SLAB)], axis=-1)


def _residual_norm_route(x1, g2_ref, wr_ref, br_ref, run_ref,
                         x1_ref, h2s_ref, rf_ref, rft_ref, cnt_ref, rows):
    x1_ref[...] = x1
    h2 = _rms(x1, g2_ref[...])
    _to_slab(h2s_ref, h2, rows)

    hi = h2.astype(jnp.bfloat16)
    lo = (h2 - hi.astype(jnp.float32)).astype(jnp.bfloat16)
    ab = jnp.dot(hi, wr_ref[...], preferred_element_type=jnp.float32)
    c = jnp.dot(lo, wr_ref[:, 0:LANES], preferred_element_type=jnp.float32)
    logits = ab[:, 0:LANES] + ab[:, LANES:2 * LANES] + c + br_ref[...]
    lt = jnp.transpose(logits)[0:ROUTE_ROWS, :]
    sub = lax.broadcasted_iota(jnp.int32, (ROUTE_ROWS, rows), 0).astype(jnp.float32)
    neg = jnp.float32(-jnp.inf)
    big = jnp.float32(1e9)

    gmask = (sub >= GRP_LANE0) & (sub < GRP_LANE0 + N_GROUPS)
    gl = jnp.where(gmask, lt, neg)
    gmax = jnp.max(gl, axis=0, keepdims=True)
    gidx = jnp.min(jnp.where(gl == gmax, sub, big), axis=0, keepdims=True) - GRP_LANE0
    gsum = jnp.sum(jnp.where(gmask, jnp.exp(gl - gmax), 0.0), axis=0, keepdims=True)
    grp_p = 1.0 / gsum

    first_e = gidx * EXPERTS_PER_GROUP
    emask = (sub >= first_e) & (sub < first_e + EXPERTS_PER_GROUP)
    el = jnp.where(emask, lt, neg)
    m1 = jnp.max(el, axis=0, keepdims=True)
    i1 = jnp.min(jnp.where(el == m1, sub, big), axis=0, keepdims=True)
    el2 = jnp.where(sub == i1, neg, el)
    m2 = jnp.max(el2, axis=0, keepdims=True)
    i2 = jnp.min(jnp.where(el2 == m2, sub, big), axis=0, keepdims=True)
    t = jnp.exp(m2 - m1)
    inv = 1.0 / (1.0 + t)
    g_first = grp_p * inv
    g_second = grp_p * t * inv

    sel1 = sub == i1
    sel2 = sub == i2
    member = jnp.where(sel1 | sel2, 1.0, 0.0).astype(jnp.bfloat16)
    c_i = lax.broadcasted_iota(jnp.int32, (rows, rows), 0)
    r_i = lax.broadcasted_iota(jnp.int32, (rows, rows), 1)
    earlier = jnp.where(c_i < r_i, 1.0, 0.0).astype(jnp.bfloat16)
    run = run_ref[...]
    cum = (jnp.dot(member, earlier, preferred_element_type=jnp.float32)
           + jnp.concatenate([run] * (rows // LANES), axis=1))
    rank1 = jnp.sum(jnp.where(sel1, cum, 0.0), axis=0, keepdims=True)
    rank2 = jnp.sum(jnp.where(sel2, cum, 0.0), axis=0, keepdims=True)
    run_new = run + jnp.sum(member.astype(jnp.float32), axis=1, keepdims=True)
    run_ref[...] = run_new
    cnt_ref[...] = run_new

    sub8 = lax.broadcasted_iota(jnp.int32, (SUBLANES, rows), 0)
    out = jnp.where(sub8 == 0, g_first, 0.0)
    out = jnp.where(sub8 == 1, g_second, out)
    out = jnp.where(sub8 == 2, i1, out)
    out = jnp.where(sub8 == 3, i2, out)
    out = jnp.where(sub8 == 4, rank1, out)
    out = jnp.where(sub8 == 5, rank2, out)
    rft_ref[...] = out
    padded = jnp.concatenate([out, jnp.zeros((LANES - SUBLANES, rows), jnp.float32)], axis=0)
    rf_ref[...] = jnp.transpose(padded)


def _route_out_shapes():
    return (
        jax.ShapeDtypeStruct((N_TOK, D_MODEL), jnp.float32),
        jax.ShapeDtypeStruct((N_TOK * SLAB, LANES), jnp.float32),
        jax.ShapeDtypeStruct((N_TOK, LANES), jnp.float32),
        jax.ShapeDtypeStruct((SUBLANES, N_TOK), jnp.float32),
        jax.ShapeDtypeStruct((ROUTE_ROWS, LANES), jnp.float32),
    )


def _route_out_specs(ts, idx):
    return (
        pl.BlockSpec((ts, D_MODEL), lambda *a: (idx(*a), 0)),
        pl.BlockSpec((ts * SLAB, LANES), lambda *a: (idx(*a), 0)),
        pl.BlockSpec((ts, LANES), lambda *a: (idx(*a), 0)),
        pl.BlockSpec((SUBLANES, ts), lambda *a: (0, idx(*a))),
        pl.BlockSpec((ROUTE_ROWS, LANES), lambda *a: (0, 0)),
    )


def _full(shape):
    return pl.BlockSpec(shape, lambda *a: (0,) * len(shape))


def _conv_kernel(x_ref, g1_ref, win_ref, bin_ref, dw_ref, dwb_ref, lng_ref, lnb_ref,
                 wout_ref, bout_ref, g2_ref, wr_ref, br_ref,
                 x1_ref, h2s_ref, rf_ref, rft_ref, cnt_ref,
                 ext_ref, conv_ref, run_ref):
    b = pl.program_id(0)
    s = pl.program_id(1)
    ts = CONV_TS

    @pl.when((b == 0) & (s == 0))
    def _():
        run_ref[...] = jnp.zeros_like(run_ref)

    @pl.when(s == 0)
    def _():
        ext_ref[0:HALO, :] = jnp.zeros((HALO, D_MODEL), jnp.float32)

    x = x_ref[...]
    h = _rms(x, g1_ref[...]).astype(jnp.bfloat16)
    u = jnp.dot(h, win_ref[...], preferred_element_type=jnp.float32) + bin_ref[...]
    glu = u[:, :D_MODEL] * jax.nn.sigmoid(u[:, D_MODEL:])
    ext_ref[HALO:HALO + ts, :] = glu

    base = HALO - (CONV_WIDTH - 1)

    def lane_chunk(c, carry):
        cols = pl.ds(pl.multiple_of(c * LANES, LANES), LANES)
        taps = [dw_ref[pl.ds(k, 1), cols] for k in range(CONV_WIDTH)]
        for r0 in range(0, ts, CONV_RC):
            acc = None
            for r in range(SUBLANES):
                rows = CONV_RC + (SUBLANES if r else 0)
                part = None
                for q in range((base + CONV_WIDTH - 1) // SUBLANES + 1):
                    k = SUBLANES * q + r - base
                    if 0 <= k < CONV_WIDTH:
                        term = ext_ref[pl.ds(r0 + SUBLANES * q, rows), cols] * taps[k]
                        part = term if part is None else part + term
                part = part[r:r + CONV_RC, :] if r else part
                acc = part if acc is None else acc + part
            conv_ref[pl.ds(r0, CONV_RC), cols] = acc
        return carry

    lax.fori_loop(0, D_MODEL // LANES, lane_chunk, 0)
    ext_ref[0:HALO, :] = ext_ref[ts:ts + HALO, :]

    v = conv_ref[...] + dwb_ref[...]
    mu = jnp.mean(v, axis=-1, keepdims=True)
    vc = v - mu
    var = jnp.mean(vc * vc, axis=-1, keepdims=True)
    y = vc * lax.rsqrt(var + EPS) * lng_ref[...] + lnb_ref[...]
    y = (y * jax.nn.sigmoid(y)).astype(jnp.bfloat16)
    mix = jnp.dot(y, wout_ref[...], preferred_element_type=jnp.float32) + bout_ref[...]
    _residual_norm_route(x + mix, g2_ref, wr_ref, br_ref, run_ref,
                         x1_ref, h2s_ref, rf_ref, rft_ref, cnt_ref, ts)


def _conv_layer(x, g1, w_in, b_in, dw, dw_b, ln_g, ln_b, w_out, b_out, g2, wr, br):
    ns = SEQ // CONV_TS
    tile = lambda b, s: b * ns + s
    return pl.pallas_call(
        _conv_kernel,
        grid=(BATCH, ns),
        in_specs=[
            pl.BlockSpec((CONV_TS, D_MODEL), lambda b, s: (tile(b, s), 0)),
            _full((1, D_MODEL)),
            _full((D_MODEL, 2 * D_MODEL)),
            _full((1, 2 * D_MODEL)),
            _full((HALO, D_MODEL)),
            _full((1, D_MODEL)), _full((1, D_MODEL)), _full((1, D_MODEL)),
            _full((D_MODEL, D_MODEL)),
            _full((1, D_MODEL)), _full((1, D_MODEL)),
            _full((D_MODEL, 2 * LANES)), _full((1, LANES)),
        ],
        out_specs=_route_out_specs(CONV_TS, tile),
        out_shape=_route_out_shapes(),
        scratch_shapes=[
            pltpu.VMEM((HALO + CONV_TS, D_MODEL), jnp.float32),
            pltpu.VMEM((CONV_TS, D_MODEL), jnp.float32),
            pltpu.VMEM((ROUTE_ROWS, LANES), jnp.float32),
        ],
        compiler_params=_cparams(2),
        name="conv_mixer",
    )(x, g1, w_in, b_in, dw, dw_b, ln_g, ln_b, w_out, b_out, g2, wr, br)


def _half_norm(z, gain):
    lane = lax.broadcasted_iota(jnp.int32, z.shape, 1)
    first = lane < HEAD_DIM
    zz = z * z
    ss_a = jnp.sum(jnp.where(first, zz, 0.0), axis=1, keepdims=True)
    ss_b = jnp.sum(jnp.where(first, 0.0, zz), axis=1, keepdims=True)
    inv = jnp.where(first, lax.rsqrt(ss_a * (1.0 / HEAD_DIM) + EPS),
                    lax.rsqrt(ss_b * (1.0 / HEAD_DIM) + EPS))
    return z * inv * gain


def _attn_kernel(q_ref, k_ref, v_ref, lam_ref, sg_ref, o_ref,
                 qa_ref, qb_ref, v1_ref, s_ref, e_ref, part_ref, *, lambda_init):
    lp = lam_ref[...]
    lam = (jnp.exp(jnp.sum(lp[0:1, :] * lp[1:2, :], axis=1, keepdims=True))
           - jnp.exp(jnp.sum(lp[2:3, :] * lp[3:4, :], axis=1, keepdims=True))
           + lambda_init)

    q = q_ref[...]
    first = lax.broadcasted_iota(jnp.int32, (SEQ, V_DIM), 1) < HEAD_DIM
    qa_ref[...] = jnp.where(first, q, jnp.zeros_like(q))
    qb_ref[...] = jnp.where(first, jnp.zeros_like(q), q)

    ones_col = lax.broadcasted_iota(jnp.int32, (SEQ, V_DIM), 1) == 0
    v1_ref[:, 0:V_DIM] = v_ref[...]
    v1_ref[:, V_DIM:2 * V_DIM] = jnp.where(ones_col, 1.0, 0.0).astype(jnp.bfloat16)

    tq = ATT_TQ
    nt = (((1,), (1,)), ((), ()))
    visible = (lax.broadcasted_iota(jnp.int32, (tq, tq), 1) // CHUNK
               <= lax.broadcasted_iota(jnp.int32, (tq, tq), 0) // CHUNK)
    halves = (qa_ref, qb_ref)
    units = [(qi, h) for qi in range(SEQ // tq) for h in range(2)]

    def stage_scores(u, slot):
        qi, h = units[u]
        k0 = qi * tq
        q = halves[h][k0:k0 + tq, :]
        dg = lax.dot_general(q, k_ref[k0:k0 + tq, :], nt, preferred_element_type=jnp.float32)
        s_ref[slot, :, k0:k0 + tq] = jnp.where(visible, dg, -jnp.inf)
        if qi:
            s_ref[slot, :, 0:k0] = lax.dot_general(q, k_ref[0:k0, :], nt,
                                                   preferred_element_type=jnp.float32)

    def stage_numerators(u, slot):
        kend = (units[u][0] + 1) * tq
        sc = s_ref[slot, :, 0:kend]
        m = jnp.max(sc, axis=1, keepdims=True)
        e_ref[slot, :, 0:kend] = jnp.exp2(sc - m).astype(jnp.bfloat16)

    def stage_values(u, slot):
        qi, h = units[u]
        k0 = qi * tq
        kend = k0 + tq
        acc = jnp.dot(e_ref[slot, :, 0:kend], v1_ref[0:kend, :],
                      preferred_element_type=jnp.float32)
        attn = acc[:, 0:V_DIM] * (1.0 / acc[:, V_DIM:V_DIM + 1])
        if h == 0:
            part_ref[...] = attn
        else:
            o = part_ref[...] - lam * attn
            o = _rms(o, sg_ref[...]) * (1.0 - lambda_init)
            o_ref[k0:kend, :] = o.astype(jnp.bfloat16)

    for step in range(len(units) + 2):
        if step < len(units):
            stage_scores(step, step % 2)
        if 1 <= step <= len(units):
            stage_numerators(step - 1, (step - 1) % 2)
        if step >= 2:
            stage_values(step - 2, step % 2)


def _attention(qk, v, lam_rows, subln_g, lambda_init):
    return pl.pallas_call(
        functools.partial(_attn_kernel, lambda_init=lambda_init),
        grid=(BATCH, N_HEADS),
        in_specs=[
            pl.BlockSpec((SEQ, V_DIM), lambda b, h: (b, h)),
            pl.BlockSpec((SEQ, V_DIM), lambda b, h: (b, N_HEADS + h)),
            pl.BlockSpec((SEQ, V_DIM), lambda b, h: (b, h)),
            _full((SUBLANES, HEAD_DIM)),
            _full((1, V_DIM)),
        ],
        out_specs=pl.BlockSpec((SEQ, V_DIM), lambda b, h: (b, h)),
        out_shape=jax.ShapeDtypeStruct((N_TOK, D_MODEL), jnp.bfloat16),
        scratch_shapes=[pltpu.VMEM((SEQ, V_DIM), jnp.bfloat16)] * 2 + [
            pltpu.VMEM((SEQ, 2 * V_DIM), jnp.bfloat16),
            pltpu.VMEM((2, ATT_TQ, SEQ), jnp.float32),
            pltpu.VMEM((2, ATT_TQ, SEQ), jnp.bfloat16),
            pltpu.VMEM((ATT_TQ, V_DIM), jnp.float32),
        ],
        compiler_params=_cparams(2),
        name="diff_attention",
    )(qk, qk, v, lam_rows, subln_g)


def _post_kernel(x_ref, m_ref, w_ref, g2_ref, wr_ref, br_ref,
                 x1_ref, h2s_ref, rf_ref, rft_ref, cnt_ref, run_ref):
    @pl.when(pl.program_id(0) == 0)
    def _():
        run_ref[...] = jnp.zeros_like(run_ref)

    mix = jnp.dot(m_ref[...], w_ref[...], preferred_element_type=jnp.float32)
    _residual_norm_route(x_ref[...] + mix, g2_ref, wr_ref, br_ref, run_ref,
                         x1_ref, h2s_ref, rf_ref, rft_ref, cnt_ref, POST_TS)


def _attn_post(x, o, w_o, g2, wr, br):
    return pl.pallas_call(
        _post_kernel,
        grid=(N_TOK // POST_TS,),
        in_specs=[
            pl.BlockSpec((POST_TS, D_MODEL), lambda i: (i, 0)),
            pl.BlockSpec((POST_TS, D_MODEL), lambda i: (i, 0)),
            _full((D_MODEL, D_MODEL)),
            _full((1, D_MODEL)),
            _full((D_MODEL, 2 * LANES)), _full((1, LANES)),
        ],
        out_specs=_route_out_specs(POST_TS, lambda i: i),
        out_shape=_route_out_shapes(),
        scratch_shapes=[pltpu.VMEM((ROUTE_ROWS, LANES), jnp.float32)],
        compiler_params=_cparams(1),
        name="attn_post",
    )(x, o, w_o, g2, wr, br)


def _unrolled(n, body):
    def group(g, carry):
        for u in range(DMA_UNROLL):
            body(g * DMA_UNROLL + u)
        return carry
    lax.fori_loop(0, n // DMA_UNROLL, group, 0)


def _expert_kernel(be_ref, nv_ref, dest_ref, pad_ref, next_ref,
                   h2s_ref, w1_hbm, w3_hbm, w2_hbm, yb_ref,
                   rowtok_ref, xbuf_ref, sem, w1f_ref, w3f_ref, w2f_ref, wsem,
                   w1b_ref, w3b_ref, w2b_ref, *, layer):
    b = pl.program_id(0)
    n_valid = nv_ref[0]
    w_hbm = (w1_hbm, w3_hbm, w2_hbm)
    w_f32 = (w1f_ref, w3f_ref, w2f_ref)

    def weight_copies(expert):
        return [pltpu.make_async_copy(w_hbm[i].at[layer, expert], w_f32[i], wsem.at[i])
                for i in range(3)]

    def row_copy(block, slot, r):
        tok = rowtok_ref[block * ROW_BLOCK + r]
        return pltpu.make_async_copy(
            h2s_ref.at[pl.ds(pl.multiple_of(tok * SLAB, SLAB), SLAB)],
            xbuf_ref.at[slot, pl.ds(r * SLAB, SLAB)],
            sem.at[slot])

    def block_arrival(slot):
        return pltpu.make_async_copy(
            h2s_ref.at[pl.ds(0, ROW_BLOCK * SLAB)], xbuf_ref.at[slot], sem.at[slot])

    @pl.when(b == 0)
    def _():
        for c in weight_copies(be_ref[0]):
            c.start()
        def clear_padding(e, carry):
            def clear(j, c):
                rowtok_ref[j] = 0
                return c
            return lax.fori_loop(pad_ref[e], pad_ref[N_EXPERTS + e], clear, carry)
        lax.fori_loop(0, N_EXPERTS, clear_padding, 0)

        def place(t):
            rowtok_ref[dest_ref[t]] = t
            rowtok_ref[dest_ref[N_TOK + t]] = t
        _unrolled(N_TOK, place)
        for s in range(GATHER_SLOTS - 1):
            early = jnp.minimum(s, n_valid - 1)
            _unrolled(ROW_BLOCK, lambda r, early=early, s=s: row_copy(early, s, r).start())

    expert = be_ref[b]
    fresh = (b == 0) | (expert != be_ref[jnp.maximum(b - 1, 0)])

    @pl.when(fresh & (b < n_valid))
    def _():
        for c in weight_copies(expert):
            c.wait()
        w1b_ref[...] = w1f_ref[...].astype(jnp.bfloat16)
        w3b_ref[...] = w3f_ref[...].astype(jnp.bfloat16)
        w2b_ref[...] = w2f_ref[...].astype(jnp.bfloat16)
        following = next_ref[expert]

        @pl.when(following >= 0)
        def _():
            for c in weight_copies(following):
                c.start()

    @pl.when(b < n_valid)
    def _():
        slot = lax.rem(b, GATHER_SLOTS)
        block_arrival(slot).wait()
        x = _from_slab(xbuf_ref.at[slot], ROW_BLOCK).astype(jnp.bfloat16)
        ahead = jnp.minimum(b + GATHER_SLOTS - 1, n_valid - 1)
        ahead_slot = lax.rem(b + GATHER_SLOTS - 1, GATHER_SLOTS)
        for r in range(ROW_BLOCK):
            row_copy(ahead, ahead_slot, r).start()
        h1 = jnp.dot(x, w1b_ref[...], preferred_element_type=jnp.float32)
        h3 = jnp.dot(x, w3b_ref[...], preferred_element_type=jnp.float32)
        act = (h1 * jax.nn.sigmoid(h1) * h3).astype(jnp.bfloat16)
        y = jnp.dot(act, w2b_ref[...], preferred_element_type=jnp.float32)
        _to_slab(yb_ref, y, ROW_BLOCK)

    @pl.when(b == n_valid - 1)
    def _():
        for s in range(1, GATHER_SLOTS):
            block_arrival(lax.rem(b + s, GATHER_SLOTS)).wait()

    @pl.when(b >= n_valid)
    def _():
        yb_ref[...] = jnp.zeros_like(yb_ref)


def _experts(layer, block_e, n_valid, dest_flat, pad_rows, next_expert, h2s, w1, w3, w2):
    hbm = pl.BlockSpec(memory_space=pl.ANY)
    w_in = [pltpu.VMEM((D_MODEL, D_EXPERT), dt) for dt in (jnp.float32, jnp.bfloat16)]
    w_out = [pltpu.VMEM((D_EXPERT, D_MODEL), dt) for dt in (jnp.float32, jnp.bfloat16)]
    return pl.pallas_call(
        functools.partial(_expert_kernel, layer=layer),
        grid_spec=pltpu.PrefetchScalarGridSpec(
            num_scalar_prefetch=5,
            grid=(N_BLOCKS,),
            in_specs=[hbm, hbm, hbm, hbm],
            out_specs=pl.BlockSpec((ROW_BLOCK * SLAB, LANES), lambda b, *_: (b, 0)),
            scratch_shapes=[
                pltpu.SMEM((N_ROWS,), jnp.int32),
                pltpu.VMEM((GATHER_SLOTS, ROW_BLOCK * SLAB, LANES), jnp.float32),
                pltpu.SemaphoreType.DMA((GATHER_SLOTS,)),
                w_in[0], w_in[0], w_out[0],
                pltpu.SemaphoreType.DMA((3,)),
                w_in[1], w_in[1], w_out[1],
            ],
        ),
        out_shape=jax.ShapeDtypeStruct((N_ROWS * SLAB, LANES), jnp.float32),
        compiler_params=_cparams(1),
        name="moe_experts",
    )(block_e, n_valid, dest_flat, pad_rows, next_expert, h2s, w1, w3, w2)


def _combine_kernel(pos_ref, x1_ref, rf_ref, yb_ref, out_ref, buf_ref, sem):
    i = pl.program_id(0)
    n = pl.num_programs(0)
    tm = COMB_TM

    def copy(tile, slot, j, k):
        p = pos_ref[k * N_TOK + tile * tm + j]
        return pltpu.make_async_copy(
            yb_ref.at[pl.ds(pl.multiple_of(p * SLAB, SLAB), SLAB)],
            buf_ref.at[slot, k, pl.ds(pl.multiple_of(j * SLAB, SLAB), SLAB)],
            sem.at[slot])

    def issue(tile, slot):
        def body(j):
            copy(tile, slot, j, 0).start()
            copy(tile, slot, j, 1).start()
        _unrolled(tm, body)

    def drain(tile, slot):
        def body(j):
            copy(tile, slot, j, 0).wait()
            copy(tile, slot, j, 1).wait()
        _unrolled(tm, body)

    slot = i % 2

    @pl.when(i == 0)
    def _():
        issue(0, 0)

    @pl.when(i + 1 < n)
    def _():
        issue(i + 1, 1 - slot)

    drain(i, slot)
    rf = rf_ref[...]
    y0 = _from_slab(buf_ref.at[slot, 0], tm)
    y1 = _from_slab(buf_ref.at[slot, 1], tm)
    out_ref[...] = x1_ref[...] + rf[:, 0:1] * y0 + rf[:, 1:2] * y1


def _combine(pos_flat, x1, rf, yb):
    return pl.pallas_call(
        _combine_kernel,
        grid_spec=pltpu.PrefetchScalarGridSpec(
            num_scalar_prefetch=1,
            grid=(N_TOK // COMB_TM,),
            in_specs=[
                pl.BlockSpec((COMB_TM, D_MODEL), lambda i, p: (i, 0)),
                pl.BlockSpec((COMB_TM, LANES), lambda i, p: (i, 0)),
                pl.BlockSpec(memory_space=pl.ANY),
            ],
            out_specs=pl.BlockSpec((COMB_TM, D_MODEL), lambda i, p: (i, 0)),
            scratch_shapes=[
                pltpu.VMEM((2, TOP_K, COMB_TM * SLAB, LANES), jnp.float32),
                pltpu.SemaphoreType.DMA((2,)),
            ],
        ),
        out_shape=jax.ShapeDtypeStruct((N_TOK, D_MODEL), jnp.float32),
        compiler_params=_cparams(1),
        name="moe_combine",
    )(pos_flat, x1, rf, yb)


def _combine_qkv_kernel(pos_ref, x1_ref, rf_ref, yb_ref, g1_ref, w_ref, qg_ref, kg_ref,
                        x2_ref, qk_ref, v_ref, buf_ref, sem):
    i = pl.program_id(0)
    n = pl.num_programs(0)
    tm = QKV_TS

    def copy(tile, slot, j, k):
        p = pos_ref[k * N_TOK + tile * tm + j]
        return pltpu.make_async_copy(
            yb_ref.at[pl.ds(pl.multiple_of(p * SLAB, SLAB), SLAB)],
            buf_ref.at[slot, k, pl.ds(j * SLAB, SLAB)],
            sem.at[slot])

    def plane_arrival(slot, k):
        return pltpu.make_async_copy(
            yb_ref.at[pl.ds(0, tm * SLAB)], buf_ref.at[slot, k], sem.at[slot])

    @pl.when(i == 0)
    def _():
        for s in range(COMBINE_SLOTS - 1):
            def early(j, s=s):
                copy(min(s, N_TOK // QKV_TS - 1), s, j, 0).start()
                copy(min(s, N_TOK // QKV_TS - 1), s, j, 1).start()
            _unrolled(tm, early)

    slot = lax.rem(i, COMBINE_SLOTS)
    plane_arrival(slot, 0).wait()
    plane_arrival(slot, 1).wait()
    rf = rf_ref[...]
    y0 = _from_slab(buf_ref.at[slot, 0], tm)
    y1 = _from_slab(buf_ref.at[slot, 1], tm)
    x2 = x1_ref[...] + rf[:, 0:1] * y0 + rf[:, 1:2] * y1
    x2_ref[...] = x2
    ahead = jnp.minimum(i + COMBINE_SLOTS - 1, n - 1)
    ahead_slot = lax.rem(i + COMBINE_SLOTS - 1, COMBINE_SLOTS)
    for j in range(tm):
        copy(ahead, ahead_slot, j, 0).start()
        copy(ahead, ahead_slot, j, 1).start()
    h = _rms(x2, g1_ref[...]).astype(jnp.bfloat16)
    qkv = jnp.dot(h, w_ref[...], preferred_element_type=jnp.float32)
    q_scale = HEAD_DIM ** -0.5 * LOG2E
    for c in range(N_HEADS):
        q_cols = slice(c * V_DIM, (c + 1) * V_DIM)
        k_cols = slice(D_MODEL + c * V_DIM, D_MODEL + (c + 1) * V_DIM)
        qk_ref[:, q_cols] = (_half_norm(qkv[:, q_cols], qg_ref[...]) * q_scale).astype(jnp.bfloat16)
        qk_ref[:, k_cols] = _half_norm(qkv[:, k_cols], kg_ref[...]).astype(jnp.bfloat16)
    v_ref[...] = qkv[:, 2 * D_MODEL:].astype(jnp.bfloat16)

    @pl.when(i == n - 1)
    def _():
        for s in range(1, COMBINE_SLOTS):
            plane_arrival(lax.rem(i + s, COMBINE_SLOTS), 0).wait()
            plane_arrival(lax.rem(i + s, COMBINE_SLOTS), 1).wait()


def _combine_qkv(pos_flat, x1, rf, yb, g1, w_qkv, qg2, kg2):
    tile = lambda i, p: (i, 0)
    return pl.pallas_call(
        _combine_qkv_kernel,
        grid_spec=pltpu.PrefetchScalarGridSpec(
            num_scalar_prefetch=1,
            grid=(N_TOK // QKV_TS,),
            in_specs=[
                pl.BlockSpec((QKV_TS, D_MODEL), tile),
                pl.BlockSpec((QKV_TS, LANES), tile),
                pl.BlockSpec(memory_space=pl.ANY),
                _full((1, D_MODEL)),
                _full((D_MODEL, 3 * D_MODEL)),
                _full((1, V_DIM)), _full((1, V_DIM)),
            ],
            out_specs=(
                pl.BlockSpec((QKV_TS, D_MODEL), tile),
                pl.BlockSpec((QKV_TS, 2 * D_MODEL), tile),
                pl.BlockSpec((QKV_TS, D_MODEL), tile),
            ),
            scratch_shapes=[
                pltpu.VMEM((COMBINE_SLOTS, TOP_K, QKV_TS * SLAB, LANES), jnp.float32),
                pltpu.SemaphoreType.DMA((COMBINE_SLOTS,)),
            ],
        ),
        out_shape=(
            jax.ShapeDtypeStruct((N_TOK, D_MODEL), jnp.float32),
            jax.ShapeDtypeStruct((N_TOK, 2 * D_MODEL), jnp.bfloat16),
            jax.ShapeDtypeStruct((N_TOK, D_MODEL), jnp.bfloat16),
        ),
        compiler_params=_cparams(1),
        name="combine_qkv",
    )(pos_flat, x1, rf, yb, g1, w_qkv, qg2, kg2)


def _moe_experts(layer, h2s, rft, cnt, w1, w3, w2):
    experts = rft[2:4].astype(jnp.int32)
    rank = rft[4:6].astype(jnp.int32)
    counts = cnt[:N_EXPERTS, 0].astype(jnp.int32)
    padded = (counts + ROW_BLOCK - 1) // ROW_BLOCK * ROW_BLOCK
    pad_ends = jnp.cumsum(padded)
    pad_starts = pad_ends - padded
    ids = jnp.arange(N_EXPERTS, dtype=jnp.int32)
    is_e = experts[None] == ids[:, None, None]
    dest = (jnp.sum(jnp.where(is_e, pad_starts[:, None, None], 0), axis=0) + rank).reshape(-1)
    n_valid = (pad_ends[-1:] // ROW_BLOCK).astype(jnp.int32)
    block_row0 = jnp.arange(N_BLOCKS, dtype=jnp.int32) * ROW_BLOCK
    block_e = jnp.minimum(
        jnp.sum((pad_ends[None, :] <= block_row0[:, None]).astype(jnp.int32), axis=1),
        N_EXPERTS - 1).astype(jnp.int32)
    pad_rows = jnp.concatenate([pad_starts + counts, pad_ends]).astype(jnp.int32)
    later_nonempty = (counts > 0)[None, :] & (ids[None, :] > ids[:, None])
    following = jnp.min(jnp.where(later_nonempty, ids[None, :], N_EXPERTS), axis=1)
    next_expert = jnp.where(following < N_EXPERTS, following, -1).astype(jnp.int32)
    yb = _experts(layer, block_e, n_valid, dest, pad_rows, next_expert, h2s, w1, w3, w2)
    return dest, yb


def _router_params(w_grp, b_grp, w_exp, b_exp):
    wr = jnp.zeros((D_MODEL, LANES), jnp.float32)
    wr = wr.at[:, :N_EXPERTS].set(w_exp).at[:, GRP_LANE0:GRP_LANE0 + N_GROUPS].set(w_grp)
    br = jnp.zeros((1, LANES), jnp.float32)
    br = br.at[0, :N_EXPERTS].set(b_exp).at[0, GRP_LANE0:GRP_LANE0 + N_GROUPS].set(b_grp)
    w_hi = wr.astype(jnp.bfloat16)
    w_lo = (wr - w_hi.astype(jnp.float32)).astype(jnp.bfloat16)
    return jnp.concatenate([w_hi, w_lo], axis=1), br


def kernel(x, norm1_g, norm2_g, conv_w_in, conv_b_in, conv_dw, conv_dw_b, conv_ln_g, conv_ln_b, conv_w_out, conv_b_out, attn_w_qkv, attn_q_g, attn_k_g, attn_lq1, attn_lk1, attn_lq2, attn_lk2, attn_subln_g, attn_w_o, moe_w_grp, moe_b_grp, moe_w_exp, moe_b_exp, moe_w1, moe_w3, moe_w2):
    assert x.shape == (BATCH, SEQ, D_MODEL) and x.dtype == jnp.float32
    assert moe_w1.shape == (2, N_EXPERTS, D_MODEL, D_EXPERT) and conv_dw.shape[1] == CONV_WIDTH
    bf16 = jnp.bfloat16
    row = lambda a: a.reshape(1, -1)
    xf = x.reshape(N_TOK, D_MODEL)

    wr, br = _router_params(moe_w_grp[0], moe_b_grp[0], moe_w_exp[0], moe_b_exp[0])
    dw = jnp.zeros((HALO, D_MODEL), jnp.float32).at[:CONV_WIDTH].set(conv_dw[0])
    x1, h2s, rf, rft, cnt = _conv_layer(
        xf, row(norm1_g[0]), conv_w_in[0].astype(bf16), row(conv_b_in[0]), dw,
        row(conv_dw_b[0]), row(conv_ln_g[0]), row(conv_ln_b[0]),
        conv_w_out[0].astype(bf16), row(conv_b_out[0]), row(norm2_g[0]), wr, br)
    dest, yb = _moe_experts(0, h2s, rft, cnt, moe_w1, moe_w3, moe_w2)

    lambda_init = 0.8 - 0.6 * math.exp(-0.3 * 1)
    wr, br = _router_params(moe_w_grp[1], moe_b_grp[1], moe_w_exp[1], moe_b_exp[1])
    two = lambda g: jnp.concatenate([g, g]).reshape(1, V_DIM)
    xf, qk, v = _combine_qkv(dest, x1, rf, yb, row(norm1_g[1]), attn_w_qkv[0].astype(bf16),
                             two(attn_q_g[0]), two(attn_k_g[0]))
    lam_rows = jnp.zeros((SUBLANES, HEAD_DIM), jnp.float32)
    lam_rows = lam_rows.at[0].set(attn_lq1[0]).at[1].set(attn_lk1[0])
    lam_rows = lam_rows.at[2].set(attn_lq2[0]).at[3].set(attn_lk2[0])
    o = _attention(qk, v, lam_rows, row(attn_subln_g[0]), lambda_init)
    x1, h2s, rf, rft, cnt = _attn_post(xf, o, attn_w_o[0].astype(bf16), row(norm2_g[1]), wr, br)
    dest, yb = _moe_experts(1, h2s, rft, cnt, moe_w1, moe_w3, moe_w2)
    xf = _combine(dest, x1, rf, yb)
    return xf.reshape(BATCH, SEQ, D_MODEL)
```

```python
import functools
import math

import jax
import jax.numpy as jnp
from jax import lax
from jax.experimental import pallas as pl
from jax.experimental.pallas import tpu as pltpu

D_MODEL = 1024
BATCH = 8
SEQ = 2048
N_TOK = BATCH * SEQ
CHUNK = 64
CONV_WIDTH = 31
N_HEADS = 8
HEAD_DIM = 64
V_DIM = 128
N_GROUPS = 4
EXPERTS_PER_GROUP = 8
N_EXPERTS = 32
TOP_K = 2
D_EXPERT = 512
EPS = 1e-6
LOG2E = math.log2(math.e)

LANES = 128
SUBLANES = 8
SLAB = D_MODEL // LANES
VMEM_LIMIT = 56 * 1024 * 1024

CONV_TS = 512
HALO = 32
CONV_RC = 64
POST_TS = 512
QKV_TS = 512
ATT_TQ = 256
ROW_BLOCK = 256
GATHER_SLOTS = 12
N_ASSIGN = N_TOK * TOP_K
N_BLOCKS = N_ASSIGN // ROW_BLOCK + N_EXPERTS
N_ROWS = N_BLOCKS * ROW_BLOCK
COMB_TM = 256
COMBINE_SLOTS = 3
DMA_UNROLL = 16
GRP_LANE0 = N_EXPERTS
ROUTE_ROWS = 40


def _cparams(n_axes):
    return pltpu.CompilerParams(
        dimension_semantics=("arbitrary",) * n_axes, vmem_limit_bytes=VMEM_LIMIT)


def _rms(x, g):
    return x * lax.rsqrt(jnp.mean(x * x, axis=-1, keepdims=True) + EPS) * g


def _to_slab(ref, val, rows):
    for j in range(SLAB):
        ref[pl.ds(j, rows, stride=SLAB), :] = val[:, j * LANES:(j + 1) * LANES]


def _from_slab(ref, rows):
    return jnp.concatenate(
        [ref[pl.ds(j, rows, stride=SLAB), :] for j in range(SLAB)], axis=-1)


def _residual_norm_route(x1, g2_ref, wr_ref, br_ref, run_ref,
                         x1_ref, h2s_ref, rf_ref, rft_ref, cnt_ref, rows):
    x1_ref[...] = x1
    h2 = _rms(x1, g2_ref[...])
    _to_slab(h2s_ref, h2, rows)

    hi = h2.astype(jnp.bfloat16)
    lo = (h2 - hi.astype(jnp.float32)).astype(jnp.bfloat16)
    ab = jnp.dot(hi, wr_ref[...], preferred_element_type=jnp.float32)
    c = jnp.dot(lo, wr_ref[:, 0:LANES], preferred_element_type=jnp.float32)
    logits = ab[:, 0:LANES] + ab[:, LANES:2 * LANES] + c + br_ref[...]
    lt = jnp.transpose(logits)[0:ROUTE_ROWS, :]
    sub = lax.broadcasted_iota(jnp.int32, (ROUTE_ROWS, rows), 0).astype(jnp.float32)
    neg = jnp.float32(-jnp.inf)
    big = jnp.float32(1e9)

    gmask = (sub >= GRP_LANE0) & (sub < GRP_LANE0 + N_GROUPS)
    gl = jnp.where(gmask, lt, neg)
    gmax = jnp.max(gl, axis=0, keepdims=True)
    gidx = jnp.min(jnp.where(gl == gmax, sub, big), axis=0, keepdims=True) - GRP_LANE0
    gsum = jnp.sum(jnp.where(gmask, jnp.exp(gl - gmax), 0.0), axis=0, keepdims=True)
    grp_p = 1.0 / gsum

    first_e = gidx * EXPERTS_PER_GROUP
    emask = (sub >= first_e) & (sub < first_e + EXPERTS_PER_GROUP)
    el = jnp.where(emask, lt, neg)
    m1 = jnp.max(el, axis=0, keepdims=True)
    i1 = jnp.min(jnp.where(el == m1, sub, big), axis=0, keepdims=True)
    el2 = jnp.where(sub == i1, neg, el)
    m2 = jnp.max(el2, axis=0, keepdims=True)
    i2 = jnp.min(jnp.where(el2 == m2, sub, big), axis=0, keepdims=True)
    t = jnp.exp(m2 - m1)
    inv = 1.0 / (1.0 + t)
    g_first = grp_p * inv
    g_second = grp_p * t * inv

    sel1 = sub == i1
    sel2 = sub == i2
    member = jnp.where(sel1 | sel2, 1.0, 0.0).astype(jnp.bfloat16)
    c_i = lax.broadcasted_iota(jnp.int32, (rows, rows), 0)
    r_i = lax.broadcasted_iota(jnp.int32, (rows, rows), 1)
    earlier = jnp.where(c_i < r_i, 1.0, 0.0).astype(jnp.bfloat16)
    run = run_ref[...]
    cum = (jnp.dot(member, earlier, preferred_element_type=jnp.float32)
           + jnp.concatenate([run] * (rows // LANES), axis=1))
    rank1 = jnp.sum(jnp.where(sel1, cum, 0.0), axis=0, keepdims=True)
    rank2 = jnp.sum(jnp.where(sel2, cum, 0.0), axis=0, keepdims=True)
    run_new = run + jnp.sum(member.astype(jnp.float32), axis=1, keepdims=True)
    run_ref[...] = run_new
    cnt_ref[...] = run_new

    sub8 = lax.broadcasted_iota(jnp.int32, (SUBLANES, rows), 0)
    out = jnp.where(sub8 == 0, g_first, 0.0)
    out = jnp.where(sub8 == 1, g_second, out)
    out = jnp.where(sub8 == 2, i1, out)
    out = jnp.where(sub8 == 3, i2, out)
    out = jnp.where(sub8 == 4, rank1, out)
    out = jnp.where(sub8 == 5, rank2, out)
    rft_ref[...] = out
    padded = jnp.concatenate([out, jnp.zeros((LANES - SUBLANES, rows), jnp.float32)], axis=0)
    rf_ref[...] = jnp.transpose(padded)


def _route_out_shapes():
    return (
        jax.ShapeDtypeStruct((N_TOK, D_MODEL), jnp.float32),
        jax.ShapeDtypeStruct((N_TOK * SLAB, LANES), jnp.float32),
        jax.ShapeDtypeStruct((N_TOK, LANES), jnp.float32),
        jax.ShapeDtypeStruct((SUBLANES, N_TOK), jnp.float32),
        jax.ShapeDtypeStruct((ROUTE_ROWS, LANES), jnp.float32),
    )


def _route_out_specs(ts, idx):
    return (
        pl.BlockSpec((ts, D_MODEL), lambda *a: (idx(*a), 0)),
        pl.BlockSpec((ts * SLAB, LANES), lambda *a: (idx(*a), 0)),
        pl.BlockSpec((ts, LANES), lambda *a: (idx(*a), 0)),
        pl.BlockSpec((SUBLANES, ts), lambda *a: (0, idx(*a))),
        pl.BlockSpec((ROUTE_ROWS, LANES), lambda *a: (0, 0)),
    )


def _full(shape):
    return pl.BlockSpec(shape, lambda *a: (0,) * len(shape))


def _conv_kernel(x_ref, g1_ref, win_ref, bin_ref, dw_ref, dwb_ref, lng_ref, lnb_ref,
                 wout_ref, bout_ref, g2_ref, wr_ref, br_ref,
                 x1_ref, h2s_ref, rf_ref, rft_ref, cnt_ref,
                 ext_ref, conv_ref, run_ref):
    b = pl.program_id(0)
    s = pl.program_id(1)
    ts = CONV_TS

    @pl.when((b == 0) & (s == 0))
    def _():
        run_ref[...] = jnp.zeros_like(run_ref)

    @pl.when(s == 0)
    def _():
        ext_ref[0:HALO, :] = jnp.zeros((HALO, D_MODEL), jnp.float32)

    x = x_ref[...]
    h = _rms(x, g1_ref[...]).astype(jnp.bfloat16)
    u = jnp.dot(h, win_ref[...], preferred_element_type=jnp.float32) + bin_ref[...]
    glu = u[:, :D_MODEL] * jax.nn.sigmoid(u[:, D_MODEL:])
    ext_ref[HALO:HALO + ts, :] = glu

    base = HALO - (CONV_WIDTH - 1)

    def lane_chunk(c, carry):
        cols = pl.ds(pl.multiple_of(c * LANES, LANES), LANES)
        taps = [dw_ref[pl.ds(k, 1), cols] for k in range(CONV_WIDTH)]
        for r0 in range(0, ts, CONV_RC):
            acc = None
            for r in range(SUBLANES):
                rows = CONV_RC + (SUBLANES if r else 0)
                part = None
                for q in range((base + CONV_WIDTH - 1) // SUBLANES + 1):
                    k = SUBLANES * q + r - base
                    if 0 <= k < CONV_WIDTH:
                        term = ext_ref[pl.ds(r0 + SUBLANES * q, rows), cols] * taps[k]
                        part = term if part is None else part + term
                part = part[r:r + CONV_RC, :] if r else part
                acc = part if acc is None else acc + part
            conv_ref[pl.ds(r0, CONV_RC), cols] = acc
        return carry

    lax.fori_loop(0, D_MODEL // LANES, lane_chunk, 0)
    ext_ref[0:HALO, :] = ext_ref[ts:ts + HALO, :]

    v = conv_ref[...] + dwb_ref[...]
    mu = jnp.mean(v, axis=-1, keepdims=True)
    vc = v - mu
    var = jnp.mean(vc * vc, axis=-1, keepdims=True)
    y = vc * lax.rsqrt(var + EPS) * lng_ref[...] + lnb_ref[...]
    y = (y * jax.nn.sigmoid(y)).astype(jnp.bfloat16)
    mix = jnp.dot(y, wout_ref[...], preferred_element_type=jnp.float32) + bout_ref[...]
    _residual_norm_route(x + mix, g2_ref, wr_ref, br_ref, run_ref,
                         x1_ref, h2s_ref, rf_ref, rft_ref, cnt_ref, ts)


def _conv_layer(x, g1, w_in, b_in, dw, dw_b, ln_g, ln_b, w_out, b_out, g2, wr, br):
    ns = SEQ // CONV_TS
    tile = lambda b, s: b * ns + s
    return pl.pallas_call(
        _conv_kernel,
        grid=(BATCH, ns),
        in_specs=[
            pl.BlockSpec((CONV_TS, D_MODEL), lambda b, s: (tile(b, s), 0)),
            _full((1, D_MODEL)),
            _full((D_MODEL, 2 * D_MODEL)),
            _full((1, 2 * D_MODEL)),
            _full((HALO, D_MODEL)),
            _full((1, D_MODEL)), _full((1, D_MODEL)), _full((1, D_MODEL)),
            _full((D_MODEL, D_MODEL)),
            _full((1, D_MODEL)), _full((1, D_MODEL)),
            _full((D_MODEL, 2 * LANES)), _full((1, LANES)),
        ],
        out_specs=_route_out_specs(CONV_TS, tile),
        out_shape=_route_out_shapes(),
        scratch_shapes=[
            pltpu.VMEM((HALO + CONV_TS, D_MODEL), jnp.float32),
            pltpu.VMEM((CONV_TS, D_MODEL), jnp.float32),
            pltpu.VMEM((ROUTE_ROWS, LANES), jnp.float32),
        ],
        compiler_params=_cparams(2),
        name="conv_mixer",
    )(x, g1, w_in, b_in, dw, dw_b, ln_g, ln_b, w_out, b_out, g2, wr, br)


def _half_norm(z, gain):
    lane = lax.broadcasted_iota(jnp.int32, z.shape, 1)
    first = lane < HEAD_DIM
    zz = z * z
    ss_a = jnp.sum(jnp.where(first, zz, 0.0), axis=1, keepdims=True)
    ss_b = jnp.sum(jnp.where(first, 0.0, zz), axis=1, keepdims=True)
    inv = jnp.where(first, lax.rsqrt(ss_a * (1.0 / HEAD_DIM) + EPS),
                    lax.rsqrt(ss_b * (1.0 / HEAD_DIM) + EPS))
    return z * inv * gain


def _attn_kernel(q_ref, k_ref, v_ref, lam_ref, sg_ref, o_ref,
                 qa_ref, qb_ref, v1_ref, s_ref, e_ref, part_ref, *, lambda_init):
    lp = lam_ref[...]
    lam = (jnp.exp(jnp.sum(lp[0:1, :] * lp[1:2, :], axis=1, keepdims=True))
           - jnp.exp(jnp.sum(lp[2:3, :] * lp[3:4, :], axis=1, keepdims=True))
           + lambda_init)

    q = q_ref[...]
    first = lax.broadcasted_iota(jnp.int32, (SEQ, V_DIM), 1) < HEAD_DIM
    qa_ref[...] = jnp.where(first, q, jnp.zeros_like(q))
    qb_ref[...] = jnp.where(first, jnp.zeros_like(q), q)

    ones_col = lax.broadcasted_iota(jnp.int32, (SEQ, V_DIM), 1) == 0
    v1_ref[:, 0:V_DIM] = v_ref[...]
    v1_ref[:, V_DIM:2 * V_DIM] = jnp.where(ones_col, 1.0, 0.0).astype(jnp.bfloat16)

    tq = ATT_TQ
    nt = (((1,), (1,)), ((), ()))
    visible = (lax.broadcasted_iota(jnp.int32, (tq, tq), 1) // CHUNK
               <= lax.broadcasted_iota(jnp.int32, (tq, tq), 0) // CHUNK)
    halves = (qa_ref, qb_ref)
    units = [(qi, h) for qi in range(SEQ // tq) for h in range(2)]

    def stage_scores(u, slot):
        qi, h = units[u]
        k0 = qi * tq
        q = halves[h][k0:k0 + tq, :]
        dg = lax.dot_general(q, k_ref[k0:k0 + tq, :], nt, preferred_element_type=jnp.float32)
        s_ref[slot, :, k0:k0 + tq] = jnp.where(visible, dg, -jnp.inf)
        if qi:
            s_ref[slot, :, 0:k0] = lax.dot_general(q, k_ref[0:k0, :], nt,
                                                   preferred_element_type=jnp.float32)

    def stage_numerators(u, slot):
        kend = (units[u][0] + 1) * tq
        sc = s_ref[slot, :, 0:kend]
        m = jnp.max(sc, axis=1, keepdims=True)
        e_ref[slot, :, 0:kend] = jnp.exp2(sc - m).astype(jnp.bfloat16)

    def stage_values(u, slot):
        qi, h = units[u]
        k0 = qi * tq
        kend = k0 + tq
        acc = jnp.dot(e_ref[slot, :, 0:kend], v1_ref[0:kend, :],
                      preferred_element_type=jnp.float32)
        attn = acc[:, 0:V_DIM] * (1.0 / acc[:, V_DIM:V_DIM + 1])
        if h == 0:
            part_ref[...] = attn
        else:
            o = part_ref[...] - lam * attn
            o = _rms(o, sg_ref[...]) * (1.0 - lambda_init)
            o_ref[k0:kend, :] = o.astype(jnp.bfloat16)

    for step in range(len(units) + 2):
        if step < len(units):
            stage_scores(step, step % 2)
        if 1 <= step <= len(units):
            stage_numerators(step - 1, (step - 1) % 2)
        if step >= 2:
            stage_values(step - 2, step % 2)


def _attention(qk, v, lam_rows, subln_g, lambda_init):
    return pl.pallas_call(
        functools.partial(_attn_kernel, lambda_init=lambda_init),
        grid=(BATCH, N_HEADS),
        in_specs=[
            pl.BlockSpec((SEQ, V_DIM), lambda b, h: (b, h)),
            pl.BlockSpec((SEQ, V_DIM), lambda b, h: (b, N_HEADS + h)),
            pl.BlockSpec((SEQ, V_DIM), lambda b, h: (b, h)),
            _full((SUBLANES, HEAD_DIM)),
            _full((1, V_DIM)),
        ],
        out_specs=pl.BlockSpec((SEQ, V_DIM), lambda b, h: (b, h)),
        out_shape=jax.ShapeDtypeStruct((N_TOK, D_MODEL), jnp.bfloat16),
        scratch_shapes=[pltpu.VMEM((SEQ, V_DIM), jnp.bfloat16)] * 2 + [
            pltpu.VMEM((SEQ, 2 * V_DIM), jnp.bfloat16),
            pltpu.VMEM((2, ATT_TQ, SEQ), jnp.float32),
            pltpu.VMEM((2, ATT_TQ, SEQ), jnp.bfloat16),
            pltpu.VMEM((ATT_TQ, V_DIM), jnp.float32),
        ],
        compiler_params=_cparams(2),
        name="diff_attention",
    )(qk, qk, v, lam_rows, subln_g)


def _post_kernel(x_ref, m_ref, w_ref, g2_ref, wr_ref, br_ref,
                 x1_ref, h2s_ref, rf_ref, rft_ref, cnt_ref, run_ref):
    @pl.when(pl.program_id(0) == 0)
    def _():
        run_ref[...] = jnp.zeros_like(run_ref)

    mix = jnp.dot(m_ref[...], w_ref[...], preferred_element_type=jnp.float32)
    _residual_norm_route(x_ref[...] + mix, g2_ref, wr_ref, br_ref, run_ref,
                         x1_ref, h2s_ref, rf_ref, rft_ref, cnt_ref, POST_TS)


def _attn_post(x, o, w_o, g2, wr, br):
    return pl.pallas_call(
        _post_kernel,
        grid=(N_TOK // POST_TS,),
        in_specs=[
            pl.BlockSpec((POST_TS, D_MODEL), lambda i: (i, 0)),
            pl.BlockSpec((POST_TS, D_MODEL), lambda i: (i, 0)),
            _full((D_MODEL, D_MODEL)),
            _full((1, D_MODEL)),
            _full((D_MODEL, 2 * LANES)), _full((1, LANES)),
        ],
        out_specs=_route_out_specs(POST_TS, lambda i: i),
        out_shape=_route_out_shapes(),
        scratch_shapes=[pltpu.VMEM((ROUTE_ROWS, LANES), jnp.float32)],
        compiler_params=_cparams(1),
        name="attn_post",
    )(x, o, w_o, g2, wr, br)


def _unrolled(n, body):
    def group(g, carry):
        for u in range(DMA_UNROLL):
            body(g * DMA_UNROLL + u)
        return carry
    lax.fori_loop(0, n // DMA_UNROLL, group, 0)


def _expert_kernel(be_ref, nv_ref, dest_ref, pad_ref, next_ref,
                   h2s_ref, w1_hbm, w3_hbm, w2_hbm, yb_ref,
                   rowtok_ref, xbuf_ref, sem, w1f_ref, w3f_ref, w2f_ref, wsem,
                   w1b_ref, w3b_ref, w2b_ref, *, layer):
    b = pl.program_id(0)
    n_valid = nv_ref[0]
    w_hbm = (w1_hbm, w3_hbm, w2_hbm)
    w_f32 = (w1f_ref, w3f_ref, w2f_ref)

    def weight_copies(expert):
        return [pltpu.make_async_copy(w_hbm[i].at[layer, expert], w_f32[i], wsem.at[i])
                for i in range(3)]

    def row_copy(block, slot, r):
        tok = rowtok_ref[block * ROW_BLOCK + r]
        return pltpu.make_async_copy(
            h2s_ref.at[pl.ds(pl.multiple_of(tok * SLAB, SLAB), SLAB)],
            xbuf_ref.at[slot, pl.ds(r * SLAB, SLAB)],
            sem.at[slot])

    def block_arrival(slot):
        return pltpu.make_async_copy(
            h2s_ref.at[pl.ds(0, ROW_BLOCK * SLAB)], xbuf_ref.at[slot], sem.at[slot])

    @pl.when(b == 0)
    def _():
        for c in weight_copies(be_ref[0]):
            c.start()
        def clear_padding(e, carry):
            def clear(j, c):
                rowtok_ref[j] = 0
                return c
            return lax.fori_loop(pad_ref[e], pad_ref[N_EXPERTS + e], clear, carry)
        lax.fori_loop(0, N_EXPERTS, clear_padding, 0)

        def place(t):
            rowtok_ref[dest_ref[t]] = t
            rowtok_ref[dest_ref[N_TOK + t]] = t
        _unrolled(N_TOK, place)
        for s in range(GATHER_SLOTS - 1):
            early = jnp.minimum(s, n_valid - 1)
            _unrolled(ROW_BLOCK, lambda r, early=early, s=s: row_copy(early, s, r).start())

    expert = be_ref[b]
    fresh = (b == 0) | (expert != be_ref[jnp.maximum(b - 1, 0)])

    @pl.when(fresh & (b < n_valid))
    def _():
        for c in weight_copies(expert):
            c.wait()
        w1b_ref[...] = w1f_ref[...].astype(jnp.bfloat16)
        w3b_ref[...] = w3f_ref[...].astype(jnp.bfloat16)
        w2b_ref[...] = w2f_ref[...].astype(jnp.bfloat16)
        following = next_ref[expert]

        @pl.when(following >= 0)
        def _():
            for c in weight_copies(following):
                c.start()

    @pl.when(b < n_valid)
    def _():
        slot = lax.rem(b, GATHER_SLOTS)
        block_arrival(slot).wait()
        x = _from_slab(xbuf_ref.at[slot], ROW_BLOCK).astype(jnp.bfloat16)
        ahead = jnp.minimum(b + GATHER_SLOTS - 1, n_valid - 1)
        ahead_slot = lax.rem(b + GATHER_SLOTS - 1, GATHER_SLOTS)
        for r in range(ROW_BLOCK):
            row_copy(ahead, ahead_slot, r).start()
        h1 = jnp.dot(x, w1b_ref[...], preferred_element_type=jnp.float32)
        h3 = jnp.dot(x, w3b_ref[...], preferred_element_type=jnp.float32)
        act = (h1 * jax.nn.sigmoid(h1) * h3).astype(jnp.bfloat16)
        y = jnp.dot(act, w2b_ref[...], preferred_element_type=jnp.float32)
        _to_slab(yb_ref, y, ROW_BLOCK)

    @pl.when(b == n_valid - 1)
    def _():
        for s in range(1, GATHER_SLOTS):
            block_arrival(lax.rem(b + s, GATHER_SLOTS)).wait()

    @pl.when(b >= n_valid)
    def _():
        yb_ref[...] = jnp.zeros_like(yb_ref)


def _experts(layer, block_e, n_valid, dest_flat, pad_rows, next_expert, h2s, w1, w3, w2):
    hbm = pl.BlockSpec(memory_space=pl.ANY)
    w_in = [pltpu.VMEM((D_MODEL, D_EXPERT), dt) for dt in (jnp.float32, jnp.bfloat16)]
    w_out = [pltpu.VMEM((D_EXPERT, D_MODEL), dt) for dt in (jnp.float32, jnp.bfloat16)]
    return pl.pallas_call(
        functools.partial(_expert_kernel, layer=layer),
        grid_spec=pltpu.PrefetchScalarGridSpec(
            num_scalar_prefetch=5,
            grid=(N_BLOCKS,),
            in_specs=[hbm, hbm, hbm, hbm],
            out_specs=pl.BlockSpec((ROW_BLOCK * SLAB, LANES), lambda b, *_: (b, 0)),
            scratch_shapes=[
                pltpu.SMEM((N_ROWS,), jnp.int32),
                pltpu.VMEM((GATHER_SLOTS, ROW_BLOCK * SLAB, LANES), jnp.float32),
                pltpu.SemaphoreType.DMA((GATHER_SLOTS,)),
                w_in[0], w_in[0], w_out[0],
                pltpu.SemaphoreType.DMA((3,)),
                w_in[1], w_in[1], w_out[1],
            ],
        ),
        out_shape=jax.ShapeDtypeStruct((N_ROWS * SLAB, LANES), jnp.float32),
        compiler_params=_cparams(1),
        name="moe_experts",
    )(block_e, n_valid, dest_flat, pad_rows, next_expert, h2s, w1, w3, w2)


def _combine_kernel(pos_ref, x1_ref, rf_ref, yb_ref, out_ref, buf_ref, sem):
    i = pl.program_id(0)
    n = pl.num_programs(0)
    tm = COMB_TM

    def copy(tile, slot, j, k):
        p = pos_ref[k * N_TOK + tile * tm + j]
        return pltpu.make_async_copy(
            yb_ref.at[pl.ds(pl.multiple_of(p * SLAB, SLAB), SLAB)],
            buf_ref.at[slot, k, pl.ds(pl.multiple_of(j * SLAB, SLAB), SLAB)],
            sem.at[slot])

    def issue(tile, slot):
        def body(j):
            copy(tile, slot, j, 0).start()
            copy(tile, slot, j, 1).start()
        _unrolled(tm, body)

    def drain(tile, slot):
        def body(j):
            copy(tile, slot, j, 0).wait()
            copy(tile, slot, j, 1).wait()
        _unrolled(tm, body)

    slot = i % 2

    @pl.when(i == 0)
    def _():
        issue(0, 0)

    @pl.when(i + 1 < n)
    def _():
        issue(i + 1, 1 - slot)

    drain(i, slot)
    rf = rf_ref[...]
    y0 = _from_slab(buf_ref.at[slot, 0], tm)
    y1 = _from_slab(buf_ref.at[slot, 1], tm)
    out_ref[...] = x1_ref[...] + rf[:, 0:1] * y0 + rf[:, 1:2] * y1


def _combine(pos_flat, x1, rf, yb):
    return pl.pallas_call(
        _combine_kernel,
        grid_spec=pltpu.PrefetchScalarGridSpec(
            num_scalar_prefetch=1,
            grid=(N_TOK // COMB_TM,),
            in_specs=[
                pl.BlockSpec((COMB_TM, D_MODEL), lambda i, p: (i, 0)),
                pl.BlockSpec((COMB_TM, LANES), lambda i, p: (i, 0)),
                pl.BlockSpec(memory_space=pl.ANY),
            ],
            out_specs=pl.BlockSpec((COMB_TM, D_MODEL), lambda i, p: (i, 0)),
            scratch_shapes=[
                pltpu.VMEM((2, TOP_K, COMB_TM * SLAB, LANES), jnp.float32),
                pltpu.SemaphoreType.DMA((2,)),
            ],
        ),
        out_shape=jax.ShapeDtypeStruct((N_TOK, D_MODEL), jnp.float32),
        compiler_params=_cparams(1),
        name="moe_combine",
    )(pos_flat, x1, rf, yb)


def _combine_qkv_kernel(pos_ref, x1_ref, rf_ref, yb_ref, g1_ref, w_ref, qg_ref, kg_ref,
                        x2_ref, qk_ref, v_ref, buf_ref, sem):
    i = pl.program_id(0)
    n = pl.num_programs(0)
    tm = QKV_TS

    def copy(tile, slot, j, k):
        p = pos_ref[k * N_TOK + tile * tm + j]
        return pltpu.make_async_copy(
            yb_ref.at[pl.ds(pl.multiple_of(p * SLAB, SLAB), SLAB)],
            buf_ref.at[slot, k, pl.ds(j * SLAB, SLAB)],
            sem.at[slot])

    def plane_arrival(slot, k):
        return pltpu.make_async_copy(
            yb_ref.at[pl.ds(0, tm * SLAB)], buf_ref.at[slot, k], sem.at[slot])

    @pl.when(i == 0)
    def _():
        for s in range(COMBINE_SLOTS - 1):
            def early(j, s=s):
                copy(min(s, N_TOK // QKV_TS - 1), s, j, 0).start()
                copy(min(s, N_TOK // QKV_TS - 1), s, j, 1).start()
            _unrolled(tm, early)

    slot = lax.rem(i, COMBINE_SLOTS)
    plane_arrival(slot, 0).wait()
    plane_arrival(slot, 1).wait()
    rf = rf_ref[...]
    y0 = _from_slab(buf_ref.at[slot, 0], tm)
    y1 = _from_slab(buf_ref.at[slot, 1], tm)
    x2 = x1_ref[...] + rf[:, 0:1] * y0 + rf[:, 1:2] * y1
    x2_ref[...] = x2
    ahead = jnp.minimum(i + COMBINE_SLOTS - 1, n - 1)
    ahead_slot = lax.rem(i + COMBINE_SLOTS - 1, COMBINE_SLOTS)
    for j in range(tm):
        copy(ahead, ahead_slot, j, 0).start()
        copy(ahead, ahead_slot, j, 1).start()
    h = _rms(x2, g1_ref[...]).astype(jnp.bfloat16)
    qkv = jnp.dot(h, w_ref[...], preferred_element_type=jnp.float32)
    q_scale = HEAD_DIM ** -0.5 * LOG2E
    for c in range(N_HEADS):
        q_cols = slice(c * V_DIM, (c + 1) * V_DIM)
        k_cols = slice(D_MODEL + c * V_DIM, D_MODEL + (c + 1) * V_DIM)
        qk_ref[:, q_cols] = (_half_norm(qkv[:, q_cols], qg_ref[...]) * q_scale).astype(jnp.bfloat16)
        qk_ref[:, k_cols] = _half_norm(qkv[:, k_cols], kg_ref[...]).astype(jnp.bfloat16)
    v_ref[...] = qkv[:, 2 * D_MODEL:].astype(jnp.bfloat16)

    @pl.when(i == n - 1)
    def _():
        for s in range(1, COMBINE_SLOTS):
            plane_arrival(lax.rem(i + s, COMBINE_SLOTS), 0).wait()
            plane_arrival(lax.rem(i + s, COMBINE_SLOTS), 1).wait()


def _combine_qkv(pos_flat, x1, rf, yb, g1, w_qkv, qg2, kg2):
    tile = lambda i, p: (i, 0)
    return pl.pallas_call(
        _combine_qkv_kernel,
        grid_spec=pltpu.PrefetchScalarGridSpec(
            num_scalar_prefetch=1,
            grid=(N_TOK // QKV_TS,),
            in_specs=[
                pl.BlockSpec((QKV_TS, D_MODEL), tile),
                pl.BlockSpec((QKV_TS, LANES), tile),
                pl.BlockSpec(memory_space=pl.ANY),
                _full((1, D_MODEL)),
                _full((D_MODEL, 3 * D_MODEL)),
                _full((1, V_DIM)), _full((1, V_DIM)),
            ],
            out_specs=(
                pl.BlockSpec((QKV_TS, D_MODEL), tile),
                pl.BlockSpec((QKV_TS, 2 * D_MODEL), tile),
                pl.BlockSpec((QKV_TS, D_MODEL), tile),
            ),
            scratch_shapes=[
                pltpu.VMEM((COMBINE_SLOTS, TOP_K, QKV_TS * SLAB, LANES), jnp.float32),
                pltpu.SemaphoreType.DMA((COMBINE_SLOTS,)),
            ],
        ),
        out_shape=(
            jax.ShapeDtypeStruct((N_TOK, D_MODEL), jnp.float32),
            jax.ShapeDtypeStruct((N_TOK, 2 * D_MODEL), jnp.bfloat16),
            jax.ShapeDtypeStruct((N_TOK, D_MODEL), jnp.bfloat16),
        ),
        compiler_params=_cparams(1),
        name="combine_qkv",
    )(pos_flat, x1, rf, yb, g1, w_qkv, qg2, kg2)


def _moe_experts(layer, h2s, rft, cnt, w1, w3, w2):
    experts = rft[2:4].astype(jnp.int32)
    rank = rft[4:6].astype(jnp.int32)
    counts = cnt[:N_EXPERTS, 0].astype(jnp.int32)
    padded = (counts + ROW_BLOCK - 1) // ROW_BLOCK * ROW_BLOCK
    pad_ends = jnp.cumsum(padded)
    pad_starts = pad_ends - padded
    ids = jnp.arange(N_EXPERTS, dtype=jnp.int32)
    is_e = experts[None] == ids[:, None, None]
    dest = (jnp.sum(jnp.where(is_e, pad_starts[:, None, None], 0), axis=0) + rank).reshape(-1)
    n_valid = (pad_ends[-1:] // ROW_BLOCK).astype(jnp.int32)
    block_row0 = jnp.arange(N_BLOCKS, dtype=jnp.int32) * ROW_BLOCK
    block_e = jnp.minimum(
        jnp.sum((pad_ends[None, :] <= block_row0[:, None]).astype(jnp.int32), axis=1),
        N_EXPERTS - 1).astype(jnp.int32)
    pad_rows = jnp.concatenate([pad_starts + counts, pad_ends]).astype(jnp.int32)
    later_nonempty = (counts > 0)[None, :] & (ids[None, :] > ids[:, None])
    following = jnp.min(jnp.where(later_nonempty, ids[None, :], N_EXPERTS), axis=1)
    next_expert = jnp.where(following < N_EXPERTS, following, -1).astype(jnp.int32)
    yb = _experts(layer, block_e, n_valid, dest, pad_rows, next_expert, h2s, w1, w3, w2)
    return dest, yb


def _router_params(w_grp, b_grp, w_exp, b_exp):
    wr = jnp.zeros((D_MODEL, LANES), jnp.float32)
    wr = wr.at[:, :N_EXPERTS].set(w_exp).at[:, GRP_LANE0:GRP_LANE0 + N_GROUPS].set(w_grp)
    br = jnp.zeros((1, LANES), jnp.float32)
    br = br.at[0, :N_EXPERTS].set(b_exp).at[0, GRP_LANE0:GRP_LANE0 + N_GROUPS].set(b_grp)
    w_hi = wr.astype(jnp.bfloat16)
    w_lo = (wr - w_hi.astype(jnp.float32)).astype(jnp.bfloat16)
    return jnp.concatenate([w_hi, w_lo], axis=1), br


def kernel(x, norm1_g, norm2_g, conv_w_in, conv_b_in, conv_dw, conv_dw_b, conv_ln_g, conv_ln_b, conv_w_out, conv_b_out, attn_w_qkv, attn_q_g, attn_k_g, attn_lq1, attn_lk1, attn_lq2, attn_lk2, attn_subln_g, attn_w_o, moe_w_grp, moe_b_grp, moe_w_exp, moe_b_exp, moe_w1, moe_w3, moe_w2):
    assert x.shape == (BATCH, SEQ, D_MODEL) and x.dtype == jnp.float32
    assert moe_w1.shape == (2, N_EXPERTS, D_MODEL, D_EXPERT) and conv_dw.shape[1] == CONV_WIDTH
    bf16 = jnp.bfloat16
    row = lambda a: a.reshape(1, -1)
    xf = x.reshape(N_TOK, D_MODEL)

    wr, br = _router_params(moe_w_grp[0], moe_b_grp[0], moe_w_exp[0], moe_b_exp[0])
    dw = jnp.zeros((HALO, D_MODEL), jnp.float32).at[:CONV_WIDTH].set(conv_dw[0])
    x1, h2s, rf, rft, cnt = _conv_layer(
        xf, row(norm1_g[0]), conv_w_in[0].astype(bf16), row(conv_b_in[0]), dw,
        row(conv_dw_b[0]), row(conv_ln_g[0]), row(conv_ln_b[0]),
        conv_w_out[0].astype(bf16), row(conv_b_out[0]), row(norm2_g[0]), wr, br)
    dest, yb = _moe_experts(0, h2s, rft, cnt, moe_w1, moe_w3, moe_w2)

    lambda_init = 0.8 - 0.6 * math.exp(-0.3 * 1)
    wr, br = _router_params(moe_w_grp[1], moe_b_grp[1], moe_w_exp[1], moe_b_exp[1])
    two = lambda g: jnp.concatenate([g, g]).reshape(1, V_DIM)
    xf, qk, v = _combine_qkv(dest, x1, rf, yb, row(norm1_g[1]), attn_w_qkv[0].astype(bf16),
                             two(attn_q_g[0]), two(attn_k_g[0]))
    lam_rows = jnp.zeros((SUBLANES, HEAD_DIM), jnp.float32)
    lam_rows = lam_rows.at[0].set(attn_lq1[0]).at[1].set(attn_lk1[0])
    lam_rows = lam_rows.at[2].set(attn_lq2[0]).at[3].set(attn_lk2[0])
    o = _attention(qk, v, lam_rows, row(attn_subln_g[0]), lambda_init)
    x1, h2s, rf, rft, cnt = _attn_post(xf, o, attn_w_o[0].astype(bf16), row(norm2_g[1]), wr, br)
    dest, yb = _moe_experts(1, h2s, rft, cnt, moe_w1, moe_w3, moe_w2)
    xf = _combine(dest, x1, rf, yb)
    return xf.reshape(BATCH, SEQ, D_MODEL)
```
